```python
import math
import jax, jax.numpy as jnp
from jax import lax
import numpy as np

D_MODEL = 4096
BATCH = 4
SEQ = 2048
DEPTH = 1
DEC_BATCH = 128
DEC_SEQ = 4
PAST_LEN = 16384
PAGE_SIZE = 128

CONV_CH = D_MODEL // 2
CONV_GROUPS = 16
CONV_K = 3
HG_HEADS = 16
HG_DK = 128
HG_DV = (D_MODEL - CONV_CH) // HG_HEADS
HG_KW = HG_HEADS * HG_DK
HG_VW = HG_HEADS * HG_DV
HG_CHUNK = 64
IN_COLS = 3 * CONV_CH + 2 * HG_KW + 2 * HG_VW
N_GROUPS = 8
EXP_PER_GROUP = 8
N_EXPERTS = N_GROUPS * EXP_PER_GROUP
TOP_K = 2
D_EXPERT = D_MODEL // 4
MOE_BLOCK = 64
EPS = 1e-6

kernel_name = "hymba_conv_hgrn2_hiermoe_step"


def rms_norm(x, g):
    xf = x.astype(jnp.float32)
    y = xf * lax.rsqrt(jnp.mean(xf * xf, axis=-1, keepdims=True) + EPS)
    return (y * g.astype(jnp.float32)).astype(x.dtype)


def hgrn2_chunked(q, log_f, k, v, s0, chunk):
    bsz, T, H, _ = q.shape
    DV = v.shape[-1]
    n = T // chunk

    def to_chunks(a):
        return jnp.moveaxis(a.reshape(bsz, n, chunk, H, a.shape[-1]), 1, 0)

    mask = jnp.tril(jnp.ones((chunk, chunk), dtype=bool))
    mid = (chunk - 1) // 2

    def step(S, inp):
        qc, gc, kc, vc = inp
        b = jnp.cumsum(gc, axis=1)
        b_ref = b[:, mid:mid + 1]
        scores = jnp.einsum('bthk,bshk->bhts', qc * jnp.exp(b - b_ref), kc * jnp.exp(b_ref - b))
        scores = jnp.where(mask, scores, 0.0)
        o = (jnp.einsum('bhts,bshv->bthv', scores, vc)
             + jnp.einsum('bthk,bhkv->bthv', qc * jnp.exp(b), S))
        b_last = b[:, -1]
        S = (jnp.exp(b_last)[..., None] * S
             + jnp.einsum('bshk,bshv->bhkv', kc * jnp.exp(b_last[:, None] - b), vc))
        return S, o

    s_fin, o = lax.scan(step, s0, (to_chunks(q), to_chunks(log_f), to_chunks(k), to_chunks(v)))
    o = jnp.moveaxis(o, 0, 1).reshape(bsz, T, H, DV)
    return o, s_fin


def token_mixer(xn, conv_buf, s0, w_in, conv_w, lb, hg_norm_g, w_out):
    bsz, T, _ = xn.shape
    f32 = jnp.float32
    proj = jnp.einsum('btd,de->bte', xn, w_in)
    splits = [CONV_CH, 2 * CONV_CH, 3 * CONV_CH, 3 * CONV_CH + HG_KW,
              3 * CONV_CH + 2 * HG_KW, 3 * CONV_CH + 2 * HG_KW + HG_VW]
    c_b, c_c, c_h, q, f_logit, v, g_out = jnp.split(proj, splits, axis=-1)

    u = c_c * c_h
    full = jnp.concatenate([conv_buf.astype(u.dtype), u], axis=1)
    conv = full[:, 0:T] * conv_w[0]
    for j in range(1, CONV_K):
        conv = conv + full[:, j:j + T] * conv_w[j]
    y_conv = c_b * conv
    new_buf = full[:, T:]

    fz = f_logit.astype(f32).reshape(bsz, T, HG_HEADS, HG_DK)
    lbh = lb.reshape(HG_HEADS, HG_DK)
    log_f = jnp.log(lbh + (1.0 - lbh) * jax.nn.sigmoid(fz))
    k = (1.0 - lbh) * jax.nn.sigmoid(-fz)
    qh = q.astype(f32).reshape(bsz, T, HG_HEADS, HG_DK)
    vh = v.astype(f32).reshape(bsz, T, HG_HEADS, HG_DV)
    o, s_new = hgrn2_chunked(qh, log_f, k, vh, s0.astype(f32), math.gcd(T, HG_CHUNK))
    o = o * lax.rsqrt(jnp.mean(o * o, axis=-1, keepdims=True) + EPS)
    o = o.reshape(bsz, T, HG_VW) * hg_norm_g.astype(f32) * jax.nn.silu(g_out.astype(f32))
    y_hg = o.astype(xn.dtype)

    out = jnp.einsum('bte,ed->btd', jnp.concatenate([y_conv, y_hg], axis=-1), w_out)
    return out, new_buf.astype(conv_buf.dtype), s_new.astype(s0.dtype)


def expert_mlp(xb, e, w_g, w_u, w_d):
    h = jax.nn.silu(xb @ w_g[e]) * (xb @ w_u[e])
    return h @ w_d[e]


def hier_moe(xf, w_gr, b_gr, w_er, b_er, w_g, w_u, w_d):
    f32 = jnp.float32
    n_tok, d = xf.shape
    n_assign = n_tok * TOP_K
    lg = (xf @ w_gr).astype(f32) + b_gr.astype(f32)
    p_group = jax.nn.softmax(lg, axis=-1)
    g_sel = jnp.argmax(lg, axis=-1).astype(jnp.int32)
    le = ((xf @ w_er).astype(f32) + b_er.astype(f32)).reshape(n_tok, N_GROUPS, EXP_PER_GROUP)
    le_sel = jnp.take_along_axis(le, g_sel[:, None, None], axis=1)[:, 0]
    top_p, top_i = lax.top_k(jax.nn.softmax(le_sel, axis=-1), TOP_K)
    p_g_sel = jnp.take_along_axis(p_group, g_sel[:, None], axis=1)
    gates = p_g_sel * top_p / jnp.sum(top_p, axis=-1, keepdims=True)
    expert_idx = g_sel[:, None] * EXP_PER_GROUP + top_i.astype(jnp.int32)

    flat_e = expert_idx.reshape(n_assign)
    flat_tok = jnp.arange(n_assign, dtype=jnp.int32) // TOP_K
    order = jnp.argsort(flat_e)
    sorted_e = flat_e[order]
    counts = jnp.bincount(flat_e, length=N_EXPERTS)
    padded = ((counts + MOE_BLOCK - 1) // MOE_BLOCK) * MOE_BLOCK
    pad_end = jnp.cumsum(padded)
    pad_start = pad_end - padded
    start = jnp.cumsum(counts) - counts
    rank = jnp.arange(n_assign, dtype=jnp.int32) - start[sorted_e]
    dest = (pad_start[sorted_e] + rank).astype(jnp.int32)
    n_blocks = -(-n_assign // MOE_BLOCK) + N_EXPERTS
    n_rows = n_blocks * MOE_BLOCK
    buf_tok = jnp.full((n_rows,), n_tok, jnp.int32).at[dest].set(flat_tok[order])
    xpad = jnp.concatenate([xf, jnp.zeros((1, d), xf.dtype)], axis=0)
    xb = xpad[buf_tok].reshape(n_blocks, MOE_BLOCK, d)
    block_start = jnp.arange(n_blocks, dtype=jnp.int32) * MOE_BLOCK
    block_e = jnp.minimum(jnp.searchsorted(pad_end, block_start, side='right'), N_EXPERTS - 1)
    yb = lax.map(lambda a: expert_mlp(a[0], a[1], w_g, w_u, w_d), (xb, block_e.astype(jnp.int32)))
    y_buf = yb.reshape(n_rows, d)
    dest_orig = jnp.zeros((n_assign,), jnp.int32).at[order].set(dest)
    y_assign = y_buf[dest_orig].reshape(n_tok, TOP_K, d)
    return jnp.einsum('nk,nkd->nd', gates.astype(xf.dtype), y_assign)


def layer(x, conv_buf, s0, norm_mix_g, w_in, conv_w, lb, hg_norm_g, w_out, norm_ffn_g,
          w_gr, b_gr, w_er, b_er, w_g, w_u, w_d):
    mix, new_buf, s_new = token_mixer(rms_norm(x, norm_mix_g), conv_buf, s0, w_in, conv_w, lb,
                                      hg_norm_g, w_out)
    h = x + mix
    hn = rms_norm(h, norm_ffn_g)
    ff = hier_moe(hn.reshape(-1, D_MODEL), w_gr, b_gr, w_er, b_er, w_g, w_u, w_d)
    return h + ff.reshape(h.shape), new_buf, s_new


def setup_inputs(seed: int = 0) -> dict:
    key = jax.random.key(seed)
    ks = jax.random.split(key, 20)
    f32 = jnp.float32

    def nrm(k, shape, scale):
        return jax.random.normal(k, shape, f32) * scale

    return {
        'x_prompt': nrm(ks[0], (BATCH, SEQ, D_MODEL), 1.0),
        'x_sample': nrm(ks[1], (DEC_BATCH, DEC_SEQ, D_MODEL), 1.0),
        'cache_conv': nrm(ks[2], (DEPTH, DEC_BATCH, CONV_K - 1, CONV_CH), 1.0),
        'state_hgrn': nrm(ks[3], (DEPTH, DEC_BATCH, HG_HEADS, HG_DK, HG_DV), 0.5),
        'norm_mix_g': 1.0 + nrm(ks[4], (DEPTH, D_MODEL), 0.02),
        'w_in': nrm(ks[5], (DEPTH, D_MODEL, IN_COLS), D_MODEL ** -0.5),
        'conv_w': nrm(ks[6], (DEPTH, CONV_K, CONV_CH), CONV_K ** -0.5),
        'lb_param': nrm(ks[7], (DEPTH + 1, HG_KW), 0.5),
        'hg_norm_g': 1.0 + nrm(ks[8], (DEPTH, HG_VW), 0.02),
        'w_out': nrm(ks[9], (DEPTH, CONV_CH + HG_VW, D_MODEL), (CONV_CH + HG_VW) ** -0.5),
        'norm_ffn_g': 1.0 + nrm(ks[10], (DEPTH, D_MODEL), 0.02),
        'w_group_router': nrm(ks[11], (DEPTH, D_MODEL, N_GROUPS), D_MODEL ** -0.5),
        'b_group_router': nrm(ks[12], (DEPTH, N_GROUPS), 0.01),
        'w_expert_router': nrm(ks[13], (DEPTH, D_MODEL, N_EXPERTS), D_MODEL ** -0.5),
        'b_expert_router': nrm(ks[14], (DEPTH, N_EXPERTS), 0.01),
        'w_exp_gate': nrm(ks[15], (DEPTH, N_EXPERTS, D_MODEL, D_EXPERT), D_MODEL ** -0.5),
        'w_exp_up': nrm(ks[16], (DEPTH, N_EXPERTS, D_MODEL, D_EXPERT), D_MODEL ** -0.5),
        'w_exp_down': nrm(ks[17], (DEPTH, N_EXPERTS, D_EXPERT, D_MODEL), D_EXPERT ** -0.5),
        'norm_final_g': 1.0 + nrm(ks[18], (D_MODEL,), 0.02),
    }


def reference(x_prompt, x_sample, cache_conv, state_hgrn, norm_mix_g, w_in, conv_w, lb_param,
              hg_norm_g, w_out, norm_ffn_g, w_group_router, b_group_router, w_expert_router,
              b_expert_router, w_exp_gate, w_exp_up, w_exp_down, norm_final_g):
    lb_all = jnp.cumsum(jax.nn.softmax(lb_param.astype(jnp.float32), axis=0), axis=0)
    xp, xs = x_prompt, x_sample
    bp = x_prompt.shape[0]
    conv_p, hgrn_p, conv_s, hgrn_s = [], [], [], []
    for l in range(DEPTH):
        params = (norm_mix_g[l], w_in[l], conv_w[l], lb_all[l], hg_norm_g[l], w_out[l], norm_ffn_g[l],
                  w_group_router[l], b_group_router[l], w_expert_router[l], b_expert_router[l],
                  w_exp_gate[l], w_exp_up[l], w_exp_down[l])
        zero_buf = jnp.zeros((bp, CONV_K - 1, CONV_CH), cache_conv.dtype)
        zero_state = jnp.zeros((bp, HG_HEADS, HG_DK, HG_DV), state_hgrn.dtype)
        xp, cp, sp = layer(xp, zero_buf, zero_state, *params)
        xs, cs, ss = layer(xs, cache_conv[l], state_hgrn[l], *params)
        conv_p.append(cp)
        hgrn_p.append(sp)
        conv_s.append(cs)
        hgrn_s.append(ss)
    y_prompt = rms_norm(xp, norm_final_g)
    y_sample = rms_norm(xs, norm_final_g)
    return (y_prompt, y_sample, jnp.stack(conv_p), jnp.stack(hgrn_p), jnp.stack(conv_s), jnp.stack(hgrn_s))
```

```python
import functools

import jax
import jax.numpy as jnp
from jax import lax
from jax.experimental import pallas as pl
from jax.experimental.pallas import tpu as pltpu

F32 = jnp.float32
BF16 = jnp.bfloat16
I32 = jnp.int32

D_MODEL = 4096
BATCH = 4
SEQ = 2048
DEC_BATCH = 128
DEC_SEQ = 4
CONV_CH = 2048
CONV_K = 3
HG_HEADS = 16
HG_DK = 128
HG_DV = 128
HG_W = HG_HEADS * HG_DK
HG_CHUNK = 64
IN_COLS = 3 * CONV_CH + 4 * HG_W
N_GROUPS = 8
EXP_PER_GROUP = 8
N_EXPERTS = 64
TOP_K = 2
D_EXPERT = 1024
EPS = 1e-6

N_PROMPT = BATCH * SEQ
N_SAMPLE = DEC_BATCH * DEC_SEQ
N_TOK = N_PROMPT + N_SAMPLE
N_ASSIGN = N_TOK * TOP_K

LANES = 128
SUBLANES = 8
ROUTE_COLS = LANES

NORM_ROWS = 256
MM_ROWS = 512
MM_COLS = 1024
CONV_ROWS = 512
CONV_COLS = 512
HG_ROWS = 512
HG_SAMPLE_SEQS = 4
MOE_BLK = 128
MOE_NB = N_ASSIGN // MOE_BLK + N_EXPERTS
MOE_ROWS = MOE_NB * MOE_BLK
MOE_FC = 512
MOE_DC = 2048
COMB_ROWS = 128

MIB = 1 << 20


def _params(sem, vmem_mib):
    return pltpu.CompilerParams(dimension_semantics=sem, vmem_limit_bytes=vmem_mib * MIB)


def _sigmoid(x):
    return 1.0 / (1.0 + jnp.exp(-x))


def _rmsnorm_rows(x, g):
    return x * lax.rsqrt(jnp.mean(x * x, axis=-1, keepdims=True) + EPS) * g


def _norm_in_kernel(xp_ref, xs_ref, g_ref, o_ref, *, n_prompt_tiles):
    i = pl.program_id(0)

    @pl.when(i < n_prompt_tiles)
    def _():
        o_ref[...] = _rmsnorm_rows(xp_ref[...], g_ref[...]).astype(o_ref.dtype)

    @pl.when(i >= n_prompt_tiles)
    def _():
        o_ref[...] = _rmsnorm_rows(xs_ref[...], g_ref[...]).astype(o_ref.dtype)


def _norm_in(xp, xs, g):
    npt = N_PROMPT // NORM_ROWS
    nst = N_SAMPLE // NORM_ROWS
    return pl.pallas_call(
        functools.partial(_norm_in_kernel, n_prompt_tiles=npt),
        grid=(npt + nst,),
        in_specs=[
            pl.BlockSpec((NORM_ROWS, D_MODEL), lambda i: (jnp.minimum(i, npt - 1), 0)),
            pl.BlockSpec((NORM_ROWS, D_MODEL), lambda i: (jnp.maximum(i - npt, 0), 0)),
            pl.BlockSpec((1, D_MODEL), lambda i: (0, 0)),
        ],
        out_specs=pl.BlockSpec((NORM_ROWS, D_MODEL), lambda i: (i, 0)),
        out_shape=jax.ShapeDtypeStruct((N_TOK, D_MODEL), BF16),
        compiler_params=_params(("arbitrary",), 40),
        name="norm_in",
    )(xp, xs, g)


def _matmul_kernel(x_ref, w_ref, o_ref):
    o_ref[...] = jnp.dot(x_ref[...], w_ref[...], preferred_element_type=F32).astype(o_ref.dtype)


def _in_proj(xn, w):
    return pl.pallas_call(
        _matmul_kernel,
        grid=(IN_COLS // MM_COLS, N_TOK // MM_ROWS),
        in_specs=[
            pl.BlockSpec((MM_ROWS, D_MODEL), lambda n, m: (m, 0)),
            pl.BlockSpec((D_MODEL, MM_COLS), lambda n, m: (0, n)),
        ],
        out_specs=pl.BlockSpec((MM_ROWS, MM_COLS), lambda n, m: (m, n)),
        out_shape=jax.ShapeDtypeStruct((N_TOK, IN_COLS), BF16),
        compiler_params=_params(("arbitrary", "arbitrary"), 48),
        name="in_proj",
    )(xn, w)


def _conv_taps(u, um1, um2, cb, w):
    conv = um2 * w[0:1] + um1 * w[1:2] + u * w[2:3]
    return cb * conv


def _conv_prompt_kernel(cb_ref, cc_ref, ch_ref, w_ref, y_ref, nb_ref, carry_ref):
    t = pl.program_id(2)

    @pl.when(t == 0)
    def _():
        carry_ref[...] = jnp.zeros_like(carry_ref)

    u = cc_ref[...].astype(F32) * ch_ref[...].astype(F32)
    prev = carry_ref[...]
    p1 = prev[SUBLANES - 1:SUBLANES]
    p2 = prev[SUBLANES - 2:SUBLANES - 1]
    row = lax.broadcasted_iota(I32, u.shape, 0)
    um1 = jnp.where(row == 0, p1, pltpu.roll(u, 1, axis=0))
    um2 = jnp.where(row == 0, p2, jnp.where(row == 1, p1, pltpu.roll(u, 2, axis=0)))
    y_ref[...] = _conv_taps(u, um1, um2, cb_ref[...].astype(F32), w_ref[...]).astype(y_ref.dtype)
    carry_ref[...] = u[CONV_ROWS - SUBLANES:CONV_ROWS]
    nb_ref[0] = u[CONV_ROWS - (CONV_K - 1):CONV_ROWS]


def _conv_prompt(proj, conv_w):
    nt = SEQ // CONV_ROWS
    nc = CONV_CH // CONV_COLS
    return pl.pallas_call(
        _conv_prompt_kernel,
        grid=(BATCH, nc, nt),
        in_specs=[
            pl.BlockSpec((CONV_ROWS, CONV_COLS), lambda b, c, t: (b * nt + t, c)),
            pl.BlockSpec((CONV_ROWS, CONV_COLS), lambda b, c, t: (b * nt + t, nc + c)),
            pl.BlockSpec((CONV_ROWS, CONV_COLS), lambda b, c, t: (b * nt + t, 2 * nc + c)),
            pl.BlockSpec((CONV_K, CONV_COLS), lambda b, c, t: (0, c)),
        ],
        out_specs=[
            pl.BlockSpec((CONV_ROWS, CONV_COLS), lambda b, c, t: (b * nt + t, c)),
            pl.BlockSpec((1, CONV_K - 1, CONV_COLS), lambda b, c, t: (b, 0, c)),
        ],
        out_shape=[
            jax.ShapeDtypeStruct((N_PROMPT, CONV_CH), BF16),
            jax.ShapeDtypeStruct((BATCH, CONV_K - 1, CONV_CH), F32),
        ],
        scratch_shapes=[pltpu.VMEM((SUBLANES, CONV_COLS), F32)],
        compiler_params=_params(("arbitrary", "arbitrary", "arbitrary"), 32),
        name="conv_prompt",
    )(proj, proj, proj, conv_w)


def _conv_sample_kernel(cb_ref, cc_ref, ch_ref, w_ref, e1_ref, e2_ref, y_ref, u_ref):
    u =cc_ref[...].astype(F32) * ch_ref[...].astype(F32)
    tpos = lax.broadcasted_iota(I32, u.shape, 0) % DEC_SEQ
    um1 = jnp.where(tpos == 0, e1_ref[...], pltpu.roll(u, 1, axis=0))
    um2 = jnp.where(tpos <= 1, e2_ref[...], pltpu.roll(u, 2, axis=0))
    y_ref[...] = _conv_taps(u, um1, um2, cb_ref[...].astype(F32), w_ref[...]).astype(y_ref.dtype)
    u_ref[...] = u


def _conv_sample(proj, conv_w, e1, e2):
    nc = CONV_CH // CONV_COLS
    rb = N_PROMPT // N_SAMPLE
    return pl.pallas_call(
        _conv_sample_kernel,
        grid=(nc,),
        in_specs=[
            pl.BlockSpec((N_SAMPLE, CONV_COLS), lambda c: (rb, c)),
            pl.BlockSpec((N_SAMPLE, CONV_COLS), lambda c: (rb, nc + c)),
            pl.BlockSpec((N_SAMPLE, CONV_COLS), lambda c: (rb, 2 * nc + c)),
            pl.BlockSpec((CONV_K, CONV_COLS), lambda c: (0, c)),
            pl.BlockSpec((N_SAMPLE, CONV_COLS), lambda c: (0, c)),
            pl.BlockSpec((N_SAMPLE, CONV_COLS), lambda c: (0, c)),
        ],
        out_specs=[
            pl.BlockSpec((N_SAMPLE, CONV_COLS), lambda c: (0, c)),
            pl.BlockSpec((N_SAMPLE, CONV_COLS), lambda c: (0, c)),
        ],
        out_shape=[
            jax.ShapeDtypeStruct((N_SAMPLE, CONV_CH), BF16),
            jax.ShapeDtypeStruct((N_SAMPLE, CONV_CH), F32),
        ],
        compiler_params=_params(("arbitrary",), 32),
        name="conv_sample",
    )(proj, proj, proj, conv_w, e1, e2)


def _hgrn_tile(q, fz, v, states, lb, seq_len):
    rows = q.shape[0]
    n_seq = rows // seq_len
    one_m_lb = 1.0 - lb
    log_f = jnp.log(lb + one_m_lb * _sigmoid(fz))
    k = one_m_lb * _sigmoid(-fz)
    row = lax.broadcasted_iota(I32, (rows, HG_DK), 0)
    tpos = row % seq_len
    seq = row // seq_len

    b = log_f
    shift = 1
    while shift < seq_len:
        b = b + jnp.where(tpos >= shift, pltpu.roll(b, shift, axis=0), 0.0)
        shift *= 2

    mid = (seq_len - 1) // 2
    b_ref = b[mid:mid + 1]
    b_last = b[seq_len - 1:seq_len]
    for j in range(1, n_seq):
        b_ref = jnp.where(seq == j, b[j * seq_len + mid:j * seq_len + mid + 1], b_ref)
        b_last = jnp.where(seq == j, b[(j + 1) * seq_len - 1:(j + 1) * seq_len], b_last)

    qe = (q * jnp.exp(b - b_ref)).astype(BF16)
    ke = (k * jnp.exp(b_ref - b)).astype(BF16)
    qb = (q * jnp.exp(b)).astype(BF16)
    kd = k * jnp.exp(b_last - b)
    vb = v.astype(BF16)

    scores = lax.dot_general(qe, ke, (((1,), (1,)), ((), ())), preferred_element_type=F32)
    rr = lax.broadcasted_iota(I32, (rows, rows), 0)
    cc = lax.broadcasted_iota(I32, (rows, rows), 1)
    causal = (cc <= rr) & ((rr // seq_len) == (cc // seq_len))
    scores = jnp.where(causal, scores, 0.0)
    o = jnp.dot(scores.astype(BF16), vb, preferred_element_type=F32)

    ones = jnp.ones((rows, HG_DV), F32)
    new_states = []
    for j in range(n_seq):
        s_j = states[j]
        o_j = jnp.dot(qb, s_j.astype(BF16), preferred_element_type=F32)
        kd_j = kd
        lf_j = log_f
        if n_seq > 1:
            o_j = jnp.where(seq == j, o_j, 0.0)
            kd_j = jnp.where(seq == j, kd, 0.0)
            lf_j = jnp.where(seq == j, log_f, 0.0)
        o = o + o_j
        inc = lax.dot_general(kd_j.astype(BF16), vb, (((0,), (0,)), ((), ())),
                              preferred_element_type=F32)
        decay = jnp.exp(lax.dot_general(lf_j, ones, (((0,), (0,)), ((), ())),
                                        precision=lax.Precision.HIGHEST,
                                        preferred_element_type=F32))
        new_states.append(decay * s_j + inc)
    return o, new_states


def _head_out(o, gate, norm_g):
    o_n = o * lax.rsqrt(jnp.mean(o * o, axis=-1, keepdims=True) + EPS)
    return o_n * norm_g * (gate * _sigmoid(gate))


def _hgrn_prompt_kernel(q_ref, f_ref, v_ref, g_ref, lb_ref, ng_ref, y_ref, s_out_ref, s_scr):
    t = pl.program_id(2)

    @pl.when(t == 0)
    def _():
        s_scr[...] = jnp.zeros_like(s_scr)

    lb = lb_ref[...]
    ng = ng_ref[...]
    state = s_scr[...]
    for c in range(HG_ROWS // HG_CHUNK):
        sl = pl.ds(c * HG_CHUNK, HG_CHUNK)
        o, (state,) = _hgrn_tile(q_ref[sl, :].astype(F32), f_ref[sl, :].astype(F32),
                                 v_ref[sl, :].astype(F32), [state], lb, HG_CHUNK)
        y_ref[sl, :] = _head_out(o, g_ref[sl, :].astype(F32), ng).astype(y_ref.dtype)
    s_scr[...] = state

    @pl.when(t == pl.num_programs(2) - 1)
    def _():
        s_out_ref[0, 0] = state


def _hgrn_prompt(proj, lb, norm_g):
    nt = SEQ // HG_ROWS
    cq = 3 * CONV_CH // HG_DK

    def spec(off):
        return pl.BlockSpec((HG_ROWS, HG_DK), lambda b, h, t: (b * nt + t, cq + off * HG_HEADS + h))

    return pl.pallas_call(
        _hgrn_prompt_kernel,
        grid=(BATCH, HG_HEADS, nt),
        in_specs=[spec(0), spec(1), spec(2), spec(3),
                  pl.BlockSpec((1, HG_DK), lambda b, h, t: (0, h)),
                  pl.BlockSpec((1, HG_DV), lambda b, h, t: (0, h))],
        out_specs=[
            pl.BlockSpec((HG_ROWS, HG_DV), lambda b, h, t: (b * nt + t, h)),
            pl.BlockSpec((1, 1, HG_DK, HG_DV), lambda b, h, t: (b, h, 0, 0)),
        ],
        out_shape=[
            jax.ShapeDtypeStruct((N_PROMPT, HG_W), BF16),
            jax.ShapeDtypeStruct((BATCH, HG_HEADS, HG_DK, HG_DV), F32),
        ],
        scratch_shapes=[pltpu.VMEM((HG_DK, HG_DV), F32)],
        compiler_params=_params(("arbitrary", "arbitrary", "arbitrary"), 32),
        name="hgrn_prompt",
    )(proj, proj, proj, proj, lb, norm_g)


def _hgrn_sample_kernel(q_ref, f_ref, v_ref, g_ref, lb_ref, ng_ref, s_ref, y_ref, s_out_ref):
    rows = HG_SAMPLE_SEQS * DEC_SEQ
    seqs_per_tile = SUBLANES // DEC_SEQ

    def head(h, carry):
        cols = pl.ds(pl.multiple_of(h * HG_DK, HG_DK), HG_DK)
        q = q_ref[:, cols].astype(F32)
        fz = f_ref[:, cols].astype(F32)
        v = v_ref[:, cols].astype(F32)
        g = g_ref[:, cols].astype(F32)
        lb = lb_ref[:, cols]
        ng = ng_ref[:, cols]
        outs = []
        for r in range(rows // SUBLANES):
            rs = slice(r * SUBLANES, (r + 1) * SUBLANES)
            states = [s_ref[r * seqs_per_tile + j, h] for j in range(seqs_per_tile)]
            o, new_states = _hgrn_tile(q[rs], fz[rs], v[rs], states, lb, DEC_SEQ)
            for j in range(seqs_per_tile):
                s_out_ref[r * seqs_per_tile + j, h] = new_states[j]
            outs.append(_head_out(o, g[rs], ng))
        y_ref[:, cols] = jnp.concatenate(outs, axis=0).astype(y_ref.dtype)
        return carry

    lax.fori_loop(0, HG_HEADS, head, 0)


def _hgrn_sample(proj, lb, norm_g, state):
    rows = HG_SAMPLE_SEQS * DEC_SEQ
    rb = N_PROMPT // rows
    cq = 3 * CONV_CH // HG_W

    def spec(off):
        return pl.BlockSpec((rows, HG_W), lambda g: (rb + g, cq + off))

    st_spec = pl.BlockSpec((HG_SAMPLE_SEQS, HG_HEADS, HG_DK, HG_DV), lambda g: (g, 0, 0, 0))
    return pl.pallas_call(
        _hgrn_sample_kernel,
        grid=(DEC_BATCH // HG_SAMPLE_SEQS,),
        in_specs=[spec(0), spec(1), spec(2), spec(3),
                  pl.BlockSpec((1, HG_W), lambda g: (0, 0)),
                  pl.BlockSpec((1, HG_W), lambda g: (0, 0)),
                  st_spec],
        out_specs=[pl.BlockSpec((rows, HG_W), lambda g: (g, 0)), st_spec],
        out_shape=[
            jax.ShapeDtypeStruct((N_SAMPLE, HG_W), BF16),
            jax.ShapeDtypeStruct((DEC_BATCH, HG_HEADS, HG_DK, HG_DV), F32),
        ],
        compiler_params=_params(("arbitrary",), 40),
        name="hgrn_sample",
    )(proj, proj, proj, proj, lb, norm_g, state)


def _out_proj_kernel(ycp_ref, yhp_ref, xp_ref, ycs_ref, yhs_ref, xs_ref, wc_ref, wh_ref, o_ref, *,
                     n_prompt_tiles):
    m = pl.program_id(1)

    def residual_mix(yc_ref, yh_ref, x_ref):
        mix = (jnp.dot(yc_ref[...], wc_ref[...], preferred_element_type=F32)
               + jnp.dot(yh_ref[...], wh_ref[...], preferred_element_type=F32))
        o_ref[...] = x_ref[...] + mix

    @pl.when(m < n_prompt_tiles)
    def _():
        residual_mix(ycp_ref, yhp_ref, xp_ref)

    @pl.when(m >= n_prompt_tiles)
    def _():
        residual_mix(ycs_ref, yhs_ref, xs_ref)


def _out_proj(yc_p, yh_p, xp, yc_s, yh_s, xs, w):
    npt = N_PROMPT // MM_ROWS

    def prompt_map(n, m):
        return jnp.minimum(m, npt - 1)

    def sample_map(n, m):
        return jnp.maximum(m - npt, 0)

    def source_specs(row_map):
        return [
            pl.BlockSpec((MM_ROWS, CONV_CH), lambda n, m: (row_map(n, m), 0)),
            pl.BlockSpec((MM_ROWS, HG_W), lambda n, m: (row_map(n, m), 0)),
            pl.BlockSpec((MM_ROWS, MM_COLS), lambda n, m: (row_map(n, m), n)),
        ]

    return pl.pallas_call(
        functools.partial(_out_proj_kernel, n_prompt_tiles=npt),
        grid=(D_MODEL // MM_COLS, N_TOK // MM_ROWS),
        in_specs=source_specs(prompt_map) + source_specs(sample_map) + [
            pl.BlockSpec((CONV_CH, MM_COLS), lambda n, m: (0, n)),
            pl.BlockSpec((HG_W, MM_COLS), lambda n, m: (1, n)),
        ],
        out_specs=pl.BlockSpec((MM_ROWS, MM_COLS), lambda n, m: (m, n)),
        out_shape=jax.ShapeDtypeStruct((N_TOK, D_MODEL), F32),
        compiler_params=_params(("arbitrary", "arbitrary"), 56),
        name="out_proj",
    )(yc_p, yh_p, xp, yc_s, yh_s, xs, w, w)


def _norm_route_kernel(h_ref, g_ref, wr_ref, br_ref, hn_ref, eidx_ref, gate_ref):
    hn = _rmsnorm_rows(h_ref[...], g_ref[...])
    hn_ref[...] = hn
    logits = jnp.dot(hn, wr_ref[...], precision=lax.Precision.HIGHEST,
                     preferred_element_type=F32) + br_ref[...]
    lane = lax.broadcasted_iota(I32, logits.shape, 1)
    lane_f = lane.astype(F32)
    neg = -jnp.inf

    def first_argmax(vals, vmax):
        first = jnp.min(jnp.where(vals == vmax, lane_f, float(ROUTE_COLS)), axis=-1, keepdims=True)
        return first.astype(I32)

    is_grp = lane < N_GROUPS
    lg = jnp.where(is_grp, logits, neg)
    mg = jnp.max(lg, axis=-1, keepdims=True)
    g_sel = first_argmax(lg, mg)
    p_grp = 1.0 / jnp.sum(jnp.where(is_grp, jnp.exp(logits - mg), 0.0), axis=-1, keepdims=True)

    in_grp = (lane >= N_GROUPS) & (((lane - N_GROUPS) // EXP_PER_GROUP) == g_sel)
    le = jnp.where(in_grp, logits, neg)
    m1 = jnp.max(le, axis=-1, keepdims=True)
    i1 = first_argmax(le, m1)
    le2 = jnp.where(lane == i1, neg, le)
    m2 = jnp.max(le2, axis=-1, keepdims=True)
    i2 = first_argmax(le2, m2)
    e2 = jnp.exp(m2 - m1)
    gate1 = p_grp / (1.0 + e2)
    gate2 = p_grp * e2 / (1.0 + e2)
    eidx_ref[...] = jnp.where(lane == 0, i1 - N_GROUPS, jnp.where(lane == 1, i2 - N_GROUPS, 0))
    gate_ref[...] = jnp.where(lane == 0, gate1, jnp.where(lane == 1, gate2, 0.0))


def _norm_route(h, g, w_r, b_r):
    return pl.pallas_call(
        _norm_route_kernel,
        grid=(N_TOK // NORM_ROWS,),
        in_specs=[
            pl.BlockSpec((NORM_ROWS, D_MODEL), lambda i: (i, 0)),
            pl.BlockSpec((1, D_MODEL), lambda i: (0, 0)),
            pl.BlockSpec((D_MODEL, ROUTE_COLS), lambda i: (0, 0)),
            pl.BlockSpec((1, ROUTE_COLS), lambda i: (0, 0)),
        ],
        out_specs=[
            pl.BlockSpec((NORM_ROWS, D_MODEL), lambda i: (i, 0)),
            pl.BlockSpec((NORM_ROWS, ROUTE_COLS), lambda i: (i, 0)),
            pl.BlockSpec((NORM_ROWS, ROUTE_COLS), lambda i: (i, 0)),
        ],
        out_shape=[
            jax.ShapeDtypeStruct((N_TOK, D_MODEL), F32),
            jax.ShapeDtypeStruct((N_TOK, ROUTE_COLS), I32),
            jax.ShapeDtypeStruct((N_TOK, ROUTE_COLS), F32),
        ],
        compiler_params=_params(("arbitrary",), 48),
        name="norm_route",
    )(h, g, w_r, b_r)


def _row_copy(src_hbm, row, buf, slot, r, sem):
    return pltpu.make_async_copy(src_hbm.at[pl.ds(row, 1)], buf.at[slot, pl.ds(r, 1)], sem.at[slot])


def _gather_issue(idx_ref, base, n_rows, n_valid, src_hbm, buf, slot, sem):
    def body(r, carry):
        row = idx_ref[base + r]

        @pl.when(row < n_valid)
        def _():
            _row_copy(src_hbm, row, buf, slot, r, sem).start()

        @pl.when(row >= n_valid)
        def _():
            buf[slot, pl.ds(r, 1), :] = jnp.zeros((1, buf.shape[-1]), buf.dtype)

        return carry

    lax.fori_loop(0, n_rows, body, 0)


def _gather_wait(idx_ref, base, n_rows, n_valid, src_hbm, buf, slot, sem):
    def body(r, carry):
        @pl.when(idx_ref[base + r] < n_valid)
        def _():
            _row_copy(src_hbm, 0, buf, slot, r, sem).wait()

        return carry

    lax.fori_loop(0, n_rows, body, 0)


def _dispatch_kernel(tok_ref, hn_hbm, o_ref, buf, sem):
    b = pl.program_id(0)
    nb = pl.num_programs(0)
    slot = b % 2

    @pl.when(b == 0)
    def _():
        _gather_issue(tok_ref, 0, MOE_BLK, N_TOK, hn_hbm, buf, 0, sem)

    @pl.when(b + 1 < nb)
    def _():
        _gather_issue(tok_ref, (b + 1) * MOE_BLK, MOE_BLK, N_TOK, hn_hbm, buf, 1 - slot, sem)

    _gather_wait(tok_ref, b * MOE_BLK, MOE_BLK, N_TOK, hn_hbm, buf, slot, sem)
    o_ref[...] = buf[slot].astype(o_ref.dtype)


def _dispatch(row_tok, hn):
    return pl.pallas_call(
        _dispatch_kernel,
        grid_spec=pltpu.PrefetchScalarGridSpec(
            num_scalar_prefetch=1,
            grid=(MOE_NB,),
            in_specs=[pl.BlockSpec(memory_space=pl.ANY)],
            out_specs=pl.BlockSpec((MOE_BLK, D_MODEL), lambda b, tok: (b, 0)),
            scratch_shapes=[pltpu.VMEM((2, MOE_BLK, D_MODEL), F32),
                            pltpu.SemaphoreType.DMA((2,))],
        ),
        out_shape=jax.ShapeDtypeStruct((MOE_ROWS, D_MODEL), BF16),
        compiler_params=_params(("arbitrary",), 32),
        name="moe_dispatch",
    )(row_tok, hn)


def _expert_up_kernel(be_ref, nbu_ref, x_ref, wg_ref, wu_ref, h_ref):
    b = pl.program_id(1)

    @pl.when(b < nbu_ref[0])
    def _():
        x = x_ref[...]
        g = jnp.dot(x, wg_ref[0].astype(BF16), preferred_element_type=F32)
        u = jnp.dot(x, wu_ref[0].astype(BF16), preferred_element_type=F32)
        h_ref[...] = (g * _sigmoid(g) * u).astype(h_ref.dtype)

    @pl.when(b >= nbu_ref[0])
    def _():
        h_ref[...] = jnp.zeros_like(h_ref)


def _expert_up(block_e, n_used, xb, w_g, w_u):
    def x_map(j, b, be, nbu):
        return (jnp.minimum(b, nbu[0] - 1), 0)

    def w_map(j, b, be, nbu):
        return (be[b], 0, j)

    return pl.pallas_call(
        _expert_up_kernel,
        grid_spec=pltpu.PrefetchScalarGridSpec(
            num_scalar_prefetch=2,
            grid=(D_EXPERT // MOE_FC, MOE_NB),
            in_specs=[
                pl.BlockSpec((MOE_BLK, D_MODEL), x_map),
                pl.BlockSpec((1, D_MODEL, MOE_FC), w_map),
                pl.BlockSpec((1, D_MODEL, MOE_FC), w_map),
            ],
            out_specs=pl.BlockSpec((MOE_BLK, MOE_FC), lambda j, b, be, nbu: (b, j)),
        ),
        out_shape=jax.ShapeDtypeStruct((MOE_ROWS, D_EXPERT), BF16),
        compiler_params=_params(("arbitrary", "arbitrary"), 56),
        name="expert_up",
    )(block_e, n_used, xb, w_g, w_u)


def _expert_down_kernel(be_ref, nbu_ref, h_ref, wd_ref, y_ref):
    b = pl.program_id(1)

    @pl.when(b < nbu_ref[0])
    def _():
        y_ref[...] = jnp.dot(h_ref[...], wd_ref[0].astype(BF16), preferred_element_type=F32)

    @pl.when(b >= nbu_ref[0])
    def _():
        y_ref[...] = jnp.zeros_like(y_ref)


def _expert_down(block_e, n_used, hb, w_d):
    return pl.pallas_call(
        _expert_down_kernel,
        grid_spec=pltpu.PrefetchScalarGridSpec(
            num_scalar_prefetch=2,
            grid=(D_MODEL // MOE_DC, MOE_NB),
            in_specs=[
                pl.BlockSpec((MOE_BLK, D_EXPERT), lambda n, b, be, nbu: (jnp.minimum(b, nbu[0] - 1), 0)),
                pl.BlockSpec((1, D_EXPERT, MOE_DC), lambda n, b, be, nbu: (be[b], 0, n)),
            ],
            out_specs=pl.BlockSpec((MOE_BLK, MOE_DC), lambda n, b, be, nbu: (b, n)),
        ),
        out_shape=jax.ShapeDtypeStruct((MOE_ROWS, D_MODEL), F32),
        compiler_params=_params(("arbitrary", "arbitrary"), 48),
        name="expert_down",
    )(block_e, n_used, hb, w_d)


def _combine_kernel(dest_ref, h_ref, gate_ref, g_ref, y_hbm, o_ref, buf, sem, *, tile0):
    i = pl.program_id(0)
    n = pl.num_programs(0)
    slot = i % 2
    rows = TOP_K * COMB_ROWS
    base = (tile0 + i) * rows

    @pl.when(i == 0)
    def _():
        _gather_issue(dest_ref, base, rows, MOE_ROWS, y_hbm, buf, 0, sem)

    @pl.when(i + 1 < n)
    def _():
        _gather_issue(dest_ref, base + rows, rows, MOE_ROWS, y_hbm, buf, 1 - slot, sem)

    _gather_wait(dest_ref, base, rows, MOE_ROWS, y_hbm, buf, slot, sem)
    gates = gate_ref[...]
    ff = gates[:, 0:1] * buf[slot, 0:COMB_ROWS] + gates[:, 1:2] * buf[slot, COMB_ROWS:rows]
    o_ref[...] = _rmsnorm_rows(h_ref[...] + ff, g_ref[...])


def _combine(dest, h, gates, g_final, y_buf, tile0, n_tiles):
    return pl.pallas_call(
        functools.partial(_combine_kernel, tile0=tile0),
        grid_spec=pltpu.PrefetchScalarGridSpec(
            num_scalar_prefetch=1,
            grid=(n_tiles,),
            in_specs=[
                pl.BlockSpec((COMB_ROWS, D_MODEL), lambda i, d: (tile0 + i, 0)),
                pl.BlockSpec((COMB_ROWS, ROUTE_COLS), lambda i, d: (tile0 + i, 0)),
                pl.BlockSpec((1, D_MODEL), lambda i, d: (0, 0)),
                pl.BlockSpec(memory_space=pl.ANY),
            ],
            out_specs=pl.BlockSpec((COMB_ROWS, D_MODEL), lambda i, d: (i, 0)),
            scratch_shapes=[pltpu.VMEM((2, TOP_K * COMB_ROWS, D_MODEL), F32),
                            pltpu.SemaphoreType.DMA((2,))],
        ),
        out_shape=jax.ShapeDtypeStruct((n_tiles * COMB_ROWS, D_MODEL), F32),
        compiler_params=_params(("arbitrary",), 40),
        name="moe_combine",
    )(dest, h, gates, g_final, y_buf)


def _dispatch_plan(eidx):
    flat_e = eidx.reshape(N_ASSIGN)
    onehot = (flat_e[:, None] == jnp.arange(N_EXPERTS, dtype=I32)[None, :]).astype(I32)
    csum = jnp.cumsum(onehot, axis=0)
    counts = csum[-1]
    rank = jnp.take_along_axis(csum, flat_e[:, None], axis=1)[:, 0] - 1
    padded = ((counts + MOE_BLK - 1) // MOE_BLK) * MOE_BLK
    pad_end = jnp.cumsum(padded)
    pad_start = pad_end - padded
    dest = (pad_start[flat_e] + rank).astype(I32)
    n_used = (pad_end[-1] // MOE_BLK).astype(I32)
    blk = jnp.arange(MOE_NB, dtype=I32)
    block_e = jnp.minimum(jnp.searchsorted(pad_end, blk * MOE_BLK, side='right'), N_EXPERTS - 1)
    block_e = jnp.where(blk < n_used, block_e, block_e[n_used - 1]).astype(I32)
    row_tok = jnp.full((MOE_ROWS,), N_TOK, I32).at[dest].set(jnp.arange(N_ASSIGN, dtype=I32) // TOP_K)
    return dest, row_tok, block_e, n_used.reshape(1)


def kernel(x_prompt, x_sample, cache_conv, state_hgrn, norm_mix_g, w_in, conv_w, lb_param, hg_norm_g,
           w_out, norm_ffn_g, w_group_router, b_group_router, w_expert_router, b_expert_router,
           w_exp_gate, w_exp_up, w_exp_down, norm_final_g):
    xp = x_prompt.reshape(N_PROMPT, D_MODEL)
    xs = x_sample.reshape(N_SAMPLE, D_MODEL)
    lb = jnp.cumsum(jax.nn.softmax(lb_param.astype(F32), axis=0), axis=0)[0].reshape(1, HG_W)

    xn = _norm_in(xp, xs, norm_mix_g[0].reshape(1, D_MODEL))
    proj = _in_proj(xn, w_in[0].astype(BF16))

    buf = cache_conv[0]
    zeros = jnp.zeros((DEC_BATCH, DEC_SEQ - 1, CONV_CH), F32)
    e1 = jnp.concatenate([buf[:, 1:2], zeros], axis=1).reshape(N_SAMPLE, CONV_CH)
    e2 = jnp.concatenate([buf, zeros[:, :DEC_SEQ - 2]], axis=1).reshape(N_SAMPLE, CONV_CH)
    yc_p, conv_p = _conv_prompt(proj, conv_w[0])
    yc_s, u_s = _conv_sample(proj, conv_w[0], e1, e2)
    conv_s = u_s.reshape(DEC_BATCH, DEC_SEQ, CONV_CH)[:, DEC_SEQ - (CONV_K - 1):]

    ng = hg_norm_g[0].reshape(1, HG_W)
    yh_p, hgrn_p = _hgrn_prompt(proj, lb, ng)
    yh_s, hgrn_s = _hgrn_sample(proj, lb, ng, state_hgrn[0])

    h = _out_proj(yc_p, yh_p, xp, yc_s, yh_s, xs, w_out[0].astype(BF16))

    pad = jnp.zeros((D_MODEL, ROUTE_COLS - N_GROUPS - N_EXPERTS), F32)
    w_r = jnp.concatenate([w_group_router[0], w_expert_router[0], pad], axis=1)
    b_r = jnp.concatenate([b_group_router[0], b_expert_router[0], pad[0]]).reshape(1, ROUTE_COLS)
    hn, eidx, gates = _norm_route(h, norm_ffn_g[0].reshape(1, D_MODEL), w_r, b_r)
    dest, row_tok, block_e, n_used = _dispatch_plan(eidx[:, :TOP_K])
    xb = _dispatch(row_tok, hn)
    hb = _expert_up(block_e, n_used, xb, w_exp_gate[0], w_exp_up[0])
    yb = _expert_down(block_e, n_used, hb, w_exp_down[0])

    dest_t = dest.reshape(N_TOK // COMB_ROWS, COMB_ROWS, TOP_K).transpose(0, 2, 1).reshape(N_ASSIGN)
    gf = norm_final_g.reshape(1, D_MODEL)
    y_p = _combine(dest_t, h, gates, gf, yb, 0, N_PROMPT // COMB_ROWS)
    y_s = _combine(dest_t, h, gates, gf, yb, N_PROMPT // COMB_ROWS, N_SAMPLE // COMB_ROWS)

    return (y_p.reshape(BATCH, SEQ, D_MODEL), y_s.reshape(DEC_BATCH, DEC_SEQ, D_MODEL),
            conv_p[None], hgrn_p[None], conv_s[None], hgrn_s[None])
```

```python
import functools

import jax
import jax.numpy as jnp
from jax import lax
from jax.experimental import pallas as pl
from jax.experimental.pallas import tpu as pltpu

F32 = jnp.float32
BF16 = jnp.bfloat16
I32 = jnp.int32

D_MODEL = 4096
BATCH = 4
SEQ = 2048
DEC_BATCH = 128
DEC_SEQ = 4
CONV_CH = 2048
CONV_K = 3
HG_HEADS = 16
HG_DK = 128
HG_DV = 128
HG_W = HG_HEADS * HG_DK
HG_CHUNK = 64
IN_COLS = 3 * CONV_CH + 4 * HG_W
N_GROUPS = 8
EXP_PER_GROUP = 8
N_EXPERTS = 64
TOP_K = 2
D_EXPERT = 1024
EPS = 1e-6

N_PROMPT = BATCH * SEQ
N_SAMPLE = DEC_BATCH * DEC_SEQ
N_TOK = N_PROMPT + N_SAMPLE
N_ASSIGN = N_TOK * TOP_K

LANES = 128
SUBLANES = 8
ROUTE_COLS = LANES

NORM_ROWS = 256
MM_ROWS = 512
MM_COLS = 1024
CONV_ROWS = 512
CONV_COLS = 512
HG_ROWS = 512
HG_SAMPLE_SEQS = 4
MOE_BLK = 128
MOE_NB = N_ASSIGN // MOE_BLK + N_EXPERTS
MOE_ROWS = MOE_NB * MOE_BLK
MOE_FC = 512
MOE_DC = 2048
COMB_ROWS = 128

MIB = 1 << 20


def _params(sem, vmem_mib, **kwargs):
    return pltpu.CompilerParams(dimension_semantics=sem, vmem_limit_bytes=vmem_mib * MIB, **kwargs)


def _sigmoid(x):
    return 1.0 / (1.0 + jnp.exp(-x))


def _rmsnorm_rows(x, g):
    return x * lax.rsqrt(jnp.mean(x * x, axis=-1, keepdims=True) + EPS) * g


def _norm_in_kernel(xp_ref, xs_ref, g_ref, o_ref, *, n_prompt_tiles):
    i = pl.program_id(0)

    @pl.when(i < n_prompt_tiles)
    def _():
        o_ref[...] = _rmsnorm_rows(xp_ref[...], g_ref[...]).astype(o_ref.dtype)

    @pl.when(i >= n_prompt_tiles)
    def _():
        o_ref[...] = _rmsnorm_rows(xs_ref[...], g_ref[...]).astype(o_ref.dtype)


def _norm_in(xp, xs, g):
    npt = N_PROMPT // NORM_ROWS
    nst = N_SAMPLE // NORM_ROWS
    return pl.pallas_call(
        functools.partial(_norm_in_kernel, n_prompt_tiles=npt),
        grid=(npt + nst,),
        in_specs=[
            pl.BlockSpec((NORM_ROWS, D_MODEL), lambda i: (jnp.minimum(i, npt - 1), 0)),
            pl.BlockSpec((NORM_ROWS, D_MODEL), lambda i: (jnp.maximum(i - npt, 0), 0)),
            pl.BlockSpec((1, D_MODEL), lambda i: (0, 0)),
        ],
        out_specs=pl.BlockSpec((NORM_ROWS, D_MODEL), lambda i: (i, 0)),
        out_shape=jax.ShapeDtypeStruct((N_TOK, D_MODEL), BF16),
        compiler_params=_params(("arbitrary",), 40),
        name="norm_in",
    )(xp, xs, g)


def _in_proj_kernel(x_ref, w_ref, o_ref, wb_ref):
    @pl.when(pl.program_id(1) == 0)
    def _():
        wb_ref[...] = w_ref[...].astype(BF16)

    o_ref[...] = jnp.dot(x_ref[...], wb_ref[...], preferred_element_type=F32).astype(o_ref.dtype)


def _in_proj(xn, w):
    return pl.pallas_call(
        _in_proj_kernel,
        grid=(IN_COLS // MM_COLS, N_TOK // MM_ROWS),
        in_specs=[
            pl.BlockSpec((MM_ROWS, D_MODEL), lambda n, m: (m, 0)),
            pl.BlockSpec((D_MODEL, MM_COLS), lambda n, m: (0, n)),
        ],
        out_specs=pl.BlockSpec((MM_ROWS, MM_COLS), lambda n, m: (m, n)),
        out_shape=jax.ShapeDtypeStruct((N_TOK, IN_COLS), BF16),
        scratch_shapes=[pltpu.VMEM((D_MODEL, MM_COLS), BF16)],
        compiler_params=_params(("arbitrary", "arbitrary"), 58),
        name="in_proj",
    )(xn, w)


def _conv_taps(u, um1, um2, cb, w):
    conv = um2 * w[0:1] + um1 * w[1:2] + u * w[2:3]
    return cb * conv


def _conv_prompt_kernel(cb_ref, cc_ref, ch_ref, w_ref, y_ref, nb_ref, carry_ref):
    t = pl.program_id(2)

    @pl.when(t == 0)
    def _():
        carry_ref[...] = jnp.zeros_like(carry_ref)

    u = cc_ref[...].astype(F32) * ch_ref[...].astype(F32)
    prev = carry_ref[...]
    p1 = prev[SUBLANES - 1:SUBLANES]
    p2 = prev[SUBLANES - 2:SUBLANES - 1]
    row = lax.broadcasted_iota(I32, u.shape, 0)
    um1 = jnp.where(row == 0, p1, pltpu.roll(u, 1, axis=0))
    um2 = jnp.where(row == 0, p2, jnp.where(row == 1, p1, pltpu.roll(u, 2, axis=0)))
    y_ref[...] = _conv_taps(u, um1, um2, cb_ref[...].astype(F32), w_ref[...]).astype(y_ref.dtype)
    carry_ref[...] = u[CONV_ROWS - SUBLANES:CONV_ROWS]
    nb_ref[0] = u[CONV_ROWS - (CONV_K - 1):CONV_ROWS]


def _conv_prompt(proj, conv_w):
    nt = SEQ // CONV_ROWS
    nc = CONV_CH // CONV_COLS
    return pl.pallas_call(
        _conv_prompt_kernel,
        grid=(BATCH, nc, nt),
        in_specs=[
            pl.BlockSpec((CONV_ROWS, CONV_COLS), lambda b, c, t: (b * nt + t, c)),
            pl.BlockSpec((CONV_ROWS, CONV_COLS), lambda b, c, t: (b * nt + t, nc + c)),
            pl.BlockSpec((CONV_ROWS, CONV_COLS), lambda b, c, t: (b * nt + t, 2 * nc + c)),
            pl.BlockSpec((CONV_K, CONV_COLS), lambda b, c, t: (0, c)),
        ],
        out_specs=[
            pl.BlockSpec((CONV_ROWS, CONV_COLS), lambda b, c, t: (b * nt + t, c)),
            pl.BlockSpec((1, CONV_K - 1, CONV_COLS), lambda b, c, t: (b, 0, c)),
        ],
        out_shape=[
            jax.ShapeDtypeStruct((N_PROMPT, CONV_CH), BF16),
            jax.ShapeDtypeStruct((BATCH, CONV_K - 1, CONV_CH), F32),
        ],
        scratch_shapes=[pltpu.VMEM((SUBLANES, CONV_COLS), F32)],
        compiler_params=_params(("arbitrary", "arbitrary", "arbitrary"), 32),
        name="conv_prompt",
    )(proj, proj, proj, conv_w)


def _conv_sample_kernel(cb_ref, cc_ref, ch_ref, w_ref, e1_ref, e2_ref, y_ref, u_ref):
    u =cc_ref[...].astype(F32) * ch_ref[...].astype(F32)
    tpos = lax.broadcasted_iota(I32, u.shape, 0) % DEC_SEQ
    um1 = jnp.where(tpos == 0, e1_ref[...], pltpu.roll(u, 1, axis=0))
    um2 = jnp.where(tpos <= 1, e2_ref[...], pltpu.roll(u, 2, axis=0))
    y_ref[...] = _conv_taps(u, um1, um2, cb_ref[...].astype(F32), w_ref[...]).astype(y_ref.dtype)
    u_ref[...] = u


def _conv_sample(proj, conv_w, e1, e2):
    nc = CONV_CH // CONV_COLS
    rb = N_PROMPT // N_SAMPLE
    return pl.pallas_call(
        _conv_sample_kernel,
        grid=(nc,),
        in_specs=[
            pl.BlockSpec((N_SAMPLE, CONV_COLS), lambda c: (rb, c)),
            pl.BlockSpec((N_SAMPLE, CONV_COLS), lambda c: (rb, nc + c)),
            pl.BlockSpec((N_SAMPLE, CONV_COLS), lambda c: (rb, 2 * nc + c)),
            pl.BlockSpec((CONV_K, CONV_COLS), lambda c: (0, c)),
            pl.BlockSpec((N_SAMPLE, CONV_COLS), lambda c: (0, c)),
            pl.BlockSpec((N_SAMPLE, CONV_COLS), lambda c: (0, c)),
        ],
        out_specs=[
            pl.BlockSpec((N_SAMPLE, CONV_COLS), lambda c: (0, c)),
            pl.BlockSpec((N_SAMPLE, CONV_COLS), lambda c: (0, c)),
        ],
        out_shape=[
            jax.ShapeDtypeStruct((N_SAMPLE, CONV_CH), BF16),
            jax.ShapeDtypeStruct((N_SAMPLE, CONV_CH), F32),
        ],
        compiler_params=_params(("arbitrary",), 32),
        name="conv_sample",
    )(proj, proj, proj, conv_w, e1, e2)


_NT = (((1,), (1,)), ((), ()))
_TN = (((0,), (0,)), ((), ()))


def _hgrn_factors(q, fz, lb, seq_len):
    rows, width = q.shape
    one_m_lb = 1.0 - lb
    log_f = jnp.log(lb + one_m_lb * _sigmoid(fz))
    k = one_m_lb * _sigmoid(-fz)
    row = lax.broadcasted_iota(I32, (rows, width), 0)
    tpos = row % seq_len

    b = log_f
    shift = 1
    while shift < seq_len:
        b = b + jnp.where(tpos >= shift, pltpu.roll(b, shift, axis=0), 0.0)
        shift *= 2

    n_seq = rows // seq_len
    mid = (seq_len - 1) // 2
    if seq_len % SUBLANES == 0:
        b3 = b.reshape(n_seq, seq_len, width)
        b_mid = b3[:, mid:mid + 1]
        b_end = b3[:, seq_len - 1:seq_len]

        def per_row(per_chunk):
            return jnp.broadcast_to(per_chunk, b3.shape).reshape(rows, width)

        b_ref, b_last = per_row(b_mid), per_row(b_end)
        e_ref, e_last = per_row(jnp.exp(b_mid)), per_row(jnp.exp(b_end - b_mid))
    else:
        seq = row // seq_len
        b_ref = b[mid:mid + 1]
        b_last = b[seq_len - 1:seq_len]
        for j in range(1, n_seq):
            b_ref = jnp.where(seq == j, b[j * seq_len + mid:j * seq_len + mid + 1], b_ref)
            b_last = jnp.where(seq == j, b[(j + 1) * seq_len - 1:(j + 1) * seq_len], b_last)
        e_ref, e_last = jnp.exp(b_ref), jnp.exp(b_last - b_ref)

    qe = q * jnp.exp(b - b_ref)
    ke = k * jnp.exp(b_ref - b)
    return qe, ke, qe * e_ref, ke * e_last, log_f, b_last


def _split3_bf16(x):
    hi = x.astype(BF16).astype(F32)
    rest = x - hi
    mid = rest.astype(BF16).astype(F32)
    lo = (rest - mid).astype(BF16).astype(F32)
    return hi, mid, lo


def _head_out(o, gate, norm_g):
    o_n = o * lax.rsqrt(jnp.mean(o * o, axis=-1, keepdims=True) + EPS)
    return o_n * norm_g * (gate * _sigmoid(gate))


def _hgrn_prompt_kernel(q_ref, f_ref, v_ref, g_ref, lb_ref, ng_ref, y_ref, s_out_ref):
    lb = lb_ref[...]
    ng = ng_ref[...]
    rr = lax.broadcasted_iota(I32, (HG_CHUNK, HG_CHUNK), 0)
    cc = lax.broadcasted_iota(I32, (HG_CHUNK, HG_CHUNK), 1)
    causal = cc <= rr

    def block(i, state_t):
        rows = pl.ds(pl.multiple_of(i * HG_ROWS, HG_ROWS), HG_ROWS)
        qe, ke, qb, kd, _, b_last = _hgrn_factors(q_ref[rows, :].astype(F32), f_ref[rows, :].astype(F32),
                                                  lb, HG_CHUNK)
        qe, ke, qb, kd = (a.astype(BF16) for a in (qe, ke, qb, kd))
        vb = v_ref[rows, :]
        outs = []
        for c in range(HG_ROWS // HG_CHUNK):
            cs = slice(c * HG_CHUNK, (c + 1) * HG_CHUNK)
            scores = lax.dot_general(qe[cs], ke[cs], _NT, preferred_element_type=F32)
            scores = jnp.where(causal, scores, 0.0).astype(BF16)
            outs.append(jnp.dot(scores, vb[cs], preferred_element_type=F32)
                        + lax.dot_general(qb[cs], state_t.astype(BF16), _NT, preferred_element_type=F32))
            inc_t = lax.dot_general(vb[cs], kd[cs], _TN, preferred_element_type=F32)
            state_t = jnp.exp(b_last[c * HG_CHUNK:c * HG_CHUNK + 1]) * state_t + inc_t
        o = jnp.concatenate(outs, axis=0)
        y_ref[rows, :] = _head_out(o, g_ref[rows, :].astype(F32), ng).astype(y_ref.dtype)
        return state_t

    state_t = lax.fori_loop(0, SEQ // HG_ROWS, block, jnp.zeros((HG_DV, HG_DK), F32))
    s_out_ref[0, 0] = state_t.T


def _hgrn_prompt(proj, lb, norm_g):
    cq = 3 * CONV_CH // HG_DK

    def spec(off):
        return pl.BlockSpec((SEQ, HG_DK), lambda b, h: (b, cq + off * HG_HEADS + h))

    return pl.pallas_call(
        _hgrn_prompt_kernel,
        grid=(BATCH, HG_HEADS),
        in_specs=[spec(0), spec(1), spec(2), spec(3),
                  pl.BlockSpec((1, HG_DK), lambda b, h: (0, h)),
                  pl.BlockSpec((1, HG_DV), lambda b, h: (0, h))],
        out_specs=[
            pl.BlockSpec((SEQ, HG_DV), lambda b, h: (b, h)),
            pl.BlockSpec((1, 1, HG_DK, HG_DV), lambda b, h: (b, h, 0, 0)),
        ],
        out_shape=[
            jax.ShapeDtypeStruct((N_PROMPT, HG_W), BF16),
            jax.ShapeDtypeStruct((BATCH, HG_HEADS, HG_DK, HG_DV), F32),
        ],
        compiler_params=_params(("arbitrary", "arbitrary"), 32),
        name="hgrn_prompt",
    )(proj, proj, proj, proj, lb, norm_g)


def _hgrn_sample_kernel(q_ref, f_ref, v_ref, g_ref, lb_ref, ng_ref, s_ref, y_ref, s_out_ref):
    rows = HG_SAMPLE_SEQS * DEC_SEQ
    qe, ke, qb, kd, log_f, _ = _hgrn_factors(q_ref[...].astype(F32), f_ref[...].astype(F32),
                                            lb_ref[...], DEC_SEQ)
    lf_parts = _split3_bf16(log_f)
    v = v_ref[...].astype(F32)
    gate = g_ref[...].astype(F32)
    ng = ng_ref[...]

    in_a = lax.broadcasted_iota(I32, (SUBLANES, HG_DV), 0) < DEC_SEQ
    rr = lax.broadcasted_iota(I32, (SUBLANES, SUBLANES), 0)
    cc = lax.broadcasted_iota(I32, (SUBLANES, SUBLANES), 1)
    causal = (cc <= rr) & ((rr // DEC_SEQ) == (cc // DEC_SEQ))
    zero_tile = jnp.zeros((SUBLANES, HG_DV), F32)
    sel = jnp.concatenate([in_a.astype(F32), 1.0 - in_a.astype(F32)], axis=1)
    dec_rhs = jnp.concatenate([sel, sel, sel, jnp.zeros_like(sel)], axis=0).astype(BF16)

    for h in range(HG_HEADS):
        cols = slice(h * HG_DK, (h + 1) * HG_DK)
        outs = []
        for r in range(rows // SUBLANES):
            rs = slice(r * SUBLANES, (r + 1) * SUBLANES)
            s_cat = jnp.concatenate([s_ref[2 * r, h], s_ref[2 * r + 1, h]], axis=1)
            v_t = v[rs, cols]
            scores = lax.dot_general(qe[rs, cols].astype(BF16), ke[rs, cols].astype(BF16), _NT,
                                     preferred_element_type=F32)
            scores = jnp.where(causal, scores, 0.0).astype(BF16)
            o_state = jnp.dot(qb[rs, cols].astype(BF16), s_cat.astype(BF16), preferred_element_type=F32)
            outs.append(jnp.dot(scores, v_t.astype(BF16), preferred_element_type=F32)
                        + jnp.where(in_a, o_state[:, :HG_DV], o_state[:, HG_DV:]))
            inc_rhs = jnp.concatenate([jnp.where(in_a, v_t, 0.0), jnp.where(in_a, 0.0, v_t)], axis=1)
            inc = lax.dot_general(kd[rs, cols].astype(BF16), inc_rhs.astype(BF16), _TN,
                                  preferred_element_type=F32)
            dec_lhs = jnp.concatenate([p[rs, cols] for p in lf_parts] + [zero_tile], axis=0)
            decay = jnp.exp(lax.dot_general(dec_lhs.astype(BF16), dec_rhs, _TN,
                                            preferred_element_type=F32))
            s_new = decay * s_cat + inc
            s_out_ref[2 * r, h] = s_new[:, :HG_DV]
            s_out_ref[2 * r + 1, h] = s_new[:, HG_DV:]
        o = jnp.concatenate(outs, axis=0)
        y_ref[:, cols] = _head_out(o, gate[:, cols], ng[:, cols]).astype(y_ref.dtype)


def _hgrn_sample(proj, lb, norm_g, state):
    rows = HG_SAMPLE_SEQS * DEC_SEQ
    rb = N_PROMPT // rows
    cq = 3 * CONV_CH // HG_W

    def spec(off):
        return pl.BlockSpec((rows, HG_W), lambda g: (rb + g, cq + off))

    st_spec = pl.BlockSpec((HG_SAMPLE_SEQS, HG_HEADS, HG_DK, HG_DV), lambda g: (g, 0, 0, 0))
    return pl.pallas_call(
        _hgrn_sample_kernel,
        grid=(DEC_BATCH // HG_SAMPLE_SEQS,),
        in_specs=[spec(0), spec(1), spec(2), spec(3),
                  pl.BlockSpec((1, HG_W), lambda g: (0, 0)),
                  pl.BlockSpec((1, HG_W), lambda g: (0, 0)),
                  st_spec],
        out_specs=[pl.BlockSpec((rows, HG_W), lambda g: (g, 0)), st_spec],
        out_shape=[
            jax.ShapeDtypeStruct((N_SAMPLE, HG_W), BF16),
            jax.ShapeDtypeStruct((DEC_BATCH, HG_HEADS, HG_DK, HG_DV), F32),
        ],
        compiler_params=_params(("arbitrary",), 40),
        name="hgrn_sample",
    )(proj, proj, proj, proj, lb, norm_g, state)


def _out_proj_kernel(ycp_ref, yhp_ref, xp_ref, ycs_ref, yhs_ref, xs_ref, wc_ref, wh_ref, o_ref, *,
                     n_prompt_tiles):
    m = pl.program_id(1)

    def residual_mix(yc_ref, yh_ref, x_ref):
        mix = (jnp.dot(yc_ref[...], wc_ref[...], preferred_element_type=F32)
               + jnp.dot(yh_ref[...], wh_ref[...], preferred_element_type=F32))
        o_ref[...] = x_ref[...] + mix

    @pl.when(m < n_prompt_tiles)
    def _():
        residual_mix(ycp_ref, yhp_ref, xp_ref)

    @pl.when(m >= n_prompt_tiles)
    def _():
        residual_mix(ycs_ref, yhs_ref, xs_ref)


def _out_proj(yc_p, yh_p, xp, yc_s, yh_s, xs, w):
    npt = N_PROMPT // MM_ROWS

    def prompt_map(n, m):
        return jnp.minimum(m, npt - 1)

    def sample_map(n, m):
        return jnp.maximum(m - npt, 0)

    def source_specs(row_map):
        return [
            pl.BlockSpec((MM_ROWS, CONV_CH), lambda n, m: (row_map(n, m), 0)),
            pl.BlockSpec((MM_ROWS, HG_W), lambda n, m: (row_map(n, m), 0)),
            pl.BlockSpec((MM_ROWS, MM_COLS), lambda n, m: (row_map(n, m), n)),
        ]

    return pl.pallas_call(
        functools.partial(_out_proj_kernel, n_prompt_tiles=npt),
        grid=(D_MODEL // MM_COLS, N_TOK // MM_ROWS),
        in_specs=source_specs(prompt_map) + source_specs(sample_map) + [
            pl.BlockSpec((CONV_CH, MM_COLS), lambda n, m: (0, n)),
            pl.BlockSpec((HG_W, MM_COLS), lambda n, m: (1, n)),
        ],
        out_specs=pl.BlockSpec((MM_ROWS, MM_COLS), lambda n, m: (m, n)),
        out_shape=jax.ShapeDtypeStruct((N_TOK, D_MODEL), F32),
        compiler_params=_params(("arbitrary", "arbitrary"), 56),
        name="out_proj",
    )(yc_p, yh_p, xp, yc_s, yh_s, xs, w, w)


def _split_bf16(x):
    hi = x.astype(BF16)
    return hi, (x - hi.astype(F32)).astype(BF16)


def _norm_route_kernel(h_ref, g_ref, wr_hi_ref, wr_lo_ref, br_ref, hn_ref, eidx_ref, gate_ref):
    hn = _rmsnorm_rows(h_ref[...], g_ref[...])
    hn_ref[...] = hn
    hn_hi, hn_lo = _split_bf16(hn)
    logits = (jnp.dot(hn_hi, wr_hi_ref[...], preferred_element_type=F32)
              + (jnp.dot(hn_hi, wr_lo_ref[...], preferred_element_type=F32)
                 + jnp.dot(hn_lo, wr_hi_ref[...], preferred_element_type=F32))) + br_ref[...]
    lane = lax.broadcasted_iota(I32, logits.shape, 1)
    lane_f = lane.astype(F32)
    neg = -jnp.inf

    def first_argmax(vals, vmax):
        first = jnp.min(jnp.where(vals == vmax, lane_f, float(ROUTE_COLS)), axis=-1, keepdims=True)
        return first.astype(I32)

    is_grp = lane < N_GROUPS
    lg = jnp.where(is_grp, logits, neg)
    mg = jnp.max(lg, axis=-1, keepdims=True)
    g_sel = first_argmax(lg, mg)
    p_grp = 1.0 / jnp.sum(jnp.where(is_grp, jnp.exp(logits - mg), 0.0), axis=-1, keepdims=True)

    in_grp = (lane >= N_GROUPS) & (((lane - N_GROUPS) // EXP_PER_GROUP) == g_sel)
    le = jnp.where(in_grp, logits, neg)
    m1 = jnp.max(le, axis=-1, keepdims=True)
    i1 = first_argmax(le, m1)
    le2 = jnp.where(lane == i1, neg, le)
    m2 = jnp.max(le2, axis=-1, keepdims=True)
    i2 = first_argmax(le2, m2)
    e2 = jnp.exp(m2 - m1)
    gate1 = p_grp / (1.0 + e2)
    gate2 = p_grp * e2 / (1.0 + e2)
    eidx_ref[...] = jnp.where(lane == 0, i1 - N_GROUPS, jnp.where(lane == 1, i2 - N_GROUPS, 0))
    gate_ref[...] = jnp.where(lane == 0, gate1, jnp.where(lane == 1, gate2, 0.0))


def _norm_route(h, g, w_r, b_r):
    wr_hi, wr_lo = _split_bf16(w_r)
    return pl.pallas_call(
        _norm_route_kernel,
        grid=(N_TOK // NORM_ROWS,),
        in_specs=[
            pl.BlockSpec((NORM_ROWS, D_MODEL), lambda i: (i, 0)),
            pl.BlockSpec((1, D_MODEL), lambda i: (0, 0)),
            pl.BlockSpec((D_MODEL, ROUTE_COLS), lambda i: (0, 0)),
            pl.BlockSpec((D_MODEL, ROUTE_COLS), lambda i: (0, 0)),
            pl.BlockSpec((1, ROUTE_COLS), lambda i: (0, 0)),
        ],
        out_specs=[
            pl.BlockSpec((NORM_ROWS, D_MODEL), lambda i: (i, 0)),
            pl.BlockSpec((NORM_ROWS, ROUTE_COLS), lambda i: (i, 0)),
            pl.BlockSpec((NORM_ROWS, ROUTE_COLS), lambda i: (i, 0)),
        ],
        out_shape=[
            jax.ShapeDtypeStruct((N_TOK, D_MODEL), F32),
            jax.ShapeDtypeStruct((N_TOK, ROUTE_COLS), I32),
            jax.ShapeDtypeStruct((N_TOK, ROUTE_COLS), F32),
        ],
        compiler_params=_params(("arbitrary",), 48),
        name="norm_route",
    )(h, g, wr_hi, wr_lo, b_r)


def _row_copy(src_hbm, row, buf, slot, r, sem):
    return pltpu.make_async_copy(src_hbm.at[pl.ds(row, 1)], buf.at[slot, pl.ds(r, 1)], sem.at[slot])


def _gather_issue(idx_ref, base, n_groups, src_hbm, buf, slot, sem):
    def body(g, carry):
        r0 = g * SUBLANES
        for j in range(SUBLANES):
            _row_copy(src_hbm, idx_ref[base + r0 + j], buf, slot, r0 + j, sem).start()
        return carry

    lax.fori_loop(0, n_groups, body, 0)


def _gather_wait(n_groups, src_hbm, buf, slot, sem):
    def body(g, carry):
        rows = pl.ds(pl.multiple_of(g * SUBLANES, SUBLANES), SUBLANES)
        pltpu.make_async_copy(src_hbm.at[pl.ds(0, SUBLANES)], buf.at[slot, rows], sem.at[slot]).wait()
        return carry

    lax.fori_loop(0, n_groups, body, 0)


def _dispatch_kernel(tok_ref, ng_ref, hn_hbm, o_ref, buf, sem):
    b = pl.program_id(0)
    nb = pl.num_programs(0)
    slot = b % 2

    def fetch(blk, dst_slot):
        n_groups = ng_ref[blk]
        _gather_issue(tok_ref, blk * MOE_BLK, n_groups, hn_hbm, buf, dst_slot, sem)

        def zero(g, carry):
            rows = pl.ds(pl.multiple_of(g * SUBLANES, SUBLANES), SUBLANES)
            buf[dst_slot, rows, :] = jnp.zeros((SUBLANES, D_MODEL), F32)
            return carry

        lax.fori_loop(n_groups, MOE_BLK // SUBLANES, zero, 0)

    @pl.when(b == 0)
    def _():
        fetch(0, 0)

    @pl.when(b + 1 < nb)
    def _():
        fetch(b + 1, 1 - slot)

    _gather_wait(ng_ref[b], hn_hbm, buf, slot, sem)
    o_ref[...] = buf[slot].astype(o_ref.dtype)


def _dispatch(row_tok, n_groups, hn):
    return pl.pallas_call(
        _dispatch_kernel,
        grid_spec=pltpu.PrefetchScalarGridSpec(
            num_scalar_prefetch=2,
            grid=(MOE_NB,),
            in_specs=[pl.BlockSpec(memory_space=pl.ANY)],
            out_specs=pl.BlockSpec((MOE_BLK, D_MODEL), lambda b, tok, ng: (b, 0)),
            scratch_shapes=[pltpu.VMEM((2, MOE_BLK, D_MODEL), F32),
                            pltpu.SemaphoreType.DMA((2,))],
        ),
        out_shape=jax.ShapeDtypeStruct((MOE_ROWS, D_MODEL), BF16),
        compiler_params=_params(("arbitrary",), 32, disable_bounds_checks=True),
        name="moe_dispatch",
    )(row_tok, n_groups, hn)


def _expert_up_kernel(be_ref, nbu_ref, x_ref, wg_ref, wu_ref, h_ref):
    b = pl.program_id(1)

    @pl.when(b < nbu_ref[0])
    def _():
        x = x_ref[...].astype(F32)
        g = jnp.dot(x, wg_ref[0], preferred_element_type=F32)
        u = jnp.dot(x, wu_ref[0], preferred_element_type=F32)
        h_ref[...] = (g * _sigmoid(g) * u).astype(h_ref.dtype)

    @pl.when(b >= nbu_ref[0])
    def _():
        h_ref[...] = jnp.zeros_like(h_ref)


def _expert_up(block_e, n_used, xb, w_g, w_u):
    def x_map(j, b, be, nbu):
        return (jnp.minimum(b, nbu[0] - 1), 0)

    def w_map(j, b, be, nbu):
        return (be[b], 0, j)

    return pl.pallas_call(
        _expert_up_kernel,
        grid_spec=pltpu.PrefetchScalarGridSpec(
            num_scalar_prefetch=2,
            grid=(D_EXPERT // MOE_FC, MOE_NB),
            in_specs=[
                pl.BlockSpec((MOE_BLK, D_MODEL), x_map),
                pl.BlockSpec((1, D_MODEL, MOE_FC), w_map),
                pl.BlockSpec((1, D_MODEL, MOE_FC), w_map),
            ],
            out_specs=pl.BlockSpec((MOE_BLK, MOE_FC), lambda j, b, be, nbu: (b, j)),
        ),
        out_shape=jax.ShapeDtypeStruct((MOE_ROWS, D_EXPERT), BF16),
        compiler_params=_params(("arbitrary", "arbitrary"), 56),
        name="expert_up",
    )(block_e, n_used, xb, w_g, w_u)


def _expert_down_kernel(be_ref, nbu_ref, h_ref, wd_ref, y_ref):
    b = pl.program_id(1)

    @pl.when(b < nbu_ref[0])
    def _():
        y_ref[...] = jnp.dot(h_ref[...].astype(F32), wd_ref[0], preferred_element_type=F32)

    @pl.when(b >= nbu_ref[0])
    def _():
        y_ref[...] = jnp.zeros_like(y_ref)


def _expert_down(block_e, n_used, hb, w_d):
    return pl.pallas_call(
        _expert_down_kernel,
        grid_spec=pltpu.PrefetchScalarGridSpec(
            num_scalar_prefetch=2,
            grid=(D_MODEL // MOE_DC, MOE_NB),
            in_specs=[
                pl.BlockSpec((MOE_BLK, D_EXPERT), lambda n, b, be, nbu: (jnp.minimum(b, nbu[0] - 1), 0)),
                pl.BlockSpec((1, D_EXPERT, MOE_DC), lambda n, b, be, nbu: (be[b], 0, n)),
            ],
            out_specs=pl.BlockSpec((MOE_BLK, MOE_DC), lambda n, b, be, nbu: (b, n)),
        ),
        out_shape=jax.ShapeDtypeStruct((MOE_ROWS, D_MODEL), F32),
        compiler_params=_params(("arbitrary", "arbitrary"), 48),
        name="expert_down",
    )(block_e, n_used, hb, w_d)


def _combine_kernel(dest_ref, h_ref, gate_ref, g_ref, y_hbm, o_ref, buf, sem, *, tile0):
    i = pl.program_id(0)
    n = pl.num_programs(0)
    slot = i % 2
    rows = TOP_K * COMB_ROWS
    n_groups = rows // SUBLANES
    base = (tile0 + i) * rows

    @pl.when(i == 0)
    def _():
        _gather_issue(dest_ref, base, n_groups, y_hbm, buf, 0, sem)

    @pl.when(i + 1 < n)
    def _():
        _gather_issue(dest_ref, base + rows, n_groups, y_hbm, buf, 1 - slot, sem)

    _gather_wait(n_groups, y_hbm, buf, slot, sem)
    gates = gate_ref[...]
    ff = gates[:, 0:1] * buf[slot, 0:COMB_ROWS] + gates[:, 1:2] * buf[slot, COMB_ROWS:rows]
    o_ref[...] = _rmsnorm_rows(h_ref[...] + ff, g_ref[...])


def _combine(dest, h, gates, g_final, y_buf, tile0, n_tiles):
    return pl.pallas_call(
        functools.partial(_combine_kernel, tile0=tile0),
        grid_spec=pltpu.PrefetchScalarGridSpec(
            num_scalar_prefetch=1,
            grid=(n_tiles,),
            in_specs=[
                pl.BlockSpec((COMB_ROWS, D_MODEL), lambda i, d: (tile0 + i, 0)),
                pl.BlockSpec((COMB_ROWS, ROUTE_COLS), lambda i, d: (tile0 + i, 0)),
                pl.BlockSpec((1, D_MODEL), lambda i, d: (0, 0)),
                pl.BlockSpec(memory_space=pl.ANY),
            ],
            out_specs=pl.BlockSpec((COMB_ROWS, D_MODEL), lambda i, d: (i, 0)),
            scratch_shapes=[pltpu.VMEM((2, TOP_K * COMB_ROWS, D_MODEL), F32),
                            pltpu.SemaphoreType.DMA((2,))],
        ),
        out_shape=jax.ShapeDtypeStruct((n_tiles * COMB_ROWS, D_MODEL), F32),
        compiler_params=_params(("arbitrary",), 40, disable_bounds_checks=True),
        name="moe_combine",
    )(dest, h, gates, g_final, y_buf)


def _dispatch_plan(eidx):
    flat_e = eidx.reshape(N_ASSIGN)
    onehot = (flat_e[:, None] == jnp.arange(N_EXPERTS, dtype=I32)[None, :]).astype(I32)
    csum = jnp.cumsum(onehot, axis=0)
    counts = csum[-1]
    rank = jnp.take_along_axis(csum, flat_e[:, None], axis=1)[:, 0] - 1
    padded = ((counts + MOE_BLK - 1) // MOE_BLK) * MOE_BLK
    pad_end = jnp.cumsum(padded)
    pad_start = pad_end - padded
    dest = (pad_start[flat_e] + rank).astype(I32)
    n_used = (pad_end[-1] // MOE_BLK).astype(I32)
    blk = jnp.arange(MOE_NB, dtype=I32)
    block_e = jnp.minimum(jnp.searchsorted(pad_end, blk * MOE_BLK, side='right'), N_EXPERTS - 1)
    rows_in_blk = jnp.clip(counts[block_e] - (blk * MOE_BLK - pad_start[block_e]), 0, MOE_BLK)
    rows_in_blk = jnp.where(blk < n_used, rows_in_blk, 0)
    n_groups = ((rows_in_blk + SUBLANES - 1) // SUBLANES).astype(I32)
    block_e = jnp.where(blk < n_used, block_e, block_e[n_used - 1]).astype(I32)
    row_tok = jnp.zeros((MOE_ROWS,), I32).at[dest].set(jnp.arange(N_ASSIGN, dtype=I32) // TOP_K)
    return dest, row_tok, n_groups, block_e, n_used.reshape(1)


def kernel(x_prompt, x_sample, cache_conv, state_hgrn, norm_mix_g, w_in, conv_w, lb_param, hg_norm_g,
           w_out, norm_ffn_g, w_group_router, b_group_router, w_expert_router, b_expert_router,
           w_exp_gate, w_exp_up, w_exp_down, norm_final_g):
    xp = x_prompt.reshape(N_PROMPT, D_MODEL)
    xs = x_sample.reshape(N_SAMPLE, D_MODEL)
    lb = jnp.cumsum(jax.nn.softmax(lb_param.astype(F32), axis=0), axis=0)[0].reshape(1, HG_W)

    xn = _norm_in(xp, xs, norm_mix_g[0].reshape(1, D_MODEL))
    proj = _in_proj(xn, w_in[0])

    buf = cache_conv[0]
    zeros = jnp.zeros((DEC_BATCH, DEC_SEQ - 1, CONV_CH), F32)
    e1 = jnp.concatenate([buf[:, 1:2], zeros], axis=1).reshape(N_SAMPLE, CONV_CH)
    e2 = jnp.concatenate([buf, zeros[:, :DEC_SEQ - 2]], axis=1).reshape(N_SAMPLE, CONV_CH)
    yc_p, conv_p = _conv_prompt(proj, conv_w[0])
    yc_s, u_s = _conv_sample(proj, conv_w[0], e1, e2)
    conv_s = u_s.reshape(DEC_BATCH, DEC_SEQ, CONV_CH)[:, DEC_SEQ - (CONV_K - 1):]

    ng = hg_norm_g[0].reshape(1, HG_W)
    yh_p, hgrn_p = _hgrn_prompt(proj, lb, ng)
    yh_s, hgrn_s = _hgrn_sample(proj, lb, ng, state_hgrn[0])

    h = _out_proj(yc_p, yh_p, xp, yc_s, yh_s, xs, w_out[0].astype(BF16))

    pad = jnp.zeros((D_MODEL, ROUTE_COLS - N_GROUPS - N_EXPERTS), F32)
    w_r = jnp.concatenate([w_group_router[0], w_expert_router[0], pad], axis=1)
    b_r = jnp.concatenate([b_group_router[0], b_expert_router[0], pad[0]]).reshape(1, ROUTE_COLS)
    hn, eidx, gates = _norm_route(h, norm_ffn_g[0].reshape(1, D_MODEL), w_r, b_r)
    dest, row_tok, n_groups, block_e, n_used = _dispatch_plan(eidx[:, :TOP_K])
    xb = _dispatch(row_tok, n_groups, hn)
    hb = _expert_up(block_e, n_used, xb, w_exp_gate[0], w_exp_up[0])
    yb = _expert_down(block_e, n_used, hb, w_exp_down[0])

    dest_t = dest.reshape(N_TOK // COMB_ROWS, COMB_ROWS, TOP_K).transpose(0, 2, 1).reshape(N_ASSIGN)
    gf = norm_final_g.reshape(1, D_MODEL)
    y_p = _combine(dest_t, h, gates, gf, yb, 0, N_PROMPT // COMB_ROWS)
    y_s = _combine(dest_t, h, gates, gf, yb, N_PROMPT // COMB_ROWS, N_SAMPLE // COMB_ROWS)

    return (y_p.reshape(BATCH, SEQ, D_MODEL), y_s.reshape(DEC_BATCH, DEC_SEQ, D_MODEL),
            conv_p[None], hgrn_p[None], conv_s[None], hgrn_s[None])
```

```python
import functools

import jax
import jax.numpy as jnp
from jax import lax
from jax.experimental import pallas as pl
from jax.experimental.pallas import tpu as pltpu

F32 = jnp.float32
BF16 = jnp.bfloat16
I32 = jnp.int32

D_MODEL = 4096
BATCH = 4
SEQ = 2048
DEC_BATCH = 128
DEC_SEQ = 4
CONV_CH = 2048
CONV_K = 3
HG_HEADS = 16
HG_DK = 128
HG_DV = 128
HG_W = HG_HEADS * HG_DK
HG_CHUNK = 64
IN_COLS = 3 * CONV_CH + 4 * HG_W
N_GROUPS = 8
EXP_PER_GROUP = 8
N_EXPERTS = 64
TOP_K = 2
D_EXPERT = 1024
EPS = 1e-6

N_PROMPT = BATCH * SEQ
N_SAMPLE = DEC_BATCH * DEC_SEQ
N_TOK = N_PROMPT + N_SAMPLE
N_ASSIGN = N_TOK * TOP_K

LANES = 128
SUBLANES = 8
ROUTE_COLS = LANES

NORM_ROWS = 256
MM_ROWS = 512
MM_COLS = 1024
CONV_ROWS = 512
CONV_COLS = 512
HG_ROWS = 512
HG_SAMPLE_SEQS = 4
MOE_BLK = 128
MOE_NB = N_ASSIGN // MOE_BLK + N_EXPERTS
MOE_ROWS = MOE_NB * MOE_BLK
MOE_FC = 512
MOE_DC = 2048
COMB_ROWS = 128

MIB = 1 << 20


def _params(sem, vmem_mib, **kwargs):
    return pltpu.CompilerParams(dimension_semantics=sem, vmem_limit_bytes=vmem_mib * MIB, **kwargs)


def _sigmoid(x):
    return 1.0 / (1.0 + jnp.exp(-x))


def _rmsnorm_rows(x, g):
    return x * lax.rsqrt(jnp.mean(x * x, axis=-1, keepdims=True) + EPS) * g


def _norm_in_kernel(xp_ref, xs_ref, g_ref, o_ref, *, n_prompt_tiles):
    i = pl.program_id(0)

    @pl.when(i < n_prompt_tiles)
    def _():
        o_ref[...] = _rmsnorm_rows(xp_ref[...], g_ref[...]).astype(o_ref.dtype)

    @pl.when(i >= n_prompt_tiles)
    def _():
        o_ref[...] = _rmsnorm_rows(xs_ref[...], g_ref[...]).astype(o_ref.dtype)


def _norm_in(xp, xs, g):
    npt = N_PROMPT // NORM_ROWS
    nst = N_SAMPLE // NORM_ROWS
    return pl.pallas_call(
        functools.partial(_norm_in_kernel, n_prompt_tiles=npt),
        grid=(npt + nst,),
        in_specs=[
            pl.BlockSpec((NORM_ROWS, D_MODEL), lambda i: (jnp.minimum(i, npt - 1), 0)),
            pl.BlockSpec((NORM_ROWS, D_MODEL), lambda i: (jnp.maximum(i - npt, 0), 0)),
            pl.BlockSpec((1, D_MODEL), lambda i: (0, 0)),
        ],
        out_specs=pl.BlockSpec((NORM_ROWS, D_MODEL), lambda i: (i, 0)),
        out_shape=jax.ShapeDtypeStruct((N_TOK, D_MODEL), BF16),
        compiler_params=_params(("arbitrary",), 40),
        name="norm_in",
    )(xp, xs, g)


def _in_proj_kernel(x_ref, w_ref, o_ref, wb_ref):
    @pl.when(pl.program_id(1) == 0)
    def _():
        wb_ref[...] = w_ref[...].astype(BF16)

    o_ref[...] = jnp.dot(x_ref[...], wb_ref[...], preferred_element_type=F32).astype(o_ref.dtype)


def _in_proj(xn, w):
    return pl.pallas_call(
        _in_proj_kernel,
        grid=(IN_COLS // MM_COLS, N_TOK // MM_ROWS),
        in_specs=[
            pl.BlockSpec((MM_ROWS, D_MODEL), lambda n, m: (m, 0)),
            pl.BlockSpec((D_MODEL, MM_COLS), lambda n, m: (0, n)),
        ],
        out_specs=pl.BlockSpec((MM_ROWS, MM_COLS), lambda n, m: (m, n)),
        out_shape=jax.ShapeDtypeStruct((N_TOK, IN_COLS), BF16),
        scratch_shapes=[pltpu.VMEM((D_MODEL, MM_COLS), BF16)],
        compiler_params=_params(("arbitrary", "arbitrary"), 58),
        name="in_proj",
    )(xn, w)


def _conv_taps(u, um1, um2, cb, w):
    conv = um2 * w[0:1] + um1 * w[1:2] + u * w[2:3]
    return cb * conv


def _conv_prompt_kernel(cb_ref, cc_ref, ch_ref, w_ref, y_ref, nb_ref, carry_ref):
    t = pl.program_id(2)

    @pl.when(t == 0)
    def _():
        carry_ref[...] = jnp.zeros_like(carry_ref)

    u = cc_ref[...].astype(F32) * ch_ref[...].astype(F32)
    prev = carry_ref[...]
    p1 = prev[SUBLANES - 1:SUBLANES]
    p2 = prev[SUBLANES - 2:SUBLANES - 1]
    row = lax.broadcasted_iota(I32, u.shape, 0)
    um1 = jnp.where(row == 0, p1, pltpu.roll(u, 1, axis=0))
    um2 = jnp.where(row == 0, p2, jnp.where(row == 1, p1, pltpu.roll(u, 2, axis=0)))
    y_ref[...] = _conv_taps(u, um1, um2, cb_ref[...].astype(F32), w_ref[...]).astype(y_ref.dtype)
    carry_ref[...] = u[CONV_ROWS - SUBLANES:CONV_ROWS]
    nb_ref[0] = u[CONV_ROWS - (CONV_K - 1):CONV_ROWS]


def _conv_prompt(proj, conv_w):
    nt = SEQ // CONV_ROWS
    nc = CONV_CH // CONV_COLS
    return pl.pallas_call(
        _conv_prompt_kernel,
        grid=(BATCH, nc, nt),
        in_specs=[
            pl.BlockSpec((CONV_ROWS, CONV_COLS), lambda b, c, t: (b * nt + t, c)),
            pl.BlockSpec((CONV_ROWS, CONV_COLS), lambda b, c, t: (b * nt + t, nc + c)),
            pl.BlockSpec((CONV_ROWS, CONV_COLS), lambda b, c, t: (b * nt + t, 2 * nc + c)),
            pl.BlockSpec((CONV_K, CONV_COLS), lambda b, c, t: (0, c)),
        ],
        out_specs=[
            pl.BlockSpec((CONV_ROWS, CONV_COLS), lambda b, c, t: (b * nt + t, c)),
            pl.BlockSpec((1, CONV_K - 1, CONV_COLS), lambda b, c, t: (b, 0, c)),
        ],
        out_shape=[
            jax.ShapeDtypeStruct((N_PROMPT, CONV_CH), BF16),
            jax.ShapeDtypeStruct((BATCH, CONV_K - 1, CONV_CH), F32),
        ],
        scratch_shapes=[pltpu.VMEM((SUBLANES, CONV_COLS), F32)],
        compiler_params=_params(("arbitrary", "arbitrary", "arbitrary"), 32),
        name="conv_prompt",
    )(proj, proj, proj, conv_w)


def _conv_sample_kernel(cb_ref, cc_ref, ch_ref, w_ref, e1_ref, e2_ref, y_ref, u_ref):
    u =cc_ref[...].astype(F32) * ch_ref[...].astype(F32)
    tpos = lax.broadcasted_iota(I32, u.shape, 0) % DEC_SEQ
    um1 = jnp.where(tpos == 0, e1_ref[...], pltpu.roll(u, 1, axis=0))
    um2 = jnp.where(tpos <= 1, e2_ref[...], pltpu.roll(u, 2, axis=0))
    y_ref[...] = _conv_taps(u, um1, um2, cb_ref[...].astype(F32), w_ref[...]).astype(y_ref.dtype)
    u_ref[...] = u


def _conv_sample(proj, conv_w, e1, e2):
    nc = CONV_CH // CONV_COLS
    rb = N_PROMPT // N_SAMPLE
    return pl.pallas_call(
        _conv_sample_kernel,
        grid=(nc,),
        in_specs=[
            pl.BlockSpec((N_SAMPLE, CONV_COLS), lambda c: (rb, c)),
            pl.BlockSpec((N_SAMPLE, CONV_COLS), lambda c: (rb, nc + c)),
            pl.BlockSpec((N_SAMPLE, CONV_COLS), lambda c: (rb, 2 * nc + c)),
            pl.BlockSpec((CONV_K, CONV_COLS), lambda c: (0, c)),
            pl.BlockSpec((N_SAMPLE, CONV_COLS), lambda c: (0, c)),
            pl.BlockSpec((N_SAMPLE, CONV_COLS), lambda c: (0, c)),
        ],
        out_specs=[
            pl.BlockSpec((N_SAMPLE, CONV_COLS), lambda c: (0, c)),
            pl.BlockSpec((N_SAMPLE, CONV_COLS), lambda c: (0, c)),
        ],
        out_shape=[
            jax.ShapeDtypeStruct((N_SAMPLE, CONV_CH), BF16),
            jax.ShapeDtypeStruct((N_SAMPLE, CONV_CH), F32),
        ],
        compiler_params=_params(("arbitrary",), 32),
        name="conv_sample",
    )(proj, proj, proj, conv_w, e1, e2)


_NT = (((1,), (1,)), ((), ()))
_TN = (((0,), (0,)), ((), ()))


def _hgrn_factors(q, fz, lb, seq_len):
    rows, width = q.shape
    one_m_lb = 1.0 - lb
    log_f = jnp.log(lb + one_m_lb * _sigmoid(fz))
    k = one_m_lb * _sigmoid(-fz)
    row = lax.broadcasted_iota(I32, (rows, width), 0)
    tpos = row % seq_len

    b = log_f
    shift = 1
    while shift < seq_len:
        b = b + jnp.where(tpos >= shift, pltpu.roll(b, shift, axis=0), 0.0)
        shift *= 2

    n_seq = rows // seq_len
    mid = (seq_len - 1) // 2
    if seq_len % SUBLANES == 0:
        b3 = b.reshape(n_seq, seq_len, width)
        b_mid = b3[:, mid:mid + 1]
        b_end = b3[:, seq_len - 1:seq_len]

        def per_row(per_chunk):
            return jnp.broadcast_to(per_chunk, b3.shape).reshape(rows, width)

        b_ref, b_last = per_row(b_mid), per_row(b_end)
        e_ref, e_last = per_row(jnp.exp(b_mid)), per_row(jnp.exp(b_end - b_mid))
    else:
        seq = row // seq_len
        b_ref = b[mid:mid + 1]
        b_last = b[seq_len - 1:seq_len]
        for j in range(1, n_seq):
            b_ref = jnp.where(seq == j, b[j * seq_len + mid:j * seq_len + mid + 1], b_ref)
            b_last = jnp.where(seq == j, b[(j + 1) * seq_len - 1:(j + 1) * seq_len], b_last)
        e_ref, e_last = jnp.exp(b_ref), jnp.exp(b_last - b_ref)

    qe = q * jnp.exp(b - b_ref)
    ke = k * jnp.exp(b_ref - b)
    return qe, ke, qe * e_ref, ke * e_last, log_f, b_last


def _split3_bf16(x):
    hi = x.astype(BF16).astype(F32)
    rest = x - hi
    mid = rest.astype(BF16).astype(F32)
    lo = (rest - mid).astype(BF16).astype(F32)
    return hi, mid, lo


def _head_out(o, gate, norm_g):
    o_n = o * lax.rsqrt(jnp.mean(o * o, axis=-1, keepdims=True) + EPS)
    return o_n * norm_g * (gate * _sigmoid(gate))


def _hgrn_prompt_kernel(q_ref, f_ref, v_ref, g_ref, lb_ref, ng_ref, y_ref, s_out_ref):
    lb = lb_ref[...]
    ng = ng_ref[...]
    rr = lax.broadcasted_iota(I32, (HG_CHUNK, HG_CHUNK), 0)
    cc = lax.broadcasted_iota(I32, (HG_CHUNK, HG_CHUNK), 1)
    causal = cc <= rr

    def block(i, state_t):
        rows = pl.ds(pl.multiple_of(i * HG_ROWS, HG_ROWS), HG_ROWS)
        qe, ke, qb, kd, _, b_last = _hgrn_factors(q_ref[rows, :].astype(F32), f_ref[rows, :].astype(F32),
                                                  lb, HG_CHUNK)
        qe, ke, qb, kd = (a.astype(BF16) for a in (qe, ke, qb, kd))
        vb = v_ref[rows, :]
        outs = []
        for c in range(HG_ROWS // HG_CHUNK):
            cs = slice(c * HG_CHUNK, (c + 1) * HG_CHUNK)
            scores = lax.dot_general(qe[cs], ke[cs], _NT, preferred_element_type=F32)
            scores = jnp.where(causal, scores, 0.0).astype(BF16)
            outs.append(jnp.dot(scores, vb[cs], preferred_element_type=F32)
                        + lax.dot_general(qb[cs], state_t.astype(BF16), _NT, preferred_element_type=F32))
            inc_t = lax.dot_general(vb[cs], kd[cs], _TN, preferred_element_type=F32)
            state_t = jnp.exp(b_last[c * HG_CHUNK:c * HG_CHUNK + 1]) * state_t + inc_t
        o = jnp.concatenate(outs, axis=0)
        y_ref[rows, :] = _head_out(o, g_ref[rows, :].astype(F32), ng).astype(y_ref.dtype)
        return state_t

    state_t = lax.fori_loop(0, SEQ // HG_ROWS, block, jnp.zeros((HG_DV, HG_DK), F32))
    s_out_ref[0, 0] = state_t.T


def _hgrn_prompt(proj, lb, norm_g):
    cq = 3 * CONV_CH // HG_DK

    def spec(off):
        return pl.BlockSpec((SEQ, HG_DK), lambda b, h: (b, cq + off * HG_HEADS + h))

    return pl.pallas_call(
        _hgrn_prompt_kernel,
        grid=(BATCH, HG_HEADS),
        in_specs=[spec(0), spec(1), spec(2), spec(3),
                  pl.BlockSpec((1, HG_DK), lambda b, h: (0, h)),
                  pl.BlockSpec((1, HG_DV), lambda b, h: (0, h))],
        out_specs=[
            pl.BlockSpec((SEQ, HG_DV), lambda b, h: (b, h)),
            pl.BlockSpec((1, 1, HG_DK, HG_DV), lambda b, h: (b, h, 0, 0)),
        ],
        out_shape=[
            jax.ShapeDtypeStruct((N_PROMPT, HG_W), BF16),
            jax.ShapeDtypeStruct((BATCH, HG_HEADS, HG_DK, HG_DV), F32),
        ],
        compiler_params=_params(("arbitrary", "arbitrary"), 32),
        name="hgrn_prompt",
    )(proj, proj, proj, proj, lb, norm_g)


def _hgrn_sample_kernel(q_ref, f_ref, v_ref, g_ref, lb_ref, ng_ref, s_ref, y_ref, s_out_ref):
    rows = HG_SAMPLE_SEQS * DEC_SEQ
    qe, ke, qb, kd, log_f, _ = _hgrn_factors(q_ref[...].astype(F32), f_ref[...].astype(F32),
                                            lb_ref[...], DEC_SEQ)
    lf_parts = _split3_bf16(log_f)
    v = v_ref[...].astype(F32)
    gate = g_ref[...].astype(F32)
    ng = ng_ref[...]

    in_a = lax.broadcasted_iota(I32, (SUBLANES, HG_DV), 0) < DEC_SEQ
    rr = lax.broadcasted_iota(I32, (SUBLANES, SUBLANES), 0)
    cc = lax.broadcasted_iota(I32, (SUBLANES, SUBLANES), 1)
    causal = (cc <= rr) & ((rr // DEC_SEQ) == (cc // DEC_SEQ))
    zero_tile = jnp.zeros((SUBLANES, HG_DV), F32)
    sel = jnp.concatenate([in_a.astype(F32), 1.0 - in_a.astype(F32)], axis=1)
    dec_rhs = jnp.concatenate([sel, sel, sel, jnp.zeros_like(sel)], axis=0).astype(BF16)

    for h in range(HG_HEADS):
        cols = slice(h * HG_DK, (h + 1) * HG_DK)
        outs = []
        for r in range(rows // SUBLANES):
            rs = slice(r * SUBLANES, (r + 1) * SUBLANES)
            s_cat = jnp.concatenate([s_ref[2 * r, h], s_ref[2 * r + 1, h]], axis=1)
            v_t = v[rs, cols]
            scores = lax.dot_general(qe[rs, cols].astype(BF16), ke[rs, cols].astype(BF16), _NT,
                                     preferred_element_type=F32)
            scores = jnp.where(causal, scores, 0.0).astype(BF16)
            o_state = jnp.dot(qb[rs, cols].astype(BF16), s_cat.astype(BF16), preferred_element_type=F32)
            outs.append(jnp.dot(scores, v_t.astype(BF16), preferred_element_type=F32)
                        + jnp.where(in_a, o_state[:, :HG_DV], o_state[:, HG_DV:]))
            inc_rhs = jnp.concatenate([jnp.where(in_a, v_t, 0.0), jnp.where(in_a, 0.0, v_t)], axis=1)
            inc = lax.dot_general(kd[rs, cols].astype(BF16), inc_rhs.astype(BF16), _TN,
                                  preferred_element_type=F32)
            dec_lhs = jnp.concatenate([p[rs, cols] for p in lf_parts] + [zero_tile], axis=0)
            decay = jnp.exp(lax.dot_general(dec_lhs.astype(BF16), dec_rhs, _TN,
                                            preferred_element_type=F32))
            s_new = decay * s_cat + inc
            s_out_ref[2 * r, h] = s_new[:, :HG_DV]
            s_out_ref[2 * r + 1, h] = s_new[:, HG_DV:]
        o = jnp.concatenate(outs, axis=0)
        y_ref[:, cols] = _head_out(o, gate[:, cols], ng[:, cols]).astype(y_ref.dtype)


def _hgrn_sample(proj, lb, norm_g, state):
    rows = HG_SAMPLE_SEQS * DEC_SEQ
    rb = N_PROMPT // rows
    cq = 3 * CONV_CH // HG_W

    def spec(off):
        return pl.BlockSpec((rows, HG_W), lambda g: (rb + g, cq + off))

    st_spec = pl.BlockSpec((HG_SAMPLE_SEQS, HG_HEADS, HG_DK, HG_DV), lambda g: (g, 0, 0, 0))
    return pl.pallas_call(
        _hgrn_sample_kernel,
        grid=(DEC_BATCH // HG_SAMPLE_SEQS,),
        in_specs=[spec(0), spec(1), spec(2), spec(3),
                  pl.BlockSpec((1, HG_W), lambda g: (0, 0)),
                  pl.BlockSpec((1, HG_W), lambda g: (0, 0)),
                  st_spec],
        out_specs=[pl.BlockSpec((rows, HG_W), lambda g: (g, 0)), st_spec],
        out_shape=[
            jax.ShapeDtypeStruct((N_SAMPLE, HG_W), BF16),
            jax.ShapeDtypeStruct((DEC_BATCH, HG_HEADS, HG_DK, HG_DV), F32),
        ],
        compiler_params=_params(("arbitrary",), 40),
        name="hgrn_sample",
    )(proj, proj, proj, proj, lb, norm_g, state)


def _out_proj_kernel(ycp_ref, yhp_ref, xp_ref, ycs_ref, yhs_ref, xs_ref, wc_ref, wh_ref, o_ref, *,
                     n_prompt_tiles):
    m = pl.program_id(1)

    def residual_mix(yc_ref, yh_ref, x_ref):
        mix = (jnp.dot(yc_ref[...], wc_ref[...], preferred_element_type=F32)
               + jnp.dot(yh_ref[...], wh_ref[...], preferred_element_type=F32))
        o_ref[...] = x_ref[...] + mix

    @pl.when(m < n_prompt_tiles)
    def _():
        residual_mix(ycp_ref, yhp_ref, xp_ref)

    @pl.when(m >= n_prompt_tiles)
    def _():
        residual_mix(ycs_ref, yhs_ref, xs_ref)


def _out_proj(yc_p, yh_p, xp, yc_s, yh_s, xs, w):
    npt = N_PROMPT // MM_ROWS

    def prompt_map(n, m):
        return jnp.minimum(m, npt - 1)

    def sample_map(n, m):
        return jnp.maximum(m - npt, 0)

    def source_specs(row_map):
        return [
            pl.BlockSpec((MM_ROWS, CONV_CH), lambda n, m: (row_map(n, m), 0)),
            pl.BlockSpec((MM_ROWS, HG_W), lambda n, m: (row_map(n, m), 0)),
            pl.BlockSpec((MM_ROWS, MM_COLS), lambda n, m: (row_map(n, m), n)),
        ]

    return pl.pallas_call(
        functools.partial(_out_proj_kernel, n_prompt_tiles=npt),
        grid=(D_MODEL // MM_COLS, N_TOK // MM_ROWS),
        in_specs=source_specs(prompt_map) + source_specs(sample_map) + [
            pl.BlockSpec((CONV_CH, MM_COLS), lambda n, m: (0, n)),
            pl.BlockSpec((HG_W, MM_COLS), lambda n, m: (1, n)),
        ],
        out_specs=pl.BlockSpec((MM_ROWS, MM_COLS), lambda n, m: (m, n)),
        out_shape=jax.ShapeDtypeStruct((N_TOK, D_MODEL), F32),
        compiler_params=_params(("arbitrary", "arbitrary"), 56),
        name="out_proj",
    )(yc_p, yh_p, xp, yc_s, yh_s, xs, w, w)


def _split_bf16(x):
    hi = x.astype(BF16)
    return hi, (x - hi.astype(F32)).astype(BF16)


def _norm_route_kernel(h_ref, g_ref, wr_hi_ref, wr_lo_ref, br_ref, hn_ref, eidx_ref, gate_ref):
    hn = _rmsnorm_rows(h_ref[...], g_ref[...])
    hn_ref[...] = hn
    hn_hi, hn_lo = _split_bf16(hn)
    logits = (jnp.dot(hn_hi, wr_hi_ref[...], preferred_element_type=F32)
              + (jnp.dot(hn_hi, wr_lo_ref[...], preferred_element_type=F32)
                 + jnp.dot(hn_lo, wr_hi_ref[...], preferred_element_type=F32))) + br_ref[...]
    lane = lax.broadcasted_iota(I32, logits.shape, 1)
    lane_f = lane.astype(F32)
    neg = -jnp.inf

    def first_argmax(vals, vmax):
        first = jnp.min(jnp.where(vals == vmax, lane_f, float(ROUTE_COLS)), axis=-1, keepdims=True)
        return first.astype(I32)

    is_grp = lane < N_GROUPS
    lg = jnp.where(is_grp, logits, neg)
    mg = jnp.max(lg, axis=-1, keepdims=True)
    g_sel = first_argmax(lg, mg)
    p_grp = 1.0 / jnp.sum(jnp.where(is_grp, jnp.exp(logits - mg), 0.0), axis=-1, keepdims=True)

    in_grp = (lane >= N_GROUPS) & (((lane - N_GROUPS) // EXP_PER_GROUP) == g_sel)
    le = jnp.where(in_grp, logits, neg)
    m1 = jnp.max(le, axis=-1, keepdims=True)
    i1 = first_argmax(le, m1)
    le2 = jnp.where(lane == i1, neg, le)
    m2 = jnp.max(le2, axis=-1, keepdims=True)
    i2 = first_argmax(le2, m2)
    e2 = jnp.exp(m2 - m1)
    gate1 = p_grp / (1.0 + e2)
    gate2 = p_grp * e2 / (1.0 + e2)
    eidx_ref[...] = jnp.where(lane == 0, i1 - N_GROUPS, jnp.where(lane == 1, i2 - N_GROUPS, 0))
    gate_ref[...] = jnp.where(lane == 0, gate1, jnp.where(lane == 1, gate2, 0.0))


def _norm_route(h, g, w_r, b_r):
    wr_hi, wr_lo = _split_bf16(w_r)
    return pl.pallas_call(
        _norm_route_kernel,
        grid=(N_TOK // NORM_ROWS,),
        in_specs=[
            pl.BlockSpec((NORM_ROWS, D_MODEL), lambda i: (i, 0)),
            pl.BlockSpec((1, D_MODEL), lambda i: (0, 0)),
            pl.BlockSpec((D_MODEL, ROUTE_COLS), lambda i: (0, 0)),
            pl.BlockSpec((D_MODEL, ROUTE_COLS), lambda i: (0, 0)),
            pl.BlockSpec((1, ROUTE_COLS), lambda i: (0, 0)),
        ],
        out_specs=[
            pl.BlockSpec((NORM_ROWS, D_MODEL), lambda i: (i, 0)),
            pl.BlockSpec((NORM_ROWS, ROUTE_COLS), lambda i: (i, 0)),
            pl.BlockSpec((NORM_ROWS, ROUTE_COLS), lambda i: (i, 0)),
        ],
        out_shape=[
            jax.ShapeDtypeStruct((N_TOK, D_MODEL), F32),
            jax.ShapeDtypeStruct((N_TOK, ROUTE_COLS), I32),
            jax.ShapeDtypeStruct((N_TOK, ROUTE_COLS), F32),
        ],
        compiler_params=_params(("arbitrary",), 48),
        name="norm_route",
    )(h, g, wr_hi, wr_lo, b_r)


def _row_copy(src_hbm, row, buf, slot, r, sem):
    return pltpu.make_async_copy(src_hbm.at[pl.ds(row, 1)], buf.at[slot, pl.ds(r, 1)], sem.at[slot])


def _gather_issue(idx_ref, base, n_groups, src_hbm, buf, slot, sem):
    def body(g, carry):
        r0 = g * SUBLANES
        for j in range(SUBLANES):
            _row_copy(src_hbm, idx_ref[base + r0 + j], buf, slot, r0 + j, sem).start()
        return carry

    lax.fori_loop(0, n_groups, body, 0)


def _gather_wait(n_groups, src_hbm, buf, slot, sem):
    def body(g, carry):
        rows = pl.ds(pl.multiple_of(g * SUBLANES, SUBLANES), SUBLANES)
        pltpu.make_async_copy(src_hbm.at[pl.ds(0, SUBLANES)], buf.at[slot, rows], sem.at[slot]).wait()
        return carry

    lax.fori_loop(0, n_groups, body, 0)


def _dispatch_kernel(tok_ref, ng_ref, hn_hbm, o_ref, buf, sem):
    b = pl.program_id(0)
    nb = pl.num_programs(0)
    slot = b % 2

    def fetch(blk, dst_slot):
        n_groups = ng_ref[blk]
        _gather_issue(tok_ref, blk * MOE_BLK, n_groups, hn_hbm, buf, dst_slot, sem)

        def zero(g, carry):
            rows = pl.ds(pl.multiple_of(g * SUBLANES, SUBLANES), SUBLANES)
            buf[dst_slot, rows, :] = jnp.zeros((SUBLANES, D_MODEL), F32)
            return carry

        lax.fori_loop(n_groups, MOE_BLK // SUBLANES, zero, 0)

    @pl.when(b == 0)
    def _():
        fetch(0, 0)

    @pl.when(b + 1 < nb)
    def _():
        fetch(b + 1, 1 - slot)

    _gather_wait(ng_ref[b], hn_hbm, buf, slot, sem)
    o_ref[...] = buf[slot].astype(o_ref.dtype)


def _dispatch(row_tok, n_groups, hn):
    return pl.pallas_call(
        _dispatch_kernel,
        grid_spec=pltpu.PrefetchScalarGridSpec(
            num_scalar_prefetch=2,
            grid=(MOE_NB,),
            in_specs=[pl.BlockSpec(memory_space=pl.ANY)],
            out_specs=pl.BlockSpec((MOE_BLK, D_MODEL), lambda b, tok, ng: (b, 0)),
            scratch_shapes=[pltpu.VMEM((2, MOE_BLK, D_MODEL), F32),
                            pltpu.SemaphoreType.DMA((2,))],
        ),
        out_shape=jax.ShapeDtypeStruct((MOE_ROWS, D_MODEL), BF16),
        compiler_params=_params(("arbitrary",), 32, disable_bounds_checks=True),
        name="moe_dispatch",
    )(row_tok, n_groups, hn)


def _stream_expert_blocks(blk0_ref, nblk_ref, n_used, in_copy, out_copy, out_buf, compute):
    e = pl.program_id(1)

    @pl.when(e == 0)
    def _():
        in_copy(0, 0).start()

    def body(g, carry):
        slot = g % 2

        @pl.when(g + 1 < n_used)
        def _():
            in_copy(g + 1, 1 - slot).start()

        in_copy(g, slot).wait()

        @pl.when(g >= 2)
        def _():
            out_copy(g - 2, slot).wait()

        compute(slot)
        out_copy(g, slot).start()
        return carry

    blk0 = blk0_ref[e]
    lax.fori_loop(blk0, blk0 + nblk_ref[e], body, 0)

    @pl.when(e == pl.num_programs(1) - 1)
    def _():
        @pl.when(n_used >= 2)
        def _():
            out_copy(n_used - 2, n_used % 2).wait()

        out_copy(n_used - 1, (n_used - 1) % 2).wait()
        out_buf[0] = jnp.zeros(out_buf.shape[1:], out_buf.dtype)

        def start_zero(g, carry):
            out_copy(g, 0).start()
            return carry

        def wait_zero(g, carry):
            out_copy(g, 0).wait()
            return carry

        lax.fori_loop(n_used, MOE_NB, start_zero, 0)
        lax.fori_loop(n_used, MOE_NB, wait_zero, 0)


def _block_rows(g):
    return pl.ds(pl.multiple_of(g * MOE_BLK, MOE_BLK), MOE_BLK)


def _expert_up_kernel(blk0_ref, nblk_ref, nbu_ref, wg_ref, wu_ref, x_hbm, h_hbm, xbuf, hbuf, sem_x, sem_h):
    cols = pl.ds(pl.multiple_of(pl.program_id(0) * MOE_FC, MOE_FC), MOE_FC)

    def x_copy(g, slot):
        return pltpu.make_async_copy(x_hbm.at[_block_rows(g)], xbuf.at[slot], sem_x.at[slot])

    def h_copy(g, slot):
        return pltpu.make_async_copy(hbuf.at[slot], h_hbm.at[_block_rows(g), cols], sem_h.at[slot])

    def compute(slot):
        x = xbuf[slot].astype(F32)
        g = jnp.dot(x, wg_ref[0], preferred_element_type=F32)
        u = jnp.dot(x, wu_ref[0], preferred_element_type=F32)
        hbuf[slot] = (g * _sigmoid(g) * u).astype(hbuf.dtype)

    _stream_expert_blocks(blk0_ref, nblk_ref, nbu_ref[0], x_copy, h_copy, hbuf, compute)


def _expert_up(blk0, nblk, n_used, xb, w_g, w_u):
    def w_map(j, e, *_):
        return (e, 0, j)

    return pl.pallas_call(
        _expert_up_kernel,
        grid_spec=pltpu.PrefetchScalarGridSpec(
            num_scalar_prefetch=3,
            grid=(D_EXPERT // MOE_FC, N_EXPERTS),
            in_specs=[
                pl.BlockSpec((1, D_MODEL, MOE_FC), w_map),
                pl.BlockSpec((1, D_MODEL, MOE_FC), w_map),
                pl.BlockSpec(memory_space=pl.ANY),
            ],
            out_specs=pl.BlockSpec(memory_space=pl.ANY),
            scratch_shapes=[pltpu.VMEM((2, MOE_BLK, D_MODEL), BF16),
                            pltpu.VMEM((2, MOE_BLK, MOE_FC), BF16),
                            pltpu.SemaphoreType.DMA((2,)),
                            pltpu.SemaphoreType.DMA((2,))],
        ),
        out_shape=jax.ShapeDtypeStruct((MOE_ROWS, D_EXPERT), BF16),
        compiler_params=_params(("arbitrary", "arbitrary"), 56),
        name="expert_up",
    )(blk0, nblk, n_used, w_g, w_u, xb)


def _expert_down_kernel(blk0_ref, nblk_ref, nbu_ref, wd_ref, h_hbm, y_hbm, hbuf, ybuf, sem_h, sem_y):
    cols = pl.ds(pl.multiple_of(pl.program_id(0) * MOE_DC, MOE_DC), MOE_DC)

    def h_copy(g, slot):
        return pltpu.make_async_copy(h_hbm.at[_block_rows(g)], hbuf.at[slot], sem_h.at[slot])

    def y_copy(g, slot):
        return pltpu.make_async_copy(ybuf.at[slot], y_hbm.at[_block_rows(g), cols], sem_y.at[slot])

    def compute(slot):
        ybuf[slot] = jnp.dot(hbuf[slot].astype(F32), wd_ref[0], preferred_element_type=F32)

    _stream_expert_blocks(blk0_ref, nblk_ref, nbu_ref[0], h_copy, y_copy, ybuf, compute)


def _expert_down(blk0, nblk, n_used, hb, w_d):
    return pl.pallas_call(
        _expert_down_kernel,
        grid_spec=pltpu.PrefetchScalarGridSpec(
            num_scalar_prefetch=3,
            grid=(D_MODEL // MOE_DC, N_EXPERTS),
            in_specs=[
                pl.BlockSpec((1, D_EXPERT, MOE_DC), lambda n, e, *_: (e, 0, n)),
                pl.BlockSpec(memory_space=pl.ANY),
            ],
            out_specs=pl.BlockSpec(memory_space=pl.ANY),
            scratch_shapes=[pltpu.VMEM((2, MOE_BLK, D_EXPERT), BF16),
                            pltpu.VMEM((2, MOE_BLK, MOE_DC), F32),
                            pltpu.SemaphoreType.DMA((2,)),
                            pltpu.SemaphoreType.DMA((2,))],
        ),
        out_shape=jax.ShapeDtypeStruct((MOE_ROWS, D_MODEL), F32),
        compiler_params=_params(("arbitrary", "arbitrary"), 48),
        name="expert_down",
    )(blk0, nblk, n_used, w_d, hb)


def _combine_kernel(dest_ref, h_ref, gate_ref, g_ref, y_hbm, o_ref, buf, sem, *, tile0):
    i = pl.program_id(0)
    n = pl.num_programs(0)
    slot = i % 2
    rows = TOP_K * COMB_ROWS
    n_groups = rows // SUBLANES
    base = (tile0 + i) * rows

    @pl.when(i == 0)
    def _():
        _gather_issue(dest_ref, base, n_groups, y_hbm, buf, 0, sem)

    @pl.when(i + 1 < n)
    def _():
        _gather_issue(dest_ref, base + rows, n_groups, y_hbm, buf, 1 - slot, sem)

    _gather_wait(n_groups, y_hbm, buf, slot, sem)
    gates = gate_ref[...]
    ff = gates[:, 0:1] * buf[slot, 0:COMB_ROWS] + gates[:, 1:2] * buf[slot, COMB_ROWS:rows]
    o_ref[...] = _rmsnorm_rows(h_ref[...] + ff, g_ref[...])


def _combine(dest, h, gates, g_final, y_buf, tile0, n_tiles):
    return pl.pallas_call(
        functools.partial(_combine_kernel, tile0=tile0),
        grid_spec=pltpu.PrefetchScalarGridSpec(
            num_scalar_prefetch=1,
            grid=(n_tiles,),
            in_specs=[
                pl.BlockSpec((COMB_ROWS, D_MODEL), lambda i, d: (tile0 + i, 0)),
                pl.BlockSpec((COMB_ROWS, ROUTE_COLS), lambda i, d: (tile0 + i, 0)),
                pl.BlockSpec((1, D_MODEL), lambda i, d: (0, 0)),
                pl.BlockSpec(memory_space=pl.ANY),
            ],
            out_specs=pl.BlockSpec((COMB_ROWS, D_MODEL), lambda i, d: (i, 0)),
            scratch_shapes=[pltpu.VMEM((2, TOP_K * COMB_ROWS, D_MODEL), F32),
                            pltpu.SemaphoreType.DMA((2,))],
        ),
        out_shape=jax.ShapeDtypeStruct((n_tiles * COMB_ROWS, D_MODEL), F32),
        compiler_params=_params(("arbitrary",), 40, disable_bounds_checks=True),
        name="moe_combine",
    )(dest, h, gates, g_final, y_buf)


def _dispatch_plan(eidx):
    flat_e = eidx.reshape(N_ASSIGN)
    onehot = (flat_e[:, None] == jnp.arange(N_EXPERTS, dtype=I32)[None, :]).astype(I32)
    csum = jnp.cumsum(onehot, axis=0)
    counts = csum[-1]
    rank = jnp.take_along_axis(csum, flat_e[:, None], axis=1)[:, 0] - 1
    padded = ((counts + MOE_BLK - 1) // MOE_BLK) * MOE_BLK
    pad_end = jnp.cumsum(padded)
    pad_start = pad_end - padded
    dest = (pad_start[flat_e] + rank).astype(I32)
    n_used = (pad_end[-1] // MOE_BLK).astype(I32)
    blk = jnp.arange(MOE_NB, dtype=I32)
    block_e = jnp.minimum(jnp.searchsorted(pad_end, blk * MOE_BLK, side='right'), N_EXPERTS - 1)
    rows_in_blk = jnp.clip(counts[block_e] - (blk * MOE_BLK - pad_start[block_e]), 0, MOE_BLK)
    rows_in_blk = jnp.where(blk < n_used, rows_in_blk, 0)
    n_groups = ((rows_in_blk + SUBLANES - 1) // SUBLANES).astype(I32)
    row_tok = jnp.zeros((MOE_ROWS,), I32).at[dest].set(jnp.arange(N_ASSIGN, dtype=I32) // TOP_K)
    blk0 = (pad_start // MOE_BLK).astype(I32)
    nblk = (padded // MOE_BLK).astype(I32)
    return dest, row_tok, n_groups, blk0, nblk, n_used.reshape(1)


def kernel(x_prompt, x_sample, cache_conv, state_hgrn, norm_mix_g, w_in, conv_w, lb_param, hg_norm_g,
           w_out, norm_ffn_g, w_group_router, b_group_router, w_expert_router, b_expert_router,
           w_exp_gate, w_exp_up, w_exp_down, norm_final_g):
    xp = x_prompt.reshape(N_PROMPT, D_MODEL)
    xs = x_sample.reshape(N_SAMPLE, D_MODEL)
    lb = jnp.cumsum(jax.nn.softmax(lb_param.astype(F32), axis=0), axis=0)[0].reshape(1, HG_W)

    xn = _norm_in(xp, xs, norm_mix_g[0].reshape(1, D_MODEL))
    proj = _in_proj(xn, w_in[0])

    buf = cache_conv[0]
    zeros = jnp.zeros((DEC_BATCH, DEC_SEQ - 1, CONV_CH), F32)
    e1 = jnp.concatenate([buf[:, 1:2], zeros], axis=1).reshape(N_SAMPLE, CONV_CH)
    e2 = jnp.concatenate([buf, zeros[:, :DEC_SEQ - 2]], axis=1).reshape(N_SAMPLE, CONV_CH)
    yc_p, conv_p = _conv_prompt(proj, conv_w[0])
    yc_s, u_s = _conv_sample(proj, conv_w[0], e1, e2)
    conv_s = u_s.reshape(DEC_BATCH, DEC_SEQ, CONV_CH)[:, DEC_SEQ - (CONV_K - 1):]

    ng = hg_norm_g[0].reshape(1, HG_W)
    yh_p, hgrn_p = _hgrn_prompt(proj, lb, ng)
    yh_s, hgrn_s = _hgrn_sample(proj, lb, ng, state_hgrn[0])

    h = _out_proj(yc_p, yh_p, xp, yc_s, yh_s, xs, w_out[0].astype(BF16))

    pad = jnp.zeros((D_MODEL, ROUTE_COLS - N_GROUPS - N_EXPERTS), F32)
    w_r = jnp.concatenate([w_group_router[0], w_expert_router[0], pad], axis=1)
    b_r = jnp.concatenate([b_group_router[0], b_expert_router[0], pad[0]]).reshape(1, ROUTE_COLS)
    hn, eidx, gates = _norm_route(h, norm_ffn_g[0].reshape(1, D_MODEL), w_r, b_r)
    dest, row_tok, n_groups, blk0, nblk, n_used = _dispatch_plan(eidx[:, :TOP_K])
    xb = _dispatch(row_tok, n_groups, hn)
    hb = _expert_up(blk0, nblk, n_used, xb, w_exp_gate[0], w_exp_up[0])
    yb = _expert_down(blk0, nblk, n_used, hb, w_exp_down[0])

    dest_t = dest.reshape(N_TOK // COMB_ROWS, COMB_ROWS, TOP_K).transpose(0, 2, 1).reshape(N_ASSIGN)
    gf = norm_final_g.reshape(1, D_MODEL)
    y_p = _combine(dest_t, h, gates, gf, yb, 0, N_PROMPT // COMB_ROWS)
    y_s = _combine(dest_t, h, gates, gf, yb, N_PROMPT // COMB_ROWS, N_SAMPLE // COMB_ROWS)

    return (y_p.reshape(BATCH, SEQ, D_MODEL), y_s.reshape(DEC_BATCH, DEC_SEQ, D_MODEL),
            conv_p[None], hgrn_p[None], conv_s[None], hgrn_s[None])
```

```python
import functools

import jax
import jax.numpy as jnp
from jax import lax
from jax.experimental import pallas as pl
from jax.experimental.pallas import tpu as pltpu

F32 = jnp.float32
BF16 = jnp.bfloat16
I32 = jnp.int32

D_MODEL = 4096
BATCH = 4
SEQ = 2048
DEC_BATCH = 128
DEC_SEQ = 4
CONV_CH = 2048
CONV_K = 3
HG_HEADS = 16
HG_DK = 128
HG_DV = 128
HG_W = HG_HEADS * HG_DK
HG_CHUNK = 64
IN_COLS = 3 * CONV_CH + 4 * HG_W
N_GROUPS = 8
EXP_PER_GROUP = 8
N_EXPERTS = 64
TOP_K = 2
D_EXPERT = 1024
EPS = 1e-6

N_PROMPT = BATCH * SEQ
N_SAMPLE = DEC_BATCH * DEC_SEQ
N_TOK = N_PROMPT + N_SAMPLE
N_ASSIGN = N_TOK * TOP_K

LANES = 128
SUBLANES = 8
ROUTE_COLS = LANES

NORM_ROWS = 256
MM_ROWS = 512
MM_COLS = 1024
CONV_ROWS = 512
CONV_COLS = 512
HG_ROWS = 512
HG_SAMPLE_SEQS = 4
MOE_BLK = 128
MOE_NB = N_ASSIGN // MOE_BLK + N_EXPERTS
MOE_ROWS = MOE_NB * MOE_BLK
MOE_FC = 512
MOE_DC = 2048
MOE_KSPLIT = 4
COMB_ROWS = 128

MIB = 1 << 20


def _params(sem, vmem_mib, **kwargs):
    return pltpu.CompilerParams(dimension_semantics=sem, vmem_limit_bytes=vmem_mib * MIB, **kwargs)


def _sigmoid(x):
    return 1.0 / (1.0 + jnp.exp(-x))


def _rmsnorm_rows(x, g):
    return x * lax.rsqrt(jnp.mean(x * x, axis=-1, keepdims=True) + EPS) * g


def _norm_in_kernel(xp_ref, xs_ref, g_ref, o_ref, *, n_prompt_tiles):
    i = pl.program_id(0)

    @pl.when(i < n_prompt_tiles)
    def _():
        o_ref[...] = _rmsnorm_rows(xp_ref[...], g_ref[...]).astype(o_ref.dtype)

    @pl.when(i >= n_prompt_tiles)
    def _():
        o_ref[...] = _rmsnorm_rows(xs_ref[...], g_ref[...]).astype(o_ref.dtype)


def _norm_in(xp, xs, g):
    npt = N_PROMPT // NORM_ROWS
    nst = N_SAMPLE // NORM_ROWS
    return pl.pallas_call(
        functools.partial(_norm_in_kernel, n_prompt_tiles=npt),
        grid=(npt + nst,),
        in_specs=[
            pl.BlockSpec((NORM_ROWS, D_MODEL), lambda i: (jnp.minimum(i, npt - 1), 0)),
            pl.BlockSpec((NORM_ROWS, D_MODEL), lambda i: (jnp.maximum(i - npt, 0), 0)),
            pl.BlockSpec((1, D_MODEL), lambda i: (0, 0)),
        ],
        out_specs=pl.BlockSpec((NORM_ROWS, D_MODEL), lambda i: (i, 0)),
        out_shape=jax.ShapeDtypeStruct((N_TOK, D_MODEL), BF16),
        compiler_params=_params(("arbitrary",), 40),
        name="norm_in",
    )(xp, xs, g)


def _in_proj_kernel(x_ref, w_ref, o_ref, wb_ref):
    @pl.when(pl.program_id(1) == 0)
    def _():
        wb_ref[...] = w_ref[...].astype(BF16)

    o_ref[...] = jnp.dot(x_ref[...], wb_ref[...], preferred_element_type=F32).astype(o_ref.dtype)


def _in_proj(xn, w):
    return pl.pallas_call(
        _in_proj_kernel,
        grid=(IN_COLS // MM_COLS, N_TOK // MM_ROWS),
        in_specs=[
            pl.BlockSpec((MM_ROWS, D_MODEL), lambda n, m: (m, 0)),
            pl.BlockSpec((D_MODEL, MM_COLS), lambda n, m: (0, n)),
        ],
        out_specs=pl.BlockSpec((MM_ROWS, MM_COLS), lambda n, m: (m, n)),
        out_shape=jax.ShapeDtypeStruct((N_TOK, IN_COLS), BF16),
        scratch_shapes=[pltpu.VMEM((D_MODEL, MM_COLS), BF16)],
        compiler_params=_params(("arbitrary", "arbitrary"), 58),
        name="in_proj",
    )(xn, w)


def _conv_taps(u, um1, um2, cb, w):
    conv = um2 * w[0:1] + um1 * w[1:2] + u * w[2:3]
    return cb * conv


def _conv_prompt_kernel(cb_ref, cc_ref, ch_ref, w_ref, y_ref, nb_ref, carry_ref):
    t = pl.program_id(2)

    @pl.when(t == 0)
    def _():
        carry_ref[...] = jnp.zeros_like(carry_ref)

    u = cc_ref[...].astype(F32) * ch_ref[...].astype(F32)
    prev = carry_ref[...]
    p1 = prev[SUBLANES - 1:SUBLANES]
    p2 = prev[SUBLANES - 2:SUBLANES - 1]
    row = lax.broadcasted_iota(I32, u.shape, 0)
    um1 = jnp.where(row == 0, p1, pltpu.roll(u, 1, axis=0))
    um2 = jnp.where(row == 0, p2, jnp.where(row == 1, p1, pltpu.roll(u, 2, axis=0)))
    y_ref[...] = _conv_taps(u, um1, um2, cb_ref[...].astype(F32), w_ref[...]).astype(y_ref.dtype)
    carry_ref[...] = u[CONV_ROWS - SUBLANES:CONV_ROWS]
    nb_ref[0] = u[CONV_ROWS - (CONV_K - 1):CONV_ROWS]


def _conv_prompt(proj, conv_w):
    nt = SEQ // CONV_ROWS
    nc = CONV_CH // CONV_COLS
    return pl.pallas_call(
        _conv_prompt_kernel,
        grid=(BATCH, nc, nt),
        in_specs=[
            pl.BlockSpec((CONV_ROWS, CONV_COLS), lambda b, c, t: (b * nt + t, c)),
            pl.BlockSpec((CONV_ROWS, CONV_COLS), lambda b, c, t: (b * nt + t, nc + c)),
            pl.BlockSpec((CONV_ROWS, CONV_COLS), lambda b, c, t: (b * nt + t, 2 * nc + c)),
            pl.BlockSpec((CONV_K, CONV_COLS), lambda b, c, t: (0, c)),
        ],
        out_specs=[
            pl.BlockSpec((CONV_ROWS, CONV_COLS), lambda b, c, t: (b * nt + t, c)),
            pl.BlockSpec((1, CONV_K - 1, CONV_COLS), lambda b, c, t: (b, 0, c)),
        ],
        out_shape=[
            jax.ShapeDtypeStruct((N_PROMPT, CONV_CH), BF16),
            jax.ShapeDtypeStruct((BATCH, CONV_K - 1, CONV_CH), F32),
        ],
        scratch_shapes=[pltpu.VMEM((SUBLANES, CONV_COLS), F32)],
        compiler_params=_params(("arbitrary", "arbitrary", "arbitrary"), 32),
        name="conv_prompt",
    )(proj, proj, proj, conv_w)


def _conv_sample_kernel(cb_ref, cc_ref, ch_ref, w_ref, e1_ref, e2_ref, y_ref, u_ref):
    u =cc_ref[...].astype(F32) * ch_ref[...].astype(F32)
    tpos = lax.broadcasted_iota(I32, u.shape, 0) % DEC_SEQ
    um1 = jnp.where(tpos == 0, e1_ref[...], pltpu.roll(u, 1, axis=0))
    um2 = jnp.where(tpos <= 1, e2_ref[...], pltpu.roll(u, 2, axis=0))
    y_ref[...] = _conv_taps(u, um1, um2, cb_ref[...].astype(F32), w_ref[...]).astype(y_ref.dtype)
    u_ref[...] = u


def _conv_sample(proj, conv_w, e1, e2):
    nc = CONV_CH // CONV_COLS
    rb = N_PROMPT // N_SAMPLE
    return pl.pallas_call(
        _conv_sample_kernel,
        grid=(nc,),
        in_specs=[
            pl.BlockSpec((N_SAMPLE, CONV_COLS), lambda c: (rb, c)),
            pl.BlockSpec((N_SAMPLE, CONV_COLS), lambda c: (rb, nc + c)),
            pl.BlockSpec((N_SAMPLE, CONV_COLS), lambda c: (rb, 2 * nc + c)),
            pl.BlockSpec((CONV_K, CONV_COLS), lambda c: (0, c)),
            pl.BlockSpec((N_SAMPLE, CONV_COLS), lambda c: (0, c)),
            pl.BlockSpec((N_SAMPLE, CONV_COLS), lambda c: (0, c)),
        ],
        out_specs=[
            pl.BlockSpec((N_SAMPLE, CONV_COLS), lambda c: (0, c)),
            pl.BlockSpec((N_SAMPLE, CONV_COLS), lambda c: (0, c)),
        ],
        out_shape=[
            jax.ShapeDtypeStruct((N_SAMPLE, CONV_CH), BF16),
            jax.ShapeDtypeStruct((N_SAMPLE, CONV_CH), F32),
        ],
        compiler_params=_params(("arbitrary",), 32),
        name="conv_sample",
    )(proj, proj, proj, conv_w, e1, e2)


_NT = (((1,), (1,)), ((), ()))
_TN = (((0,), (0,)), ((), ()))


def _hgrn_factors(q, fz, lb, seq_len):
    rows, width = q.shape
    one_m_lb = 1.0 - lb
    log_f = jnp.log(lb + one_m_lb * _sigmoid(fz))
    k = one_m_lb * _sigmoid(-fz)
    row = lax.broadcasted_iota(I32, (rows, width), 0)
    tpos = row % seq_len

    b = log_f
    shift = 1
    while shift < seq_len:
        b = b + jnp.where(tpos >= shift, pltpu.roll(b, shift, axis=0), 0.0)
        shift *= 2

    n_seq = rows // seq_len
    mid = (seq_len - 1) // 2
    if seq_len % SUBLANES == 0:
        b3 = b.reshape(n_seq, seq_len, width)
        b_mid = b3[:, mid:mid + 1]
        b_end = b3[:, seq_len - 1:seq_len]

        def per_row(per_chunk):
            return jnp.broadcast_to(per_chunk, b3.shape).reshape(rows, width)

        b_ref, b_last = per_row(b_mid), per_row(b_end)
        e_ref, e_last = per_row(jnp.exp(b_mid)), per_row(jnp.exp(b_end - b_mid))
    else:
        seq = row // seq_len
        b_ref = b[mid:mid + 1]
        b_last = b[seq_len - 1:seq_len]
        for j in range(1, n_seq):
            b_ref = jnp.where(seq == j, b[j * seq_len + mid:j * seq_len + mid + 1], b_ref)
            b_last = jnp.where(seq == j, b[(j + 1) * seq_len - 1:(j + 1) * seq_len], b_last)
        e_ref, e_last = jnp.exp(b_ref), jnp.exp(b_last - b_ref)

    qe = q * jnp.exp(b - b_ref)
    ke = k * jnp.exp(b_ref - b)
    return qe, ke, qe * e_ref, ke * e_last, log_f, b_last


def _split3_bf16(x):
    hi = x.astype(BF16).astype(F32)
    rest = x - hi
    mid = rest.astype(BF16).astype(F32)
    lo = (rest - mid).astype(BF16).astype(F32)
    return hi, mid, lo


def _head_out(o, gate, norm_g):
    o_n = o * lax.rsqrt(jnp.mean(o * o, axis=-1, keepdims=True) + EPS)
    return o_n * norm_g * (gate * _sigmoid(gate))


def _hgrn_prompt_kernel(q_ref, f_ref, v_ref, g_ref, lb_ref, ng_ref, y_ref, s_out_ref):
    lb = lb_ref[...]
    ng = ng_ref[...]
    rr = lax.broadcasted_iota(I32, (HG_CHUNK, HG_CHUNK), 0)
    cc = lax.broadcasted_iota(I32, (HG_CHUNK, HG_CHUNK), 1)
    causal = cc <= rr

    def block(i, state_t):
        rows = pl.ds(pl.multiple_of(i * HG_ROWS, HG_ROWS), HG_ROWS)
        qe, ke, qb, kd, _, b_last = _hgrn_factors(q_ref[rows, :].astype(F32), f_ref[rows, :].astype(F32),
                                                  lb, HG_CHUNK)
        qe, ke, qb, kd = (a.astype(BF16) for a in (qe, ke, qb, kd))
        vb = v_ref[rows, :]
        outs = []
        for c in range(HG_ROWS // HG_CHUNK):
            cs = slice(c * HG_CHUNK, (c + 1) * HG_CHUNK)
            scores = lax.dot_general(qe[cs], ke[cs], _NT, preferred_element_type=F32)
            scores = jnp.where(causal, scores, 0.0).astype(BF16)
            outs.append(jnp.dot(scores, vb[cs], preferred_element_type=F32)
                        + lax.dot_general(qb[cs], state_t.astype(BF16), _NT, preferred_element_type=F32))
            inc_t = lax.dot_general(vb[cs], kd[cs], _TN, preferred_element_type=F32)
            state_t = jnp.exp(b_last[c * HG_CHUNK:c * HG_CHUNK + 1]) * state_t + inc_t
        o = jnp.concatenate(outs, axis=0)
        y_ref[rows, :] = _head_out(o, g_ref[rows, :].astype(F32), ng).astype(y_ref.dtype)
        return state_t

    state_t = lax.fori_loop(0, SEQ // HG_ROWS, block, jnp.zeros((HG_DV, HG_DK), F32))
    s_out_ref[0, 0] = state_t.T


def _hgrn_prompt(proj, lb, norm_g):
    cq = 3 * CONV_CH // HG_DK

    def spec(off):
        return pl.BlockSpec((SEQ, HG_DK), lambda b, h: (b, cq + off * HG_HEADS + h))

    return pl.pallas_call(
        _hgrn_prompt_kernel,
        grid=(BATCH, HG_HEADS),
        in_specs=[spec(0), spec(1), spec(2), spec(3),
                  pl.BlockSpec((1, HG_DK), lambda b, h: (0, h)),
                  pl.BlockSpec((1, HG_DV), lambda b, h: (0, h))],
        out_specs=[
            pl.BlockSpec((SEQ, HG_DV), lambda b, h: (b, h)),
            pl.BlockSpec((1, 1, HG_DK, HG_DV), lambda b, h: (b, h, 0, 0)),
        ],
        out_shape=[
            jax.ShapeDtypeStruct((N_PROMPT, HG_W), BF16),
            jax.ShapeDtypeStruct((BATCH, HG_HEADS, HG_DK, HG_DV), F32),
        ],
        compiler_params=_params(("arbitrary", "arbitrary"), 32),
        name="hgrn_prompt",
    )(proj, proj, proj, proj, lb, norm_g)


def _hgrn_sample_kernel(q_ref, f_ref, v_ref, g_ref, lb_ref, ng_ref, s_ref, y_ref, s_out_ref):
    rows = HG_SAMPLE_SEQS * DEC_SEQ
    qe, ke, qb, kd, log_f, _ = _hgrn_factors(q_ref[...].astype(F32), f_ref[...].astype(F32),
                                            lb_ref[...], DEC_SEQ)
    lf_parts = _split3_bf16(log_f)
    v = v_ref[...].astype(F32)
    gate = g_ref[...].astype(F32)
    ng = ng_ref[...]

    in_a = lax.broadcasted_iota(I32, (SUBLANES, HG_DV), 0) < DEC_SEQ
    rr = lax.broadcasted_iota(I32, (SUBLANES, SUBLANES), 0)
    cc = lax.broadcasted_iota(I32, (SUBLANES, SUBLANES), 1)
    causal = (cc <= rr) & ((rr // DEC_SEQ) == (cc // DEC_SEQ))
    zero_tile = jnp.zeros((SUBLANES, HG_DV), F32)
    sel = jnp.concatenate([in_a.astype(F32), 1.0 - in_a.astype(F32)], axis=1)
    dec_rhs = jnp.concatenate([sel, sel, sel, jnp.zeros_like(sel)], axis=0).astype(BF16)

    for h in range(HG_HEADS):
        cols = slice(h * HG_DK, (h + 1) * HG_DK)
        outs = []
        for r in range(rows // SUBLANES):
            rs = slice(r * SUBLANES, (r + 1) * SUBLANES)
            s_cat = jnp.concatenate([s_ref[2 * r, h], s_ref[2 * r + 1, h]], axis=1)
            v_t = v[rs, cols]
            scores = lax.dot_general(qe[rs, cols].astype(BF16), ke[rs, cols].astype(BF16), _NT,
                                     preferred_element_type=F32)
            scores = jnp.where(causal, scores, 0.0).astype(BF16)
            o_state = jnp.dot(qb[rs, cols].astype(BF16), s_cat.astype(BF16), preferred_element_type=F32)
            outs.append(jnp.dot(scores, v_t.astype(BF16), preferred_element_type=F32)
                        + jnp.where(in_a, o_state[:, :HG_DV], o_state[:, HG_DV:]))
            inc_rhs = jnp.concatenate([jnp.where(in_a, v_t, 0.0), jnp.where(in_a, 0.0, v_t)], axis=1)
            inc = lax.dot_general(kd[rs, cols].astype(BF16), inc_rhs.astype(BF16), _TN,
                                  preferred_element_type=F32)
            dec_lhs = jnp.concatenate([p[rs, cols] for p in lf_parts] + [zero_tile], axis=0)
            decay = jnp.exp(lax.dot_general(dec_lhs.astype(BF16), dec_rhs, _TN,
                                            preferred_element_type=F32))
            s_new = decay * s_cat + inc
            s_out_ref[2 * r, h] = s_new[:, :HG_DV]
            s_out_ref[2 * r + 1, h] = s_new[:, HG_DV:]
        o = jnp.concatenate(outs, axis=0)
        y_ref[:, cols] = _head_out(o, gate[:, cols], ng[:, cols]).astype(y_ref.dtype)


def _hgrn_sample(proj, lb, norm_g, state):
    rows = HG_SAMPLE_SEQS * DEC_SEQ
    rb = N_PROMPT // rows
    cq = 3 * CONV_CH // HG_W

    def spec(off):
        return pl.BlockSpec((rows, HG_W), lambda g: (rb + g, cq + off))

    st_spec = pl.BlockSpec((HG_SAMPLE_SEQS, HG_HEADS, HG_DK, HG_DV), lambda g: (g, 0, 0, 0))
    return pl.pallas_call(
        _hgrn_sample_kernel,
        grid=(DEC_BATCH // HG_SAMPLE_SEQS,),
        in_specs=[spec(0), spec(1), spec(2), spec(3),
                  pl.BlockSpec((1, HG_W), lambda g: (0, 0)),
                  pl.BlockSpec((1, HG_W), lambda g: (0, 0)),
                  st_spec],
        out_specs=[pl.BlockSpec((rows, HG_W), lambda g: (g, 0)), st_spec],
        out_shape=[
            jax.ShapeDtypeStruct((N_SAMPLE, HG_W), BF16),
            jax.ShapeDtypeStruct((DEC_BATCH, HG_HEADS, HG_DK, HG_DV), F32),
        ],
        compiler_params=_params(("arbitrary",), 40),
        name="hgrn_sample",
    )(proj, proj, proj, proj, lb, norm_g, state)


def _out_proj_kernel(ycp_ref, yhp_ref, xp_ref, ycs_ref, yhs_ref, xs_ref, wc_ref, wh_ref, o_ref, *,
                     n_prompt_tiles):
    m = pl.program_id(1)

    def residual_mix(yc_ref, yh_ref, x_ref):
        mix = (jnp.dot(yc_ref[...], wc_ref[...], preferred_element_type=F32)
               + jnp.dot(yh_ref[...], wh_ref[...], preferred_element_type=F32))
        o_ref[...] = x_ref[...] + mix

    @pl.when(m < n_prompt_tiles)
    def _():
        residual_mix(ycp_ref, yhp_ref, xp_ref)

    @pl.when(m >= n_prompt_tiles)
    def _():
        residual_mix(ycs_ref, yhs_ref, xs_ref)


def _out_proj(yc_p, yh_p, xp, yc_s, yh_s, xs, w):
    npt = N_PROMPT // MM_ROWS

    def prompt_map(n, m):
        return jnp.minimum(m, npt - 1)

    def sample_map(n, m):
        return jnp.maximum(m - npt, 0)

    def source_specs(row_map):
        return [
            pl.BlockSpec((MM_ROWS, CONV_CH), lambda n, m: (row_map(n, m), 0)),
            pl.BlockSpec((MM_ROWS, HG_W), lambda n, m: (row_map(n, m), 0)),
            pl.BlockSpec((MM_ROWS, MM_COLS), lambda n, m: (row_map(n, m), n)),
        ]

    return pl.pallas_call(
        functools.partial(_out_proj_kernel, n_prompt_tiles=npt),
        grid=(D_MODEL // MM_COLS, N_TOK // MM_ROWS),
        in_specs=source_specs(prompt_map) + source_specs(sample_map) + [
            pl.BlockSpec((CONV_CH, MM_COLS), lambda n, m: (0, n)),
            pl.BlockSpec((HG_W, MM_COLS), lambda n, m: (1, n)),
        ],
        out_specs=pl.BlockSpec((MM_ROWS, MM_COLS), lambda n, m: (m, n)),
        out_shape=jax.ShapeDtypeStruct((N_TOK, D_MODEL), F32),
        compiler_params=_params(("arbitrary", "arbitrary"), 56),
        name="out_proj",
    )(yc_p, yh_p, xp, yc_s, yh_s, xs, w, w)


def _split_bf16(x):
    hi = x.astype(BF16)
    return hi, (x - hi.astype(F32)).astype(BF16)


def _norm_route_kernel(h_ref, g_ref, wr_hi_ref, wr_lo_ref, br_ref, hn_ref, eidx_ref, gate_ref):
    hn = _rmsnorm_rows(h_ref[...], g_ref[...])
    hn_ref[...] = hn
    hn_hi, hn_lo = _split_bf16(hn)
    logits = (jnp.dot(hn_hi, wr_hi_ref[...], preferred_element_type=F32)
              + (jnp.dot(hn_hi, wr_lo_ref[...], preferred_element_type=F32)
                 + jnp.dot(hn_lo, wr_hi_ref[...], preferred_element_type=F32))) + br_ref[...]
    lane = lax.broadcasted_iota(I32, logits.shape, 1)
    lane_f = lane.astype(F32)
    neg = -jnp.inf

    def first_argmax(vals, vmax):
        first = jnp.min(jnp.where(vals == vmax, lane_f, float(ROUTE_COLS)), axis=-1, keepdims=True)
        return first.astype(I32)

    is_grp = lane < N_GROUPS
    lg = jnp.where(is_grp, logits, neg)
    mg = jnp.max(lg, axis=-1, keepdims=True)
    g_sel = first_argmax(lg, mg)
    p_grp = 1.0 / jnp.sum(jnp.where(is_grp, jnp.exp(logits - mg), 0.0), axis=-1, keepdims=True)

    in_grp = (lane >= N_GROUPS) & (((lane - N_GROUPS) // EXP_PER_GROUP) == g_sel)
    le = jnp.where(in_grp, logits, neg)
    m1 = jnp.max(le, axis=-1, keepdims=True)
    i1 = first_argmax(le, m1)
    le2 = jnp.where(lane == i1, neg, le)
    m2 = jnp.max(le2, axis=-1, keepdims=True)
    i2 = first_argmax(le2, m2)
    e2 = jnp.exp(m2 - m1)
    gate1 = p_grp / (1.0 + e2)
    gate2 = p_grp * e2 / (1.0 + e2)
    eidx_ref[...] = jnp.where(lane == 0, i1 - N_GROUPS, jnp.where(lane == 1, i2 - N_GROUPS, 0))
    gate_ref[...] = jnp.where(lane == 0, gate1, jnp.where(lane == 1, gate2, 0.0))


def _norm_route(h, g, w_r, b_r):
    wr_hi, wr_lo = _split_bf16(w_r)
    return pl.pallas_call(
        _norm_route_kernel,
        grid=(N_TOK // NORM_ROWS,),
        in_specs=[
            pl.BlockSpec((NORM_ROWS, D_MODEL), lambda i: (i, 0)),
            pl.BlockSpec((1, D_MODEL), lambda i: (0, 0)),
            pl.BlockSpec((D_MODEL, ROUTE_COLS), lambda i: (0, 0)),
            pl.BlockSpec((D_MODEL, ROUTE_COLS), lambda i: (0, 0)),
            pl.BlockSpec((1, ROUTE_COLS), lambda i: (0, 0)),
        ],
        out_specs=[
            pl.BlockSpec((NORM_ROWS, D_MODEL), lambda i: (i, 0)),
            pl.BlockSpec((NORM_ROWS, ROUTE_COLS), lambda i: (i, 0)),
            pl.BlockSpec((NORM_ROWS, ROUTE_COLS), lambda i: (i, 0)),
        ],
        out_shape=[
            jax.ShapeDtypeStruct((N_TOK, D_MODEL), F32),
            jax.ShapeDtypeStruct((N_TOK, ROUTE_COLS), I32),
            jax.ShapeDtypeStruct((N_TOK, ROUTE_COLS), F32),
        ],
        compiler_params=_params(("arbitrary",), 48),
        name="norm_route",
    )(h, g, wr_hi, wr_lo, b_r)


def _row_copy(src_hbm, row, buf, slot, r, sem):
    return pltpu.make_async_copy(src_hbm.at[pl.ds(row, 1)], buf.at[slot, pl.ds(r, 1)], sem.at[slot])


def _gather_issue(idx_ref, base, n_groups, src_hbm, buf, slot, sem):
    def body(g, carry):
        r0 = g * SUBLANES
        for j in range(SUBLANES):
            _row_copy(src_hbm, idx_ref[base + r0 + j], buf, slot, r0 + j, sem).start(priority=j % 2)
        return carry

    lax.fori_loop(0, n_groups, body, 0)


def _gather_wait(n_groups, src_hbm, buf, slot, sem):
    def body(g, carry):
        rows = pl.ds(pl.multiple_of(g * SUBLANES, SUBLANES), SUBLANES)
        pltpu.make_async_copy(src_hbm.at[pl.ds(0, SUBLANES)], buf.at[slot, rows], sem.at[slot]).wait()
        return carry

    lax.fori_loop(0, n_groups, body, 0)


def _dispatch_kernel(tok_ref, ng_ref, hn_hbm, o_ref, buf, sem):
    b = pl.program_id(0)
    nb = pl.num_programs(0)
    slot = b % 2

    def fetch(blk, dst_slot):
        n_groups = ng_ref[blk]
        _gather_issue(tok_ref, blk * MOE_BLK, n_groups, hn_hbm, buf, dst_slot, sem)

        def zero(g, carry):
            rows = pl.ds(pl.multiple_of(g * SUBLANES, SUBLANES), SUBLANES)
            buf[dst_slot, rows, :] = jnp.zeros((SUBLANES, D_MODEL), F32)
            return carry

        lax.fori_loop(n_groups, MOE_BLK // SUBLANES, zero, 0)

    @pl.when(b == 0)
    def _():
        fetch(0, 0)

    @pl.when(b + 1 < nb)
    def _():
        fetch(b + 1, 1 - slot)

    _gather_wait(ng_ref[b], hn_hbm, buf, slot, sem)
    o_ref[...] = buf[slot].astype(o_ref.dtype)


def _dispatch(row_tok, n_groups, hn):
    return pl.pallas_call(
        _dispatch_kernel,
        grid_spec=pltpu.PrefetchScalarGridSpec(
            num_scalar_prefetch=2,
            grid=(MOE_NB,),
            in_specs=[pl.BlockSpec(memory_space=pl.ANY)],
            out_specs=pl.BlockSpec((MOE_BLK, D_MODEL), lambda b, tok, ng: (b, 0)),
            scratch_shapes=[pltpu.VMEM((2, MOE_BLK, D_MODEL), F32),
                            pltpu.SemaphoreType.DMA((2,))],
        ),
        out_shape=jax.ShapeDtypeStruct((MOE_ROWS, D_MODEL), BF16),
        compiler_params=_params(("arbitrary",), 32, disable_bounds_checks=True),
        name="moe_dispatch",
    )(row_tok, n_groups, hn)


def _stream_expert_blocks(blk0_ref, nblk_ref, n_used, in_copy, out_copy, out_buf, compute):
    e = pl.program_id(1)

    @pl.when(e == 0)
    def _():
        in_copy(0, 0).start()

    def body(g, carry):
        slot = g % 2

        @pl.when(g + 1 < n_used)
        def _():
            in_copy(g + 1, 1 - slot).start()

        in_copy(g, slot).wait()

        @pl.when(g >= 2)
        def _():
            out_copy(g - 2, slot).wait()

        compute(slot)
        out_copy(g, slot).start()
        return carry

    blk0 = blk0_ref[e]
    lax.fori_loop(blk0, blk0 + nblk_ref[e], body, 0)

    @pl.when(e == pl.num_programs(1) - 1)
    def _():
        @pl.when(n_used >= 2)
        def _():
            out_copy(n_used - 2, n_used % 2).wait()

        out_copy(n_used - 1, (n_used - 1) % 2).wait()
        out_buf[0] = jnp.zeros(out_buf.shape[1:], out_buf.dtype)

        def start_zero(g, carry):
            out_copy(g, 0).start()
            return carry

        def wait_zero(g, carry):
            out_copy(g, 0).wait()
            return carry

        lax.fori_loop(n_used, MOE_NB, start_zero, 0)
        lax.fori_loop(n_used, MOE_NB, wait_zero, 0)


def _block_rows(g):
    return pl.ds(pl.multiple_of(g * MOE_BLK, MOE_BLK), MOE_BLK)


def _dot_k_split(x, w_refs):
    kc = x.shape[1] // len(w_refs)
    acc = jnp.dot(x[:, :kc], w_refs[0][0], preferred_element_type=F32)
    for i in range(1, len(w_refs)):
        acc = acc + jnp.dot(x[:, i * kc:(i + 1) * kc], w_refs[i][0], preferred_element_type=F32)
    return acc


def _expert_up_kernel(blk0_ref, nblk_ref, nbu_ref, *refs):
    wg_refs, wu_refs = refs[:MOE_KSPLIT], refs[MOE_KSPLIT:2 * MOE_KSPLIT]
    x_hbm, h_hbm, xbuf, hbuf, sem_x, sem_h = refs[2 * MOE_KSPLIT:]
    cols = pl.ds(pl.multiple_of(pl.program_id(0) * MOE_FC, MOE_FC), MOE_FC)

    def x_copy(g, slot):
        return pltpu.make_async_copy(x_hbm.at[_block_rows(g)], xbuf.at[slot], sem_x.at[slot])

    def h_copy(g, slot):
        return pltpu.make_async_copy(hbuf.at[slot], h_hbm.at[_block_rows(g), cols], sem_h.at[slot])

    def compute(slot):
        x = xbuf[slot].astype(F32)
        g = _dot_k_split(x, wg_refs)
        u = _dot_k_split(x, wu_refs)
        hbuf[slot] = (g * _sigmoid(g) * u).astype(hbuf.dtype)

    _stream_expert_blocks(blk0_ref, nblk_ref, nbu_ref[0], x_copy, h_copy, hbuf, compute)


def _k_split_specs(k_rows, cols):
    def spec(i):
        return pl.BlockSpec((1, k_rows // MOE_KSPLIT, cols), lambda c, e, *_: (e, i, c))

    return [spec(i) for i in range(MOE_KSPLIT)]


def _expert_up(blk0, nblk, n_used, xb, w_g, w_u):
    w_specs = _k_split_specs(D_MODEL, MOE_FC)
    return pl.pallas_call(
        _expert_up_kernel,
        grid_spec=pltpu.PrefetchScalarGridSpec(
            num_scalar_prefetch=3,
            grid=(D_EXPERT // MOE_FC, N_EXPERTS),
            in_specs=w_specs + w_specs + [pl.BlockSpec(memory_space=pl.ANY)],
            out_specs=pl.BlockSpec(memory_space=pl.ANY),
            scratch_shapes=[pltpu.VMEM((2, MOE_BLK, D_MODEL), BF16),
                            pltpu.VMEM((2, MOE_BLK, MOE_FC), BF16),
                            pltpu.SemaphoreType.DMA((2,)),
                            pltpu.SemaphoreType.DMA((2,))],
        ),
        out_shape=jax.ShapeDtypeStruct((MOE_ROWS, D_EXPERT), BF16),
        compiler_params=_params(("arbitrary", "arbitrary"), 56),
        name="expert_up",
    )(blk0, nblk, n_used, *([w_g] * MOE_KSPLIT), *([w_u] * MOE_KSPLIT), xb)


def _expert_down_kernel(blk0_ref, nblk_ref, nbu_ref, *refs):
    wd_refs = refs[:MOE_KSPLIT]
    h_hbm, y_hbm, hbuf, ybuf, sem_h, sem_y = refs[MOE_KSPLIT:]
    cols =pl.ds(pl.multiple_of(pl.program_id(0) * MOE_DC, MOE_DC), MOE_DC)

    def h_copy(g, slot):
        return pltpu.make_async_copy(h_hbm.at[_block_rows(g)], hbuf.at[slot], sem_h.at[slot])

    def y_copy(g, slot):
        return pltpu.make_async_copy(ybuf.at[slot], y_hbm.at[_block_rows(g), cols], sem_y.at[slot])

    def compute(slot):
        ybuf[slot] = _dot_k_split(hbuf[slot].astype(F32), wd_refs)

    _stream_expert_blocks(blk0_ref, nblk_ref, nbu_ref[0], h_copy, y_copy, ybuf, compute)


def _expert_down(blk0, nblk, n_used, hb, w_d):
    return pl.pallas_call(
        _expert_down_kernel,
        grid_spec=pltpu.PrefetchScalarGridSpec(
            num_scalar_prefetch=3,
            grid=(D_MODEL // MOE_DC, N_EXPERTS),
            in_specs=_k_split_specs(D_EXPERT, MOE_DC) + [pl.BlockSpec(memory_space=pl.ANY)],
            out_specs=pl.BlockSpec(memory_space=pl.ANY),
            scratch_shapes=[pltpu.VMEM((2, MOE_BLK, D_EXPERT), BF16),
                            pltpu.VMEM((2, MOE_BLK, MOE_DC), F32),
                            pltpu.SemaphoreType.DMA((2,)),
                            pltpu.SemaphoreType.DMA((2,))],
        ),
        out_shape=jax.ShapeDtypeStruct((MOE_ROWS, D_MODEL), F32),
        compiler_params=_params(("arbitrary", "arbitrary"), 48),
        name="expert_down",
    )(blk0, nblk, n_used, *([w_d] * MOE_KSPLIT), hb)


def _combine_kernel(dest_ref, h_ref, gate_ref, g_ref, y_hbm, o_ref, buf, sem, *, tile0):
    i = pl.program_id(0)
    n = pl.num_programs(0)
    slot = i % 2
    rows = TOP_K * COMB_ROWS
    n_groups = rows // SUBLANES
    base = (tile0 + i) * rows

    @pl.when(i == 0)
    def _():
        _gather_issue(dest_ref, base, n_groups, y_hbm, buf, 0, sem)

    @pl.when(i + 1 < n)
    def _():
        _gather_issue(dest_ref, base + rows, n_groups, y_hbm, buf, 1 - slot, sem)

    _gather_wait(n_groups, y_hbm, buf, slot, sem)
    gates = gate_ref[...]
    ff = gates[:, 0:1] * buf[slot, 0:COMB_ROWS] + gates[:, 1:2] * buf[slot, COMB_ROWS:rows]
    o_ref[...] = _rmsnorm_rows(h_ref[...] + ff, g_ref[...])


def _combine(dest, h, gates, g_final, y_buf, tile0, n_tiles):
    return pl.pallas_call(
        functools.partial(_combine_kernel, tile0=tile0),
        grid_spec=pltpu.PrefetchScalarGridSpec(
            num_scalar_prefetch=1,
            grid=(n_tiles,),
            in_specs=[
                pl.BlockSpec((COMB_ROWS, D_MODEL), lambda i, d: (tile0 + i, 0)),
                pl.BlockSpec((COMB_ROWS, ROUTE_COLS), lambda i, d: (tile0 + i, 0)),
                pl.BlockSpec((1, D_MODEL), lambda i, d: (0, 0)),
                pl.BlockSpec(memory_space=pl.ANY),
            ],
            out_specs=pl.BlockSpec((COMB_ROWS, D_MODEL), lambda i, d: (i, 0)),
            scratch_shapes=[pltpu.VMEM((2, TOP_K * COMB_ROWS, D_MODEL), F32),
                            pltpu.SemaphoreType.DMA((2,))],
        ),
        out_shape=jax.ShapeDtypeStruct((n_tiles * COMB_ROWS, D_MODEL), F32),
        compiler_params=_params(("arbitrary",), 40, disable_bounds_checks=True),
        name="moe_combine",
    )(dest, h, gates, g_final, y_buf)


def _dispatch_plan(eidx):
    flat_e = eidx.reshape(N_ASSIGN)
    onehot = (flat_e[:, None] == jnp.arange(N_EXPERTS, dtype=I32)[None, :]).astype(I32)
    csum = jnp.cumsum(onehot, axis=0)
    counts = csum[-1]
    rank = jnp.take_along_axis(csum, flat_e[:, None], axis=1)[:, 0] - 1
    padded = ((counts + MOE_BLK - 1) // MOE_BLK) * MOE_BLK
    pad_end = jnp.cumsum(padded)
    pad_start = pad_end - padded
    dest = (pad_start[flat_e] + rank).astype(I32)
    n_used = (pad_end[-1] // MOE_BLK).astype(I32)
    blk = jnp.arange(MOE_NB, dtype=I32)
    block_e = jnp.minimum(jnp.searchsorted(pad_end, blk * MOE_BLK, side='right'), N_EXPERTS - 1)
    rows_in_blk = jnp.clip(counts[block_e] - (blk * MOE_BLK - pad_start[block_e]), 0, MOE_BLK)
    rows_in_blk = jnp.where(blk < n_used, rows_in_blk, 0)
    n_groups = ((rows_in_blk + SUBLANES - 1) // SUBLANES).astype(I32)
    row_tok = jnp.zeros((MOE_ROWS,), I32).at[dest].set(jnp.arange(N_ASSIGN, dtype=I32) // TOP_K)
    blk0 = (pad_start // MOE_BLK).astype(I32)
    nblk = (padded // MOE_BLK).astype(I32)
    return dest, row_tok, n_groups, blk0, nblk, n_used.reshape(1)


def kernel(x_prompt, x_sample, cache_conv, state_hgrn, norm_mix_g, w_in, conv_w, lb_param, hg_norm_g,
           w_out, norm_ffn_g, w_group_router, b_group_router, w_expert_router, b_expert_router,
           w_exp_gate, w_exp_up, w_exp_down, norm_final_g):
    xp = x_prompt.reshape(N_PROMPT, D_MODEL)
    xs = x_sample.reshape(N_SAMPLE, D_MODEL)
    lb = jnp.cumsum(jax.nn.softmax(lb_param.astype(F32), axis=0), axis=0)[0].reshape(1, HG_W)

    xn = _norm_in(xp, xs, norm_mix_g[0].reshape(1, D_MODEL))
    proj = _in_proj(xn, w_in[0])

    buf = cache_conv[0]
    zeros = jnp.zeros((DEC_BATCH, DEC_SEQ - 1, CONV_CH), F32)
    e1 = jnp.concatenate([buf[:, 1:2], zeros], axis=1).reshape(N_SAMPLE, CONV_CH)
    e2 = jnp.concatenate([buf, zeros[:, :DEC_SEQ - 2]], axis=1).reshape(N_SAMPLE, CONV_CH)
    yc_p, conv_p = _conv_prompt(proj, conv_w[0])
    yc_s, u_s = _conv_sample(proj, conv_w[0], e1, e2)
    conv_s = u_s.reshape(DEC_BATCH, DEC_SEQ, CONV_CH)[:, DEC_SEQ - (CONV_K - 1):]

    ng = hg_norm_g[0].reshape(1, HG_W)
    yh_p, hgrn_p = _hgrn_prompt(proj, lb, ng)
    yh_s, hgrn_s = _hgrn_sample(proj, lb, ng, state_hgrn[0])

    h = _out_proj(yc_p, yh_p, xp, yc_s, yh_s, xs, w_out[0].astype(BF16))

    pad = jnp.zeros((D_MODEL, ROUTE_COLS - N_GROUPS - N_EXPERTS), F32)
    w_r = jnp.concatenate([w_group_router[0], w_expert_router[0], pad], axis=1)
    b_r = jnp.concatenate([b_group_router[0], b_expert_router[0], pad[0]]).reshape(1, ROUTE_COLS)
    hn, eidx, gates = _norm_route(h, norm_ffn_g[0].reshape(1, D_MODEL), w_r, b_r)
    dest, row_tok, n_groups, blk0, nblk, n_used = _dispatch_plan(eidx[:, :TOP_K])
    xb = _dispatch(row_tok, n_groups, hn)
    hb = _expert_up(blk0, nblk, n_used, xb, w_exp_gate[0], w_exp_up[0])
    yb = _expert_down(blk0, nblk, n_used, hb, w_exp_down[0])

    dest_t = dest.reshape(N_TOK // COMB_ROWS, COMB_ROWS, TOP_K).transpose(0, 2, 1).reshape(N_ASSIGN)
    gf = norm_final_g.reshape(1, D_MODEL)
    y_p = _combine(dest_t, h, gates, gf, yb, 0, N_PROMPT // COMB_ROWS)
    y_s = _combine(dest_t, h, gates, gf, yb, N_PROMPT // COMB_ROWS, N_SAMPLE // COMB_ROWS)

    return (y_p.reshape(BATCH, SEQ, D_MODEL), y_s.reshape(DEC_BATCH, DEC_SEQ, D_MODEL),
            conv_p[None], hgrn_p[None], conv_s[None], hgrn_s[None])
```

```python
import functools

import jax
import jax.numpy as jnp
from jax import lax
from jax.experimental import pallas as pl
from jax.experimental.pallas import tpu as pltpu

F32 = jnp.float32
BF16 = jnp.bfloat16
I32 = jnp.int32

D_MODEL = 4096
BATCH = 4
SEQ = 2048
DEC_BATCH = 128
DEC_SEQ = 4
CONV_CH = 2048
CONV_K = 3
HG_HEADS = 16
HG_DK = 128
HG_DV = 128
HG_W = HG_HEADS * HG_DK
HG_CHUNK = 64
IN_COLS = 3 * CONV_CH + 4 * HG_W
N_GROUPS = 8
EXP_PER_GROUP = 8
N_EXPERTS = 64
TOP_K = 2
D_EXPERT = 1024
EPS = 1e-6

N_PROMPT = BATCH * SEQ
N_SAMPLE = DEC_BATCH * DEC_SEQ
N_TOK = N_PROMPT + N_SAMPLE
N_ASSIGN = N_TOK * TOP_K

LANES = 128
SUBLANES = 8
ROUTE_COLS = LANES

NORM_ROWS = 256
MM_ROWS = 512
MM_COLS = 1024
CONV_ROWS = 512
CONV_COLS = 512
HG_ROWS = 512
HG_SAMPLE_SEQS = 4
MOE_BLK = 128
MOE_NB = N_ASSIGN // MOE_BLK + N_EXPERTS
MOE_ROWS = MOE_NB * MOE_BLK
MOE_FC = 512
MOE_DC = 2048
MOE_KSPLIT = 4
ROW_BLOCK_DMA_PRIORITY = 1
COMB_ROWS = 128

MIB = 1 << 20


def _params(sem, vmem_mib, **kwargs):
    return pltpu.CompilerParams(dimension_semantics=sem, vmem_limit_bytes=vmem_mib * MIB, **kwargs)


def _sigmoid(x):
    return 1.0 / (1.0 + jnp.exp(-x))


def _rmsnorm_rows(x, g):
    return x * lax.rsqrt(jnp.mean(x * x, axis=-1, keepdims=True) + EPS) * g


def _norm_in_kernel(xp_ref, xs_ref, g_ref, o_ref, *, n_prompt_tiles):
    i = pl.program_id(0)

    @pl.when(i < n_prompt_tiles)
    def _():
        o_ref[...] = _rmsnorm_rows(xp_ref[...], g_ref[...]).astype(o_ref.dtype)

    @pl.when(i >= n_prompt_tiles)
    def _():
        o_ref[...] = _rmsnorm_rows(xs_ref[...], g_ref[...]).astype(o_ref.dtype)


def _norm_in(xp, xs, g):
    npt = N_PROMPT // NORM_ROWS
    nst = N_SAMPLE // NORM_ROWS
    return pl.pallas_call(
        functools.partial(_norm_in_kernel, n_prompt_tiles=npt),
        grid=(npt + nst,),
        in_specs=[
            pl.BlockSpec((NORM_ROWS, D_MODEL), lambda i: (jnp.minimum(i, npt - 1), 0)),
            pl.BlockSpec((NORM_ROWS, D_MODEL), lambda i: (jnp.maximum(i - npt, 0), 0)),
            pl.BlockSpec((1, D_MODEL), lambda i: (0, 0)),
        ],
        out_specs=pl.BlockSpec((NORM_ROWS, D_MODEL), lambda i: (i, 0)),
        out_shape=jax.ShapeDtypeStruct((N_TOK, D_MODEL), BF16),
        compiler_params=_params(("arbitrary",), 40),
        name="norm_in",
    )(xp, xs, g)


def _in_proj_kernel(x_ref, w_ref, o_ref, wb_ref):
    @pl.when(pl.program_id(1) == 0)
    def _():
        wb_ref[...] = w_ref[...].astype(BF16)

    o_ref[...] = jnp.dot(x_ref[...], wb_ref[...], preferred_element_type=F32).astype(o_ref.dtype)


def _in_proj(xn, w):
    return pl.pallas_call(
        _in_proj_kernel,
        grid=(IN_COLS // MM_COLS, N_TOK // MM_ROWS),
        in_specs=[
            pl.BlockSpec((MM_ROWS, D_MODEL), lambda n, m: (m, 0)),
            pl.BlockSpec((D_MODEL, MM_COLS), lambda n, m: (0, n)),
        ],
        out_specs=pl.BlockSpec((MM_ROWS, MM_COLS), lambda n, m: (m, n)),
        out_shape=jax.ShapeDtypeStruct((N_TOK, IN_COLS), BF16),
        scratch_shapes=[pltpu.VMEM((D_MODEL, MM_COLS), BF16)],
        compiler_params=_params(("arbitrary", "arbitrary"), 58),
        name="in_proj",
    )(xn, w)


def _conv_taps(u, um1, um2, cb, w):
    conv = um2 * w[0:1] + um1 * w[1:2] + u * w[2:3]
    return cb * conv


def _conv_prompt_kernel(cb_ref, cc_ref, ch_ref, w_ref, y_ref, nb_ref, carry_ref):
    t = pl.program_id(2)

    @pl.when(t == 0)
    def _():
        carry_ref[...] = jnp.zeros_like(carry_ref)

    u = cc_ref[...].astype(F32) * ch_ref[...].astype(F32)
    prev = carry_ref[...]
    p1 = prev[SUBLANES - 1:SUBLANES]
    p2 = prev[SUBLANES - 2:SUBLANES - 1]
    row = lax.broadcasted_iota(I32, u.shape, 0)
    um1 = jnp.where(row == 0, p1, pltpu.roll(u, 1, axis=0))
    um2 = jnp.where(row == 0, p2, jnp.where(row == 1, p1, pltpu.roll(u, 2, axis=0)))
    y_ref[...] = _conv_taps(u, um1, um2, cb_ref[...].astype(F32), w_ref[...]).astype(y_ref.dtype)
    carry_ref[...] = u[CONV_ROWS - SUBLANES:CONV_ROWS]
    nb_ref[0] = u[CONV_ROWS - (CONV_K - 1):CONV_ROWS]


def _conv_prompt(proj, conv_w):
    nt = SEQ // CONV_ROWS
    nc = CONV_CH // CONV_COLS
    return pl.pallas_call(
        _conv_prompt_kernel,
        grid=(BATCH, nc, nt),
        in_specs=[
            pl.BlockSpec((CONV_ROWS, CONV_COLS), lambda b, c, t: (b * nt + t, c)),
            pl.BlockSpec((CONV_ROWS, CONV_COLS), lambda b, c, t: (b * nt + t, nc + c)),
            pl.BlockSpec((CONV_ROWS, CONV_COLS), lambda b, c, t: (b * nt + t, 2 * nc + c)),
            pl.BlockSpec((CONV_K, CONV_COLS), lambda b, c, t: (0, c)),
        ],
        out_specs=[
            pl.BlockSpec((CONV_ROWS, CONV_COLS), lambda b, c, t: (b * nt + t, c)),
            pl.BlockSpec((1, CONV_K - 1, CONV_COLS), lambda b, c, t: (b, 0, c)),
        ],
        out_shape=[
            jax.ShapeDtypeStruct((N_PROMPT, CONV_CH), BF16),
            jax.ShapeDtypeStruct((BATCH, CONV_K - 1, CONV_CH), F32),
        ],
        scratch_shapes=[pltpu.VMEM((SUBLANES, CONV_COLS), F32)],
        compiler_params=_params(("arbitrary", "arbitrary", "arbitrary"), 32),
        name="conv_prompt",
    )(proj, proj, proj, conv_w)


def _conv_sample_kernel(cb_ref, cc_ref, ch_ref, w_ref, e1_ref, e2_ref, y_ref, u_ref):
    u =cc_ref[...].astype(F32) * ch_ref[...].astype(F32)
    tpos = lax.broadcasted_iota(I32, u.shape, 0) % DEC_SEQ
    um1 = jnp.where(tpos == 0, e1_ref[...], pltpu.roll(u, 1, axis=0))
    um2 = jnp.where(tpos <= 1, e2_ref[...], pltpu.roll(u, 2, axis=0))
    y_ref[...] = _conv_taps(u, um1, um2, cb_ref[...].astype(F32), w_ref[...]).astype(y_ref.dtype)
    u_ref[...] = u


def _conv_sample(proj, conv_w, e1, e2):
    nc = CONV_CH // CONV_COLS
    rb = N_PROMPT // N_SAMPLE
    return pl.pallas_call(
        _conv_sample_kernel,
        grid=(nc,),
        in_specs=[
            pl.BlockSpec((N_SAMPLE, CONV_COLS), lambda c: (rb, c)),
            pl.BlockSpec((N_SAMPLE, CONV_COLS), lambda c: (rb, nc + c)),
            pl.BlockSpec((N_SAMPLE, CONV_COLS), lambda c: (rb, 2 * nc + c)),
            pl.BlockSpec((CONV_K, CONV_COLS), lambda c: (0, c)),
            pl.BlockSpec((N_SAMPLE, CONV_COLS), lambda c: (0, c)),
            pl.BlockSpec((N_SAMPLE, CONV_COLS), lambda c: (0, c)),
        ],
        out_specs=[
            pl.BlockSpec((N_SAMPLE, CONV_COLS), lambda c: (0, c)),
            pl.BlockSpec((N_SAMPLE, CONV_COLS), lambda c: (0, c)),
        ],
        out_shape=[
            jax.ShapeDtypeStruct((N_SAMPLE, CONV_CH), BF16),
            jax.ShapeDtypeStruct((N_SAMPLE, CONV_CH), F32),
        ],
        compiler_params=_params(("arbitrary",), 32),
        name="conv_sample",
    )(proj, proj, proj, conv_w, e1, e2)


_NT = (((1,), (1,)), ((), ()))
_TN = (((0,), (0,)), ((), ()))


def _hgrn_factors(q, fz, lb, seq_len):
    rows, width = q.shape
    one_m_lb = 1.0 - lb
    log_f = jnp.log(lb + one_m_lb * _sigmoid(fz))
    k = one_m_lb * _sigmoid(-fz)
    row = lax.broadcasted_iota(I32, (rows, width), 0)
    tpos = row % seq_len

    b = log_f
    shift = 1
    while shift < seq_len:
        b = b + jnp.where(tpos >= shift, pltpu.roll(b, shift, axis=0), 0.0)
        shift *= 2

    n_seq = rows // seq_len
    mid = (seq_len - 1) // 2
    if seq_len % SUBLANES == 0:
        b3 = b.reshape(n_seq, seq_len, width)
        b_mid = b3[:, mid:mid + 1]
        b_end = b3[:, seq_len - 1:seq_len]

        def per_row(per_chunk):
            return jnp.broadcast_to(per_chunk, b3.shape).reshape(rows, width)

        b_ref, b_last = per_row(b_mid), per_row(b_end)
        e_ref, e_last = per_row(jnp.exp(b_mid)), per_row(jnp.exp(b_end - b_mid))
    else:
        seq = row // seq_len
        b_ref = b[mid:mid + 1]
        b_last = b[seq_len - 1:seq_len]
        for j in range(1, n_seq):
            b_ref = jnp.where(seq == j, b[j * seq_len + mid:j * seq_len + mid + 1], b_ref)
            b_last = jnp.where(seq == j, b[(j + 1) * seq_len - 1:(j + 1) * seq_len], b_last)
        e_ref, e_last = jnp.exp(b_ref), jnp.exp(b_last - b_ref)

    qe = q * jnp.exp(b - b_ref)
    ke = k * jnp.exp(b_ref - b)
    return qe, ke, qe * e_ref, ke * e_last, log_f, b_last


def _split3_bf16(x):
    hi = x.astype(BF16).astype(F32)
    rest = x - hi
    mid = rest.astype(BF16).astype(F32)
    lo = (rest - mid).astype(BF16).astype(F32)
    return hi, mid, lo


def _head_out(o, gate, norm_g):
    o_n = o * lax.rsqrt(jnp.mean(o * o, axis=-1, keepdims=True) + EPS)
    return o_n * norm_g * (gate * _sigmoid(gate))


def _hgrn_prompt_kernel(q_ref, f_ref, v_ref, g_ref, lb_ref, ng_ref, y_ref, s_out_ref):
    lb = lb_ref[...]
    ng = ng_ref[...]
    rr = lax.broadcasted_iota(I32, (HG_CHUNK, HG_CHUNK), 0)
    cc = lax.broadcasted_iota(I32, (HG_CHUNK, HG_CHUNK), 1)
    causal = cc <= rr

    def block(i, state_t):
        rows = pl.ds(pl.multiple_of(i * HG_ROWS, HG_ROWS), HG_ROWS)
        qe, ke, qb, kd, _, b_last = _hgrn_factors(q_ref[rows, :].astype(F32), f_ref[rows, :].astype(F32),
                                                  lb, HG_CHUNK)
        qe, ke, qb, kd = (a.astype(BF16) for a in (qe, ke, qb, kd))
        vb = v_ref[rows, :]
        outs = []
        for c in range(HG_ROWS // HG_CHUNK):
            cs = slice(c * HG_CHUNK, (c + 1) * HG_CHUNK)
            scores = lax.dot_general(qe[cs], ke[cs], _NT, preferred_element_type=F32)
            scores = jnp.where(causal, scores, 0.0).astype(BF16)
            outs.append(jnp.dot(scores, vb[cs], preferred_element_type=F32)
                        + lax.dot_general(qb[cs], state_t.astype(BF16), _NT, preferred_element_type=F32))
            inc_t = lax.dot_general(vb[cs], kd[cs], _TN, preferred_element_type=F32)
            state_t = jnp.exp(b_last[c * HG_CHUNK:c * HG_CHUNK + 1]) * state_t + inc_t
        o = jnp.concatenate(outs, axis=0)
        y_ref[rows, :] = _head_out(o, g_ref[rows, :].astype(F32), ng).astype(y_ref.dtype)
        return state_t

    state_t = lax.fori_loop(0, SEQ // HG_ROWS, block, jnp.zeros((HG_DV, HG_DK), F32))
    s_out_ref[0, 0] = state_t.T


def _hgrn_prompt(proj, lb, norm_g):
    cq = 3 * CONV_CH // HG_DK

    def spec(off):
        return pl.BlockSpec((SEQ, HG_DK), lambda b, h: (b, cq + off * HG_HEADS + h))

    return pl.pallas_call(
        _hgrn_prompt_kernel,
        grid=(BATCH, HG_HEADS),
        in_specs=[spec(0), spec(1), spec(2), spec(3),
                  pl.BlockSpec((1, HG_DK), lambda b, h: (0, h)),
                  pl.BlockSpec((1, HG_DV), lambda b, h: (0, h))],
        out_specs=[
            pl.BlockSpec((SEQ, HG_DV), lambda b, h: (b, h)),
            pl.BlockSpec((1, 1, HG_DK, HG_DV), lambda b, h: (b, h, 0, 0)),
        ],
        out_shape=[
            jax.ShapeDtypeStruct((N_PROMPT, HG_W), BF16),
            jax.ShapeDtypeStruct((BATCH, HG_HEADS, HG_DK, HG_DV), F32),
        ],
        compiler_params=_params(("arbitrary", "arbitrary"), 32),
        name="hgrn_prompt",
    )(proj, proj, proj, proj, lb, norm_g)


def _hgrn_sample_kernel(q_ref, f_ref, v_ref, g_ref, lb_ref, ng_ref, s_ref, y_ref, s_out_ref):
    rows = HG_SAMPLE_SEQS * DEC_SEQ
    qe, ke, qb, kd, log_f, _ = _hgrn_factors(q_ref[...].astype(F32), f_ref[...].astype(F32),
                                            lb_ref[...], DEC_SEQ)
    lf_parts = _split3_bf16(log_f)
    v = v_ref[...].astype(F32)
    gate = g_ref[...].astype(F32)
    ng = ng_ref[...]

    in_a = lax.broadcasted_iota(I32, (SUBLANES, HG_DV), 0) < DEC_SEQ
    rr = lax.broadcasted_iota(I32, (SUBLANES, SUBLANES), 0)
    cc = lax.broadcasted_iota(I32, (SUBLANES, SUBLANES), 1)
    causal = (cc <= rr) & ((rr // DEC_SEQ) == (cc // DEC_SEQ))
    zero_tile = jnp.zeros((SUBLANES, HG_DV), F32)
    sel = jnp.concatenate([in_a.astype(F32), 1.0 - in_a.astype(F32)], axis=1)
    dec_rhs = jnp.concatenate([sel, sel, sel, jnp.zeros_like(sel)], axis=0).astype(BF16)

    for h in range(HG_HEADS):
        cols = slice(h * HG_DK, (h + 1) * HG_DK)
        outs = []
        for r in range(rows // SUBLANES):
            rs = slice(r * SUBLANES, (r + 1) * SUBLANES)
            s_cat = jnp.concatenate([s_ref[2 * r, h], s_ref[2 * r + 1, h]], axis=1)
            v_t = v[rs, cols]
            scores = lax.dot_general(qe[rs, cols].astype(BF16), ke[rs, cols].astype(BF16), _NT,
                                     preferred_element_type=F32)
            scores = jnp.where(causal, scores, 0.0).astype(BF16)
            o_state = jnp.dot(qb[rs, cols].astype(BF16), s_cat.astype(BF16), preferred_element_type=F32)
            outs.append(jnp.dot(scores, v_t.astype(BF16), preferred_element_type=F32)
                        + jnp.where(in_a, o_state[:, :HG_DV], o_state[:, HG_DV:]))
            inc_rhs = jnp.concatenate([jnp.where(in_a, v_t, 0.0), jnp.where(in_a, 0.0, v_t)], axis=1)
            inc = lax.dot_general(kd[rs, cols].astype(BF16), inc_rhs.astype(BF16), _TN,
                                  preferred_element_type=F32)
            dec_lhs = jnp.concatenate([p[rs, cols] for p in lf_parts] + [zero_tile], axis=0)
            decay = jnp.exp(lax.dot_general(dec_lhs.astype(BF16), dec_rhs, _TN,
                                            preferred_element_type=F32))
            s_new = decay * s_cat + inc
            s_out_ref[2 * r, h] = s_new[:, :HG_DV]
            s_out_ref[2 * r + 1, h] = s_new[:, HG_DV:]
        o = jnp.concatenate(outs, axis=0)
        y_ref[:, cols] = _head_out(o, gate[:, cols], ng[:, cols]).astype(y_ref.dtype)


def _hgrn_sample(proj, lb, norm_g, state):
    rows = HG_SAMPLE_SEQS * DEC_SEQ
    rb = N_PROMPT // rows
    cq = 3 * CONV_CH // HG_W

    def spec(off):
        return pl.BlockSpec((rows, HG_W), lambda g: (rb + g, cq + off))

    st_spec = pl.BlockSpec((HG_SAMPLE_SEQS, HG_HEADS, HG_DK, HG_DV), lambda g: (g, 0, 0, 0))
    return pl.pallas_call(
        _hgrn_sample_kernel,
        grid=(DEC_BATCH // HG_SAMPLE_SEQS,),
        in_specs=[spec(0), spec(1), spec(2), spec(3),
                  pl.BlockSpec((1, HG_W), lambda g: (0, 0)),
                  pl.BlockSpec((1, HG_W), lambda g: (0, 0)),
                  st_spec],
        out_specs=[pl.BlockSpec((rows, HG_W), lambda g: (g, 0)), st_spec],
        out_shape=[
            jax.ShapeDtypeStruct((N_SAMPLE, HG_W), BF16),
            jax.ShapeDtypeStruct((DEC_BATCH, HG_HEADS, HG_DK, HG_DV), F32),
        ],
        compiler_params=_params(("arbitrary",), 40),
        name="hgrn_sample",
    )(proj, proj, proj, proj, lb, norm_g, state)


def _out_proj_kernel(ycp_ref, yhp_ref, xp_ref, ycs_ref, yhs_ref, xs_ref, wc_ref, wh_ref, o_ref, *,
                     n_prompt_tiles):
    m = pl.program_id(1)

    def residual_mix(yc_ref, yh_ref, x_ref):
        mix = (jnp.dot(yc_ref[...], wc_ref[...], preferred_element_type=F32)
               + jnp.dot(yh_ref[...], wh_ref[...], preferred_element_type=F32))
        o_ref[...] = x_ref[...] + mix

    @pl.when(m < n_prompt_tiles)
    def _():
        residual_mix(ycp_ref, yhp_ref, xp_ref)

    @pl.when(m >= n_prompt_tiles)
    def _():
        residual_mix(ycs_ref, yhs_ref, xs_ref)


def _out_proj(yc_p, yh_p, xp, yc_s, yh_s, xs, w):
    npt = N_PROMPT // MM_ROWS

    def prompt_map(n, m):
        return jnp.minimum(m, npt - 1)

    def sample_map(n, m):
        return jnp.maximum(m - npt, 0)

    def source_specs(row_map):
        return [
            pl.BlockSpec((MM_ROWS, CONV_CH), lambda n, m: (row_map(n, m), 0)),
            pl.BlockSpec((MM_ROWS, HG_W), lambda n, m: (row_map(n, m), 0)),
            pl.BlockSpec((MM_ROWS, MM_COLS), lambda n, m: (row_map(n, m), n)),
        ]

    return pl.pallas_call(
        functools.partial(_out_proj_kernel, n_prompt_tiles=npt),
        grid=(D_MODEL // MM_COLS, N_TOK // MM_ROWS),
        in_specs=source_specs(prompt_map) + source_specs(sample_map) + [
            pl.BlockSpec((CONV_CH, MM_COLS), lambda n, m: (0, n)),
            pl.BlockSpec((HG_W, MM_COLS), lambda n, m: (1, n)),
        ],
        out_specs=pl.BlockSpec((MM_ROWS, MM_COLS), lambda n, m: (m, n)),
        out_shape=jax.ShapeDtypeStruct((N_TOK, D_MODEL), F32),
        compiler_params=_params(("arbitrary", "arbitrary"), 56),
        name="out_proj",
    )(yc_p, yh_p, xp, yc_s, yh_s, xs, w, w)


def _split_bf16(x):
    hi = x.astype(BF16)
    return hi, (x - hi.astype(F32)).astype(BF16)


def _norm_route_kernel(h_ref, g_ref, wr_hi_ref, wr_lo_ref, br_ref, hn_ref, eidx_ref, gate_ref):
    hn = _rmsnorm_rows(h_ref[...], g_ref[...])
    hn_ref[...] = hn
    hn_hi, hn_lo = _split_bf16(hn)
    logits = (jnp.dot(hn_hi, wr_hi_ref[...], preferred_element_type=F32)
              + (jnp.dot(hn_hi, wr_lo_ref[...], preferred_element_type=F32)
                 + jnp.dot(hn_lo, wr_hi_ref[...], preferred_element_type=F32))) + br_ref[...]
    lane = lax.broadcasted_iota(I32, logits.shape, 1)
    lane_f = lane.astype(F32)
    neg = -jnp.inf

    def first_argmax(vals, vmax):
        first = jnp.min(jnp.where(vals == vmax, lane_f, float(ROUTE_COLS)), axis=-1, keepdims=True)
        return first.astype(I32)

    is_grp = lane < N_GROUPS
    lg = jnp.where(is_grp, logits, neg)
    mg = jnp.max(lg, axis=-1, keepdims=True)
    g_sel = first_argmax(lg, mg)
    p_grp = 1.0 / jnp.sum(jnp.where(is_grp, jnp.exp(logits - mg), 0.0), axis=-1, keepdims=True)

    in_grp = (lane >= N_GROUPS) & (((lane - N_GROUPS) // EXP_PER_GROUP) == g_sel)
    le = jnp.where(in_grp, logits, neg)
    m1 = jnp.max(le, axis=-1, keepdims=True)
    i1 = first_argmax(le, m1)
    le2 = jnp.where(lane == i1, neg, le)
    m2 = jnp.max(le2, axis=-1, keepdims=True)
    i2 = first_argmax(le2, m2)
    e2 = jnp.exp(m2 - m1)
    gate1 = p_grp / (1.0 + e2)
    gate2 = p_grp * e2 / (1.0 + e2)
    eidx_ref[...] = jnp.where(lane == 0, i1 - N_GROUPS, jnp.where(lane == 1, i2 - N_GROUPS, 0))
    gate_ref[...] = jnp.where(lane == 0, gate1, jnp.where(lane == 1, gate2, 0.0))


def _norm_route(h, g, w_r, b_r):
    wr_hi, wr_lo = _split_bf16(w_r)
    return pl.pallas_call(
        _norm_route_kernel,
        grid=(N_TOK // NORM_ROWS,),
        in_specs=[
            pl.BlockSpec((NORM_ROWS, D_MODEL), lambda i: (i, 0)),
            pl.BlockSpec((1, D_MODEL), lambda i: (0, 0)),
            pl.BlockSpec((D_MODEL, ROUTE_COLS), lambda i: (0, 0)),
            pl.BlockSpec((D_MODEL, ROUTE_COLS), lambda i: (0, 0)),
            pl.BlockSpec((1, ROUTE_COLS), lambda i: (0, 0)),
        ],
        out_specs=[
            pl.BlockSpec((NORM_ROWS, D_MODEL), lambda i: (i, 0)),
            pl.BlockSpec((NORM_ROWS, ROUTE_COLS), lambda i: (i, 0)),
            pl.BlockSpec((NORM_ROWS, ROUTE_COLS), lambda i: (i, 0)),
        ],
        out_shape=[
            jax.ShapeDtypeStruct((N_TOK, D_MODEL), F32),
            jax.ShapeDtypeStruct((N_TOK, ROUTE_COLS), I32),
            jax.ShapeDtypeStruct((N_TOK, ROUTE_COLS), F32),
        ],
        compiler_params=_params(("arbitrary",), 48),
        name="norm_route",
    )(h, g, wr_hi, wr_lo, b_r)


def _row_copy(src_hbm, row, buf, slot, r, sem):
    return pltpu.make_async_copy(src_hbm.at[pl.ds(row, 1)], buf.at[slot, pl.ds(r, 1)], sem.at[slot])


def _gather_issue(idx_ref, base, n_groups, src_hbm, buf, slot, sem):
    def body(g, carry):
        r0 = g * SUBLANES
        for j in range(SUBLANES):
            _row_copy(src_hbm, idx_ref[base + r0 + j], buf, slot, r0 + j, sem).start()
        return carry

    lax.fori_loop(0, n_groups, body, 0)


def _gather_wait(n_groups, src_hbm, buf, slot, sem):
    def body(g, carry):
        rows = pl.ds(pl.multiple_of(g * SUBLANES, SUBLANES), SUBLANES)
        pltpu.make_async_copy(src_hbm.at[pl.ds(0, SUBLANES)], buf.at[slot, rows], sem.at[slot]).wait()
        return carry

    lax.fori_loop(0, n_groups, body, 0)


def _dispatch_kernel(tok_ref, ng_ref, hn_hbm, o_ref, buf, sem):
    b = pl.program_id(0)
    nb = pl.num_programs(0)
    slot = b % 2

    def fetch(blk, dst_slot):
        n_groups = ng_ref[blk]
        _gather_issue(tok_ref, blk * MOE_BLK, n_groups, hn_hbm, buf, dst_slot, sem)

        def zero(g, carry):
            rows = pl.ds(pl.multiple_of(g * SUBLANES, SUBLANES), SUBLANES)
            buf[dst_slot, rows, :] = jnp.zeros((SUBLANES, D_MODEL), F32)
            return carry

        lax.fori_loop(n_groups, MOE_BLK // SUBLANES, zero, 0)

    @pl.when(b == 0)
    def _():
        fetch(0, 0)

    @pl.when(b + 1 < nb)
    def _():
        fetch(b + 1, 1 - slot)

    _gather_wait(ng_ref[b], hn_hbm, buf, slot, sem)
    o_ref[...] = buf[slot].astype(o_ref.dtype)


def _dispatch(row_tok, n_groups, hn):
    return pl.pallas_call(
        _dispatch_kernel,
        grid_spec=pltpu.PrefetchScalarGridSpec(
            num_scalar_prefetch=2,
            grid=(MOE_NB,),
            in_specs=[pl.BlockSpec(memory_space=pl.ANY)],
            out_specs=pl.BlockSpec((MOE_BLK, D_MODEL), lambda b, tok, ng: (b, 0)),
            scratch_shapes=[pltpu.VMEM((2, MOE_BLK, D_MODEL), F32),
                            pltpu.SemaphoreType.DMA((2,))],
        ),
        out_shape=jax.ShapeDtypeStruct((MOE_ROWS, D_MODEL), BF16),
        compiler_params=_params(("arbitrary",), 32, disable_bounds_checks=True),
        name="moe_dispatch",
    )(row_tok, n_groups, hn)


def _stream_expert_blocks(blk0_ref, nblk_ref, n_used, in_copy, out_copy, out_buf, compute):
    e = pl.program_id(1)

    @pl.when(e == 0)
    def _():
        in_copy(0, 0).start(priority=ROW_BLOCK_DMA_PRIORITY)

    def body(g, carry):
        slot = g % 2

        @pl.when(g + 1 < n_used)
        def _():
            in_copy(g + 1, 1 - slot).start(priority=ROW_BLOCK_DMA_PRIORITY)

        in_copy(g, slot).wait()

        @pl.when(g >= 2)
        def _():
            out_copy(g - 2, slot).wait()

        compute(slot)
        out_copy(g, slot).start(priority=ROW_BLOCK_DMA_PRIORITY)
        return carry

    blk0 = blk0_ref[e]
    lax.fori_loop(blk0, blk0 + nblk_ref[e], body, 0)

    @pl.when(e == pl.num_programs(1) - 1)
    def _():
        @pl.when(n_used >= 2)
        def _():
            out_copy(n_used - 2, n_used % 2).wait()

        out_copy(n_used - 1, (n_used - 1) % 2).wait()
        out_buf[0] = jnp.zeros(out_buf.shape[1:], out_buf.dtype)

        def start_zero(g, carry):
            out_copy(g, 0).start()
            return carry

        def wait_zero(g, carry):
            out_copy(g, 0).wait()
            return carry

        lax.fori_loop(n_used, MOE_NB, start_zero, 0)
        lax.fori_loop(n_used, MOE_NB, wait_zero, 0)


def _block_rows(g):
    return pl.ds(pl.multiple_of(g * MOE_BLK, MOE_BLK), MOE_BLK)


def _dot_k_split(x, w_refs):
    kc = x.shape[1] // len(w_refs)
    acc = jnp.dot(x[:, :kc], w_refs[0][0], preferred_element_type=F32)
    for i in range(1, len(w_refs)):
        acc = acc + jnp.dot(x[:, i * kc:(i + 1) * kc], w_refs[i][0], preferred_element_type=F32)
    return acc


def _expert_up_kernel(blk0_ref, nblk_ref, nbu_ref, *refs):
    wg_refs, wu_refs = refs[:MOE_KSPLIT], refs[MOE_KSPLIT:2 * MOE_KSPLIT]
    x_hbm, h_hbm, xbuf, hbuf, sem_x, sem_h = refs[2 * MOE_KSPLIT:]
    cols = pl.ds(pl.multiple_of(pl.program_id(0) * MOE_FC, MOE_FC), MOE_FC)

    def x_copy(g, slot):
        return pltpu.make_async_copy(x_hbm.at[_block_rows(g)], xbuf.at[slot], sem_x.at[slot])

    def h_copy(g, slot):
        return pltpu.make_async_copy(hbuf.at[slot], h_hbm.at[_block_rows(g), cols], sem_h.at[slot])

    def compute(slot):
        x = xbuf[slot].astype(F32)
        g = _dot_k_split(x, wg_refs)
        u = _dot_k_split(x, wu_refs)
        hbuf[slot] = (g * _sigmoid(g) * u).astype(hbuf.dtype)

    _stream_expert_blocks(blk0_ref, nblk_ref, nbu_ref[0], x_copy, h_copy, hbuf, compute)


def _k_split_specs(k_rows, cols):
    def spec(i):
        return pl.BlockSpec((1, k_rows // MOE_KSPLIT, cols), lambda c, e, *_: (e, i, c))

    return [spec(i) for i in range(MOE_KSPLIT)]


def _expert_up(blk0, nblk, n_used, xb, w_g, w_u):
    w_specs = _k_split_specs(D_MODEL, MOE_FC)
    return pl.pallas_call(
        _expert_up_kernel,
        grid_spec=pltpu.PrefetchScalarGridSpec(
            num_scalar_prefetch=3,
            grid=(D_EXPERT // MOE_FC, N_EXPERTS),
            in_specs=w_specs + w_specs + [pl.BlockSpec(memory_space=pl.ANY)],
            out_specs=pl.BlockSpec(memory_space=pl.ANY),
            scratch_shapes=[pltpu.VMEM((2, MOE_BLK, D_MODEL), BF16),
                            pltpu.VMEM((2, MOE_BLK, MOE_FC), BF16),
                            pltpu.SemaphoreType.DMA((2,)),
                            pltpu.SemaphoreType.DMA((2,))],
        ),
        out_shape=jax.ShapeDtypeStruct((MOE_ROWS, D_EXPERT), BF16),
        compiler_params=_params(("arbitrary", "arbitrary"), 56),
        name="expert_up",
    )(blk0, nblk, n_used, *([w_g] * MOE_KSPLIT), *([w_u] * MOE_KSPLIT), xb)


def _expert_down_kernel(blk0_ref, nblk_ref, nbu_ref, *refs):
    wd_refs = refs[:MOE_KSPLIT]
    h_hbm, y_hbm, hbuf, ybuf, sem_h, sem_y = refs[MOE_KSPLIT:]
    cols =pl.ds(pl.multiple_of(pl.program_id(0) * MOE_DC, MOE_DC), MOE_DC)

    def h_copy(g, slot):
        return pltpu.make_async_copy(h_hbm.at[_block_rows(g)], hbuf.at[slot], sem_h.at[slot])

    def y_copy(g, slot):
        return pltpu.make_async_copy(ybuf.at[slot], y_hbm.at[_block_rows(g), cols], sem_y.at[slot])

    def compute(slot):
        ybuf[slot] = _dot_k_split(hbuf[slot].astype(F32), wd_refs)

    _stream_expert_blocks(blk0_ref, nblk_ref, nbu_ref[0], h_copy, y_copy, ybuf, compute)


def _expert_down(blk0, nblk, n_used, hb, w_d):
    return pl.pallas_call(
        _expert_down_kernel,
        grid_spec=pltpu.PrefetchScalarGridSpec(
            num_scalar_prefetch=3,
            grid=(D_MODEL // MOE_DC, N_EXPERTS),
            in_specs=_k_split_specs(D_EXPERT, MOE_DC) + [pl.BlockSpec(memory_space=pl.ANY)],
            out_specs=pl.BlockSpec(memory_space=pl.ANY),
            scratch_shapes=[pltpu.VMEM((2, MOE_BLK, D_EXPERT), BF16),
                            pltpu.VMEM((2, MOE_BLK, MOE_DC), F32),
                            pltpu.SemaphoreType.DMA((2,)),
                            pltpu.SemaphoreType.DMA((2,))],
        ),
        out_shape=jax.ShapeDtypeStruct((MOE_ROWS, D_MODEL), F32),
        compiler_params=_params(("arbitrary", "arbitrary"), 48),
        name="expert_down",
    )(blk0, nblk, n_used, *([w_d] * MOE_KSPLIT), hb)


def _combine_kernel(dest_ref, h_ref, gate_ref, g_ref, y_hbm, o_ref, buf, sem, *, tile0):
    i = pl.program_id(0)
    n = pl.num_programs(0)
    slot = i % 2
    rows = TOP_K * COMB_ROWS
    n_groups = rows // SUBLANES
    base = (tile0 + i) * rows

    @pl.when(i == 0)
    def _():
        _gather_issue(dest_ref, base, n_groups, y_hbm, buf, 0, sem)

    @pl.when(i + 1 < n)
    def _():
        _gather_issue(dest_ref, base + rows, n_groups, y_hbm, buf, 1 - slot, sem)

    _gather_wait(n_groups, y_hbm, buf, slot, sem)
    gates = gate_ref[...]
    ff = gates[:, 0:1] * buf[slot, 0:COMB_ROWS] + gates[:, 1:2] * buf[slot, COMB_ROWS:rows]
    o_ref[...] = _rmsnorm_rows(h_ref[...] + ff, g_ref[...])


def _combine(dest, h, gates, g_final, y_buf, tile0, n_tiles):
    return pl.pallas_call(
        functools.partial(_combine_kernel, tile0=tile0),
        grid_spec=pltpu.PrefetchScalarGridSpec(
            num_scalar_prefetch=1,
            grid=(n_tiles,),
            in_specs=[
                pl.BlockSpec((COMB_ROWS, D_MODEL), lambda i, d: (tile0 + i, 0)),
                pl.BlockSpec((COMB_ROWS, ROUTE_COLS), lambda i, d: (tile0 + i, 0)),
                pl.BlockSpec((1, D_MODEL), lambda i, d: (0, 0)),
                pl.BlockSpec(memory_space=pl.ANY),
            ],
            out_specs=pl.BlockSpec((COMB_ROWS, D_MODEL), lambda i, d: (i, 0)),
            scratch_shapes=[pltpu.VMEM((2, TOP_K * COMB_ROWS, D_MODEL), F32),
                            pltpu.SemaphoreType.DMA((2,))],
        ),
        out_shape=jax.ShapeDtypeStruct((n_tiles * COMB_ROWS, D_MODEL), F32),
        compiler_params=_params(("arbitrary",), 40, disable_bounds_checks=True),
        name="moe_combine",
    )(dest, h, gates, g_final, y_buf)


def _dispatch_plan(eidx):
    flat_e = eidx.reshape(N_ASSIGN)
    onehot = (flat_e[:, None] == jnp.arange(N_EXPERTS, dtype=I32)[None, :]).astype(I32)
    csum = jnp.cumsum(onehot, axis=0)
    counts = csum[-1]
    rank = jnp.take_along_axis(csum, flat_e[:, None], axis=1)[:, 0] - 1
    padded = ((counts + MOE_BLK - 1) // MOE_BLK) * MOE_BLK
    pad_end = jnp.cumsum(padded)
    pad_start = pad_end - padded
    dest = (pad_start[flat_e] + rank).astype(I32)
    n_used = (pad_end[-1] // MOE_BLK).astype(I32)
    blk = jnp.arange(MOE_NB, dtype=I32)
    block_e = jnp.minimum(jnp.searchsorted(pad_end, blk * MOE_BLK, side='right'), N_EXPERTS - 1)
    rows_in_blk = jnp.clip(counts[block_e] - (blk * MOE_BLK - pad_start[block_e]), 0, MOE_BLK)
    rows_in_blk = jnp.where(blk < n_used, rows_in_blk, 0)
    n_groups = ((rows_in_blk + SUBLANES - 1) // SUBLANES).astype(I32)
    row_tok = jnp.zeros((MOE_ROWS,), I32).at[dest].set(jnp.arange(N_ASSIGN, dtype=I32) // TOP_K)
    blk0 = (pad_start // MOE_BLK).astype(I32)
    nblk = (padded // MOE_BLK).astype(I32)
    return dest, row_tok, n_groups, blk0, nblk, n_used.reshape(1)


def kernel(x_prompt, x_sample, cache_conv, state_hgrn, norm_mix_g, w_in, conv_w, lb_param, hg_norm_g,
           w_out, norm_ffn_g, w_group_router, b_group_router, w_expert_router, b_expert_router,
           w_exp_gate, w_exp_up, w_exp_down, norm_final_g):
    xp = x_prompt.reshape(N_PROMPT, D_MODEL)
    xs = x_sample.reshape(N_SAMPLE, D_MODEL)
    lb = jnp.cumsum(jax.nn.softmax(lb_param.astype(F32), axis=0), axis=0)[0].reshape(1, HG_W)

    xn = _norm_in(xp, xs, norm_mix_g[0].reshape(1, D_MODEL))
    proj = _in_proj(xn, w_in[0])

    buf = cache_conv[0]
    zeros = jnp.zeros((DEC_BATCH, DEC_SEQ - 1, CONV_CH), F32)
    e1 = jnp.concatenate([buf[:, 1:2], zeros], axis=1).reshape(N_SAMPLE, CONV_CH)
    e2 = jnp.concatenate([buf, zeros[:, :DEC_SEQ - 2]], axis=1).reshape(N_SAMPLE, CONV_CH)
    yc_p, conv_p = _conv_prompt(proj, conv_w[0])
    yc_s, u_s = _conv_sample(proj, conv_w[0], e1, e2)
    conv_s = u_s.reshape(DEC_BATCH, DEC_SEQ, CONV_CH)[:, DEC_SEQ - (CONV_K - 1):]

    ng = hg_norm_g[0].reshape(1, HG_W)
    yh_p, hgrn_p = _hgrn_prompt(proj, lb, ng)
    yh_s, hgrn_s = _hgrn_sample(proj, lb, ng, state_hgrn[0])

    h = _out_proj(yc_p, yh_p, xp, yc_s, yh_s, xs, w_out[0].astype(BF16))

    pad = jnp.zeros((D_MODEL, ROUTE_COLS - N_GROUPS - N_EXPERTS), F32)
    w_r = jnp.concatenate([w_group_router[0], w_expert_router[0], pad], axis=1)
    b_r = jnp.concatenate([b_group_router[0], b_expert_router[0], pad[0]]).reshape(1, ROUTE_COLS)
    hn, eidx, gates = _norm_route(h, norm_ffn_g[0].reshape(1, D_MODEL), w_r, b_r)
    dest, row_tok, n_groups, blk0, nblk, n_used = _dispatch_plan(eidx[:, :TOP_K])
    xb = _dispatch(row_tok, n_groups, hn)
    hb = _expert_up(blk0, nblk, n_used, xb, w_exp_gate[0], w_exp_up[0])
    yb = _expert_down(blk0, nblk, n_used, hb, w_exp_down[0])

    dest_t = dest.reshape(N_TOK // COMB_ROWS, COMB_ROWS, TOP_K).transpose(0, 2, 1).reshape(N_ASSIGN)
    gf = norm_final_g.reshape(1, D_MODEL)
    y_p = _combine(dest_t, h, gates, gf, yb, 0, N_PROMPT // COMB_ROWS)
    y_s = _combine(dest_t, h, gates, gf, yb, N_PROMPT // COMB_ROWS, N_SAMPLE // COMB_ROWS)

    return (y_p.reshape(BATCH, SEQ, D_MODEL), y_s.reshape(DEC_BATCH, DEC_SEQ, D_MODEL),
            conv_p[None], hgrn_p[None], conv_s[None], hgrn_s[None])
```

```python
import functools

import jax
import jax.numpy as jnp
from jax import lax
from jax.experimental import pallas as pl
from jax.experimental.pallas import tpu as pltpu

F32 = jnp.float32
BF16 = jnp.bfloat16
I32 = jnp.int32

D_MODEL = 4096
BATCH = 4
SEQ = 2048
DEC_BATCH = 128
DEC_SEQ = 4
CONV_CH = 2048
CONV_K = 3
HG_HEADS = 16
HG_DK = 128
HG_DV = 128
HG_W = HG_HEADS * HG_DK
HG_CHUNK = 64
IN_COLS = 3 * CONV_CH + 4 * HG_W
N_GROUPS = 8
EXP_PER_GROUP = 8
N_EXPERTS = 64
TOP_K = 2
D_EXPERT = 1024
EPS = 1e-6

N_PROMPT = BATCH * SEQ
N_SAMPLE = DEC_BATCH * DEC_SEQ
N_TOK = N_PROMPT + N_SAMPLE
N_ASSIGN = N_TOK * TOP_K

LANES = 128
SUBLANES = 8
ROUTE_COLS = LANES

NORM_ROWS = 256
MM_ROWS = 512
MM_COLS = 1024
CONV_ROWS = 512
CONV_COLS = 512
HG_ROWS = 512
HG_SAMPLE_SEQS = 4
MOE_BLK = 128
MOE_NB = N_ASSIGN // MOE_BLK + N_EXPERTS
MOE_ROWS = MOE_NB * MOE_BLK
MOE_FC = 512
MOE_DC = 2048
MOE_CHUNK_BLKS = 4
MOE_NCH = N_EXPERTS + N_ASSIGN // (MOE_BLK * MOE_CHUNK_BLKS)
MOE_IN_SLOTS = 3
MOE_OUT_SLOTS = 2
MOE_K_CHUNKS = 4
COMB_ROWS = 128

MIB = 1 << 20


def _params(sem, vmem_mib, **kwargs):
    return pltpu.CompilerParams(dimension_semantics=sem, vmem_limit_bytes=vmem_mib * MIB, **kwargs)


def _sigmoid(x):
    return 1.0 / (1.0 + jnp.exp(-x))


def _rmsnorm_rows(x, g):
    return x * lax.rsqrt(jnp.mean(x * x, axis=-1, keepdims=True) + EPS) * g


def _norm_in_kernel(xp_ref, xs_ref, g_ref, o_ref, *, n_prompt_tiles):
    i = pl.program_id(0)

    @pl.when(i < n_prompt_tiles)
    def _():
        o_ref[...] = _rmsnorm_rows(xp_ref[...], g_ref[...]).astype(o_ref.dtype)

    @pl.when(i >= n_prompt_tiles)
    def _():
        o_ref[...] = _rmsnorm_rows(xs_ref[...], g_ref[...]).astype(o_ref.dtype)


def _norm_in(xp, xs, g):
    npt = N_PROMPT // NORM_ROWS
    nst = N_SAMPLE // NORM_ROWS
    return pl.pallas_call(
        functools.partial(_norm_in_kernel, n_prompt_tiles=npt),
        grid=(npt + nst,),
        in_specs=[
            pl.BlockSpec((NORM_ROWS, D_MODEL), lambda i: (jnp.minimum(i, npt - 1), 0)),
            pl.BlockSpec((NORM_ROWS, D_MODEL), lambda i: (jnp.maximum(i - npt, 0), 0)),
            pl.BlockSpec((1, D_MODEL), lambda i: (0, 0)),
        ],
        out_specs=pl.BlockSpec((NORM_ROWS, D_MODEL), lambda i: (i, 0)),
        out_shape=jax.ShapeDtypeStruct((N_TOK, D_MODEL), BF16),
        compiler_params=_params(("arbitrary",), 40),
        name="norm_in",
    )(xp, xs, g)


def _in_proj_kernel(x_ref, w_ref, o_ref, wb_ref):
    @pl.when(pl.program_id(1) == 0)
    def _():
        wb_ref[...] = w_ref[...].astype(BF16)

    o_ref[...] = jnp.dot(x_ref[...], wb_ref[...], preferred_element_type=F32).astype(o_ref.dtype)


def _in_proj(xn, w):
    return pl.pallas_call(
        _in_proj_kernel,
        grid=(IN_COLS // MM_COLS, N_TOK // MM_ROWS),
        in_specs=[
            pl.BlockSpec((MM_ROWS, D_MODEL), lambda n, m: (m, 0)),
            pl.BlockSpec((D_MODEL, MM_COLS), lambda n, m: (0, n)),
        ],
        out_specs=pl.BlockSpec((MM_ROWS, MM_COLS), lambda n, m: (m, n)),
        out_shape=jax.ShapeDtypeStruct((N_TOK, IN_COLS), BF16),
        scratch_shapes=[pltpu.VMEM((D_MODEL, MM_COLS), BF16)],
        compiler_params=_params(("arbitrary", "arbitrary"), 58),
        name="in_proj",
    )(xn, w)


def _conv_taps(u, um1, um2, cb, w):
    conv = um2 * w[0:1] + um1 * w[1:2] + u * w[2:3]
    return cb * conv


def _conv_prompt_kernel(cb_ref, cc_ref, ch_ref, w_ref, y_ref, nb_ref, carry_ref):
    t = pl.program_id(2)

    @pl.when(t == 0)
    def _():
        carry_ref[...] = jnp.zeros_like(carry_ref)

    u = cc_ref[...].astype(F32) * ch_ref[...].astype(F32)
    prev = carry_ref[...]
    p1 = prev[SUBLANES - 1:SUBLANES]
    p2 = prev[SUBLANES - 2:SUBLANES - 1]
    row = lax.broadcasted_iota(I32, u.shape, 0)
    um1 = jnp.where(row == 0, p1, pltpu.roll(u, 1, axis=0))
    um2 = jnp.where(row == 0, p2, jnp.where(row == 1, p1, pltpu.roll(u, 2, axis=0)))
    y_ref[...] = _conv_taps(u, um1, um2, cb_ref[...].astype(F32), w_ref[...]).astype(y_ref.dtype)
    carry_ref[...] = u[CONV_ROWS - SUBLANES:CONV_ROWS]
    nb_ref[0] = u[CONV_ROWS - (CONV_K - 1):CONV_ROWS]


def _conv_prompt(proj, conv_w):
    nt = SEQ // CONV_ROWS
    nc = CONV_CH // CONV_COLS
    return pl.pallas_call(
        _conv_prompt_kernel,
        grid=(BATCH, nc, nt),
        in_specs=[
            pl.BlockSpec((CONV_ROWS, CONV_COLS), lambda b, c, t: (b * nt + t, c)),
            pl.BlockSpec((CONV_ROWS, CONV_COLS), lambda b, c, t: (b * nt + t, nc + c)),
            pl.BlockSpec((CONV_ROWS, CONV_COLS), lambda b, c, t: (b * nt + t, 2 * nc + c)),
            pl.BlockSpec((CONV_K, CONV_COLS), lambda b, c, t: (0, c)),
        ],
        out_specs=[
            pl.BlockSpec((CONV_ROWS, CONV_COLS), lambda b, c, t: (b * nt + t, c)),
            pl.BlockSpec((1, CONV_K - 1, CONV_COLS), lambda b, c, t: (b, 0, c)),
        ],
        out_shape=[
            jax.ShapeDtypeStruct((N_PROMPT, CONV_CH), BF16),
            jax.ShapeDtypeStruct((BATCH, CONV_K - 1, CONV_CH), F32),
        ],
        scratch_shapes=[pltpu.VMEM((SUBLANES, CONV_COLS), F32)],
        compiler_params=_params(("arbitrary", "arbitrary", "arbitrary"), 32),
        name="conv_prompt",
    )(proj, proj, proj, conv_w)


def _conv_sample_kernel(cb_ref, cc_ref, ch_ref, w_ref, e1_ref, e2_ref, y_ref, u_ref):
    u =cc_ref[...].astype(F32) * ch_ref[...].astype(F32)
    tpos = lax.broadcasted_iota(I32, u.shape, 0) % DEC_SEQ
    um1 = jnp.where(tpos == 0, e1_ref[...], pltpu.roll(u, 1, axis=0))
    um2 = jnp.where(tpos <= 1, e2_ref[...], pltpu.roll(u, 2, axis=0))
    y_ref[...] = _conv_taps(u, um1, um2, cb_ref[...].astype(F32), w_ref[...]).astype(y_ref.dtype)
    u_ref[...] = u


def _conv_sample(proj, conv_w, e1, e2):
    nc = CONV_CH // CONV_COLS
    rb = N_PROMPT // N_SAMPLE
    return pl.pallas_call(
        _conv_sample_kernel,
        grid=(nc,),
        in_specs=[
            pl.BlockSpec((N_SAMPLE, CONV_COLS), lambda c: (rb, c)),
            pl.BlockSpec((N_SAMPLE, CONV_COLS), lambda c: (rb, nc + c)),
            pl.BlockSpec((N_SAMPLE, CONV_COLS), lambda c: (rb, 2 * nc + c)),
            pl.BlockSpec((CONV_K, CONV_COLS), lambda c: (0, c)),
            pl.BlockSpec((N_SAMPLE, CONV_COLS), lambda c: (0, c)),
            pl.BlockSpec((N_SAMPLE, CONV_COLS), lambda c: (0, c)),
        ],
        out_specs=[
            pl.BlockSpec((N_SAMPLE, CONV_COLS), lambda c: (0, c)),
            pl.BlockSpec((N_SAMPLE, CONV_COLS), lambda c: (0, c)),
        ],
        out_shape=[
            jax.ShapeDtypeStruct((N_SAMPLE, CONV_CH), BF16),
            jax.ShapeDtypeStruct((N_SAMPLE, CONV_CH), F32),
        ],
        compiler_params=_params(("arbitrary",), 32),
        name="conv_sample",
    )(proj, proj, proj, conv_w, e1, e2)


_NT = (((1,), (1,)), ((), ()))
_TN = (((0,), (0,)), ((), ()))


def _hgrn_factors(q, fz, lb, seq_len):
    rows, width = q.shape
    one_m_lb = 1.0 - lb
    log_f = jnp.log(lb + one_m_lb * _sigmoid(fz))
    k = one_m_lb * _sigmoid(-fz)
    row = lax.broadcasted_iota(I32, (rows, width), 0)
    tpos = row % seq_len

    b = log_f
    shift = 1
    while shift < seq_len:
        b = b + jnp.where(tpos >= shift, pltpu.roll(b, shift, axis=0), 0.0)
        shift *= 2

    n_seq = rows // seq_len
    mid = (seq_len - 1) // 2
    if seq_len % SUBLANES == 0:
        b3 = b.reshape(n_seq, seq_len, width)
        b_mid = b3[:, mid:mid + 1]
        b_end = b3[:, seq_len - 1:seq_len]

        def per_row(per_chunk):
            return jnp.broadcast_to(per_chunk, b3.shape).reshape(rows, width)

        b_ref, b_last = per_row(b_mid), per_row(b_end)
        e_ref, e_last = per_row(jnp.exp(b_mid)), per_row(jnp.exp(b_end - b_mid))
    else:
        seq = row // seq_len
        b_ref = b[mid:mid + 1]
        b_last = b[seq_len - 1:seq_len]
        for j in range(1, n_seq):
            b_ref = jnp.where(seq == j, b[j * seq_len + mid:j * seq_len + mid + 1], b_ref)
            b_last = jnp.where(seq == j, b[(j + 1) * seq_len - 1:(j + 1) * seq_len], b_last)
        e_ref, e_last = jnp.exp(b_ref), jnp.exp(b_last - b_ref)

    qe = q * jnp.exp(b - b_ref)
    ke = k * jnp.exp(b_ref - b)
    return qe, ke, qe * e_ref, ke * e_last, log_f, b_last


def _split3_bf16(x):
    hi = x.astype(BF16).astype(F32)
    rest = x - hi
    mid = rest.astype(BF16).astype(F32)
    lo = (rest - mid).astype(BF16).astype(F32)
    return hi, mid, lo


def _head_out(o, gate, norm_g):
    o_n = o * lax.rsqrt(jnp.mean(o * o, axis=-1, keepdims=True) + EPS)
    return o_n * norm_g * (gate * _sigmoid(gate))


def _hgrn_prompt_kernel(q_ref, f_ref, v_ref, g_ref, lb_ref, ng_ref, y_ref, s_out_ref):
    lb = lb_ref[...]
    ng = ng_ref[...]
    rr = lax.broadcasted_iota(I32, (HG_CHUNK, HG_CHUNK), 0)
    cc = lax.broadcasted_iota(I32, (HG_CHUNK, HG_CHUNK), 1)
    causal = cc <= rr

    def block(i, state_t):
        rows = pl.ds(pl.multiple_of(i * HG_ROWS, HG_ROWS), HG_ROWS)
        qe, ke, qb, kd, _, b_last = _hgrn_factors(q_ref[rows, :].astype(F32), f_ref[rows, :].astype(F32),
                                                  lb, HG_CHUNK)
        qe, ke, qb, kd = (a.astype(BF16) for a in (qe, ke, qb, kd))
        vb = v_ref[rows, :]
        outs = []
        for c in range(HG_ROWS // HG_CHUNK):
            cs = slice(c * HG_CHUNK, (c + 1) * HG_CHUNK)
            scores = lax.dot_general(qe[cs], ke[cs], _NT, preferred_element_type=F32)
            scores = jnp.where(causal, scores, 0.0).astype(BF16)
            outs.append(jnp.dot(scores, vb[cs], preferred_element_type=F32)
                        + lax.dot_general(qb[cs], state_t.astype(BF16), _NT, preferred_element_type=F32))
            inc_t = lax.dot_general(vb[cs], kd[cs], _TN, preferred_element_type=F32)
            state_t = jnp.exp(b_last[c * HG_CHUNK:c * HG_CHUNK + 1]) * state_t + inc_t
        o = jnp.concatenate(outs, axis=0)
        y_ref[rows, :] = _head_out(o, g_ref[rows, :].astype(F32), ng).astype(y_ref.dtype)
        return state_t

    state_t = lax.fori_loop(0, SEQ // HG_ROWS, block, jnp.zeros((HG_DV, HG_DK), F32))
    s_out_ref[0, 0] = state_t.T


def _hgrn_prompt(proj, lb, norm_g):
    cq = 3 * CONV_CH // HG_DK

    def spec(off):
        return pl.BlockSpec((SEQ, HG_DK), lambda b, h: (b, cq + off * HG_HEADS + h))

    return pl.pallas_call(
        _hgrn_prompt_kernel,
        grid=(BATCH, HG_HEADS),
        in_specs=[spec(0), spec(1), spec(2), spec(3),
                  pl.BlockSpec((1, HG_DK), lambda b, h: (0, h)),
                  pl.BlockSpec((1, HG_DV), lambda b, h: (0, h))],
        out_specs=[
            pl.BlockSpec((SEQ, HG_DV), lambda b, h: (b, h)),
            pl.BlockSpec((1, 1, HG_DK, HG_DV), lambda b, h: (b, h, 0, 0)),
        ],
        out_shape=[
            jax.ShapeDtypeStruct((N_PROMPT, HG_W), BF16),
            jax.ShapeDtypeStruct((BATCH, HG_HEADS, HG_DK, HG_DV), F32),
        ],
        compiler_params=_params(("arbitrary", "arbitrary"), 32),
        name="hgrn_prompt",
    )(proj, proj, proj, proj, lb, norm_g)


def _hgrn_sample_kernel(q_ref, f_ref, v_ref, g_ref, lb_ref, ng_ref, s_ref, y_ref, s_out_ref):
    rows = HG_SAMPLE_SEQS * DEC_SEQ
    qe, ke, qb, kd, log_f, _ = _hgrn_factors(q_ref[...].astype(F32), f_ref[...].astype(F32),
                                            lb_ref[...], DEC_SEQ)
    lf_parts = _split3_bf16(log_f)
    v = v_ref[...].astype(F32)
    gate = g_ref[...].astype(F32)
    ng = ng_ref[...]

    in_a = lax.broadcasted_iota(I32, (SUBLANES, HG_DV), 0) < DEC_SEQ
    rr = lax.broadcasted_iota(I32, (SUBLANES, SUBLANES), 0)
    cc = lax.broadcasted_iota(I32, (SUBLANES, SUBLANES), 1)
    causal = (cc <= rr) & ((rr // DEC_SEQ) == (cc // DEC_SEQ))
    zero_tile = jnp.zeros((SUBLANES, HG_DV), F32)
    sel = jnp.concatenate([in_a.astype(F32), 1.0 - in_a.astype(F32)], axis=1)
    dec_rhs = jnp.concatenate([sel, sel, sel, jnp.zeros_like(sel)], axis=0).astype(BF16)

    for h in range(HG_HEADS):
        cols = slice(h * HG_DK, (h + 1) * HG_DK)
        outs = []
        for r in range(rows // SUBLANES):
            rs = slice(r * SUBLANES, (r + 1) * SUBLANES)
            s_cat = jnp.concatenate([s_ref[2 * r, h], s_ref[2 * r + 1, h]], axis=1)
            v_t = v[rs, cols]
            scores = lax.dot_general(qe[rs, cols].astype(BF16), ke[rs, cols].astype(BF16), _NT,
                                     preferred_element_type=F32)
            scores = jnp.where(causal, scores, 0.0).astype(BF16)
            o_state = jnp.dot(qb[rs, cols].astype(BF16), s_cat.astype(BF16), preferred_element_type=F32)
            outs.append(jnp.dot(scores, v_t.astype(BF16), preferred_element_type=F32)
                        + jnp.where(in_a, o_state[:, :HG_DV], o_state[:, HG_DV:]))
            inc_rhs = jnp.concatenate([jnp.where(in_a, v_t, 0.0), jnp.where(in_a, 0.0, v_t)], axis=1)
            inc = lax.dot_general(kd[rs, cols].astype(BF16), inc_rhs.astype(BF16), _TN,
                                  preferred_element_type=F32)
            dec_lhs = jnp.concatenate([p[rs, cols] for p in lf_parts] + [zero_tile], axis=0)
            decay = jnp.exp(lax.dot_general(dec_lhs.astype(BF16), dec_rhs, _TN,
                                            preferred_element_type=F32))
            s_new = decay * s_cat + inc
            s_out_ref[2 * r, h] = s_new[:, :HG_DV]
            s_out_ref[2 * r + 1, h] = s_new[:, HG_DV:]
        o = jnp.concatenate(outs, axis=0)
        y_ref[:, cols] = _head_out(o, gate[:, cols], ng[:, cols]).astype(y_ref.dtype)


def _hgrn_sample(proj, lb, norm_g, state):
    rows = HG_SAMPLE_SEQS * DEC_SEQ
    rb = N_PROMPT // rows
    cq = 3 * CONV_CH // HG_W

    def spec(off):
        return pl.BlockSpec((rows, HG_W), lambda g: (rb + g, cq + off))

    st_spec = pl.BlockSpec((HG_SAMPLE_SEQS, HG_HEADS, HG_DK, HG_DV), lambda g: (g, 0, 0, 0))
    return pl.pallas_call(
        _hgrn_sample_kernel,
        grid=(DEC_BATCH // HG_SAMPLE_SEQS,),
        in_specs=[spec(0), spec(1), spec(2), spec(3),
                  pl.BlockSpec((1, HG_W), lambda g: (0, 0)),
                  pl.BlockSpec((1, HG_W), lambda g: (0, 0)),
                  st_spec],
        out_specs=[pl.BlockSpec((rows, HG_W), lambda g: (g, 0)), st_spec],
        out_shape=[
            jax.ShapeDtypeStruct((N_SAMPLE, HG_W), BF16),
            jax.ShapeDtypeStruct((DEC_BATCH, HG_HEADS, HG_DK, HG_DV), F32),
        ],
        compiler_params=_params(("arbitrary",), 40),
        name="hgrn_sample",
    )(proj, proj, proj, proj, lb, norm_g, state)


def _out_proj_kernel(ycp_ref, yhp_ref, xp_ref, ycs_ref, yhs_ref, xs_ref, wc_ref, wh_ref, o_ref, *,
                     n_prompt_tiles):
    m = pl.program_id(1)

    def residual_mix(yc_ref, yh_ref, x_ref):
        mix = (jnp.dot(yc_ref[...], wc_ref[...], preferred_element_type=F32)
               + jnp.dot(yh_ref[...], wh_ref[...], preferred_element_type=F32))
        o_ref[...] = x_ref[...] + mix

    @pl.when(m < n_prompt_tiles)
    def _():
        residual_mix(ycp_ref, yhp_ref, xp_ref)

    @pl.when(m >= n_prompt_tiles)
    def _():
        residual_mix(ycs_ref, yhs_ref, xs_ref)


def _out_proj(yc_p, yh_p, xp, yc_s, yh_s, xs, w):
    npt = N_PROMPT // MM_ROWS

    def prompt_map(n, m):
        return jnp.minimum(m, npt - 1)

    def sample_map(n, m):
        return jnp.maximum(m - npt, 0)

    def source_specs(row_map):
        return [
            pl.BlockSpec((MM_ROWS, CONV_CH), lambda n, m: (row_map(n, m), 0)),
            pl.BlockSpec((MM_ROWS, HG_W), lambda n, m: (row_map(n, m), 0)),
            pl.BlockSpec((MM_ROWS, MM_COLS), lambda n, m: (row_map(n, m), n)),
        ]

    return pl.pallas_call(
        functools.partial(_out_proj_kernel, n_prompt_tiles=npt),
        grid=(D_MODEL // MM_COLS, N_TOK // MM_ROWS),
        in_specs=source_specs(prompt_map) + source_specs(sample_map) + [
            pl.BlockSpec((CONV_CH, MM_COLS), lambda n, m: (0, n)),
            pl.BlockSpec((HG_W, MM_COLS), lambda n, m: (1, n)),
        ],
        out_specs=pl.BlockSpec((MM_ROWS, MM_COLS), lambda n, m: (m, n)),
        out_shape=jax.ShapeDtypeStruct((N_TOK, D_MODEL), F32),
        compiler_params=_params(("arbitrary", "arbitrary"), 56),
        name="out_proj",
    )(yc_p, yh_p, xp, yc_s, yh_s, xs, w, w)


def _split_bf16(x):
    hi = x.astype(BF16)
    return hi, (x - hi.astype(F32)).astype(BF16)


def _norm_route_kernel(h_ref, g_ref, wr_hi_ref, wr_lo_ref, br_ref, hn_ref, eidx_ref, gate_ref):
    hn = _rmsnorm_rows(h_ref[...], g_ref[...])
    hn_ref[...] = hn
    hn_hi, hn_lo = _split_bf16(hn)
    logits = (jnp.dot(hn_hi, wr_hi_ref[...], preferred_element_type=F32)
              + (jnp.dot(hn_hi, wr_lo_ref[...], preferred_element_type=F32)
                 + jnp.dot(hn_lo, wr_hi_ref[...], preferred_element_type=F32))) + br_ref[...]
    lane = lax.broadcasted_iota(I32, logits.shape, 1)
    lane_f = lane.astype(F32)
    neg = -jnp.inf

    def first_argmax(vals, vmax):
        first = jnp.min(jnp.where(vals == vmax, lane_f, float(ROUTE_COLS)), axis=-1, keepdims=True)
        return first.astype(I32)

    is_grp = lane < N_GROUPS
    lg = jnp.where(is_grp, logits, neg)
    mg = jnp.max(lg, axis=-1, keepdims=True)
    g_sel = first_argmax(lg, mg)
    p_grp = 1.0 / jnp.sum(jnp.where(is_grp, jnp.exp(logits - mg), 0.0), axis=-1, keepdims=True)

    in_grp = (lane >= N_GROUPS) & (((lane - N_GROUPS) // EXP_PER_GROUP) == g_sel)
    le = jnp.where(in_grp, logits, neg)
    m1 = jnp.max(le, axis=-1, keepdims=True)
    i1 = first_argmax(le, m1)
    le2 = jnp.where(lane == i1, neg, le)
    m2 = jnp.max(le2, axis=-1, keepdims=True)
    i2 = first_argmax(le2, m2)
    e2 = jnp.exp(m2 - m1)
    gate1 = p_grp / (1.0 + e2)
    gate2 = p_grp * e2 / (1.0 + e2)
    eidx_ref[...] = jnp.where(lane == 0, i1 - N_GROUPS, jnp.where(lane == 1, i2 - N_GROUPS, 0))
    gate_ref[...] = jnp.where(lane == 0, gate1, jnp.where(lane == 1, gate2, 0.0))


def _norm_route(h, g, w_r, b_r):
    wr_hi, wr_lo = _split_bf16(w_r)
    return pl.pallas_call(
        _norm_route_kernel,
        grid=(N_TOK // NORM_ROWS,),
        in_specs=[
            pl.BlockSpec((NORM_ROWS, D_MODEL), lambda i: (i, 0)),
            pl.BlockSpec((1, D_MODEL), lambda i: (0, 0)),
            pl.BlockSpec((D_MODEL, ROUTE_COLS), lambda i: (0, 0)),
            pl.BlockSpec((D_MODEL, ROUTE_COLS), lambda i: (0, 0)),
            pl.BlockSpec((1, ROUTE_COLS), lambda i: (0, 0)),
        ],
        out_specs=[
            pl.BlockSpec((NORM_ROWS, D_MODEL), lambda i: (i, 0)),
            pl.BlockSpec((NORM_ROWS, ROUTE_COLS), lambda i: (i, 0)),
            pl.BlockSpec((NORM_ROWS, ROUTE_COLS), lambda i: (i, 0)),
        ],
        out_shape=[
            jax.ShapeDtypeStruct((N_TOK, D_MODEL), F32),
            jax.ShapeDtypeStruct((N_TOK, ROUTE_COLS), I32),
            jax.ShapeDtypeStruct((N_TOK, ROUTE_COLS), F32),
        ],
        compiler_params=_params(("arbitrary",), 48),
        name="norm_route",
    )(h, g, wr_hi, wr_lo, b_r)


def _row_copy(src_hbm, row, buf, slot, r, sem):
    return pltpu.make_async_copy(src_hbm.at[pl.ds(row, 1)], buf.at[slot, pl.ds(r, 1)], sem.at[slot])


def _gather_issue(idx_ref, base, n_groups, src_hbm, buf, slot, sem):
    def body(g, carry):
        r0 = g * SUBLANES
        for j in range(SUBLANES):
            _row_copy(src_hbm, idx_ref[base + r0 + j], buf, slot, r0 + j, sem).start()
        return carry

    lax.fori_loop(0, n_groups, body, 0)


def _gather_wait(n_groups, src_hbm, buf, slot, sem):
    def body(g, carry):
        rows = pl.ds(pl.multiple_of(g * SUBLANES, SUBLANES), SUBLANES)
        pltpu.make_async_copy(src_hbm.at[pl.ds(0, SUBLANES)], buf.at[slot, rows], sem.at[slot]).wait()
        return carry

    lax.fori_loop(0, n_groups, body, 0)


def _dispatch_kernel(tok_ref, ng_ref, hn_hbm, o_ref, buf, sem):
    b = pl.program_id(0)
    nb = pl.num_programs(0)
    slot = b % 2

    def fetch(blk, dst_slot):
        n_groups = ng_ref[blk]
        _gather_issue(tok_ref, blk * MOE_BLK, n_groups, hn_hbm, buf, dst_slot, sem)

        def zero(g, carry):
            rows = pl.ds(pl.multiple_of(g * SUBLANES, SUBLANES), SUBLANES)
            buf[dst_slot, rows, :] = jnp.zeros((SUBLANES, D_MODEL), F32)
            return carry

        lax.fori_loop(n_groups, MOE_BLK // SUBLANES, zero, 0)

    @pl.when(b == 0)
    def _():
        fetch(0, 0)

    @pl.when(b + 1 < nb)
    def _():
        fetch(b + 1, 1 - slot)

    _gather_wait(ng_ref[b], hn_hbm, buf, slot, sem)
    o_ref[...] = buf[slot].astype(o_ref.dtype)


def _dispatch(row_tok, n_groups, hn):
    return pl.pallas_call(
        _dispatch_kernel,
        grid_spec=pltpu.PrefetchScalarGridSpec(
            num_scalar_prefetch=2,
            grid=(MOE_NB,),
            in_specs=[pl.BlockSpec(memory_space=pl.ANY)],
            out_specs=pl.BlockSpec((MOE_BLK, D_MODEL), lambda b, tok, ng: (b, 0)),
            scratch_shapes=[pltpu.VMEM((2, MOE_BLK, D_MODEL), F32),
                            pltpu.SemaphoreType.DMA((2,))],
        ),
        out_shape=jax.ShapeDtypeStruct((MOE_ROWS, D_MODEL), BF16),
        compiler_params=_params(("arbitrary",), 32, disable_bounds_checks=True),
        name="moe_dispatch",
    )(row_tok, n_groups, hn)


def _stream_expert_chunks(blk0_ref, n_ref, n_used, in_copy, out_copy, out_buf, compute):
    c = pl.program_id(1)
    n_chunks = pl.num_programs(1)

    def for_blocks(chunk, fn):
        blk0 = blk0_ref[chunk]

        def body(i, carry):
            fn(blk0 + i, i)
            return carry

        lax.fori_loop(0, n_ref[chunk], body, 0)

    def start_loads(chunk):
        for_blocks(chunk, lambda g, i: in_copy(g, chunk % MOE_IN_SLOTS, i).start())

    @pl.when(c == 0)
    def _():
        start_loads(0)
        start_loads(1)

    @pl.when(c + 2 < n_chunks)
    def _():
        start_loads(c + 2)

    in_slot = c % MOE_IN_SLOTS
    out_slot = c % MOE_OUT_SLOTS
    for_blocks(c, lambda g, i: in_copy(g, in_slot, i).wait())

    @pl.when(c >= MOE_OUT_SLOTS)
    def _():
        for_blocks(c - MOE_OUT_SLOTS, lambda g, i: out_copy(g, out_slot, i).wait())

    for m in range(1, MOE_CHUNK_BLKS + 1):
        @pl.when(n_ref[c] == m)
        def _(m=m):
            compute(m, in_slot, out_slot)

    for_blocks(c, lambda g, i: out_copy(g, out_slot, i).start())

    @pl.when(c == n_chunks - 1)
    def _():
        for chunk in (c - 1, c):
            for_blocks(chunk, lambda g, i: out_copy(g, chunk % MOE_OUT_SLOTS, i).wait())
        out_buf[0, 0:MOE_BLK] = jnp.zeros((MOE_BLK,) + out_buf.shape[2:], out_buf.dtype)

        def start_zero(g, carry):
            out_copy(g, 0, 0).start()
            return carry

        def wait_zero(g, carry):
            out_copy(g, 0, 0).wait()
            return carry

        lax.fori_loop(n_used, MOE_NB, start_zero, 0)
        lax.fori_loop(n_used, MOE_NB, wait_zero, 0)


def _block_rows(g):
    return pl.ds(pl.multiple_of(g * MOE_BLK, MOE_BLK), MOE_BLK)


def _slot_rows(i):
    return pl.ds(pl.multiple_of(i * MOE_BLK, MOE_BLK), MOE_BLK)


def _expert_up_kernel(blk0_ref, n_ref, e_ref, nbu_ref, wg_ref, wu_ref, x_hbm, h_hbm, xbuf, hbuf, sem_x, sem_h):
    del e_ref
    cols = pl.ds(pl.multiple_of(pl.program_id(0) * MOE_FC, MOE_FC), MOE_FC)

    def x_copy(g, slot, i):
        return pltpu.make_async_copy(x_hbm.at[_block_rows(g)], xbuf.at[slot, _slot_rows(i)], sem_x.at[slot])

    def h_copy(g, slot, i):
        return pltpu.make_async_copy(hbuf.at[slot, _slot_rows(i)], h_hbm.at[_block_rows(g), cols],
                                     sem_h.at[slot])

    def compute(m, in_slot, out_slot):
        rows = m * MOE_BLK
        kc = D_MODEL // MOE_K_CHUNKS
        g = u = None
        for i in range(MOE_K_CHUNKS):
            ks = slice(i * kc, (i + 1) * kc)
            xk = xbuf[in_slot, :rows, ks].astype(F32)
            gi = jnp.dot(xk, wg_ref[0, ks, :], preferred_element_type=F32)
            ui = jnp.dot(xk, wu_ref[0, ks, :], preferred_element_type=F32)
            g, u = (gi, ui) if g is None else (g + gi, u + ui)
        hbuf[out_slot, :rows] = (g * _sigmoid(g) * u).astype(hbuf.dtype)

    _stream_expert_chunks(blk0_ref, n_ref, nbu_ref[0], x_copy, h_copy, hbuf, compute)


def _chunk_scratch(in_cols, in_dtype, out_cols, out_dtype):
    rows = MOE_CHUNK_BLKS * MOE_BLK
    return [pltpu.VMEM((MOE_IN_SLOTS, rows, in_cols), in_dtype),
            pltpu.VMEM((MOE_OUT_SLOTS, rows, out_cols), out_dtype),
            pltpu.SemaphoreType.DMA((MOE_IN_SLOTS,)),
            pltpu.SemaphoreType.DMA((MOE_OUT_SLOTS,))]


def _expert_up(chunks, n_used, xb, w_g, w_u):
    def w_map(j, c, blk0, n, e, nbu):
        return (e[c], 0, j)

    return pl.pallas_call(
        _expert_up_kernel,
        grid_spec=pltpu.PrefetchScalarGridSpec(
            num_scalar_prefetch=4,
            grid=(D_EXPERT // MOE_FC, MOE_NCH),
            in_specs=[pl.BlockSpec((1, D_MODEL, MOE_FC), w_map),
                      pl.BlockSpec((1, D_MODEL, MOE_FC), w_map),
                      pl.BlockSpec(memory_space=pl.ANY)],
            out_specs=pl.BlockSpec(memory_space=pl.ANY),
            scratch_shapes=_chunk_scratch(D_MODEL, BF16, MOE_FC, BF16),
        ),
        out_shape=jax.ShapeDtypeStruct((MOE_ROWS, D_EXPERT), BF16),
        compiler_params=_params(("arbitrary", "arbitrary"), 58),
        name="expert_up",
    )(*chunks, n_used, w_g, w_u, xb)


def _expert_down_kernel(blk0_ref, n_ref, e_ref, nbu_ref, wd_ref, h_hbm, y_hbm, hbuf, ybuf, sem_h, sem_y):
    del e_ref
    cols = pl.ds(pl.multiple_of(pl.program_id(0) * MOE_DC, MOE_DC), MOE_DC)

    def h_copy(g, slot, i):
        return pltpu.make_async_copy(h_hbm.at[_block_rows(g)], hbuf.at[slot, _slot_rows(i)], sem_h.at[slot])

    def y_copy(g, slot, i):
        return pltpu.make_async_copy(ybuf.at[slot, _slot_rows(i)], y_hbm.at[_block_rows(g), cols],
                                     sem_y.at[slot])

    def compute(m, in_slot, out_slot):
        rows = m * MOE_BLK
        ybuf[out_slot, :rows] = jnp.dot(hbuf[in_slot, :rows].astype(F32), wd_ref[0],
                                        preferred_element_type=F32)

    _stream_expert_chunks(blk0_ref, n_ref, nbu_ref[0], h_copy, y_copy, ybuf, compute)


def _expert_down(chunks, n_used, hb, w_d):
    return pl.pallas_call(
        _expert_down_kernel,
        grid_spec=pltpu.PrefetchScalarGridSpec(
            num_scalar_prefetch=4,
            grid=(D_MODEL // MOE_DC, MOE_NCH),
            in_specs=[pl.BlockSpec((1, D_EXPERT, MOE_DC), lambda n, c, blk0, nb, e, nbu: (e[c], 0, n)),
                      pl.BlockSpec(memory_space=pl.ANY)],
            out_specs=pl.BlockSpec(memory_space=pl.ANY),
            scratch_shapes=_chunk_scratch(D_EXPERT, BF16, MOE_DC, F32),
        ),
        out_shape=jax.ShapeDtypeStruct((MOE_ROWS, D_MODEL), F32),
        compiler_params=_params(("arbitrary", "arbitrary"), 48),
        name="expert_down",
    )(*chunks, n_used, w_d, hb)


def _combine_kernel(dest_ref, h_ref, gate_ref, g_ref, y_hbm, o_ref, buf, sem, *, tile0):
    i = pl.program_id(0)
    n = pl.num_programs(0)
    slot = i % 2
    rows = TOP_K * COMB_ROWS
    n_groups = rows // SUBLANES
    base = (tile0 + i) * rows

    @pl.when(i == 0)
    def _():
        _gather_issue(dest_ref, base, n_groups, y_hbm, buf, 0, sem)

    @pl.when(i + 1 < n)
    def _():
        _gather_issue(dest_ref, base + rows, n_groups, y_hbm, buf, 1 - slot, sem)

    _gather_wait(n_groups, y_hbm, buf, slot, sem)
    gates = gate_ref[...]
    ff = gates[:, 0:1] * buf[slot, 0:COMB_ROWS] + gates[:, 1:2] * buf[slot, COMB_ROWS:rows]
    o_ref[...] = _rmsnorm_rows(h_ref[...] + ff, g_ref[...])


def _combine(dest, h, gates, g_final, y_buf, tile0, n_tiles):
    return pl.pallas_call(
        functools.partial(_combine_kernel, tile0=tile0),
        grid_spec=pltpu.PrefetchScalarGridSpec(
            num_scalar_prefetch=1,
            grid=(n_tiles,),
            in_specs=[
                pl.BlockSpec((COMB_ROWS, D_MODEL), lambda i, d: (tile0 + i, 0)),
                pl.BlockSpec((COMB_ROWS, ROUTE_COLS), lambda i, d: (tile0 + i, 0)),
                pl.BlockSpec((1, D_MODEL), lambda i, d: (0, 0)),
                pl.BlockSpec(memory_space=pl.ANY),
            ],
            out_specs=pl.BlockSpec((COMB_ROWS, D_MODEL), lambda i, d: (i, 0)),
            scratch_shapes=[pltpu.VMEM((2, TOP_K * COMB_ROWS, D_MODEL), F32),
                            pltpu.SemaphoreType.DMA((2,))],
        ),
        out_shape=jax.ShapeDtypeStruct((n_tiles * COMB_ROWS, D_MODEL), F32),
        compiler_params=_params(("arbitrary",), 40, disable_bounds_checks=True),
        name="moe_combine",
    )(dest, h, gates, g_final, y_buf)


def _dispatch_plan(eidx):
    flat_e = eidx.reshape(N_ASSIGN)
    onehot = (flat_e[:, None] == jnp.arange(N_EXPERTS, dtype=I32)[None, :]).astype(I32)
    csum = jnp.cumsum(onehot, axis=0)
    counts = csum[-1]
    rank = jnp.take_along_axis(csum, flat_e[:, None], axis=1)[:, 0] - 1
    padded = ((counts + MOE_BLK - 1) // MOE_BLK) * MOE_BLK
    pad_end = jnp.cumsum(padded)
    pad_start = pad_end - padded
    dest = (pad_start[flat_e] + rank).astype(I32)
    n_used = (pad_end[-1] // MOE_BLK).astype(I32)
    blk = jnp.arange(MOE_NB, dtype=I32)
    block_e = jnp.minimum(jnp.searchsorted(pad_end, blk * MOE_BLK, side='right'), N_EXPERTS - 1)
    rows_in_blk = jnp.clip(counts[block_e] - (blk * MOE_BLK - pad_start[block_e]), 0, MOE_BLK)
    rows_in_blk = jnp.where(blk < n_used, rows_in_blk, 0)
    n_groups = ((rows_in_blk + SUBLANES - 1) // SUBLANES).astype(I32)
    row_tok = jnp.zeros((MOE_ROWS,), I32).at[dest].set(jnp.arange(N_ASSIGN, dtype=I32) // TOP_K)
    blk0 = pad_start // MOE_BLK
    nblk = padded // MOE_BLK
    nch = (nblk + MOE_CHUNK_BLKS - 1) // MOE_CHUNK_BLKS
    ch_end = jnp.cumsum(nch)
    ch = jnp.arange(MOE_NCH, dtype=I32)
    ch_e = jnp.minimum(jnp.searchsorted(ch_end, ch, side='right'), N_EXPERTS - 1)
    local = ch - (ch_end - nch)[ch_e]
    live = ch < ch_end[-1]
    ch_blk0 = (blk0[ch_e] + MOE_CHUNK_BLKS * local).astype(I32)
    ch_n = jnp.where(live, jnp.clip(nblk[ch_e] - MOE_CHUNK_BLKS * local, 0, MOE_CHUNK_BLKS), 0).astype(I32)
    ch_e = jnp.where(live, ch_e, ch_e[ch_end[-1] - 1]).astype(I32)
    return dest, row_tok, n_groups, (ch_blk0, ch_n, ch_e), n_used.reshape(1)


def kernel(x_prompt, x_sample, cache_conv, state_hgrn, norm_mix_g, w_in, conv_w, lb_param, hg_norm_g,
           w_out, norm_ffn_g, w_group_router, b_group_router, w_expert_router, b_expert_router,
           w_exp_gate, w_exp_up, w_exp_down, norm_final_g):
    xp = x_prompt.reshape(N_PROMPT, D_MODEL)
    xs = x_sample.reshape(N_SAMPLE, D_MODEL)
    lb = jnp.cumsum(jax.nn.softmax(lb_param.astype(F32), axis=0), axis=0)[0].reshape(1, HG_W)

    xn = _norm_in(xp, xs, norm_mix_g[0].reshape(1, D_MODEL))
    proj = _in_proj(xn, w_in[0])

    buf = cache_conv[0]
    zeros = jnp.zeros((DEC_BATCH, DEC_SEQ - 1, CONV_CH), F32)
    e1 = jnp.concatenate([buf[:, 1:2], zeros], axis=1).reshape(N_SAMPLE, CONV_CH)
    e2 = jnp.concatenate([buf, zeros[:, :DEC_SEQ - 2]], axis=1).reshape(N_SAMPLE, CONV_CH)
    yc_p, conv_p = _conv_prompt(proj, conv_w[0])
    yc_s, u_s = _conv_sample(proj, conv_w[0], e1, e2)
    conv_s = u_s.reshape(DEC_BATCH, DEC_SEQ, CONV_CH)[:, DEC_SEQ - (CONV_K - 1):]

    ng = hg_norm_g[0].reshape(1, HG_W)
    yh_p, hgrn_p = _hgrn_prompt(proj, lb, ng)
    yh_s, hgrn_s = _hgrn_sample(proj, lb, ng, state_hgrn[0])

    h = _out_proj(yc_p, yh_p, xp, yc_s, yh_s, xs, w_out[0].astype(BF16))

    pad = jnp.zeros((D_MODEL, ROUTE_COLS - N_GROUPS - N_EXPERTS), F32)
    w_r = jnp.concatenate([w_group_router[0], w_expert_router[0], pad], axis=1)
    b_r = jnp.concatenate([b_group_router[0], b_expert_router[0], pad[0]]).reshape(1, ROUTE_COLS)
    hn, eidx, gates = _norm_route(h, norm_ffn_g[0].reshape(1, D_MODEL), w_r, b_r)
    dest, row_tok, n_groups, chunks, n_used = _dispatch_plan(eidx[:, :TOP_K])
    xb = _dispatch(row_tok, n_groups, hn)
    hb = _expert_up(chunks, n_used, xb, w_exp_gate[0], w_exp_up[0])
    yb = _expert_down(chunks, n_used, hb, w_exp_down[0])

    dest_t = dest.reshape(N_TOK // COMB_ROWS, COMB_ROWS, TOP_K).transpose(0, 2, 1).reshape(N_ASSIGN)
    gf = norm_final_g.reshape(1, D_MODEL)
    y_p = _combine(dest_t, h, gates, gf, yb, 0, N_PROMPT // COMB_ROWS)
    y_s = _combine(dest_t, h, gates, gf, yb, N_PROMPT // COMB_ROWS, N_SAMPLE // COMB_ROWS)

    return (y_p.reshape(BATCH, SEQ, D_MODEL), y_s.reshape(DEC_BATCH, DEC_SEQ, D_MODEL),
            conv_p[None], hgrn_p[None], conv_s[None], hgrn_s[None])
```

```python
import functools

import jax
import jax.numpy as jnp
from jax import lax
from jax.experimental import pallas as pl
from jax.experimental.pallas import tpu as pltpu

F32 = jnp.float32
BF16 = jnp.bfloat16
I32 = jnp.int32

D_MODEL = 4096
BATCH = 4
SEQ = 2048
DEC_BATCH = 128
DEC_SEQ = 4
CONV_CH = 2048
CONV_K = 3
HG_HEADS = 16
HG_DK = 128
HG_DV = 128
HG_W = HG_HEADS * HG_DK
HG_CHUNK = 64
IN_COLS = 3 * CONV_CH + 4 * HG_W
N_GROUPS = 8
EXP_PER_GROUP = 8
N_EXPERTS = 64
TOP_K = 2
D_EXPERT = 1024
EPS = 1e-6

N_PROMPT = BATCH * SEQ
N_SAMPLE = DEC_BATCH * DEC_SEQ
N_TOK = N_PROMPT + N_SAMPLE
N_ASSIGN = N_TOK * TOP_K

LANES = 128
SUBLANES = 8
ROUTE_COLS = LANES

NORM_ROWS = 256
MM_ROWS = 512
MM_COLS = 1024
CONV_ROWS = 512
CONV_COLS = 512
HG_ROWS = 512
HG_SAMPLE_SEQS = 4
MOE_BLK = 128
MOE_NB = N_ASSIGN // MOE_BLK + N_EXPERTS
MOE_ROWS = MOE_NB * MOE_BLK
MOE_FC = 512
MOE_DC = 2048
MOE_CHUNK_BLKS = 4
MOE_NCH = N_EXPERTS + N_ASSIGN // (MOE_BLK * MOE_CHUNK_BLKS)
MOE_IN_SLOTS = 3
MOE_OUT_SLOTS = 2
MOE_K_CHUNKS = 4
PLAN_ROWS = 512
DISPATCH_LAG = 8
COMB_ROWS = 128

MIB = 1 << 20


def _params(sem, vmem_mib, **kwargs):
    return pltpu.CompilerParams(dimension_semantics=sem, vmem_limit_bytes=vmem_mib * MIB, **kwargs)


def _sigmoid(x):
    return 1.0 / (1.0 + jnp.exp(-x))


def _rmsnorm_rows(x, g):
    return x * lax.rsqrt(jnp.mean(x * x, axis=-1, keepdims=True) + EPS) * g


def _norm_in_kernel(xp_ref, xs_ref, g_ref, o_ref, *, n_prompt_tiles):
    i = pl.program_id(0)

    @pl.when(i < n_prompt_tiles)
    def _():
        o_ref[...] = _rmsnorm_rows(xp_ref[...], g_ref[...]).astype(o_ref.dtype)

    @pl.when(i >= n_prompt_tiles)
    def _():
        o_ref[...] = _rmsnorm_rows(xs_ref[...], g_ref[...]).astype(o_ref.dtype)


def _norm_in(xp, xs, g):
    npt = N_PROMPT // NORM_ROWS
    nst = N_SAMPLE // NORM_ROWS
    return pl.pallas_call(
        functools.partial(_norm_in_kernel, n_prompt_tiles=npt),
        grid=(npt + nst,),
        in_specs=[
            pl.BlockSpec((NORM_ROWS, D_MODEL), lambda i: (jnp.minimum(i, npt - 1), 0)),
            pl.BlockSpec((NORM_ROWS, D_MODEL), lambda i: (jnp.maximum(i - npt, 0), 0)),
            pl.BlockSpec((1, D_MODEL), lambda i: (0, 0)),
        ],
        out_specs=pl.BlockSpec((NORM_ROWS, D_MODEL), lambda i: (i, 0)),
        out_shape=jax.ShapeDtypeStruct((N_TOK, D_MODEL), BF16),
        compiler_params=_params(("arbitrary",), 40),
        name="norm_in",
    )(xp, xs, g)


def _in_proj_kernel(x_ref, w_ref, o_ref, wb_ref):
    @pl.when(pl.program_id(1) == 0)
    def _():
        wb_ref[...] = w_ref[...].astype(BF16)

    o_ref[...] = jnp.dot(x_ref[...], wb_ref[...], preferred_element_type=F32).astype(o_ref.dtype)


def _in_proj(xn, w):
    return pl.pallas_call(
        _in_proj_kernel,
        grid=(IN_COLS // MM_COLS, N_TOK // MM_ROWS),
        in_specs=[
            pl.BlockSpec((MM_ROWS, D_MODEL), lambda n, m: (m, 0)),
            pl.BlockSpec((D_MODEL, MM_COLS), lambda n, m: (0, n)),
        ],
        out_specs=pl.BlockSpec((MM_ROWS, MM_COLS), lambda n, m: (m, n)),
        out_shape=jax.ShapeDtypeStruct((N_TOK, IN_COLS), BF16),
        scratch_shapes=[pltpu.VMEM((D_MODEL, MM_COLS), BF16)],
        compiler_params=_params(("arbitrary", "arbitrary"), 58),
        name="in_proj",
    )(xn, w)


def _conv_taps(u, um1, um2, cb, w):
    conv = um2 * w[0:1] + um1 * w[1:2] + u * w[2:3]
    return cb * conv


def _conv_prompt_kernel(cb_ref, cc_ref, ch_ref, w_ref, y_ref, nb_ref, carry_ref):
    t = pl.program_id(2)

    @pl.when(t == 0)
    def _():
        carry_ref[...] = jnp.zeros_like(carry_ref)

    u = cc_ref[...].astype(F32) * ch_ref[...].astype(F32)
    prev = carry_ref[...]
    p1 = prev[SUBLANES - 1:SUBLANES]
    p2 = prev[SUBLANES - 2:SUBLANES - 1]
    row = lax.broadcasted_iota(I32, u.shape, 0)
    um1 = jnp.where(row == 0, p1, pltpu.roll(u, 1, axis=0))
    um2 = jnp.where(row == 0, p2, jnp.where(row == 1, p1, pltpu.roll(u, 2, axis=0)))
    y_ref[...] = _conv_taps(u, um1, um2, cb_ref[...].astype(F32), w_ref[...]).astype(y_ref.dtype)
    carry_ref[...] = u[CONV_ROWS - SUBLANES:CONV_ROWS]
    nb_ref[0] = u[CONV_ROWS - (CONV_K - 1):CONV_ROWS]


def _conv_prompt(proj, conv_w):
    nt = SEQ // CONV_ROWS
    nc = CONV_CH // CONV_COLS
    return pl.pallas_call(
        _conv_prompt_kernel,
        grid=(BATCH, nc, nt),
        in_specs=[
            pl.BlockSpec((CONV_ROWS, CONV_COLS), lambda b, c, t: (b * nt + t, c)),
            pl.BlockSpec((CONV_ROWS, CONV_COLS), lambda b, c, t: (b * nt + t, nc + c)),
            pl.BlockSpec((CONV_ROWS, CONV_COLS), lambda b, c, t: (b * nt + t, 2 * nc + c)),
            pl.BlockSpec((CONV_K, CONV_COLS), lambda b, c, t: (0, c)),
        ],
        out_specs=[
            pl.BlockSpec((CONV_ROWS, CONV_COLS), lambda b, c, t: (b * nt + t, c)),
            pl.BlockSpec((1, CONV_K - 1, CONV_COLS), lambda b, c, t: (b, 0, c)),
        ],
        out_shape=[
            jax.ShapeDtypeStruct((N_PROMPT, CONV_CH), BF16),
            jax.ShapeDtypeStruct((BATCH, CONV_K - 1, CONV_CH), F32),
        ],
        scratch_shapes=[pltpu.VMEM((SUBLANES, CONV_COLS), F32)],
        compiler_params=_params(("arbitrary", "arbitrary", "arbitrary"), 32),
        name="conv_prompt",
    )(proj, proj, proj, conv_w)


def _conv_sample_kernel(cb_ref, cc_ref, ch_ref, w_ref, e1_ref, e2_ref, y_ref, u_ref):
    u =cc_ref[...].astype(F32) * ch_ref[...].astype(F32)
    tpos = lax.broadcasted_iota(I32, u.shape, 0) % DEC_SEQ
    um1 = jnp.where(tpos == 0, e1_ref[...], pltpu.roll(u, 1, axis=0))
    um2 = jnp.where(tpos <= 1, e2_ref[...], pltpu.roll(u, 2, axis=0))
    y_ref[...] = _conv_taps(u, um1, um2, cb_ref[...].astype(F32), w_ref[...]).astype(y_ref.dtype)
    u_ref[...] = u


def _conv_sample(proj, conv_w, e1, e2):
    nc = CONV_CH // CONV_COLS
    rb = N_PROMPT // N_SAMPLE
    return pl.pallas_call(
        _conv_sample_kernel,
        grid=(nc,),
        in_specs=[
            pl.BlockSpec((N_SAMPLE, CONV_COLS), lambda c: (rb, c)),
            pl.BlockSpec((N_SAMPLE, CONV_COLS), lambda c: (rb, nc + c)),
            pl.BlockSpec((N_SAMPLE, CONV_COLS), lambda c: (rb, 2 * nc + c)),
            pl.BlockSpec((CONV_K, CONV_COLS), lambda c: (0, c)),
            pl.BlockSpec((N_SAMPLE, CONV_COLS), lambda c: (0, c)),
            pl.BlockSpec((N_SAMPLE, CONV_COLS), lambda c: (0, c)),
        ],
        out_specs=[
            pl.BlockSpec((N_SAMPLE, CONV_COLS), lambda c: (0, c)),
            pl.BlockSpec((N_SAMPLE, CONV_COLS), lambda c: (0, c)),
        ],
        out_shape=[
            jax.ShapeDtypeStruct((N_SAMPLE, CONV_CH), BF16),
            jax.ShapeDtypeStruct((N_SAMPLE, CONV_CH), F32),
        ],
        compiler_params=_params(("arbitrary",), 32),
        name="conv_sample",
    )(proj, proj, proj, conv_w, e1, e2)


_NT = (((1,), (1,)), ((), ()))
_TN = (((0,), (0,)), ((), ()))


def _hgrn_factors(q, fz, lb, seq_len):
    rows, width = q.shape
    one_m_lb = 1.0 - lb
    log_f = jnp.log(lb + one_m_lb * _sigmoid(fz))
    k = one_m_lb * _sigmoid(-fz)
    row = lax.broadcasted_iota(I32, (rows, width), 0)
    tpos = row % seq_len

    b = log_f
    shift = 1
    while shift < seq_len:
        b = b + jnp.where(tpos >= shift, pltpu.roll(b, shift, axis=0), 0.0)
        shift *= 2

    n_seq = rows // seq_len
    mid = (seq_len - 1) // 2
    if seq_len % SUBLANES == 0:
        b3 = b.reshape(n_seq, seq_len, width)
        b_mid = b3[:, mid:mid + 1]
        b_end = b3[:, seq_len - 1:seq_len]

        def per_row(per_chunk):
            return jnp.broadcast_to(per_chunk, b3.shape).reshape(rows, width)

        b_ref, b_last = per_row(b_mid), per_row(b_end)
        e_ref, e_last = per_row(jnp.exp(b_mid)), per_row(jnp.exp(b_end - b_mid))
    else:
        seq = row // seq_len
        b_ref = b[mid:mid + 1]
        b_last = b[seq_len - 1:seq_len]
        for j in range(1, n_seq):
            b_ref = jnp.where(seq == j, b[j * seq_len + mid:j * seq_len + mid + 1], b_ref)
            b_last = jnp.where(seq == j, b[(j + 1) * seq_len - 1:(j + 1) * seq_len], b_last)
        e_ref, e_last = jnp.exp(b_ref), jnp.exp(b_last - b_ref)

    qe = q * jnp.exp(b - b_ref)
    ke = k * jnp.exp(b_ref - b)
    return qe, ke, qe * e_ref, ke * e_last, log_f, b_last


def _split3_bf16(x):
    hi = x.astype(BF16).astype(F32)
    rest = x - hi
    mid = rest.astype(BF16).astype(F32)
    lo = (rest - mid).astype(BF16).astype(F32)
    return hi, mid, lo


def _head_out(o, gate, norm_g):
    o_n = o * lax.rsqrt(jnp.mean(o * o, axis=-1, keepdims=True) + EPS)
    return o_n * norm_g * (gate * _sigmoid(gate))


def _hgrn_prompt_kernel(q_ref, f_ref, v_ref, g_ref, lb_ref, ng_ref, y_ref, s_out_ref):
    lb = lb_ref[...]
    ng = ng_ref[...]
    rr = lax.broadcasted_iota(I32, (HG_CHUNK, HG_CHUNK), 0)
    cc = lax.broadcasted_iota(I32, (HG_CHUNK, HG_CHUNK), 1)
    causal = cc <= rr

    def block(i, state_t):
        rows = pl.ds(pl.multiple_of(i * HG_ROWS, HG_ROWS), HG_ROWS)
        qe, ke, qb, kd, _, b_last = _hgrn_factors(q_ref[rows, :].astype(F32), f_ref[rows, :].astype(F32),
                                                  lb, HG_CHUNK)
        qe, ke, qb, kd = (a.astype(BF16) for a in (qe, ke, qb, kd))
        vb = v_ref[rows, :]
        outs = []
        for c in range(HG_ROWS // HG_CHUNK):
            cs = slice(c * HG_CHUNK, (c + 1) * HG_CHUNK)
            scores = lax.dot_general(qe[cs], ke[cs], _NT, preferred_element_type=F32)
            scores = jnp.where(causal, scores, 0.0).astype(BF16)
            outs.append(jnp.dot(scores, vb[cs], preferred_element_type=F32)
                        + lax.dot_general(qb[cs], state_t.astype(BF16), _NT, preferred_element_type=F32))
            inc_t = lax.dot_general(vb[cs], kd[cs], _TN, preferred_element_type=F32)
            state_t = jnp.exp(b_last[c * HG_CHUNK:c * HG_CHUNK + 1]) * state_t + inc_t
        o = jnp.concatenate(outs, axis=0)
        y_ref[rows, :] = _head_out(o, g_ref[rows, :].astype(F32), ng).astype(y_ref.dtype)
        return state_t

    state_t = lax.fori_loop(0, SEQ // HG_ROWS, block, jnp.zeros((HG_DV, HG_DK), F32))
    s_out_ref[0, 0] = state_t.T


def _hgrn_prompt(proj, lb, norm_g):
    cq = 3 * CONV_CH // HG_DK

    def spec(off):
        return pl.BlockSpec((SEQ, HG_DK), lambda b, h: (b, cq + off * HG_HEADS + h))

    return pl.pallas_call(
        _hgrn_prompt_kernel,
        grid=(BATCH, HG_HEADS),
        in_specs=[spec(0), spec(1), spec(2), spec(3),
                  pl.BlockSpec((1, HG_DK), lambda b, h: (0, h)),
                  pl.BlockSpec((1, HG_DV), lambda b, h: (0, h))],
        out_specs=[
            pl.BlockSpec((SEQ, HG_DV), lambda b, h: (b, h)),
            pl.BlockSpec((1, 1, HG_DK, HG_DV), lambda b, h: (b, h, 0, 0)),
        ],
        out_shape=[
            jax.ShapeDtypeStruct((N_PROMPT, HG_W), BF16),
            jax.ShapeDtypeStruct((BATCH, HG_HEADS, HG_DK, HG_DV), F32),
        ],
        compiler_params=_params(("arbitrary", "arbitrary"), 32),
        name="hgrn_prompt",
    )(proj, proj, proj, proj, lb, norm_g)


def _hgrn_sample_kernel(q_ref, f_ref, v_ref, g_ref, lb_ref, ng_ref, s_ref, y_ref, s_out_ref):
    rows = HG_SAMPLE_SEQS * DEC_SEQ
    qe, ke, qb, kd, log_f, _ = _hgrn_factors(q_ref[...].astype(F32), f_ref[...].astype(F32),
                                            lb_ref[...], DEC_SEQ)
    lf_parts = _split3_bf16(log_f)
    v = v_ref[...].astype(F32)
    gate = g_ref[...].astype(F32)
    ng = ng_ref[...]

    in_a = lax.broadcasted_iota(I32, (SUBLANES, HG_DV), 0) < DEC_SEQ
    rr = lax.broadcasted_iota(I32, (SUBLANES, SUBLANES), 0)
    cc = lax.broadcasted_iota(I32, (SUBLANES, SUBLANES), 1)
    causal = (cc <= rr) & ((rr // DEC_SEQ) == (cc // DEC_SEQ))
    zero_tile = jnp.zeros((SUBLANES, HG_DV), F32)
    sel = jnp.concatenate([in_a.astype(F32), 1.0 - in_a.astype(F32)], axis=1)
    dec_rhs = jnp.concatenate([sel, sel, sel, jnp.zeros_like(sel)], axis=0).astype(BF16)

    for h in range(HG_HEADS):
        cols = slice(h * HG_DK, (h + 1) * HG_DK)
        outs = []
        for r in range(rows // SUBLANES):
            rs = slice(r * SUBLANES, (r + 1) * SUBLANES)
            s_cat = jnp.concatenate([s_ref[2 * r, h], s_ref[2 * r + 1, h]], axis=1)
            v_t = v[rs, cols]
            scores = lax.dot_general(qe[rs, cols].astype(BF16), ke[rs, cols].astype(BF16), _NT,
                                     preferred_element_type=F32)
            scores = jnp.where(causal, scores, 0.0).astype(BF16)
            o_state = jnp.dot(qb[rs, cols].astype(BF16), s_cat.astype(BF16), preferred_element_type=F32)
            outs.append(jnp.dot(scores, v_t.astype(BF16), preferred_element_type=F32)
                        + jnp.where(in_a, o_state[:, :HG_DV], o_state[:, HG_DV:]))
            inc_rhs = jnp.concatenate([jnp.where(in_a, v_t, 0.0), jnp.where(in_a, 0.0, v_t)], axis=1)
            inc = lax.dot_general(kd[rs, cols].astype(BF16), inc_rhs.astype(BF16), _TN,
                                  preferred_element_type=F32)
            dec_lhs = jnp.concatenate([p[rs, cols] for p in lf_parts] + [zero_tile], axis=0)
            decay = jnp.exp(lax.dot_general(dec_lhs.astype(BF16), dec_rhs, _TN,
                                            preferred_element_type=F32))
            s_new = decay * s_cat + inc
            s_out_ref[2 * r, h] = s_new[:, :HG_DV]
            s_out_ref[2 * r + 1, h] = s_new[:, HG_DV:]
        o = jnp.concatenate(outs, axis=0)
        y_ref[:, cols] = _head_out(o, gate[:, cols], ng[:, cols]).astype(y_ref.dtype)


def _hgrn_sample(proj, lb, norm_g, state):
    rows = HG_SAMPLE_SEQS * DEC_SEQ
    rb = N_PROMPT // rows
    cq = 3 * CONV_CH // HG_W

    def spec(off):
        return pl.BlockSpec((rows, HG_W), lambda g: (rb + g, cq + off))

    st_spec = pl.BlockSpec((HG_SAMPLE_SEQS, HG_HEADS, HG_DK, HG_DV), lambda g: (g, 0, 0, 0))
    return pl.pallas_call(
        _hgrn_sample_kernel,
        grid=(DEC_BATCH // HG_SAMPLE_SEQS,),
        in_specs=[spec(0), spec(1), spec(2), spec(3),
                  pl.BlockSpec((1, HG_W), lambda g: (0, 0)),
                  pl.BlockSpec((1, HG_W), lambda g: (0, 0)),
                  st_spec],
        out_specs=[pl.BlockSpec((rows, HG_W), lambda g: (g, 0)), st_spec],
        out_shape=[
            jax.ShapeDtypeStruct((N_SAMPLE, HG_W), BF16),
            jax.ShapeDtypeStruct((DEC_BATCH, HG_HEADS, HG_DK, HG_DV), F32),
        ],
        compiler_params=_params(("arbitrary",), 40),
        name="hgrn_sample",
    )(proj, proj, proj, proj, lb, norm_g, state)


def _out_proj_kernel(ycp_ref, yhp_ref, xp_ref, ycs_ref, yhs_ref, xs_ref, wc_ref, wh_ref, o_ref, *,
                     n_prompt_tiles):
    m = pl.program_id(1)

    def residual_mix(yc_ref, yh_ref, x_ref):
        mix = (jnp.dot(yc_ref[...], wc_ref[...], preferred_element_type=F32)
               + jnp.dot(yh_ref[...], wh_ref[...], preferred_element_type=F32))
        o_ref[...] = x_ref[...] + mix

    @pl.when(m < n_prompt_tiles)
    def _():
        residual_mix(ycp_ref, yhp_ref, xp_ref)

    @pl.when(m >= n_prompt_tiles)
    def _():
        residual_mix(ycs_ref, yhs_ref, xs_ref)


def _out_proj(yc_p, yh_p, xp, yc_s, yh_s, xs, w):
    npt = N_PROMPT // MM_ROWS

    def prompt_map(n, m):
        return jnp.minimum(m, npt - 1)

    def sample_map(n, m):
        return jnp.maximum(m - npt, 0)

    def source_specs(row_map):
        return [
            pl.BlockSpec((MM_ROWS, CONV_CH), lambda n, m: (row_map(n, m), 0)),
            pl.BlockSpec((MM_ROWS, HG_W), lambda n, m: (row_map(n, m), 0)),
            pl.BlockSpec((MM_ROWS, MM_COLS), lambda n, m: (row_map(n, m), n)),
        ]

    return pl.pallas_call(
        functools.partial(_out_proj_kernel, n_prompt_tiles=npt),
        grid=(D_MODEL // MM_COLS, N_TOK // MM_ROWS),
        in_specs=source_specs(prompt_map) + source_specs(sample_map) + [
            pl.BlockSpec((CONV_CH, MM_COLS), lambda n, m: (0, n)),
            pl.BlockSpec((HG_W, MM_COLS), lambda n, m: (1, n)),
        ],
        out_specs=pl.BlockSpec((MM_ROWS, MM_COLS), lambda n, m: (m, n)),
        out_shape=jax.ShapeDtypeStruct((N_TOK, D_MODEL), F32),
        compiler_params=_params(("arbitrary", "arbitrary"), 56),
        name="out_proj",
    )(yc_p, yh_p, xp, yc_s, yh_s, xs, w, w)


def _split_bf16(x):
    hi = x.astype(BF16)
    return hi, (x - hi.astype(F32)).astype(BF16)


_HI16 = 0xFFFF0000


def _pack_bf16_halves(x):
    half = x.shape[1] // 2
    bits = lax.bitcast_convert_type(x.astype(F32), jnp.uint32)
    return (bits[:, :half] >> 16) | (bits[:, half:] & jnp.uint32(_HI16))


def _unpack_bf16_half(packed, high):
    bits = (packed & jnp.uint32(_HI16)) if high else (packed << 16)
    return lax.bitcast_convert_type(bits, F32)


def _norm_route_kernel(h_ref, g_ref, wr_hi_ref, wr_lo_ref, br_ref, hn_ref, eidx_ref, gate_ref):
    hn = _rmsnorm_rows(h_ref[...], g_ref[...])
    hn_hi, hn_lo = _split_bf16(hn)
    hn_ref[...] = _pack_bf16_halves(hn_hi)
    logits = (jnp.dot(hn_hi, wr_hi_ref[...], preferred_element_type=F32)
              + (jnp.dot(hn_hi, wr_lo_ref[...], preferred_element_type=F32)
                 + jnp.dot(hn_lo, wr_hi_ref[...], preferred_element_type=F32))) + br_ref[...]
    lane = lax.broadcasted_iota(I32, logits.shape, 1)
    lane_f = lane.astype(F32)
    neg = -jnp.inf

    def first_argmax(vals, vmax):
        first = jnp.min(jnp.where(vals == vmax, lane_f, float(ROUTE_COLS)), axis=-1, keepdims=True)
        return first.astype(I32)

    is_grp = lane < N_GROUPS
    lg = jnp.where(is_grp, logits, neg)
    mg = jnp.max(lg, axis=-1, keepdims=True)
    g_sel = first_argmax(lg, mg)
    p_grp = 1.0 / jnp.sum(jnp.where(is_grp, jnp.exp(logits - mg), 0.0), axis=-1, keepdims=True)

    in_grp = (lane >= N_GROUPS) & (((lane - N_GROUPS) // EXP_PER_GROUP) == g_sel)
    le = jnp.where(in_grp, logits, neg)
    m1 = jnp.max(le, axis=-1, keepdims=True)
    i1 = first_argmax(le, m1)
    le2 = jnp.where(lane == i1, neg, le)
    m2 = jnp.max(le2, axis=-1, keepdims=True)
    i2 = first_argmax(le2, m2)
    e2 = jnp.exp(m2 - m1)
    gate1 = p_grp / (1.0 + e2)
    gate2 = p_grp * e2 / (1.0 + e2)
    eidx_ref[...] = jnp.where(lane == 0, i1 - N_GROUPS, jnp.where(lane == 1, i2 - N_GROUPS, 0))
    gate_ref[...] = jnp.where(lane == 0, gate1, jnp.where(lane == 1, gate2, 0.0))


def _norm_route(h, g, w_r, b_r):
    wr_hi, wr_lo = _split_bf16(w_r)
    return pl.pallas_call(
        _norm_route_kernel,
        grid=(N_TOK // NORM_ROWS,),
        in_specs=[
            pl.BlockSpec((NORM_ROWS, D_MODEL), lambda i: (i, 0)),
            pl.BlockSpec((1, D_MODEL), lambda i: (0, 0)),
            pl.BlockSpec((D_MODEL, ROUTE_COLS), lambda i: (0, 0)),
            pl.BlockSpec((D_MODEL, ROUTE_COLS), lambda i: (0, 0)),
            pl.BlockSpec((1, ROUTE_COLS), lambda i: (0, 0)),
        ],
        out_specs=[
            pl.BlockSpec((NORM_ROWS, D_MODEL // 2), lambda i: (i, 0)),
            pl.BlockSpec((NORM_ROWS, ROUTE_COLS), lambda i: (i, 0)),
            pl.BlockSpec((NORM_ROWS, ROUTE_COLS), lambda i: (i, 0)),
        ],
        out_shape=[
            jax.ShapeDtypeStruct((N_TOK, D_MODEL // 2), jnp.uint32),
            jax.ShapeDtypeStruct((N_TOK, ROUTE_COLS), I32),
            jax.ShapeDtypeStruct((N_TOK, ROUTE_COLS), F32),
        ],
        compiler_params=_params(("arbitrary",), 48),
        name="norm_route",
    )(h, g, wr_hi, wr_lo, b_r)


def _row_copy(src_hbm, row, buf, slot, r, sem):
    return pltpu.make_async_copy(src_hbm.at[pl.ds(row, 1)], buf.at[slot, pl.ds(r, 1)], sem.at[slot])


def _gather_issue(idx_ref, base, n_groups, src_hbm, buf, slot, sem):
    def body(g, carry):
        r0 = g * SUBLANES
        for j in range(SUBLANES):
            _row_copy(src_hbm, idx_ref[base + r0 + j], buf, slot, r0 + j, sem).start()
        return carry

    lax.fori_loop(0, n_groups, body, 0)


def _gather_wait(n_groups, src_hbm, buf, slot, sem):
    def body(g, carry):
        rows = pl.ds(pl.multiple_of(g * SUBLANES, SUBLANES), SUBLANES)
        pltpu.make_async_copy(src_hbm.at[pl.ds(0, SUBLANES)], buf.at[slot, rows], sem.at[slot]).wait()
        return carry

    lax.fori_loop(0, n_groups, body, 0)


def _route_plan_kernel(eidx_ref, dest_ref, counts_ref, start_ref, rank_scr):
    n_tiles = N_TOK // PLAN_ROWS
    lane = lax.broadcasted_iota(I32, (PLAN_ROWS, ROUTE_COLS), 1)
    ri = lax.broadcasted_iota(I32, (PLAN_ROWS, PLAN_ROWS), 0)
    ci = lax.broadcasted_iota(I32, (PLAN_ROWS, PLAN_ROWS), 1)
    earlier = jnp.where(ci < ri, 1.0, 0.0).astype(BF16)

    def tile_rows(t):
        return pl.ds(pl.multiple_of(t * PLAN_ROWS, PLAN_ROWS), PLAN_ROWS)

    def pick(vals, col):
        return jnp.sum(jnp.where(lane == col, vals, 0.0), axis=-1, keepdims=True)

    def on_lanes01(v0, v1):
        return jnp.where(lane == 0, v0, jnp.where(lane == 1, v1, 0.0))

    def rank_tile(t, seen):
        e = eidx_ref[tile_rows(t), :]
        e0, e1 = e[:, 0:1], e[:, 1:2]
        uses = jnp.where((lane == e0) | (lane == e1), 1.0, 0.0)
        before = jnp.dot(earlier, uses.astype(BF16), preferred_element_type=F32) + seen
        rank_scr[tile_rows(t), :] = on_lanes01(pick(before, e0), pick(before, e1))
        return seen + jnp.sum(uses, axis=0, keepdims=True)

    counts = lax.fori_loop(0, n_tiles, rank_tile, jnp.zeros((1, ROUTE_COLS), F32))
    padded = jnp.ceil(counts * (1.0 / MOE_BLK)) * MOE_BLK
    ui = lax.broadcasted_iota(I32, (ROUTE_COLS, ROUTE_COLS), 0)
    uj = lax.broadcasted_iota(I32, (ROUTE_COLS, ROUTE_COLS), 1)
    start = jnp.dot(jnp.broadcast_to(padded, (SUBLANES, ROUTE_COLS)), jnp.where(ui < uj, 1.0, 0.0),
                    precision=lax.Precision.HIGHEST, preferred_element_type=F32)[0:1]
    counts_ref[...] = counts.astype(I32)
    start_ref[...] = start.astype(I32)

    def dest_tile(t, carry):
        e = eidx_ref[tile_rows(t), :]
        rank = rank_scr[tile_rows(t), :]
        d0 = pick(start, e[:, 0:1]) + rank[:, 0:1]
        d1 = pick(start, e[:, 1:2]) + rank[:, 1:2]
        dest_ref[tile_rows(t), :] = on_lanes01(d0, d1).astype(I32)
        return carry

    lax.fori_loop(0, n_tiles, dest_tile, 0)


def _route_plan(eidx):
    return pl.pallas_call(
        _route_plan_kernel,
        out_shape=[
            jax.ShapeDtypeStruct((N_TOK, ROUTE_COLS), I32),
            jax.ShapeDtypeStruct((1, ROUTE_COLS), I32),
            jax.ShapeDtypeStruct((1, ROUTE_COLS), I32),
        ],
        scratch_shapes=[pltpu.VMEM((N_TOK, ROUTE_COLS), F32)],
        compiler_params=pltpu.CompilerParams(vmem_limit_bytes=40 * MIB),
        name="route_plan",
    )(eidx)


def _dispatch_kernel(dest_ref, counts_ref, start_ref, hn_hbm, xb_hbm, zbuf, sem_tok, sem_zero):
    zbuf[...] = jnp.zeros_like(zbuf)

    def zero_row(r):
        return pltpu.make_async_copy(zbuf.at[pl.ds(0, 1)], xb_hbm.at[pl.ds(r, 1)], sem_zero.at[0])

    def zero_group(g):
        rows = pl.ds(pl.multiple_of(g * SUBLANES, SUBLANES), SUBLANES)
        return pltpu.make_async_copy(zbuf, xb_hbm.at[rows], sem_zero.at[0])

    def zero_range(lo, hi, op):
        lo_grp = jnp.minimum((lo + SUBLANES - 1) // SUBLANES, hi // SUBLANES)

        def row_body(r, carry):
            getattr(zero_row(r), op)()
            return carry

        def grp_body(g, carry):
            getattr(zero_group(g), op)()
            return carry

        lax.fori_loop(lo, jnp.minimum(lo_grp * SUBLANES, hi), row_body, 0)
        lax.fori_loop(lo_grp, hi // SUBLANES, grp_body, 0)

    def zero_fill(op):
        def body(e, end):
            first = start_ref[e] + counts_ref[e]
            last = (first + MOE_BLK - 1) // MOE_BLK * MOE_BLK
            zero_range(first, last, op)
            return last

        used_end = lax.fori_loop(0, N_EXPERTS, body, 0)
        zero_range(used_end, MOE_ROWS, op)

    zero_fill("start")

    def tok_copy(n, a):
        return pltpu.make_async_copy(hn_hbm.at[pl.ds(n, 1)], xb_hbm.at[pl.ds(dest_ref[a], 1)], sem_tok.at[0])

    def group_wait():
        rows = pl.ds(0, TOP_K * SUBLANES)
        pltpu.make_async_copy(hn_hbm.at[rows], xb_hbm.at[rows], sem_tok.at[0]).wait()

    def tok_group(g, carry):
        for j in range(SUBLANES):
            n = g * SUBLANES + j
            for k in range(TOP_K):
                tok_copy(n, TOP_K * n + k).start()

        @pl.when(g >= DISPATCH_LAG)
        def _():
            group_wait()

        return carry

    n_groups = N_TOK // SUBLANES
    lax.fori_loop(0, n_groups, tok_group, 0)
    for _ in range(DISPATCH_LAG):
        group_wait()
    zero_fill("wait")


def _dispatch(dest_flat, counts, start, hn_packed):
    return pl.pallas_call(
        _dispatch_kernel,
        grid_spec=pltpu.PrefetchScalarGridSpec(
            num_scalar_prefetch=3,
            grid=(1,),
            in_specs=[pl.BlockSpec(memory_space=pl.ANY)],
            out_specs=pl.BlockSpec(memory_space=pl.ANY),
            scratch_shapes=[pltpu.VMEM((SUBLANES, D_MODEL // 2), jnp.uint32),
                            pltpu.SemaphoreType.DMA((1,)),
                            pltpu.SemaphoreType.DMA((1,))],
        ),
        out_shape=jax.ShapeDtypeStruct((MOE_ROWS, D_MODEL // 2), jnp.uint32),
        compiler_params=_params(("arbitrary",), 16, disable_bounds_checks=True),
        name="moe_dispatch",
    )(dest_flat, counts, start, hn_packed)


def _stream_expert_chunks(blk0_ref, n_ref, n_used, in_copy, out_copy, out_buf, compute):
    c = pl.program_id(1)
    n_chunks = pl.num_programs(1)

    def for_blocks(chunk, fn):
        blk0 = blk0_ref[chunk]

        def body(i, carry):
            fn(blk0 + i, i)
            return carry

        lax.fori_loop(0, n_ref[chunk], body, 0)

    def start_loads(chunk):
        for_blocks(chunk, lambda g, i: in_copy(g, chunk % MOE_IN_SLOTS, i).start())

    @pl.when(c == 0)
    def _():
        start_loads(0)
        start_loads(1)

    @pl.when(c + 2 < n_chunks)
    def _():
        start_loads(c + 2)

    in_slot = c % MOE_IN_SLOTS
    out_slot = c % MOE_OUT_SLOTS
    for_blocks(c, lambda g, i: in_copy(g, in_slot, i).wait())

    @pl.when(c >= MOE_OUT_SLOTS)
    def _():
        for_blocks(c - MOE_OUT_SLOTS, lambda g, i: out_copy(g, out_slot, i).wait())

    for m in range(1, MOE_CHUNK_BLKS + 1):
        @pl.when(n_ref[c] == m)
        def _(m=m):
            compute(m, in_slot, out_slot)

    for_blocks(c, lambda g, i: out_copy(g, out_slot, i).start())

    @pl.when(c == n_chunks - 1)
    def _():
        for chunk in (c - 1, c):
            for_blocks(chunk, lambda g, i: out_copy(g, chunk % MOE_OUT_SLOTS, i).wait())
        out_buf[0, 0:MOE_BLK] = jnp.zeros((MOE_BLK,) + out_buf.shape[2:], out_buf.dtype)

        def start_zero(g, carry):
            out_copy(g, 0, 0).start()
            return carry

        def wait_zero(g, carry):
            out_copy(g, 0, 0).wait()
            return carry

        lax.fori_loop(n_used, MOE_NB, start_zero, 0)
        lax.fori_loop(n_used, MOE_NB, wait_zero, 0)


def _block_rows(g):
    return pl.ds(pl.multiple_of(g * MOE_BLK, MOE_BLK), MOE_BLK)


def _slot_rows(i):
    return pl.ds(pl.multiple_of(i * MOE_BLK, MOE_BLK), MOE_BLK)


def _expert_up_kernel(blk0_ref, n_ref, e_ref, nbu_ref, wg_ref, wu_ref, x_hbm, h_hbm, xbuf, hbuf, sem_x, sem_h):
    del e_ref
    cols = pl.ds(pl.multiple_of(pl.program_id(0) * MOE_FC, MOE_FC), MOE_FC)

    def x_copy(g, slot, i):
        return pltpu.make_async_copy(x_hbm.at[_block_rows(g)], xbuf.at[slot, _slot_rows(i)], sem_x.at[slot])

    def h_copy(g, slot, i):
        return pltpu.make_async_copy(hbuf.at[slot, _slot_rows(i)], h_hbm.at[_block_rows(g), cols],
                                     sem_h.at[slot])

    def compute(m, in_slot, out_slot):
        rows = m * MOE_BLK
        kc = D_MODEL // MOE_K_CHUNKS
        half_chunks = MOE_K_CHUNKS // 2
        g = u = None
        for i in range(MOE_K_CHUNKS):
            ks = slice(i * kc, (i + 1) * kc)
            ps = slice((i % half_chunks) * kc, (i % half_chunks + 1) * kc)
            xk = _unpack_bf16_half(xbuf[in_slot, :rows, ps], high=i >= half_chunks)
            gi = jnp.dot(xk, wg_ref[0, ks, :], preferred_element_type=F32)
            ui = jnp.dot(xk, wu_ref[0, ks, :], preferred_element_type=F32)
            g, u = (gi, ui) if g is None else (g + gi, u + ui)
        hbuf[out_slot, :rows] = (g * _sigmoid(g) * u).astype(hbuf.dtype)

    _stream_expert_chunks(blk0_ref, n_ref, nbu_ref[0], x_copy, h_copy, hbuf, compute)


def _chunk_scratch(in_cols, in_dtype, out_cols, out_dtype):
    rows = MOE_CHUNK_BLKS * MOE_BLK
    return [pltpu.VMEM((MOE_IN_SLOTS, rows, in_cols), in_dtype),
            pltpu.VMEM((MOE_OUT_SLOTS, rows, out_cols), out_dtype),
            pltpu.SemaphoreType.DMA((MOE_IN_SLOTS,)),
            pltpu.SemaphoreType.DMA((MOE_OUT_SLOTS,))]


def _expert_up(chunks, n_used, xb, w_g, w_u):
    def w_map(j, c, blk0, n, e, nbu):
        return (e[c], 0, j)

    return pl.pallas_call(
        _expert_up_kernel,
        grid_spec=pltpu.PrefetchScalarGridSpec(
            num_scalar_prefetch=4,
            grid=(D_EXPERT // MOE_FC, MOE_NCH),
            in_specs=[pl.BlockSpec((1, D_MODEL, MOE_FC), w_map),
                      pl.BlockSpec((1, D_MODEL, MOE_FC), w_map),
                      pl.BlockSpec(memory_space=pl.ANY)],
            out_specs=pl.BlockSpec(memory_space=pl.ANY),
            scratch_shapes=_chunk_scratch(D_MODEL // 2, jnp.uint32, MOE_FC, BF16),
        ),
        out_shape=jax.ShapeDtypeStruct((MOE_ROWS, D_EXPERT), BF16),
        compiler_params=_params(("arbitrary", "arbitrary"), 58),
        name="expert_up",
    )(*chunks, n_used, w_g, w_u, xb)


def _expert_down_kernel(blk0_ref, n_ref, e_ref, nbu_ref, wd_ref, h_hbm, y_hbm, hbuf, ybuf, sem_h, sem_y):
    del e_ref
    cols = pl.ds(pl.multiple_of(pl.program_id(0) * MOE_DC, MOE_DC), MOE_DC)

    def h_copy(g, slot, i):
        return pltpu.make_async_copy(h_hbm.at[_block_rows(g)], hbuf.at[slot, _slot_rows(i)], sem_h.at[slot])

    def y_copy(g, slot, i):
        return pltpu.make_async_copy(ybuf.at[slot, _slot_rows(i)], y_hbm.at[_block_rows(g), cols],
                                     sem_y.at[slot])

    def compute(m, in_slot, out_slot):
        rows = m * MOE_BLK
        ybuf[out_slot, :rows] = jnp.dot(hbuf[in_slot, :rows].astype(F32), wd_ref[0],
                                        preferred_element_type=F32)

    _stream_expert_chunks(blk0_ref, n_ref, nbu_ref[0], h_copy, y_copy, ybuf, compute)


def _expert_down(chunks, n_used, hb, w_d):
    return pl.pallas_call(
        _expert_down_kernel,
        grid_spec=pltpu.PrefetchScalarGridSpec(
            num_scalar_prefetch=4,
            grid=(D_MODEL // MOE_DC, MOE_NCH),
            in_specs=[pl.BlockSpec((1, D_EXPERT, MOE_DC), lambda n, c, blk0, nb, e, nbu: (e[c], 0, n)),
                      pl.BlockSpec(memory_space=pl.ANY)],
            out_specs=pl.BlockSpec(memory_space=pl.ANY),
            scratch_shapes=_chunk_scratch(D_EXPERT, BF16, MOE_DC, F32),
        ),
        out_shape=jax.ShapeDtypeStruct((MOE_ROWS, D_MODEL), F32),
        compiler_params=_params(("arbitrary", "arbitrary"), 48),
        name="expert_down",
    )(*chunks, n_used, w_d, hb)


def _combine_kernel(dest_ref, h_ref, gate_ref, g_ref, y_hbm, o_ref, buf, sem, *, tile0):
    i = pl.program_id(0)
    n = pl.num_programs(0)
    slot = i % 2
    rows = TOP_K * COMB_ROWS
    n_groups = rows // SUBLANES
    base = (tile0 + i) * rows

    @pl.when(i == 0)
    def _():
        _gather_issue(dest_ref, base, n_groups, y_hbm, buf, 0, sem)

    @pl.when(i + 1 < n)
    def _():
        _gather_issue(dest_ref, base + rows, n_groups, y_hbm, buf, 1 - slot, sem)

    _gather_wait(n_groups, y_hbm, buf, slot, sem)
    gates = gate_ref[...]
    ff = gates[:, 0:1] * buf[slot, 0:COMB_ROWS] + gates[:, 1:2] * buf[slot, COMB_ROWS:rows]
    o_ref[...] = _rmsnorm_rows(h_ref[...] + ff, g_ref[...])


def _combine(dest, h, gates, g_final, y_buf, tile0, n_tiles):
    return pl.pallas_call(
        functools.partial(_combine_kernel, tile0=tile0),
        grid_spec=pltpu.PrefetchScalarGridSpec(
            num_scalar_prefetch=1,
            grid=(n_tiles,),
            in_specs=[
                pl.BlockSpec((COMB_ROWS, D_MODEL), lambda i, d: (tile0 + i, 0)),
                pl.BlockSpec((COMB_ROWS, ROUTE_COLS), lambda i, d: (tile0 + i, 0)),
                pl.BlockSpec((1, D_MODEL), lambda i, d: (0, 0)),
                pl.BlockSpec(memory_space=pl.ANY),
            ],
            out_specs=pl.BlockSpec((COMB_ROWS, D_MODEL), lambda i, d: (i, 0)),
            scratch_shapes=[pltpu.VMEM((2, TOP_K * COMB_ROWS, D_MODEL), F32),
                            pltpu.SemaphoreType.DMA((2,))],
        ),
        out_shape=jax.ShapeDtypeStruct((n_tiles * COMB_ROWS, D_MODEL), F32),
        compiler_params=_params(("arbitrary",), 40, disable_bounds_checks=True),
        name="moe_combine",
    )(dest, h, gates, g_final, y_buf)


def _chunk_list(counts, start):
    blk0 = start // MOE_BLK
    nblk = (counts + MOE_BLK - 1) // MOE_BLK
    n_used = (blk0[-1] + nblk[-1]).astype(I32)
    nch = (nblk + MOE_CHUNK_BLKS - 1) // MOE_CHUNK_BLKS
    ch_end = jnp.cumsum(nch)
    ch = jnp.arange(MOE_NCH, dtype=I32)
    ch_e = jnp.minimum(jnp.searchsorted(ch_end, ch, side='right'), N_EXPERTS - 1)
    local = ch - (ch_end - nch)[ch_e]
    live = ch < ch_end[-1]
    ch_blk0 = (blk0[ch_e] + MOE_CHUNK_BLKS * local).astype(I32)
    ch_n = jnp.where(live, jnp.clip(nblk[ch_e] - MOE_CHUNK_BLKS * local, 0, MOE_CHUNK_BLKS), 0).astype(I32)
    ch_e = jnp.where(live, ch_e, ch_e[ch_end[-1] - 1]).astype(I32)
    return (ch_blk0, ch_n, ch_e), n_used.reshape(1)


def kernel(x_prompt, x_sample, cache_conv, state_hgrn, norm_mix_g, w_in, conv_w, lb_param, hg_norm_g,
           w_out, norm_ffn_g, w_group_router, b_group_router, w_expert_router, b_expert_router,
           w_exp_gate, w_exp_up, w_exp_down, norm_final_g):
    xp = x_prompt.reshape(N_PROMPT, D_MODEL)
    xs = x_sample.reshape(N_SAMPLE, D_MODEL)
    lb = jnp.cumsum(jax.nn.softmax(lb_param.astype(F32), axis=0), axis=0)[0].reshape(1, HG_W)

    xn = _norm_in(xp, xs, norm_mix_g[0].reshape(1, D_MODEL))
    proj = _in_proj(xn, w_in[0])

    buf = cache_conv[0]
    zeros = jnp.zeros((DEC_BATCH, DEC_SEQ - 1, CONV_CH), F32)
    e1 = jnp.concatenate([buf[:, 1:2], zeros], axis=1).reshape(N_SAMPLE, CONV_CH)
    e2 = jnp.concatenate([buf, zeros[:, :DEC_SEQ - 2]], axis=1).reshape(N_SAMPLE, CONV_CH)
    yc_p, conv_p = _conv_prompt(proj, conv_w[0])
    yc_s, u_s = _conv_sample(proj, conv_w[0], e1, e2)
    conv_s = u_s.reshape(DEC_BATCH, DEC_SEQ, CONV_CH)[:, DEC_SEQ - (CONV_K - 1):]

    ng = hg_norm_g[0].reshape(1, HG_W)
    yh_p, hgrn_p = _hgrn_prompt(proj, lb, ng)
    yh_s, hgrn_s = _hgrn_sample(proj, lb, ng, state_hgrn[0])

    h = _out_proj(yc_p, yh_p, xp, yc_s, yh_s, xs, w_out[0].astype(BF16))

    pad = jnp.zeros((D_MODEL, ROUTE_COLS - N_GROUPS - N_EXPERTS), F32)
    w_r = jnp.concatenate([w_group_router[0], w_expert_router[0], pad], axis=1)
    b_r = jnp.concatenate([b_group_router[0], b_expert_router[0], pad[0]]).reshape(1, ROUTE_COLS)
    hn_packed, eidx, gates = _norm_route(h, norm_ffn_g[0].reshape(1, D_MODEL), w_r, b_r)
    dest, counts, start = _route_plan(eidx)
    dest = dest[:, :TOP_K]
    counts, start = counts[0, :N_EXPERTS], start[0, :N_EXPERTS]
    chunks, n_used = _chunk_list(counts, start)
    xb = _dispatch(dest.reshape(N_ASSIGN), counts, start, hn_packed)
    hb = _expert_up(chunks, n_used, xb, w_exp_gate[0], w_exp_up[0])
    yb = _expert_down(chunks, n_used, hb, w_exp_down[0])

    dest_t = dest.reshape(N_TOK // COMB_ROWS, COMB_ROWS, TOP_K).transpose(0, 2, 1).reshape(N_ASSIGN)
    gf = norm_final_g.reshape(1, D_MODEL)
    y_p = _combine(dest_t, h, gates, gf, yb, 0, N_PROMPT // COMB_ROWS)
    y_s = _combine(dest_t, h, gates, gf, yb, N_PROMPT // COMB_ROWS, N_SAMPLE // COMB_ROWS)

    return (y_p.reshape(BATCH, SEQ, D_MODEL), y_s.reshape(DEC_BATCH, DEC_SEQ, D_MODEL),
            conv_p[None], hgrn_p[None], conv_s[None], hgrn_s[None])
```

```python
import functools

import jax
import jax.numpy as jnp
from jax import lax
from jax.experimental import pallas as pl
from jax.experimental.pallas import tpu as pltpu

F32 = jnp.float32
BF16 = jnp.bfloat16
I32 = jnp.int32

D_MODEL = 4096
BATCH = 4
SEQ = 2048
DEC_BATCH = 128
DEC_SEQ = 4
CONV_CH = 2048
CONV_K = 3
HG_HEADS = 16
HG_DK = 128
HG_DV = 128
HG_W = HG_HEADS * HG_DK
HG_CHUNK = 64
IN_COLS = 3 * CONV_CH + 4 * HG_W
N_GROUPS = 8
EXP_PER_GROUP = 8
N_EXPERTS = 64
TOP_K = 2
D_EXPERT = 1024
EPS = 1e-6

N_PROMPT = BATCH * SEQ
N_SAMPLE = DEC_BATCH * DEC_SEQ
N_TOK = N_PROMPT + N_SAMPLE
N_ASSIGN = N_TOK * TOP_K

LANES = 128
SUBLANES = 8
ROUTE_COLS = LANES

NORM_ROWS = 256
MM_ROWS = 512
MM_COLS = 1024
CONV_ROWS = 512
CONV_COLS = 512
HG_ROWS = 512
HG_SAMPLE_SEQS = 4
MOE_BLK = 128
MOE_NB = N_ASSIGN // MOE_BLK + N_EXPERTS
MOE_ROWS = MOE_NB * MOE_BLK
MOE_FC = 512
MOE_DC = 2048
MOE_CHUNK_BLKS = 4
MOE_NCH = N_EXPERTS + N_ASSIGN // (MOE_BLK * MOE_CHUNK_BLKS)
MOE_IN_SLOTS = 3
MOE_OUT_SLOTS = 2
MOE_K_CHUNKS = 4
PLAN_ROWS = 512
COMB_ROWS = 128

MIB = 1 << 20


def _params(sem, vmem_mib, **kwargs):
    return pltpu.CompilerParams(dimension_semantics=sem, vmem_limit_bytes=vmem_mib * MIB, **kwargs)


def _sigmoid(x):
    return 1.0 / (1.0 + jnp.exp(-x))


def _rmsnorm_rows(x, g):
    return x * lax.rsqrt(jnp.mean(x * x, axis=-1, keepdims=True) + EPS) * g


def _norm_in_kernel(xp_ref, xs_ref, g_ref, o_ref, *, n_prompt_tiles):
    i = pl.program_id(0)

    @pl.when(i < n_prompt_tiles)
    def _():
        o_ref[...] = _rmsnorm_rows(xp_ref[...], g_ref[...]).astype(o_ref.dtype)

    @pl.when(i >= n_prompt_tiles)
    def _():
        o_ref[...] = _rmsnorm_rows(xs_ref[...], g_ref[...]).astype(o_ref.dtype)


def _norm_in(xp, xs, g):
    npt = N_PROMPT // NORM_ROWS
    nst = N_SAMPLE // NORM_ROWS
    return pl.pallas_call(
        functools.partial(_norm_in_kernel, n_prompt_tiles=npt),
        grid=(npt + nst,),
        in_specs=[
            pl.BlockSpec((NORM_ROWS, D_MODEL), lambda i: (jnp.minimum(i, npt - 1), 0)),
            pl.BlockSpec((NORM_ROWS, D_MODEL), lambda i: (jnp.maximum(i - npt, 0), 0)),
            pl.BlockSpec((1, D_MODEL), lambda i: (0, 0)),
        ],
        out_specs=pl.BlockSpec((NORM_ROWS, D_MODEL), lambda i: (i, 0)),
        out_shape=jax.ShapeDtypeStruct((N_TOK, D_MODEL), BF16),
        compiler_params=_params(("arbitrary",), 40),
        name="norm_in",
    )(xp, xs, g)


def _in_proj_kernel(x_ref, w_ref, o_ref, wb_ref):
    @pl.when(pl.program_id(1) == 0)
    def _():
        wb_ref[...] = w_ref[...].astype(BF16)

    o_ref[...] = jnp.dot(x_ref[...], wb_ref[...], preferred_element_type=F32).astype(o_ref.dtype)


def _in_proj(xn, w):
    return pl.pallas_call(
        _in_proj_kernel,
        grid=(IN_COLS // MM_COLS, N_TOK // MM_ROWS),
        in_specs=[
            pl.BlockSpec((MM_ROWS, D_MODEL), lambda n, m: (m, 0)),
            pl.BlockSpec((D_MODEL, MM_COLS), lambda n, m: (0, n)),
        ],
        out_specs=pl.BlockSpec((MM_ROWS, MM_COLS), lambda n, m: (m, n)),
        out_shape=jax.ShapeDtypeStruct((N_TOK, IN_COLS), BF16),
        scratch_shapes=[pltpu.VMEM((D_MODEL, MM_COLS), BF16)],
        compiler_params=_params(("arbitrary", "arbitrary"), 58),
        name="in_proj",
    )(xn, w)


def _conv_taps(u, um1, um2, cb, w):
    conv = um2 * w[0:1] + um1 * w[1:2] + u * w[2:3]
    return cb * conv


def _conv_prompt_kernel(cb_ref, cc_ref, ch_ref, w_ref, y_ref, nb_ref, carry_ref):
    t = pl.program_id(2)

    @pl.when(t == 0)
    def _():
        carry_ref[...] = jnp.zeros_like(carry_ref)

    u = cc_ref[...].astype(F32) * ch_ref[...].astype(F32)
    prev = carry_ref[...]
    p1 = prev[SUBLANES - 1:SUBLANES]
    p2 = prev[SUBLANES - 2:SUBLANES - 1]
    row = lax.broadcasted_iota(I32, u.shape, 0)
    um1 = jnp.where(row == 0, p1, pltpu.roll(u, 1, axis=0))
    um2 = jnp.where(row == 0, p2, jnp.where(row == 1, p1, pltpu.roll(u, 2, axis=0)))
    y_ref[...] = _conv_taps(u, um1, um2, cb_ref[...].astype(F32), w_ref[...]).astype(y_ref.dtype)
    carry_ref[...] = u[CONV_ROWS - SUBLANES:CONV_ROWS]
    nb_ref[0] = u[CONV_ROWS - (CONV_K - 1):CONV_ROWS]


def _conv_prompt(proj, conv_w):
    nt = SEQ // CONV_ROWS
    nc = CONV_CH // CONV_COLS
    return pl.pallas_call(
        _conv_prompt_kernel,
        grid=(BATCH, nc, nt),
        in_specs=[
            pl.BlockSpec((CONV_ROWS, CONV_COLS), lambda b, c, t: (b * nt + t, c)),
            pl.BlockSpec((CONV_ROWS, CONV_COLS), lambda b, c, t: (b * nt + t, nc + c)),
            pl.BlockSpec((CONV_ROWS, CONV_COLS), lambda b, c, t: (b * nt + t, 2 * nc + c)),
            pl.BlockSpec((CONV_K, CONV_COLS), lambda b, c, t: (0, c)),
        ],
        out_specs=[
            pl.BlockSpec((CONV_ROWS, CONV_COLS), lambda b, c, t: (b * nt + t, c)),
            pl.BlockSpec((1, CONV_K - 1, CONV_COLS), lambda b, c, t: (b, 0, c)),
        ],
        out_shape=[
            jax.ShapeDtypeStruct((N_PROMPT, CONV_CH), BF16),
            jax.ShapeDtypeStruct((BATCH, CONV_K - 1, CONV_CH), F32),
        ],
        scratch_shapes=[pltpu.VMEM((SUBLANES, CONV_COLS), F32)],
        compiler_params=_params(("arbitrary", "arbitrary", "arbitrary"), 32),
        name="conv_prompt",
    )(proj, proj, proj, conv_w)


def _conv_sample_kernel(cb_ref, cc_ref, ch_ref, w_ref, e1_ref, e2_ref, y_ref, u_ref):
    u =cc_ref[...].astype(F32) * ch_ref[...].astype(F32)
    tpos = lax.broadcasted_iota(I32, u.shape, 0) % DEC_SEQ
    um1 = jnp.where(tpos == 0, e1_ref[...], pltpu.roll(u, 1, axis=0))
    um2 = jnp.where(tpos <= 1, e2_ref[...], pltpu.roll(u, 2, axis=0))
    y_ref[...] = _conv_taps(u, um1, um2, cb_ref[...].astype(F32), w_ref[...]).astype(y_ref.dtype)
    u_ref[...] = u


def _conv_sample(proj, conv_w, e1, e2):
    nc = CONV_CH // CONV_COLS
    rb = N_PROMPT // N_SAMPLE
    return pl.pallas_call(
        _conv_sample_kernel,
        grid=(nc,),
        in_specs=[
            pl.BlockSpec((N_SAMPLE, CONV_COLS), lambda c: (rb, c)),
            pl.BlockSpec((N_SAMPLE, CONV_COLS), lambda c: (rb, nc + c)),
            pl.BlockSpec((N_SAMPLE, CONV_COLS), lambda c: (rb, 2 * nc + c)),
            pl.BlockSpec((CONV_K, CONV_COLS), lambda c: (0, c)),
            pl.BlockSpec((N_SAMPLE, CONV_COLS), lambda c: (0, c)),
            pl.BlockSpec((N_SAMPLE, CONV_COLS), lambda c: (0, c)),
        ],
        out_specs=[
            pl.BlockSpec((N_SAMPLE, CONV_COLS), lambda c: (0, c)),
            pl.BlockSpec((N_SAMPLE, CONV_COLS), lambda c: (0, c)),
        ],
        out_shape=[
            jax.ShapeDtypeStruct((N_SAMPLE, CONV_CH), BF16),
            jax.ShapeDtypeStruct((N_SAMPLE, CONV_CH), F32),
        ],
        compiler_params=_params(("arbitrary",), 32),
        name="conv_sample",
    )(proj, proj, proj, conv_w, e1, e2)


_NT = (((1,), (1,)), ((), ()))
_TN = (((0,), (0,)), ((), ()))


def _hgrn_factors(q, fz, lb, seq_len):
    rows, width = q.shape
    one_m_lb = 1.0 - lb
    log_f = jnp.log(lb + one_m_lb * _sigmoid(fz))
    k = one_m_lb * _sigmoid(-fz)
    row = lax.broadcasted_iota(I32, (rows, width), 0)
    tpos = row % seq_len

    b = log_f
    shift = 1
    while shift < seq_len:
        b = b + jnp.where(tpos >= shift, pltpu.roll(b, shift, axis=0), 0.0)
        shift *= 2

    n_seq = rows // seq_len
    mid = (seq_len - 1) // 2
    if seq_len % SUBLANES == 0:
        b3 = b.reshape(n_seq, seq_len, width)
        b_mid = b3[:, mid:mid + 1]
        b_end = b3[:, seq_len - 1:seq_len]

        def per_row(per_chunk):
            return jnp.broadcast_to(per_chunk, b3.shape).reshape(rows, width)

        b_ref, b_last = per_row(b_mid), per_row(b_end)
        e_ref, e_last = per_row(jnp.exp(b_mid)), per_row(jnp.exp(b_end - b_mid))
    else:
        seq = row // seq_len
        b_ref = b[mid:mid + 1]
        b_last = b[seq_len - 1:seq_len]
        for j in range(1, n_seq):
            b_ref = jnp.where(seq == j, b[j * seq_len + mid:j * seq_len + mid + 1], b_ref)
            b_last = jnp.where(seq == j, b[(j + 1) * seq_len - 1:(j + 1) * seq_len], b_last)
        e_ref, e_last = jnp.exp(b_ref), jnp.exp(b_last - b_ref)

    qe = q * jnp.exp(b - b_ref)
    ke = k * jnp.exp(b_ref - b)
    return qe, ke, qe * e_ref, ke * e_last, log_f, b_last


def _split3_bf16(x):
    hi = x.astype(BF16).astype(F32)
    rest = x - hi
    mid = rest.astype(BF16).astype(F32)
    lo = (rest - mid).astype(BF16).astype(F32)
    return hi, mid, lo


def _head_out(o, gate, norm_g):
    o_n = o * lax.rsqrt(jnp.mean(o * o, axis=-1, keepdims=True) + EPS)
    return o_n * norm_g * (gate * _sigmoid(gate))


def _hgrn_prompt_kernel(q_ref, f_ref, v_ref, g_ref, lb_ref, ng_ref, y_ref, s_out_ref):
    lb = lb_ref[...]
    ng = ng_ref[...]
    rr = lax.broadcasted_iota(I32, (HG_CHUNK, HG_CHUNK), 0)
    cc = lax.broadcasted_iota(I32, (HG_CHUNK, HG_CHUNK), 1)
    causal = cc <= rr

    def block(i, state_t):
        rows = pl.ds(pl.multiple_of(i * HG_ROWS, HG_ROWS), HG_ROWS)
        qe, ke, qb, kd, _, b_last = _hgrn_factors(q_ref[rows, :].astype(F32), f_ref[rows, :].astype(F32),
                                                  lb, HG_CHUNK)
        qe, ke, qb, kd = (a.astype(BF16) for a in (qe, ke, qb, kd))
        vb = v_ref[rows, :]
        outs = []
        for c in range(HG_ROWS // HG_CHUNK):
            cs = slice(c * HG_CHUNK, (c + 1) * HG_CHUNK)
            scores = lax.dot_general(qe[cs], ke[cs], _NT, preferred_element_type=F32)
            scores = jnp.where(causal, scores, 0.0).astype(BF16)
            outs.append(jnp.dot(scores, vb[cs], preferred_element_type=F32)
                        + lax.dot_general(qb[cs], state_t.astype(BF16), _NT, preferred_element_type=F32))
            inc_t = lax.dot_general(vb[cs], kd[cs], _TN, preferred_element_type=F32)
            state_t = jnp.exp(b_last[c * HG_CHUNK:c * HG_CHUNK + 1]) * state_t + inc_t
        o = jnp.concatenate(outs, axis=0)
        y_ref[rows, :] = _head_out(o, g_ref[rows, :].astype(F32), ng).astype(y_ref.dtype)
        return state_t

    state_t = lax.fori_loop(0, SEQ // HG_ROWS, block, jnp.zeros((HG_DV, HG_DK), F32))
    s_out_ref[0, 0] = state_t.T


def _hgrn_prompt(proj, lb, norm_g):
    cq = 3 * CONV_CH // HG_DK

    def spec(off):
        return pl.BlockSpec((SEQ, HG_DK), lambda b, h: (b, cq + off * HG_HEADS + h))

    return pl.pallas_call(
        _hgrn_prompt_kernel,
        grid=(BATCH, HG_HEADS),
        in_specs=[spec(0), spec(1), spec(2), spec(3),
                  pl.BlockSpec((1, HG_DK), lambda b, h: (0, h)),
                  pl.BlockSpec((1, HG_DV), lambda b, h: (0, h))],
        out_specs=[
            pl.BlockSpec((SEQ, HG_DV), lambda b, h: (b, h)),
            pl.BlockSpec((1, 1, HG_DK, HG_DV), lambda b, h: (b, h, 0, 0)),
        ],
        out_shape=[
            jax.ShapeDtypeStruct((N_PROMPT, HG_W), BF16),
            jax.ShapeDtypeStruct((BATCH, HG_HEADS, HG_DK, HG_DV), F32),
        ],
        compiler_params=_params(("arbitrary", "arbitrary"), 32),
        name="hgrn_prompt",
    )(proj, proj, proj, proj, lb, norm_g)


def _hgrn_sample_kernel(q_ref, f_ref, v_ref, g_ref, lb_ref, ng_ref, s_ref, y_ref, s_out_ref):
    rows = HG_SAMPLE_SEQS * DEC_SEQ
    qe, ke, qb, kd, log_f, _ = _hgrn_factors(q_ref[...].astype(F32), f_ref[...].astype(F32),
                                            lb_ref[...], DEC_SEQ)
    lf_parts = _split3_bf16(log_f)
    v = v_ref[...].astype(F32)
    gate = g_ref[...].astype(F32)
    ng = ng_ref[...]

    in_a = lax.broadcasted_iota(I32, (SUBLANES, HG_DV), 0) < DEC_SEQ
    rr = lax.broadcasted_iota(I32, (SUBLANES, SUBLANES), 0)
    cc = lax.broadcasted_iota(I32, (SUBLANES, SUBLANES), 1)
    causal = (cc <= rr) & ((rr // DEC_SEQ) == (cc // DEC_SEQ))
    zero_tile = jnp.zeros((SUBLANES, HG_DV), F32)
    sel = jnp.concatenate([in_a.astype(F32), 1.0 - in_a.astype(F32)], axis=1)
    dec_rhs = jnp.concatenate([sel, sel, sel, jnp.zeros_like(sel)], axis=0).astype(BF16)

    for h in range(HG_HEADS):
        cols = slice(h * HG_DK, (h + 1) * HG_DK)
        outs = []
        for r in range(rows // SUBLANES):
            rs = slice(r * SUBLANES, (r + 1) * SUBLANES)
            s_cat = jnp.concatenate([s_ref[2 * r, h], s_ref[2 * r + 1, h]], axis=1)
            v_t = v[rs, cols]
            scores = lax.dot_general(qe[rs, cols].astype(BF16), ke[rs, cols].astype(BF16), _NT,
                                     preferred_element_type=F32)
            scores = jnp.where(causal, scores, 0.0).astype(BF16)
            o_state = jnp.dot(qb[rs, cols].astype(BF16), s_cat.astype(BF16), preferred_element_type=F32)
            outs.append(jnp.dot(scores, v_t.astype(BF16), preferred_element_type=F32)
                        + jnp.where(in_a, o_state[:, :HG_DV], o_state[:, HG_DV:]))
            inc_rhs = jnp.concatenate([jnp.where(in_a, v_t, 0.0), jnp.where(in_a, 0.0, v_t)], axis=1)
            inc = lax.dot_general(kd[rs, cols].astype(BF16), inc_rhs.astype(BF16), _TN,
                                  preferred_element_type=F32)
            dec_lhs = jnp.concatenate([p[rs, cols] for p in lf_parts] + [zero_tile], axis=0)
            decay = jnp.exp(lax.dot_general(dec_lhs.astype(BF16), dec_rhs, _TN,
                                            preferred_element_type=F32))
            s_new = decay * s_cat + inc
            s_out_ref[2 * r, h] = s_new[:, :HG_DV]
            s_out_ref[2 * r + 1, h] = s_new[:, HG_DV:]
        o = jnp.concatenate(outs, axis=0)
        y_ref[:, cols] = _head_out(o, gate[:, cols], ng[:, cols]).astype(y_ref.dtype)


def _hgrn_sample(proj, lb, norm_g, state):
    rows = HG_SAMPLE_SEQS * DEC_SEQ
    rb = N_PROMPT // rows
    cq = 3 * CONV_CH // HG_W

    def spec(off):
        return pl.BlockSpec((rows, HG_W), lambda g: (rb + g, cq + off))

    st_spec = pl.BlockSpec((HG_SAMPLE_SEQS, HG_HEADS, HG_DK, HG_DV), lambda g: (g, 0, 0, 0))
    return pl.pallas_call(
        _hgrn_sample_kernel,
        grid=(DEC_BATCH // HG_SAMPLE_SEQS,),
        in_specs=[spec(0), spec(1), spec(2), spec(3),
                  pl.BlockSpec((1, HG_W), lambda g: (0, 0)),
                  pl.BlockSpec((1, HG_W), lambda g: (0, 0)),
                  st_spec],
        out_specs=[pl.BlockSpec((rows, HG_W), lambda g: (g, 0)), st_spec],
        out_shape=[
            jax.ShapeDtypeStruct((N_SAMPLE, HG_W), BF16),
            jax.ShapeDtypeStruct((DEC_BATCH, HG_HEADS, HG_DK, HG_DV), F32),
        ],
        compiler_params=_params(("arbitrary",), 40),
        name="hgrn_sample",
    )(proj, proj, proj, proj, lb, norm_g, state)


def _out_proj_kernel(ycp_ref, yhp_ref, xp_ref, ycs_ref, yhs_ref, xs_ref, wc_ref, wh_ref, o_ref, *,
                     n_prompt_tiles):
    m = pl.program_id(1)

    def residual_mix(yc_ref, yh_ref, x_ref):
        mix = (jnp.dot(yc_ref[...], wc_ref[...], preferred_element_type=F32)
               + jnp.dot(yh_ref[...], wh_ref[...], preferred_element_type=F32))
        o_ref[...] = x_ref[...] + mix

    @pl.when(m < n_prompt_tiles)
    def _():
        residual_mix(ycp_ref, yhp_ref, xp_ref)

    @pl.when(m >= n_prompt_tiles)
    def _():
        residual_mix(ycs_ref, yhs_ref, xs_ref)


def _out_proj(yc_p, yh_p, xp, yc_s, yh_s, xs, w):
    npt = N_PROMPT // MM_ROWS

    def prompt_map(n, m):
        return jnp.minimum(m, npt - 1)

    def sample_map(n, m):
        return jnp.maximum(m - npt, 0)

    def source_specs(row_map):
        return [
            pl.BlockSpec((MM_ROWS, CONV_CH), lambda n, m: (row_map(n, m), 0)),
            pl.BlockSpec((MM_ROWS, HG_W), lambda n, m: (row_map(n, m), 0)),
            pl.BlockSpec((MM_ROWS, MM_COLS), lambda n, m: (row_map(n, m), n)),
        ]

    return pl.pallas_call(
        functools.partial(_out_proj_kernel, n_prompt_tiles=npt),
        grid=(D_MODEL // MM_COLS, N_TOK // MM_ROWS),
        in_specs=source_specs(prompt_map) + source_specs(sample_map) + [
            pl.BlockSpec((CONV_CH, MM_COLS), lambda n, m: (0, n)),
            pl.BlockSpec((HG_W, MM_COLS), lambda n, m: (1, n)),
        ],
        out_specs=pl.BlockSpec((MM_ROWS, MM_COLS), lambda n, m: (m, n)),
        out_shape=jax.ShapeDtypeStruct((N_TOK, D_MODEL), F32),
        compiler_params=_params(("arbitrary", "arbitrary"), 56),
        name="out_proj",
    )(yc_p, yh_p, xp, yc_s, yh_s, xs, w, w)


def _split_bf16(x):
    hi = x.astype(BF16)
    return hi, (x - hi.astype(F32)).astype(BF16)


_HI16 = 0xFFFF0000


def _pack_bf16_halves(x):
    half = x.shape[1] // 2
    bits = lax.bitcast_convert_type(x.astype(F32), jnp.uint32)
    return (bits[:, :half] >> 16) | (bits[:, half:] & jnp.uint32(_HI16))


def _unpack_bf16_half(packed, high):
    bits = (packed & jnp.uint32(_HI16)) if high else (packed << 16)
    return lax.bitcast_convert_type(bits, F32)


def _norm_route_kernel(h_ref, g_ref, wr_hi_ref, wr_lo_ref, br_ref, hn_ref, eidx_ref, gate_ref):
    hn = _rmsnorm_rows(h_ref[...], g_ref[...])
    hn_hi, hn_lo = _split_bf16(hn)
    hn_ref[...] = _pack_bf16_halves(hn_hi)
    logits = (jnp.dot(hn_hi, wr_hi_ref[...], preferred_element_type=F32)
              + (jnp.dot(hn_hi, wr_lo_ref[...], preferred_element_type=F32)
                 + jnp.dot(hn_lo, wr_hi_ref[...], preferred_element_type=F32))) + br_ref[...]
    lane = lax.broadcasted_iota(I32, logits.shape, 1)
    lane_f = lane.astype(F32)
    neg = -jnp.inf

    def first_argmax(vals, vmax):
        first = jnp.min(jnp.where(vals == vmax, lane_f, float(ROUTE_COLS)), axis=-1, keepdims=True)
        return first.astype(I32)

    is_grp = lane < N_GROUPS
    lg = jnp.where(is_grp, logits, neg)
    mg = jnp.max(lg, axis=-1, keepdims=True)
    g_sel = first_argmax(lg, mg)
    p_grp = 1.0 / jnp.sum(jnp.where(is_grp, jnp.exp(logits - mg), 0.0), axis=-1, keepdims=True)

    in_grp = (lane >= N_GROUPS) & (((lane - N_GROUPS) // EXP_PER_GROUP) == g_sel)
    le = jnp.where(in_grp, logits, neg)
    m1 = jnp.max(le, axis=-1, keepdims=True)
    i1 = first_argmax(le, m1)
    le2 = jnp.where(lane == i1, neg, le)
    m2 = jnp.max(le2, axis=-1, keepdims=True)
    i2 = first_argmax(le2, m2)
    e2 = jnp.exp(m2 - m1)
    gate1 = p_grp / (1.0 + e2)
    gate2 = p_grp * e2 / (1.0 + e2)
    eidx_ref[...] = jnp.where(lane == 0, i1 - N_GROUPS, jnp.where(lane == 1, i2 - N_GROUPS, 0))
    gate_ref[...] = jnp.where(lane == 0, gate1, jnp.where(lane == 1, gate2, 0.0))


def _norm_route(h, g, w_r, b_r):
    wr_hi, wr_lo = _split_bf16(w_r)
    return pl.pallas_call(
        _norm_route_kernel,
        grid=(N_TOK // NORM_ROWS,),
        in_specs=[
            pl.BlockSpec((NORM_ROWS, D_MODEL), lambda i: (i, 0)),
            pl.BlockSpec((1, D_MODEL), lambda i: (0, 0)),
            pl.BlockSpec((D_MODEL, ROUTE_COLS), lambda i: (0, 0)),
            pl.BlockSpec((D_MODEL, ROUTE_COLS), lambda i: (0, 0)),
            pl.BlockSpec((1, ROUTE_COLS), lambda i: (0, 0)),
        ],
        out_specs=[
            pl.BlockSpec((NORM_ROWS, D_MODEL // 2), lambda i: (i, 0)),
            pl.BlockSpec((NORM_ROWS, ROUTE_COLS), lambda i: (i, 0)),
            pl.BlockSpec((NORM_ROWS, ROUTE_COLS), lambda i: (i, 0)),
        ],
        out_shape=[
            jax.ShapeDtypeStruct((N_TOK, D_MODEL // 2), jnp.uint32),
            jax.ShapeDtypeStruct((N_TOK, ROUTE_COLS), I32),
            jax.ShapeDtypeStruct((N_TOK, ROUTE_COLS), F32),
        ],
        compiler_params=_params(("arbitrary",), 48),
        name="norm_route",
    )(h, g, wr_hi, wr_lo, b_r)


def _row_copy(src_hbm, row, buf, slot, r, sem):
    return pltpu.make_async_copy(src_hbm.at[pl.ds(row, 1)], buf.at[slot, pl.ds(r, 1)], sem.at[slot])


def _gather_issue(idx_ref, base, n_groups, src_hbm, buf, slot, sem):
    for r in range(n_groups * SUBLANES):
        _row_copy(src_hbm, idx_ref[base + r], buf, slot, r, sem).start()


def _gather_wait(n_groups, src_hbm, buf, slot, sem):
    for g in range(n_groups):
        rows = pl.ds(g * SUBLANES, SUBLANES)
        pltpu.make_async_copy(src_hbm.at[pl.ds(0, SUBLANES)], buf.at[slot, rows], sem.at[slot]).wait()


def _route_plan_kernel(eidx_ref, dest_ref, counts_ref, start_ref, rank_scr):
    n_tiles = N_TOK // PLAN_ROWS
    lane = lax.broadcasted_iota(I32, (PLAN_ROWS, ROUTE_COLS), 1)
    ri = lax.broadcasted_iota(I32, (PLAN_ROWS, PLAN_ROWS), 0)
    ci = lax.broadcasted_iota(I32, (PLAN_ROWS, PLAN_ROWS), 1)
    earlier = jnp.where(ci < ri, 1.0, 0.0).astype(BF16)

    def tile_rows(t):
        return pl.ds(pl.multiple_of(t * PLAN_ROWS, PLAN_ROWS), PLAN_ROWS)

    def pick(vals, col):
        return jnp.sum(jnp.where(lane == col, vals, 0.0), axis=-1, keepdims=True)

    def on_lanes01(v0, v1):
        return jnp.where(lane == 0, v0, jnp.where(lane == 1, v1, 0.0))

    def rank_tile(t, seen):
        e = eidx_ref[tile_rows(t), :]
        e0, e1 = e[:, 0:1], e[:, 1:2]
        uses = jnp.where((lane == e0) | (lane == e1), 1.0, 0.0)
        before = jnp.dot(earlier, uses.astype(BF16), preferred_element_type=F32) + seen
        rank_scr[tile_rows(t), :] = on_lanes01(pick(before, e0), pick(before, e1))
        return seen + jnp.sum(uses, axis=0, keepdims=True)

    counts = lax.fori_loop(0, n_tiles, rank_tile, jnp.zeros((1, ROUTE_COLS), F32))
    padded = jnp.ceil(counts * (1.0 / MOE_BLK)) * MOE_BLK
    ui = lax.broadcasted_iota(I32, (ROUTE_COLS, ROUTE_COLS), 0)
    uj = lax.broadcasted_iota(I32, (ROUTE_COLS, ROUTE_COLS), 1)
    start = jnp.dot(jnp.broadcast_to(padded, (SUBLANES, ROUTE_COLS)), jnp.where(ui < uj, 1.0, 0.0),
                    precision=lax.Precision.HIGHEST, preferred_element_type=F32)[0:1]
    counts_ref[...] = counts.astype(I32)
    start_ref[...] = start.astype(I32)

    def dest_tile(t, carry):
        e = eidx_ref[tile_rows(t), :]
        rank = rank_scr[tile_rows(t), :]
        d0 = pick(start, e[:, 0:1]) + rank[:, 0:1]
        d1 = pick(start, e[:, 1:2]) + rank[:, 1:2]
        dest_ref[tile_rows(t), :] = on_lanes01(d0, d1).astype(I32)
        return carry

    lax.fori_loop(0, n_tiles, dest_tile, 0)


def _route_plan(eidx):
    return pl.pallas_call(
        _route_plan_kernel,
        out_shape=[
            jax.ShapeDtypeStruct((N_TOK, ROUTE_COLS), I32),
            jax.ShapeDtypeStruct((1, ROUTE_COLS), I32),
            jax.ShapeDtypeStruct((1, ROUTE_COLS), I32),
        ],
        scratch_shapes=[pltpu.VMEM((N_TOK, ROUTE_COLS), F32)],
        compiler_params=pltpu.CompilerParams(vmem_limit_bytes=40 * MIB),
        name="route_plan",
    )(eidx)


def _dispatch_kernel(dest_ref, counts_ref, start_ref, hn_hbm, o_ref, row_tok, buf, sem):
    b = pl.program_id(0)
    nb = pl.num_programs(0)
    slot = b % 2
    groups_per_blk = MOE_BLK // SUBLANES
    last = N_EXPERTS - 1
    n_used = (start_ref[last] + counts_ref[last] + MOE_BLK - 1) // MOE_BLK

    def fetch(blk, dst_slot):
        @pl.when(blk < n_used)
        def _():
            _gather_issue(row_tok, blk * MOE_BLK, groups_per_blk, hn_hbm, buf, dst_slot, sem)

        @pl.when(blk >= n_used)
        def _():
            buf[dst_slot] = jnp.zeros(buf.shape[1:], buf.dtype)

    @pl.when(b == 0)
    def _():
        def pad_rows(e, carry):
            first = start_ref[e] + counts_ref[e]

            def body(r, c):
                row_tok[r] = 0
                return c

            lax.fori_loop(first, (first + MOE_BLK - 1) // MOE_BLK * MOE_BLK, body, 0)
            return carry

        lax.fori_loop(0, N_EXPERTS, pad_rows, 0)

        def invert(g, carry):
            for j in range(SUBLANES):
                a = g * SUBLANES + j
                row_tok[dest_ref[a]] = g * (SUBLANES // TOP_K) + j // TOP_K
            return carry

        lax.fori_loop(0, N_ASSIGN // SUBLANES, invert, 0)
        fetch(0, 0)

    @pl.when(b + 1 < nb)
    def _():
        fetch(b + 1, 1 - slot)

    @pl.when(b < n_used)
    def _():
        _gather_wait(groups_per_blk, hn_hbm, buf, slot, sem)

    o_ref[...] = buf[slot]


def _dispatch(dest_flat, counts, start, hn_packed):
    return pl.pallas_call(
        _dispatch_kernel,
        grid_spec=pltpu.PrefetchScalarGridSpec(
            num_scalar_prefetch=3,
            grid=(MOE_NB,),
            in_specs=[pl.BlockSpec(memory_space=pl.ANY)],
            out_specs=pl.BlockSpec((MOE_BLK, D_MODEL // 2), lambda b, *_: (b, 0)),
            scratch_shapes=[pltpu.SMEM((MOE_ROWS,), I32),
                            pltpu.VMEM((2, MOE_BLK, D_MODEL // 2), jnp.uint32),
                            pltpu.SemaphoreType.DMA((2,))],
        ),
        out_shape=jax.ShapeDtypeStruct((MOE_ROWS, D_MODEL // 2), jnp.uint32),
        compiler_params=_params(("arbitrary",), 16, disable_bounds_checks=True),
        name="moe_dispatch",
    )(dest_flat, counts, start, hn_packed)


def _stream_expert_chunks(blk0_ref, n_ref, n_used, in_copy, out_copy, out_buf, compute):
    c = pl.program_id(1)
    n_chunks = pl.num_programs(1)

    def for_blocks(chunk, fn):
        blk0 = blk0_ref[chunk]

        def body(i, carry):
            fn(blk0 + i, i)
            return carry

        lax.fori_loop(0, n_ref[chunk], body, 0)

    def start_loads(chunk):
        for_blocks(chunk, lambda g, i: in_copy(g, chunk % MOE_IN_SLOTS, i).start())

    @pl.when(c == 0)
    def _():
        start_loads(0)
        start_loads(1)

    @pl.when(c + 2 < n_chunks)
    def _():
        start_loads(c + 2)

    in_slot = c % MOE_IN_SLOTS
    out_slot = c % MOE_OUT_SLOTS
    for_blocks(c, lambda g, i: in_copy(g, in_slot, i).wait())

    @pl.when(c >= MOE_OUT_SLOTS)
    def _():
        for_blocks(c - MOE_OUT_SLOTS, lambda g, i: out_copy(g, out_slot, i).wait())

    for m in range(1, MOE_CHUNK_BLKS + 1):
        @pl.when(n_ref[c] == m)
        def _(m=m):
            compute(m, in_slot, out_slot)

    for_blocks(c, lambda g, i: out_copy(g, out_slot, i).start())

    @pl.when(c == n_chunks - 1)
    def _():
        for chunk in (c - 1, c):
            for_blocks(chunk, lambda g, i: out_copy(g, chunk % MOE_OUT_SLOTS, i).wait())
        out_buf[0, 0:MOE_BLK] = jnp.zeros((MOE_BLK,) + out_buf.shape[2:], out_buf.dtype)

        def start_zero(g, carry):
            out_copy(g, 0, 0).start()
            return carry

        def wait_zero(g, carry):
            out_copy(g, 0, 0).wait()
            return carry

        lax.fori_loop(n_used, MOE_NB, start_zero, 0)
        lax.fori_loop(n_used, MOE_NB, wait_zero, 0)


def _block_rows(g):
    return pl.ds(pl.multiple_of(g * MOE_BLK, MOE_BLK), MOE_BLK)


def _slot_rows(i):
    return pl.ds(pl.multiple_of(i * MOE_BLK, MOE_BLK), MOE_BLK)


def _expert_up_kernel(blk0_ref, n_ref, e_ref, nbu_ref, wg_ref, wu_ref, x_hbm, h_hbm, xbuf, hbuf, sem_x, sem_h):
    del e_ref
    cols = pl.ds(pl.multiple_of(pl.program_id(0) * MOE_FC, MOE_FC), MOE_FC)

    def x_copy(g, slot, i):
        return pltpu.make_async_copy(x_hbm.at[_block_rows(g)], xbuf.at[slot, _slot_rows(i)], sem_x.at[slot])

    def h_copy(g, slot, i):
        return pltpu.make_async_copy(hbuf.at[slot, _slot_rows(i)], h_hbm.at[_block_rows(g), cols],
                                     sem_h.at[slot])

    def compute(m, in_slot, out_slot):
        rows = m * MOE_BLK
        kc = D_MODEL // MOE_K_CHUNKS
        half_chunks = MOE_K_CHUNKS // 2
        g = u = None
        for i in range(MOE_K_CHUNKS):
            ks = slice(i * kc, (i + 1) * kc)
            ps = slice((i % half_chunks) * kc, (i % half_chunks + 1) * kc)
            xk = _unpack_bf16_half(xbuf[in_slot, :rows, ps], high=i >= half_chunks)
            gi = jnp.dot(xk, wg_ref[0, ks, :], preferred_element_type=F32)
            ui = jnp.dot(xk, wu_ref[0, ks, :], preferred_element_type=F32)
            g, u = (gi, ui) if g is None else (g + gi, u + ui)
        hbuf[out_slot, :rows] = (g * _sigmoid(g) * u).astype(hbuf.dtype)

    _stream_expert_chunks(blk0_ref, n_ref, nbu_ref[0], x_copy, h_copy, hbuf, compute)


def _chunk_scratch(in_cols, in_dtype, out_cols, out_dtype):
    rows = MOE_CHUNK_BLKS * MOE_BLK
    return [pltpu.VMEM((MOE_IN_SLOTS, rows, in_cols), in_dtype),
            pltpu.VMEM((MOE_OUT_SLOTS, rows, out_cols), out_dtype),
            pltpu.SemaphoreType.DMA((MOE_IN_SLOTS,)),
            pltpu.SemaphoreType.DMA((MOE_OUT_SLOTS,))]


def _expert_up(chunks, n_used, xb, w_g, w_u):
    def w_map(j, c, blk0, n, e, nbu):
        return (e[c], 0, j)

    return pl.pallas_call(
        _expert_up_kernel,
        grid_spec=pltpu.PrefetchScalarGridSpec(
            num_scalar_prefetch=4,
            grid=(D_EXPERT // MOE_FC, MOE_NCH),
            in_specs=[pl.BlockSpec((1, D_MODEL, MOE_FC), w_map),
                      pl.BlockSpec((1, D_MODEL, MOE_FC), w_map),
                      pl.BlockSpec(memory_space=pl.ANY)],
            out_specs=pl.BlockSpec(memory_space=pl.ANY),
            scratch_shapes=_chunk_scratch(D_MODEL // 2, jnp.uint32, MOE_FC, BF16),
        ),
        out_shape=jax.ShapeDtypeStruct((MOE_ROWS, D_EXPERT), BF16),
        compiler_params=_params(("arbitrary", "arbitrary"), 58),
        name="expert_up",
    )(*chunks, n_used, w_g, w_u, xb)


def _expert_down_kernel(blk0_ref, n_ref, e_ref, nbu_ref, wd_ref, h_hbm, y_hbm, hbuf, ybuf, sem_h, sem_y):
    del e_ref
    cols = pl.ds(pl.multiple_of(pl.program_id(0) * MOE_DC, MOE_DC), MOE_DC)

    def h_copy(g, slot, i):
        return pltpu.make_async_copy(h_hbm.at[_block_rows(g)], hbuf.at[slot, _slot_rows(i)], sem_h.at[slot])

    def y_copy(g, slot, i):
        return pltpu.make_async_copy(ybuf.at[slot, _slot_rows(i)], y_hbm.at[_block_rows(g), cols],
                                     sem_y.at[slot])

    def compute(m, in_slot, out_slot):
        rows = m * MOE_BLK
        ybuf[out_slot, :rows] = jnp.dot(hbuf[in_slot, :rows].astype(F32), wd_ref[0],
                                        preferred_element_type=F32)

    _stream_expert_chunks(blk0_ref, n_ref, nbu_ref[0], h_copy, y_copy, ybuf, compute)


def _expert_down(chunks, n_used, hb, w_d):
    return pl.pallas_call(
        _expert_down_kernel,
        grid_spec=pltpu.PrefetchScalarGridSpec(
            num_scalar_prefetch=4,
            grid=(D_MODEL // MOE_DC, MOE_NCH),
            in_specs=[pl.BlockSpec((1, D_EXPERT, MOE_DC), lambda n, c, blk0, nb, e, nbu: (e[c], 0, n)),
                      pl.BlockSpec(memory_space=pl.ANY)],
            out_specs=pl.BlockSpec(memory_space=pl.ANY),
            scratch_shapes=_chunk_scratch(D_EXPERT, BF16, MOE_DC, F32),
        ),
        out_shape=jax.ShapeDtypeStruct((MOE_ROWS, D_MODEL), F32),
        compiler_params=_params(("arbitrary", "arbitrary"), 48),
        name="expert_down",
    )(*chunks, n_used, w_d, hb)


def _combine_kernel(dest_ref, h_ref, gate_ref, g_ref, y_hbm, o_ref, buf, sem, *, tile0):
    i = pl.program_id(0)
    n = pl.num_programs(0)
    slot = i % 2
    rows = TOP_K * COMB_ROWS
    n_groups = rows // SUBLANES
    base = (tile0 + i) * rows

    @pl.when(i == 0)
    def _():
        _gather_issue(dest_ref, base, n_groups, y_hbm, buf, 0, sem)

    @pl.when(i + 1 < n)
    def _():
        _gather_issue(dest_ref, base + rows, n_groups, y_hbm, buf, 1 - slot, sem)

    _gather_wait(n_groups, y_hbm, buf, slot, sem)
    gates = gate_ref[...]
    ff = gates[:, 0:1] * buf[slot, 0:COMB_ROWS] + gates[:, 1:2] * buf[slot, COMB_ROWS:rows]
    o_ref[...] = _rmsnorm_rows(h_ref[...] + ff, g_ref[...])


def _combine(dest, h, gates, g_final, y_buf, tile0, n_tiles):
    return pl.pallas_call(
        functools.partial(_combine_kernel, tile0=tile0),
        grid_spec=pltpu.PrefetchScalarGridSpec(
            num_scalar_prefetch=1,
            grid=(n_tiles,),
            in_specs=[
                pl.BlockSpec((COMB_ROWS, D_MODEL), lambda i, d: (tile0 + i, 0)),
                pl.BlockSpec((COMB_ROWS, ROUTE_COLS), lambda i, d: (tile0 + i, 0)),
                pl.BlockSpec((1, D_MODEL), lambda i, d: (0, 0)),
                pl.BlockSpec(memory_space=pl.ANY),
            ],
            out_specs=pl.BlockSpec((COMB_ROWS, D_MODEL), lambda i, d: (i, 0)),
            scratch_shapes=[pltpu.VMEM((2, TOP_K * COMB_ROWS, D_MODEL), F32),
                            pltpu.SemaphoreType.DMA((2,))],
        ),
        out_shape=jax.ShapeDtypeStruct((n_tiles * COMB_ROWS, D_MODEL), F32),
        compiler_params=_params(("arbitrary",), 40, disable_bounds_checks=True),
        name="moe_combine",
    )(dest, h, gates, g_final, y_buf)


def _chunk_list(counts, start):
    blk0 = start // MOE_BLK
    nblk = (counts + MOE_BLK - 1) // MOE_BLK
    n_used = (blk0[-1] + nblk[-1]).astype(I32)
    nch = (nblk + MOE_CHUNK_BLKS - 1) // MOE_CHUNK_BLKS
    ch_end = jnp.cumsum(nch)
    ch = jnp.arange(MOE_NCH, dtype=I32)
    ch_e = jnp.minimum(jnp.searchsorted(ch_end, ch, side='right'), N_EXPERTS - 1)
    local = ch - (ch_end - nch)[ch_e]
    live = ch < ch_end[-1]
    ch_blk0 = (blk0[ch_e] + MOE_CHUNK_BLKS * local).astype(I32)
    ch_n = jnp.where(live, jnp.clip(nblk[ch_e] - MOE_CHUNK_BLKS * local, 0, MOE_CHUNK_BLKS), 0).astype(I32)
    ch_e = jnp.where(live, ch_e, ch_e[ch_end[-1] - 1]).astype(I32)
    return (ch_blk0, ch_n, ch_e), n_used.reshape(1)


def kernel(x_prompt, x_sample, cache_conv, state_hgrn, norm_mix_g, w_in, conv_w, lb_param, hg_norm_g,
           w_out, norm_ffn_g, w_group_router, b_group_router, w_expert_router, b_expert_router,
           w_exp_gate, w_exp_up, w_exp_down, norm_final_g):
    xp = x_prompt.reshape(N_PROMPT, D_MODEL)
    xs = x_sample.reshape(N_SAMPLE, D_MODEL)
    lb = jnp.cumsum(jax.nn.softmax(lb_param.astype(F32), axis=0), axis=0)[0].reshape(1, HG_W)

    xn = _norm_in(xp, xs, norm_mix_g[0].reshape(1, D_MODEL))
    proj = _in_proj(xn, w_in[0])

    buf = cache_conv[0]
    zeros = jnp.zeros((DEC_BATCH, DEC_SEQ - 1, CONV_CH), F32)
    e1 = jnp.concatenate([buf[:, 1:2], zeros], axis=1).reshape(N_SAMPLE, CONV_CH)
    e2 = jnp.concatenate([buf, zeros[:, :DEC_SEQ - 2]], axis=1).reshape(N_SAMPLE, CONV_CH)
    yc_p, conv_p = _conv_prompt(proj, conv_w[0])
    yc_s, u_s = _conv_sample(proj, conv_w[0], e1, e2)
    conv_s = u_s.reshape(DEC_BATCH, DEC_SEQ, CONV_CH)[:, DEC_SEQ - (CONV_K - 1):]

    ng = hg_norm_g[0].reshape(1, HG_W)
    yh_p, hgrn_p = _hgrn_prompt(proj, lb, ng)
    yh_s, hgrn_s = _hgrn_sample(proj, lb, ng, state_hgrn[0])

    h = _out_proj(yc_p, yh_p, xp, yc_s, yh_s, xs, w_out[0].astype(BF16))

    pad = jnp.zeros((D_MODEL, ROUTE_COLS - N_GROUPS - N_EXPERTS), F32)
    w_r = jnp.concatenate([w_group_router[0], w_expert_router[0], pad], axis=1)
    b_r = jnp.concatenate([b_group_router[0], b_expert_router[0], pad[0]]).reshape(1, ROUTE_COLS)
    hn_packed, eidx, gates = _norm_route(h, norm_ffn_g[0].reshape(1, D_MODEL), w_r, b_r)
    dest, counts, start = _route_plan(eidx)
    dest = dest[:, :TOP_K]
    counts, start = counts[0, :N_EXPERTS], start[0, :N_EXPERTS]
    chunks, n_used = _chunk_list(counts, start)
    xb = _dispatch(dest.reshape(N_ASSIGN), counts, start, hn_packed)
    hb = _expert_up(chunks, n_used, xb, w_exp_gate[0], w_exp_up[0])
    yb = _expert_down(chunks, n_used, hb, w_exp_down[0])

    dest_t = dest.reshape(N_TOK // COMB_ROWS, COMB_ROWS, TOP_K).transpose(0, 2, 1).reshape(N_ASSIGN)
    gf = norm_final_g.reshape(1, D_MODEL)
    y_p = _combine(dest_t, h, gates, gf, yb, 0, N_PROMPT // COMB_ROWS)
    y_s = _combine(dest_t, h, gates, gf, yb, N_PROMPT // COMB_ROWS, N_SAMPLE // COMB_ROWS)

    return (y_p.reshape(BATCH, SEQ, D_MODEL), y_s.reshape(DEC_BATCH, DEC_SEQ, D_MODEL),
            conv_p[None], hgrn_p[None], conv_s[None], hgrn_s[None])
```

```python
import functools

import jax
import jax.numpy as jnp
from jax import lax
from jax.experimental import pallas as pl
from jax.experimental.pallas import tpu as pltpu

F32 = jnp.float32
BF16 = jnp.bfloat16
I32 = jnp.int32

D_MODEL = 4096
BATCH = 4
SEQ = 2048
DEC_BATCH = 128
DEC_SEQ = 4
CONV_CH = 2048
CONV_K = 3
HG_HEADS = 16
HG_DK = 128
HG_DV = 128
HG_W = HG_HEADS * HG_DK
HG_CHUNK = 64
IN_COLS = 3 * CONV_CH + 4 * HG_W
N_GROUPS = 8
EXP_PER_GROUP = 8
N_EXPERTS = 64
TOP_K = 2
D_EXPERT = 1024
EPS = 1e-6

N_PROMPT = BATCH * SEQ
N_SAMPLE = DEC_BATCH * DEC_SEQ
N_TOK = N_PROMPT + N_SAMPLE
N_ASSIGN = N_TOK * TOP_K

LANES = 128
SUBLANES = 8
ROUTE_COLS = LANES

NORM_ROWS = 256
MM_ROWS = 512
MM_COLS = 1024
CONV_ROWS = 512
CONV_COLS = 512
HG_ROWS = 512
HG_SAMPLE_SEQS = 4
MOE_BLK = 128
MOE_NB = N_ASSIGN // MOE_BLK + N_EXPERTS
MOE_ROWS = MOE_NB * MOE_BLK
MOE_FC = 512
MOE_DC = 2048
MOE_CHUNK_BLKS = 4
MOE_NCH = N_EXPERTS + N_ASSIGN // (MOE_BLK * MOE_CHUNK_BLKS)
MOE_IN_SLOTS = 3
MOE_OUT_SLOTS = 2
MOE_K_CHUNKS = 4
PLAN_ROWS = 512
DISPATCH_PART = 32
COMB_ROWS = 128

MIB = 1 << 20


def _params(sem, vmem_mib, **kwargs):
    return pltpu.CompilerParams(dimension_semantics=sem, vmem_limit_bytes=vmem_mib * MIB, **kwargs)


def _sigmoid(x):
    return 1.0 / (1.0 + jnp.exp(-x))


def _rmsnorm_rows(x, g):
    return x * lax.rsqrt(jnp.mean(x * x, axis=-1, keepdims=True) + EPS) * g


def _norm_in_kernel(xp_ref, xs_ref, g_ref, o_ref, *, n_prompt_tiles):
    i = pl.program_id(0)

    @pl.when(i < n_prompt_tiles)
    def _():
        o_ref[...] = _rmsnorm_rows(xp_ref[...], g_ref[...]).astype(o_ref.dtype)

    @pl.when(i >= n_prompt_tiles)
    def _():
        o_ref[...] = _rmsnorm_rows(xs_ref[...], g_ref[...]).astype(o_ref.dtype)


def _norm_in(xp, xs, g):
    npt = N_PROMPT // NORM_ROWS
    nst = N_SAMPLE // NORM_ROWS
    return pl.pallas_call(
        functools.partial(_norm_in_kernel, n_prompt_tiles=npt),
        grid=(npt + nst,),
        in_specs=[
            pl.BlockSpec((NORM_ROWS, D_MODEL), lambda i: (jnp.minimum(i, npt - 1), 0)),
            pl.BlockSpec((NORM_ROWS, D_MODEL), lambda i: (jnp.maximum(i - npt, 0), 0)),
            pl.BlockSpec((1, D_MODEL), lambda i: (0, 0)),
        ],
        out_specs=pl.BlockSpec((NORM_ROWS, D_MODEL), lambda i: (i, 0)),
        out_shape=jax.ShapeDtypeStruct((N_TOK, D_MODEL), BF16),
        compiler_params=_params(("arbitrary",), 40),
        name="norm_in",
    )(xp, xs, g)


def _in_proj_kernel(x_ref, w_ref, o_ref, wb_ref):
    @pl.when(pl.program_id(1) == 0)
    def _():
        wb_ref[...] = w_ref[...].astype(BF16)

    o_ref[...] = jnp.dot(x_ref[...], wb_ref[...], preferred_element_type=F32).astype(o_ref.dtype)


def _in_proj(xn, w):
    return pl.pallas_call(
        _in_proj_kernel,
        grid=(IN_COLS // MM_COLS, N_TOK // MM_ROWS),
        in_specs=[
            pl.BlockSpec((MM_ROWS, D_MODEL), lambda n, m: (m, 0)),
            pl.BlockSpec((D_MODEL, MM_COLS), lambda n, m: (0, n)),
        ],
        out_specs=pl.BlockSpec((MM_ROWS, MM_COLS), lambda n, m: (m, n)),
        out_shape=jax.ShapeDtypeStruct((N_TOK, IN_COLS), BF16),
        scratch_shapes=[pltpu.VMEM((D_MODEL, MM_COLS), BF16)],
        compiler_params=_params(("arbitrary", "arbitrary"), 58),
        name="in_proj",
    )(xn, w)


def _conv_taps(u, um1, um2, cb, w):
    conv = um2 * w[0:1] + um1 * w[1:2] + u * w[2:3]
    return cb * conv


def _conv_prompt_kernel(cb_ref, cc_ref, ch_ref, w_ref, y_ref, nb_ref, carry_ref):
    t = pl.program_id(2)

    @pl.when(t == 0)
    def _():
        carry_ref[...] = jnp.zeros_like(carry_ref)

    u = cc_ref[...].astype(F32) * ch_ref[...].astype(F32)
    prev = carry_ref[...]
    p1 = prev[SUBLANES - 1:SUBLANES]
    p2 = prev[SUBLANES - 2:SUBLANES - 1]
    row = lax.broadcasted_iota(I32, u.shape, 0)
    um1 = jnp.where(row == 0, p1, pltpu.roll(u, 1, axis=0))
    um2 = jnp.where(row == 0, p2, jnp.where(row == 1, p1, pltpu.roll(u, 2, axis=0)))
    y_ref[...] = _conv_taps(u, um1, um2, cb_ref[...].astype(F32), w_ref[...]).astype(y_ref.dtype)
    carry_ref[...] = u[CONV_ROWS - SUBLANES:CONV_ROWS]
    nb_ref[0] = u[CONV_ROWS - (CONV_K - 1):CONV_ROWS]


def _conv_prompt(proj, conv_w):
    nt = SEQ // CONV_ROWS
    nc = CONV_CH // CONV_COLS
    return pl.pallas_call(
        _conv_prompt_kernel,
        grid=(BATCH, nc, nt),
        in_specs=[
            pl.BlockSpec((CONV_ROWS, CONV_COLS), lambda b, c, t: (b * nt + t, c)),
            pl.BlockSpec((CONV_ROWS, CONV_COLS), lambda b, c, t: (b * nt + t, nc + c)),
            pl.BlockSpec((CONV_ROWS, CONV_COLS), lambda b, c, t: (b * nt + t, 2 * nc + c)),
            pl.BlockSpec((CONV_K, CONV_COLS), lambda b, c, t: (0, c)),
        ],
        out_specs=[
            pl.BlockSpec((CONV_ROWS, CONV_COLS), lambda b, c, t: (b * nt + t, c)),
            pl.BlockSpec((1, CONV_K - 1, CONV_COLS), lambda b, c, t: (b, 0, c)),
        ],
        out_shape=[
            jax.ShapeDtypeStruct((N_PROMPT, CONV_CH), BF16),
            jax.ShapeDtypeStruct((BATCH, CONV_K - 1, CONV_CH), F32),
        ],
        scratch_shapes=[pltpu.VMEM((SUBLANES, CONV_COLS), F32)],
        compiler_params=_params(("arbitrary", "arbitrary", "arbitrary"), 32),
        name="conv_prompt",
    )(proj, proj, proj, conv_w)


def _conv_sample_kernel(cb_ref, cc_ref, ch_ref, w_ref, e1_ref, e2_ref, y_ref, u_ref):
    u =cc_ref[...].astype(F32) * ch_ref[...].astype(F32)
    tpos = lax.broadcasted_iota(I32, u.shape, 0) % DEC_SEQ
    um1 = jnp.where(tpos == 0, e1_ref[...], pltpu.roll(u, 1, axis=0))
    um2 = jnp.where(tpos <= 1, e2_ref[...], pltpu.roll(u, 2, axis=0))
    y_ref[...] = _conv_taps(u, um1, um2, cb_ref[...].astype(F32), w_ref[...]).astype(y_ref.dtype)
    u_ref[...] = u


def _conv_sample(proj, conv_w, e1, e2):
    nc = CONV_CH // CONV_COLS
    rb = N_PROMPT // N_SAMPLE
    return pl.pallas_call(
        _conv_sample_kernel,
        grid=(nc,),
        in_specs=[
            pl.BlockSpec((N_SAMPLE, CONV_COLS), lambda c: (rb, c)),
            pl.BlockSpec((N_SAMPLE, CONV_COLS), lambda c: (rb, nc + c)),
            pl.BlockSpec((N_SAMPLE, CONV_COLS), lambda c: (rb, 2 * nc + c)),
            pl.BlockSpec((CONV_K, CONV_COLS), lambda c: (0, c)),
            pl.BlockSpec((N_SAMPLE, CONV_COLS), lambda c: (0, c)),
            pl.BlockSpec((N_SAMPLE, CONV_COLS), lambda c: (0, c)),
        ],
        out_specs=[
            pl.BlockSpec((N_SAMPLE, CONV_COLS), lambda c: (0, c)),
            pl.BlockSpec((N_SAMPLE, CONV_COLS), lambda c: (0, c)),
        ],
        out_shape=[
            jax.ShapeDtypeStruct((N_SAMPLE, CONV_CH), BF16),
            jax.ShapeDtypeStruct((N_SAMPLE, CONV_CH), F32),
        ],
        compiler_params=_params(("arbitrary",), 32),
        name="conv_sample",
    )(proj, proj, proj, conv_w, e1, e2)


_NT = (((1,), (1,)), ((), ()))
_TN = (((0,), (0,)), ((), ()))


def _hgrn_factors(q, fz, lb, seq_len):
    rows, width = q.shape
    one_m_lb = 1.0 - lb
    log_f = jnp.log(lb + one_m_lb * _sigmoid(fz))
    k = one_m_lb * _sigmoid(-fz)
    row = lax.broadcasted_iota(I32, (rows, width), 0)
    tpos = row % seq_len

    b = log_f
    shift = 1
    while shift < seq_len:
        b = b + jnp.where(tpos >= shift, pltpu.roll(b, shift, axis=0), 0.0)
        shift *= 2

    n_seq = rows // seq_len
    mid = (seq_len - 1) // 2
    if seq_len % SUBLANES == 0:
        b3 = b.reshape(n_seq, seq_len, width)
        b_mid = b3[:, mid:mid + 1]
        b_end = b3[:, seq_len - 1:seq_len]

        def per_row(per_chunk):
            return jnp.broadcast_to(per_chunk, b3.shape).reshape(rows, width)

        b_ref, b_last = per_row(b_mid), per_row(b_end)
        e_ref, e_last = per_row(jnp.exp(b_mid)), per_row(jnp.exp(b_end - b_mid))
    else:
        seq = row // seq_len
        b_ref = b[mid:mid + 1]
        b_last = b[seq_len - 1:seq_len]
        for j in range(1, n_seq):
            b_ref = jnp.where(seq == j, b[j * seq_len + mid:j * seq_len + mid + 1], b_ref)
            b_last = jnp.where(seq == j, b[(j + 1) * seq_len - 1:(j + 1) * seq_len], b_last)
        e_ref, e_last = jnp.exp(b_ref), jnp.exp(b_last - b_ref)

    qe = q * jnp.exp(b - b_ref)
    ke = k * jnp.exp(b_ref - b)
    return qe, ke, qe * e_ref, ke * e_last, log_f, b_last


def _split3_bf16(x):
    hi = x.astype(BF16).astype(F32)
    rest = x - hi
    mid = rest.astype(BF16).astype(F32)
    lo = (rest - mid).astype(BF16).astype(F32)
    return hi, mid, lo


def _head_out(o, gate, norm_g):
    o_n = o * lax.rsqrt(jnp.mean(o * o, axis=-1, keepdims=True) + EPS)
    return o_n * norm_g * (gate * _sigmoid(gate))


def _hgrn_prompt_kernel(q_ref, f_ref, v_ref, g_ref, lb_ref, ng_ref, y_ref, s_out_ref):
    lb = lb_ref[...]
    ng = ng_ref[...]
    rr = lax.broadcasted_iota(I32, (HG_CHUNK, HG_CHUNK), 0)
    cc = lax.broadcasted_iota(I32, (HG_CHUNK, HG_CHUNK), 1)
    causal = cc <= rr

    def block(i, state_t):
        rows = pl.ds(pl.multiple_of(i * HG_ROWS, HG_ROWS), HG_ROWS)
        qe, ke, qb, kd, _, b_last = _hgrn_factors(q_ref[rows, :].astype(F32), f_ref[rows, :].astype(F32),
                                                  lb, HG_CHUNK)
        qe, ke, qb, kd = (a.astype(BF16) for a in (qe, ke, qb, kd))
        vb = v_ref[rows, :]
        outs = []
        for c in range(HG_ROWS // HG_CHUNK):
            cs = slice(c * HG_CHUNK, (c + 1) * HG_CHUNK)
            scores = lax.dot_general(qe[cs], ke[cs], _NT, preferred_element_type=F32)
            scores = jnp.where(causal, scores, 0.0).astype(BF16)
            outs.append(jnp.dot(scores, vb[cs], preferred_element_type=F32)
                        + lax.dot_general(qb[cs], state_t.astype(BF16), _NT, preferred_element_type=F32))
            inc_t = lax.dot_general(vb[cs], kd[cs], _TN, preferred_element_type=F32)
            state_t = jnp.exp(b_last[c * HG_CHUNK:c * HG_CHUNK + 1]) * state_t + inc_t
        o = jnp.concatenate(outs, axis=0)
        y_ref[rows, :] = _head_out(o, g_ref[rows, :].astype(F32), ng).astype(y_ref.dtype)
        return state_t

    state_t = lax.fori_loop(0, SEQ // HG_ROWS, block, jnp.zeros((HG_DV, HG_DK), F32))
    s_out_ref[0, 0] = state_t.T


def _hgrn_prompt(proj, lb, norm_g):
    cq = 3 * CONV_CH // HG_DK

    def spec(off):
        return pl.BlockSpec((SEQ, HG_DK), lambda b, h: (b, cq + off * HG_HEADS + h))

    return pl.pallas_call(
        _hgrn_prompt_kernel,
        grid=(BATCH, HG_HEADS),
        in_specs=[spec(0), spec(1), spec(2), spec(3),
                  pl.BlockSpec((1, HG_DK), lambda b, h: (0, h)),
                  pl.BlockSpec((1, HG_DV), lambda b, h: (0, h))],
        out_specs=[
            pl.BlockSpec((SEQ, HG_DV), lambda b, h: (b, h)),
            pl.BlockSpec((1, 1, HG_DK, HG_DV), lambda b, h: (b, h, 0, 0)),
        ],
        out_shape=[
            jax.ShapeDtypeStruct((N_PROMPT, HG_W), BF16),
            jax.ShapeDtypeStruct((BATCH, HG_HEADS, HG_DK, HG_DV), F32),
        ],
        compiler_params=_params(("arbitrary", "arbitrary"), 32),
        name="hgrn_prompt",
    )(proj, proj, proj, proj, lb, norm_g)


def _hgrn_sample_kernel(q_ref, f_ref, v_ref, g_ref, lb_ref, ng_ref, s_ref, y_ref, s_out_ref):
    rows = HG_SAMPLE_SEQS * DEC_SEQ
    qe, ke, qb, kd, log_f, _ = _hgrn_factors(q_ref[...].astype(F32), f_ref[...].astype(F32),
                                            lb_ref[...], DEC_SEQ)
    lf_parts = _split3_bf16(log_f)
    v = v_ref[...].astype(F32)
    gate = g_ref[...].astype(F32)
    ng = ng_ref[...]

    in_a = lax.broadcasted_iota(I32, (SUBLANES, HG_DV), 0) < DEC_SEQ
    rr = lax.broadcasted_iota(I32, (SUBLANES, SUBLANES), 0)
    cc = lax.broadcasted_iota(I32, (SUBLANES, SUBLANES), 1)
    causal = (cc <= rr) & ((rr // DEC_SEQ) == (cc // DEC_SEQ))
    zero_tile = jnp.zeros((SUBLANES, HG_DV), F32)
    sel = jnp.concatenate([in_a.astype(F32), 1.0 - in_a.astype(F32)], axis=1)
    dec_rhs = jnp.concatenate([sel, sel, sel, jnp.zeros_like(sel)], axis=0).astype(BF16)

    for h in range(HG_HEADS):
        cols = slice(h * HG_DK, (h + 1) * HG_DK)
        outs = []
        for r in range(rows // SUBLANES):
            rs = slice(r * SUBLANES, (r + 1) * SUBLANES)
            s_cat = jnp.concatenate([s_ref[2 * r, h], s_ref[2 * r + 1, h]], axis=1)
            v_t = v[rs, cols]
            scores = lax.dot_general(qe[rs, cols].astype(BF16), ke[rs, cols].astype(BF16), _NT,
                                     preferred_element_type=F32)
            scores = jnp.where(causal, scores, 0.0).astype(BF16)
            o_state = jnp.dot(qb[rs, cols].astype(BF16), s_cat.astype(BF16), preferred_element_type=F32)
            outs.append(jnp.dot(scores, v_t.astype(BF16), preferred_element_type=F32)
                        + jnp.where(in_a, o_state[:, :HG_DV], o_state[:, HG_DV:]))
            inc_rhs = jnp.concatenate([jnp.where(in_a, v_t, 0.0), jnp.where(in_a, 0.0, v_t)], axis=1)
            inc = lax.dot_general(kd[rs, cols].astype(BF16), inc_rhs.astype(BF16), _TN,
                                  preferred_element_type=F32)
            dec_lhs = jnp.concatenate([p[rs, cols] for p in lf_parts] + [zero_tile], axis=0)
            decay = jnp.exp(lax.dot_general(dec_lhs.astype(BF16), dec_rhs, _TN,
                                            preferred_element_type=F32))
            s_new = decay * s_cat + inc
            s_out_ref[2 * r, h] = s_new[:, :HG_DV]
            s_out_ref[2 * r + 1, h] = s_new[:, HG_DV:]
        o = jnp.concatenate(outs, axis=0)
        y_ref[:, cols] = _head_out(o, gate[:, cols], ng[:, cols]).astype(y_ref.dtype)


def _hgrn_sample(proj, lb, norm_g, state):
    rows = HG_SAMPLE_SEQS * DEC_SEQ
    rb = N_PROMPT // rows
    cq = 3 * CONV_CH // HG_W

    def spec(off):
        return pl.BlockSpec((rows, HG_W), lambda g: (rb + g, cq + off))

    st_spec = pl.BlockSpec((HG_SAMPLE_SEQS, HG_HEADS, HG_DK, HG_DV), lambda g: (g, 0, 0, 0))
    return pl.pallas_call(
        _hgrn_sample_kernel,
        grid=(DEC_BATCH // HG_SAMPLE_SEQS,),
        in_specs=[spec(0), spec(1), spec(2), spec(3),
                  pl.BlockSpec((1, HG_W), lambda g: (0, 0)),
                  pl.BlockSpec((1, HG_W), lambda g: (0, 0)),
                  st_spec],
        out_specs=[pl.BlockSpec((rows, HG_W), lambda g: (g, 0)), st_spec],
        out_shape=[
            jax.ShapeDtypeStruct((N_SAMPLE, HG_W), BF16),
            jax.ShapeDtypeStruct((DEC_BATCH, HG_HEADS, HG_DK, HG_DV), F32),
        ],
        compiler_params=_params(("arbitrary",), 40),
        name="hgrn_sample",
    )(proj, proj, proj, proj, lb, norm_g, state)


def _out_proj_kernel(ycp_ref, yhp_ref, xp_ref, ycs_ref, yhs_ref, xs_ref, wc_ref, wh_ref, o_ref, *,
                     n_prompt_tiles):
    m = pl.program_id(1)

    def residual_mix(yc_ref, yh_ref, x_ref):
        mix = (jnp.dot(yc_ref[...], wc_ref[...], preferred_element_type=F32)
               + jnp.dot(yh_ref[...], wh_ref[...], preferred_element_type=F32))
        o_ref[...] = x_ref[...] + mix

    @pl.when(m < n_prompt_tiles)
    def _():
        residual_mix(ycp_ref, yhp_ref, xp_ref)

    @pl.when(m >= n_prompt_tiles)
    def _():
        residual_mix(ycs_ref, yhs_ref, xs_ref)


def _out_proj(yc_p, yh_p, xp, yc_s, yh_s, xs, w):
    npt = N_PROMPT // MM_ROWS

    def prompt_map(n, m):
        return jnp.minimum(m, npt - 1)

    def sample_map(n, m):
        return jnp.maximum(m - npt, 0)

    def source_specs(row_map):
        return [
            pl.BlockSpec((MM_ROWS, CONV_CH), lambda n, m: (row_map(n, m), 0)),
            pl.BlockSpec((MM_ROWS, HG_W), lambda n, m: (row_map(n, m), 0)),
            pl.BlockSpec((MM_ROWS, MM_COLS), lambda n, m: (row_map(n, m), n)),
        ]

    return pl.pallas_call(
        functools.partial(_out_proj_kernel, n_prompt_tiles=npt),
        grid=(D_MODEL // MM_COLS, N_TOK // MM_ROWS),
        in_specs=source_specs(prompt_map) + source_specs(sample_map) + [
            pl.BlockSpec((CONV_CH, MM_COLS), lambda n, m: (0, n)),
            pl.BlockSpec((HG_W, MM_COLS), lambda n, m: (1, n)),
        ],
        out_specs=pl.BlockSpec((MM_ROWS, MM_COLS), lambda n, m: (m, n)),
        out_shape=jax.ShapeDtypeStruct((N_TOK, D_MODEL), F32),
        compiler_params=_params(("arbitrary", "arbitrary"), 56),
        name="out_proj",
    )(yc_p, yh_p, xp, yc_s, yh_s, xs, w, w)


def _split_bf16(x):
    hi = x.astype(BF16)
    return hi, (x - hi.astype(F32)).astype(BF16)


_HI16 = 0xFFFF0000


def _pack_bf16_halves(x):
    half = x.shape[1] // 2
    bits = lax.bitcast_convert_type(x.astype(F32), jnp.uint32)
    return (bits[:, :half] >> 16) | (bits[:, half:] & jnp.uint32(_HI16))


def _unpack_bf16_half(packed, high):
    bits = (packed & jnp.uint32(_HI16)) if high else (packed << 16)
    return lax.bitcast_convert_type(bits, F32)


def _norm_route_kernel(h_ref, g_ref, wr_hi_ref, wr_lo_ref, br_ref, hn_ref, eidx_ref, gate_ref):
    hn = _rmsnorm_rows(h_ref[...], g_ref[...])
    hn_hi, hn_lo = _split_bf16(hn)
    hn_ref[...] = _pack_bf16_halves(hn_hi)
    logits = (jnp.dot(hn_hi, wr_hi_ref[...], preferred_element_type=F32)
              + (jnp.dot(hn_hi, wr_lo_ref[...], preferred_element_type=F32)
                 + jnp.dot(hn_lo, wr_hi_ref[...], preferred_element_type=F32))) + br_ref[...]
    lane = lax.broadcasted_iota(I32, logits.shape, 1)
    lane_f = lane.astype(F32)
    neg = -jnp.inf

    def first_argmax(vals, vmax):
        first = jnp.min(jnp.where(vals == vmax, lane_f, float(ROUTE_COLS)), axis=-1, keepdims=True)
        return first.astype(I32)

    is_grp = lane < N_GROUPS
    lg = jnp.where(is_grp, logits, neg)
    mg = jnp.max(lg, axis=-1, keepdims=True)
    g_sel = first_argmax(lg, mg)
    p_grp = 1.0 / jnp.sum(jnp.where(is_grp, jnp.exp(logits - mg), 0.0), axis=-1, keepdims=True)

    in_grp = (lane >= N_GROUPS) & (((lane - N_GROUPS) // EXP_PER_GROUP) == g_sel)
    le = jnp.where(in_grp, logits, neg)
    m1 = jnp.max(le, axis=-1, keepdims=True)
    i1 = first_argmax(le, m1)
    le2 = jnp.where(lane == i1, neg, le)
    m2 = jnp.max(le2, axis=-1, keepdims=True)
    i2 = first_argmax(le2, m2)
    e2 = jnp.exp(m2 - m1)
    gate1 = p_grp / (1.0 + e2)
    gate2 = p_grp * e2 / (1.0 + e2)
    eidx_ref[...] = jnp.where(lane == 0, i1 - N_GROUPS, jnp.where(lane == 1, i2 - N_GROUPS, 0))
    gate_ref[...] = jnp.where(lane == 0, gate1, jnp.where(lane == 1, gate2, 0.0))


def _norm_route(h, g, w_r, b_r):
    wr_hi, wr_lo = _split_bf16(w_r)
    return pl.pallas_call(
        _norm_route_kernel,
        grid=(N_TOK // NORM_ROWS,),
        in_specs=[
            pl.BlockSpec((NORM_ROWS, D_MODEL), lambda i: (i, 0)),
            pl.BlockSpec((1, D_MODEL), lambda i: (0, 0)),
            pl.BlockSpec((D_MODEL, ROUTE_COLS), lambda i: (0, 0)),
            pl.BlockSpec((D_MODEL, ROUTE_COLS), lambda i: (0, 0)),
            pl.BlockSpec((1, ROUTE_COLS), lambda i: (0, 0)),
        ],
        out_specs=[
            pl.BlockSpec((NORM_ROWS, D_MODEL // 2), lambda i: (i, 0)),
            pl.BlockSpec((NORM_ROWS, ROUTE_COLS), lambda i: (i, 0)),
            pl.BlockSpec((NORM_ROWS, ROUTE_COLS), lambda i: (i, 0)),
        ],
        out_shape=[
            jax.ShapeDtypeStruct((N_TOK, D_MODEL // 2), jnp.uint32),
            jax.ShapeDtypeStruct((N_TOK, ROUTE_COLS), I32),
            jax.ShapeDtypeStruct((N_TOK, ROUTE_COLS), F32),
        ],
        compiler_params=_params(("arbitrary",), 48),
        name="norm_route",
    )(h, g, wr_hi, wr_lo, b_r)


def _row_copy(src_hbm, row, buf, slot, r, sem):
    return pltpu.make_async_copy(src_hbm.at[pl.ds(row, 1)], buf.at[slot, pl.ds(r, 1)], sem.at[slot])


def _gather_issue(idx_ref, base, row0, n_rows, src_hbm, buf, slot, sem):
    for r in range(row0, row0 + n_rows):
        _row_copy(src_hbm, idx_ref[base + r], buf, slot, r, sem).start()


def _gather_wait(row0, n_rows, src_hbm, buf, slot, sem):
    for r in range(row0, row0 + n_rows, SUBLANES):
        rows = pl.ds(r, SUBLANES)
        pltpu.make_async_copy(src_hbm.at[pl.ds(0, SUBLANES)], buf.at[slot, rows], sem.at[slot]).wait()


def _route_plan_kernel(eidx_ref, dest_ref, counts_ref, start_ref, rank_scr):
    n_tiles = N_TOK // PLAN_ROWS
    lane = lax.broadcasted_iota(I32, (PLAN_ROWS, ROUTE_COLS), 1)
    ri = lax.broadcasted_iota(I32, (PLAN_ROWS, PLAN_ROWS), 0)
    ci = lax.broadcasted_iota(I32, (PLAN_ROWS, PLAN_ROWS), 1)
    earlier = jnp.where(ci < ri, 1.0, 0.0).astype(BF16)

    def tile_rows(t):
        return pl.ds(pl.multiple_of(t * PLAN_ROWS, PLAN_ROWS), PLAN_ROWS)

    def pick(vals, col):
        return jnp.sum(jnp.where(lane == col, vals, 0.0), axis=-1, keepdims=True)

    def on_lanes01(v0, v1):
        return jnp.where(lane == 0, v0, jnp.where(lane == 1, v1, 0.0))

    def rank_tile(t, seen):
        e = eidx_ref[tile_rows(t), :]
        e0, e1 = e[:, 0:1], e[:, 1:2]
        uses = jnp.where((lane == e0) | (lane == e1), 1.0, 0.0)
        before = jnp.dot(earlier, uses.astype(BF16), preferred_element_type=F32) + seen
        rank_scr[tile_rows(t), :] = on_lanes01(pick(before, e0), pick(before, e1))
        return seen + jnp.sum(uses, axis=0, keepdims=True)

    counts = lax.fori_loop(0, n_tiles, rank_tile, jnp.zeros((1, ROUTE_COLS), F32))
    padded = jnp.ceil(counts * (1.0 / MOE_BLK)) * MOE_BLK
    ui = lax.broadcasted_iota(I32, (ROUTE_COLS, ROUTE_COLS), 0)
    uj = lax.broadcasted_iota(I32, (ROUTE_COLS, ROUTE_COLS), 1)
    start = jnp.dot(jnp.broadcast_to(padded, (SUBLANES, ROUTE_COLS)), jnp.where(ui < uj, 1.0, 0.0),
                    precision=lax.Precision.HIGHEST, preferred_element_type=F32)[0:1]
    counts_ref[...] = counts.astype(I32)
    start_ref[...] = start.astype(I32)

    def dest_tile(t, carry):
        e = eidx_ref[tile_rows(t), :]
        rank = rank_scr[tile_rows(t), :]
        d0 = pick(start, e[:, 0:1]) + rank[:, 0:1]
        d1 = pick(start, e[:, 1:2]) + rank[:, 1:2]
        dest_ref[tile_rows(t), :] = on_lanes01(d0, d1).astype(I32)
        return carry

    lax.fori_loop(0, n_tiles, dest_tile, 0)


def _route_plan(eidx):
    return pl.pallas_call(
        _route_plan_kernel,
        out_shape=[
            jax.ShapeDtypeStruct((N_TOK, ROUTE_COLS), I32),
            jax.ShapeDtypeStruct((1, ROUTE_COLS), I32),
            jax.ShapeDtypeStruct((1, ROUTE_COLS), I32),
        ],
        scratch_shapes=[pltpu.VMEM((N_TOK, ROUTE_COLS), F32)],
        compiler_params=pltpu.CompilerParams(vmem_limit_bytes=40 * MIB),
        name="route_plan",
    )(eidx)


def _dispatch_kernel(dest_ref, counts_ref, start_ref, hn_hbm, o_ref, row_tok, n_parts, buf, sem):
    b = pl.program_id(0)
    nb = pl.num_programs(0)
    slot = b % 2
    parts_per_blk = MOE_BLK // DISPATCH_PART

    def fetch(blk, dst_slot):
        for p in range(parts_per_blk):
            @pl.when(p < n_parts[blk])
            def _(p=p):
                _gather_issue(row_tok, blk * MOE_BLK, p * DISPATCH_PART, DISPATCH_PART, hn_hbm, buf,
                              dst_slot, sem)

            @pl.when(p >= n_parts[blk])
            def _(p=p):
                buf[dst_slot, p * DISPATCH_PART:(p + 1) * DISPATCH_PART] = jnp.zeros(
                    (DISPATCH_PART,) + buf.shape[2:], buf.dtype)

    @pl.when(b == 0)
    def _():
        def clear(blk, carry):
            n_parts[blk] = 0
            return carry

        lax.fori_loop(0, MOE_NB, clear, 0)

        def per_expert(e, carry):
            count, first = counts_ref[e], start_ref[e]

            def per_block(i, c):
                rows = jnp.minimum(count - i * MOE_BLK, MOE_BLK)
                parts = (rows + DISPATCH_PART - 1) // DISPATCH_PART
                n_parts[first // MOE_BLK + i] = parts

                def pad_row(r, c2):
                    row_tok[r] = r % N_PROMPT
                    return c2

                row0 = first + i * MOE_BLK
                lax.fori_loop(row0 + rows, row0 + parts * DISPATCH_PART, pad_row, 0)
                return c

            lax.fori_loop(0, (count + MOE_BLK - 1) // MOE_BLK, per_block, 0)
            return carry

        lax.fori_loop(0, N_EXPERTS, per_expert, 0)

        def invert(g, carry):
            for j in range(SUBLANES):
                a = g * SUBLANES + j
                row_tok[dest_ref[a]] = g * (SUBLANES // TOP_K) + j // TOP_K
            return carry

        lax.fori_loop(0, N_ASSIGN // SUBLANES, invert, 0)
        fetch(0, 0)

    @pl.when(b + 1 < nb)
    def _():
        fetch(b + 1, 1 - slot)

    for p in range(parts_per_blk):
        @pl.when(p < n_parts[b])
        def _(p=p):
            _gather_wait(p * DISPATCH_PART, DISPATCH_PART, hn_hbm, buf, slot, sem)

    o_ref[...] = buf[slot]


def _dispatch(dest_flat, counts, start, hn_packed):
    return pl.pallas_call(
        _dispatch_kernel,
        grid_spec=pltpu.PrefetchScalarGridSpec(
            num_scalar_prefetch=3,
            grid=(MOE_NB,),
            in_specs=[pl.BlockSpec(memory_space=pl.ANY)],
            out_specs=pl.BlockSpec((MOE_BLK, D_MODEL // 2), lambda b, *_: (b, 0)),
            scratch_shapes=[pltpu.SMEM((MOE_ROWS,), I32),
                            pltpu.SMEM((MOE_NB,), I32),
                            pltpu.VMEM((2, MOE_BLK, D_MODEL // 2), jnp.uint32),
                            pltpu.SemaphoreType.DMA((2,))],
        ),
        out_shape=jax.ShapeDtypeStruct((MOE_ROWS, D_MODEL // 2), jnp.uint32),
        compiler_params=_params(("arbitrary",), 16, disable_bounds_checks=True),
        name="moe_dispatch",
    )(dest_flat, counts, start, hn_packed)


def _stream_expert_chunks(blk0_ref, n_ref, n_used, in_copy, out_copy, out_buf, compute):
    c = pl.program_id(1)
    n_chunks = pl.num_programs(1)

    def for_blocks(chunk, fn):
        blk0 = blk0_ref[chunk]

        def body(i, carry):
            fn(blk0 + i, i)
            return carry

        lax.fori_loop(0, n_ref[chunk], body, 0)

    def start_loads(chunk):
        for_blocks(chunk, lambda g, i: in_copy(g, chunk % MOE_IN_SLOTS, i).start())

    @pl.when(c == 0)
    def _():
        start_loads(0)
        start_loads(1)

    @pl.when(c + 2 < n_chunks)
    def _():
        start_loads(c + 2)

    in_slot = c % MOE_IN_SLOTS
    out_slot = c % MOE_OUT_SLOTS
    for_blocks(c, lambda g, i: in_copy(g, in_slot, i).wait())

    @pl.when(c >= MOE_OUT_SLOTS)
    def _():
        for_blocks(c - MOE_OUT_SLOTS, lambda g, i: out_copy(g, out_slot, i).wait())

    for m in range(1, MOE_CHUNK_BLKS + 1):
        @pl.when(n_ref[c] == m)
        def _(m=m):
            compute(m, in_slot, out_slot)

    for_blocks(c, lambda g, i: out_copy(g, out_slot, i).start())

    @pl.when(c == n_chunks - 1)
    def _():
        for chunk in (c - 1, c):
            for_blocks(chunk, lambda g, i: out_copy(g, chunk % MOE_OUT_SLOTS, i).wait())
        out_buf[0, 0:MOE_BLK] = jnp.zeros((MOE_BLK,) + out_buf.shape[2:], out_buf.dtype)

        def start_zero(g, carry):
            out_copy(g, 0, 0).start()
            return carry

        def wait_zero(g, carry):
            out_copy(g, 0, 0).wait()
            return carry

        lax.fori_loop(n_used, MOE_NB, start_zero, 0)
        lax.fori_loop(n_used, MOE_NB, wait_zero, 0)


def _block_rows(g):
    return pl.ds(pl.multiple_of(g * MOE_BLK, MOE_BLK), MOE_BLK)


def _slot_rows(i):
    return pl.ds(pl.multiple_of(i * MOE_BLK, MOE_BLK), MOE_BLK)


def _expert_up_kernel(blk0_ref, n_ref, e_ref, nbu_ref, wg_ref, wu_ref, x_hbm, h_hbm, xbuf, hbuf, sem_x, sem_h):
    del e_ref
    cols = pl.ds(pl.multiple_of(pl.program_id(0) * MOE_FC, MOE_FC), MOE_FC)

    def x_copy(g, slot, i):
        return pltpu.make_async_copy(x_hbm.at[_block_rows(g)], xbuf.at[slot, _slot_rows(i)], sem_x.at[slot])

    def h_copy(g, slot, i):
        return pltpu.make_async_copy(hbuf.at[slot, _slot_rows(i)], h_hbm.at[_block_rows(g), cols],
                                     sem_h.at[slot])

    def compute(m, in_slot, out_slot):
        rows = m * MOE_BLK
        kc = D_MODEL // MOE_K_CHUNKS
        half_chunks = MOE_K_CHUNKS // 2
        g = u = None
        for i in range(MOE_K_CHUNKS):
            ks = slice(i * kc, (i + 1) * kc)
            ps = slice((i % half_chunks) * kc, (i % half_chunks + 1) * kc)
            xk = _unpack_bf16_half(xbuf[in_slot, :rows, ps], high=i >= half_chunks)
            gi = jnp.dot(xk, wg_ref[0, ks, :], preferred_element_type=F32)
            ui = jnp.dot(xk, wu_ref[0, ks, :], preferred_element_type=F32)
            g, u = (gi, ui) if g is None else (g + gi, u + ui)
        hbuf[out_slot, :rows] = (g * _sigmoid(g) * u).astype(hbuf.dtype)

    _stream_expert_chunks(blk0_ref, n_ref, nbu_ref[0], x_copy, h_copy, hbuf, compute)


def _chunk_scratch(in_cols, in_dtype, out_cols, out_dtype):
    rows = MOE_CHUNK_BLKS * MOE_BLK
    return [pltpu.VMEM((MOE_IN_SLOTS, rows, in_cols), in_dtype),
            pltpu.VMEM((MOE_OUT_SLOTS, rows, out_cols), out_dtype),
            pltpu.SemaphoreType.DMA((MOE_IN_SLOTS,)),
            pltpu.SemaphoreType.DMA((MOE_OUT_SLOTS,))]


def _expert_up(chunks, n_used, xb, w_g, w_u):
    def w_map(j, c, blk0, n, e, nbu):
        return (e[c], 0, j)

    return pl.pallas_call(
        _expert_up_kernel,
        grid_spec=pltpu.PrefetchScalarGridSpec(
            num_scalar_prefetch=4,
            grid=(D_EXPERT // MOE_FC, MOE_NCH),
            in_specs=[pl.BlockSpec((1, D_MODEL, MOE_FC), w_map),
                      pl.BlockSpec((1, D_MODEL, MOE_FC), w_map),
                      pl.BlockSpec(memory_space=pl.ANY)],
            out_specs=pl.BlockSpec(memory_space=pl.ANY),
            scratch_shapes=_chunk_scratch(D_MODEL // 2, jnp.uint32, MOE_FC, BF16),
        ),
        out_shape=jax.ShapeDtypeStruct((MOE_ROWS, D_EXPERT), BF16),
        compiler_params=_params(("arbitrary", "arbitrary"), 58),
        name="expert_up",
    )(*chunks, n_used, w_g, w_u, xb)


def _expert_down_kernel(blk0_ref, n_ref, e_ref, nbu_ref, wd_ref, h_hbm, y_hbm, hbuf, ybuf, sem_h, sem_y):
    del e_ref
    wc = MOE_DC // 2
    cols = pl.ds(pl.multiple_of(pl.program_id(0) * wc, wc), wc)

    def h_copy(g, slot, i):
        return pltpu.make_async_copy(h_hbm.at[_block_rows(g)], hbuf.at[slot, _slot_rows(i)], sem_h.at[slot])

    def y_copy(g, slot, i):
        return pltpu.make_async_copy(ybuf.at[slot, _slot_rows(i)], y_hbm.at[_block_rows(g), cols],
                                     sem_y.at[slot])

    def compute(m, in_slot, out_slot):
        rows = m * MOE_BLK
        y = jnp.dot(hbuf[in_slot, :rows].astype(F32), wd_ref[0], preferred_element_type=F32)
        ybuf[out_slot, :rows] = _pack_bf16_halves(y.astype(BF16))

    _stream_expert_chunks(blk0_ref, n_ref, nbu_ref[0], h_copy, y_copy, ybuf, compute)


def _expert_down(chunks, n_used, hb, w_d):
    return pl.pallas_call(
        _expert_down_kernel,
        grid_spec=pltpu.PrefetchScalarGridSpec(
            num_scalar_prefetch=4,
            grid=(D_MODEL // MOE_DC, MOE_NCH),
            in_specs=[pl.BlockSpec((1, D_EXPERT, MOE_DC), lambda n, c, blk0, nb, e, nbu: (e[c], 0, n)),
                      pl.BlockSpec(memory_space=pl.ANY)],
            out_specs=pl.BlockSpec(memory_space=pl.ANY),
            scratch_shapes=_chunk_scratch(D_EXPERT, BF16, MOE_DC // 2, jnp.uint32),
        ),
        out_shape=jax.ShapeDtypeStruct((MOE_ROWS, D_MODEL // 2), jnp.uint32),
        compiler_params=_params(("arbitrary", "arbitrary"), 48),
        name="expert_down",
    )(*chunks, n_used, w_d, hb)


def _combine_kernel(dest_ref, h_ref, gate_ref, g_ref, y_hbm, o_ref, buf, sem, *, tile0):
    i = pl.program_id(0)
    n = pl.num_programs(0)
    slot = i % 2
    rows = TOP_K * COMB_ROWS
    base = (tile0 + i) * rows

    @pl.when(i == 0)
    def _():
        _gather_issue(dest_ref, base, 0, rows, y_hbm, buf, 0, sem)

    @pl.when(i + 1 < n)
    def _():
        _gather_issue(dest_ref, base + rows, 0, rows, y_hbm, buf, 1 - slot, sem)

    _gather_wait(0, rows, y_hbm, buf, slot, sem)

    def expert_rows(k):
        words = buf[slot, k * COMB_ROWS:(k + 1) * COMB_ROWS]
        wc = MOE_DC // 2
        parts = []
        for p in range(D_MODEL // MOE_DC):
            w = words[:, p * wc:(p + 1) * wc]
            parts += [_unpack_bf16_half(w, high=False), _unpack_bf16_half(w, high=True)]
        return jnp.concatenate(parts, axis=1)

    gates = gate_ref[...]
    ff = gates[:, 0:1] * expert_rows(0) + gates[:, 1:2] * expert_rows(1)
    o_ref[...] = _rmsnorm_rows(h_ref[...] + ff, g_ref[...])


def _combine(dest, h, gates, g_final, y_buf, tile0, n_tiles):
    return pl.pallas_call(
        functools.partial(_combine_kernel, tile0=tile0),
        grid_spec=pltpu.PrefetchScalarGridSpec(
            num_scalar_prefetch=1,
            grid=(n_tiles,),
            in_specs=[
                pl.BlockSpec((COMB_ROWS, D_MODEL), lambda i, d: (tile0 + i, 0)),
                pl.BlockSpec((COMB_ROWS, ROUTE_COLS), lambda i, d: (tile0 + i, 0)),
                pl.BlockSpec((1, D_MODEL), lambda i, d: (0, 0)),
                pl.BlockSpec(memory_space=pl.ANY),
            ],
            out_specs=pl.BlockSpec((COMB_ROWS, D_MODEL), lambda i, d: (i, 0)),
            scratch_shapes=[pltpu.VMEM((2, TOP_K * COMB_ROWS, D_MODEL // 2), jnp.uint32),
                            pltpu.SemaphoreType.DMA((2,))],
        ),
        out_shape=jax.ShapeDtypeStruct((n_tiles * COMB_ROWS, D_MODEL), F32),
        compiler_params=_params(("arbitrary",), 40, disable_bounds_checks=True),
        name="moe_combine",
    )(dest, h, gates, g_final, y_buf)


def _chunk_list(counts, start):
    blk0 = start // MOE_BLK
    nblk = (counts + MOE_BLK - 1) // MOE_BLK
    n_used = (blk0[-1] + nblk[-1]).astype(I32)
    nch = (nblk + MOE_CHUNK_BLKS - 1) // MOE_CHUNK_BLKS
    ch_end = jnp.cumsum(nch)
    ch = jnp.arange(MOE_NCH, dtype=I32)
    ch_e = jnp.minimum(jnp.searchsorted(ch_end, ch, side='right'), N_EXPERTS - 1)
    local = ch - (ch_end - nch)[ch_e]
    live = ch < ch_end[-1]
    ch_blk0 = (blk0[ch_e] + MOE_CHUNK_BLKS * local).astype(I32)
    ch_n = jnp.where(live, jnp.clip(nblk[ch_e] - MOE_CHUNK_BLKS * local, 0, MOE_CHUNK_BLKS), 0).astype(I32)
    ch_e = jnp.where(live, ch_e, ch_e[ch_end[-1] - 1]).astype(I32)
    return (ch_blk0, ch_n, ch_e), n_used.reshape(1)


def kernel(x_prompt, x_sample, cache_conv, state_hgrn, norm_mix_g, w_in, conv_w, lb_param, hg_norm_g,
           w_out, norm_ffn_g, w_group_router, b_group_router, w_expert_router, b_expert_router,
           w_exp_gate, w_exp_up, w_exp_down, norm_final_g):
    xp = x_prompt.reshape(N_PROMPT, D_MODEL)
    xs = x_sample.reshape(N_SAMPLE, D_MODEL)
    lb = jnp.cumsum(jax.nn.softmax(lb_param.astype(F32), axis=0), axis=0)[0].reshape(1, HG_W)

    xn = _norm_in(xp, xs, norm_mix_g[0].reshape(1, D_MODEL))
    proj = _in_proj(xn, w_in[0])

    buf = cache_conv[0]
    zeros = jnp.zeros((DEC_BATCH, DEC_SEQ - 1, CONV_CH), F32)
    e1 = jnp.concatenate([buf[:, 1:2], zeros], axis=1).reshape(N_SAMPLE, CONV_CH)
    e2 = jnp.concatenate([buf, zeros[:, :DEC_SEQ - 2]], axis=1).reshape(N_SAMPLE, CONV_CH)
    yc_p, conv_p = _conv_prompt(proj, conv_w[0])
    yc_s, u_s = _conv_sample(proj, conv_w[0], e1, e2)
    conv_s = u_s.reshape(DEC_BATCH, DEC_SEQ, CONV_CH)[:, DEC_SEQ - (CONV_K - 1):]

    ng = hg_norm_g[0].reshape(1, HG_W)
    yh_p, hgrn_p = _hgrn_prompt(proj, lb, ng)
    yh_s, hgrn_s = _hgrn_sample(proj, lb, ng, state_hgrn[0])

    h = _out_proj(yc_p, yh_p, xp, yc_s, yh_s, xs, w_out[0].astype(BF16))

    pad = jnp.zeros((D_MODEL, ROUTE_COLS - N_GROUPS - N_EXPERTS), F32)
    w_r = jnp.concatenate([w_group_router[0], w_expert_router[0], pad], axis=1)
    b_r = jnp.concatenate([b_group_router[0], b_expert_router[0], pad[0]]).reshape(1, ROUTE_COLS)
    hn_packed, eidx, gates = _norm_route(h, norm_ffn_g[0].reshape(1, D_MODEL), w_r, b_r)
    dest, counts, start = _route_plan(eidx)
    dest = dest[:, :TOP_K]
    counts, start = counts[0, :N_EXPERTS], start[0, :N_EXPERTS]
    chunks, n_used = _chunk_list(counts, start)
    xb = _dispatch(dest.reshape(N_ASSIGN), counts, start, hn_packed)
    hb = _expert_up(chunks, n_used, xb, w_exp_gate[0], w_exp_up[0])
    yb = _expert_down(chunks, n_used, hb, w_exp_down[0])

    dest_t = dest.reshape(N_TOK // COMB_ROWS, COMB_ROWS, TOP_K).transpose(0, 2, 1).reshape(N_ASSIGN)
    gf = norm_final_g.reshape(1, D_MODEL)
    y_p = _combine(dest_t, h, gates, gf, yb, 0, N_PROMPT // COMB_ROWS)
    y_s = _combine(dest_t, h, gates, gf, yb, N_PROMPT // COMB_ROWS, N_SAMPLE // COMB_ROWS)

    return (y_p.reshape(BATCH, SEQ, D_MODEL), y_s.reshape(DEC_BATCH, DEC_SEQ, D_MODEL),
            conv_p[None], hgrn_p[None], conv_s[None], hgrn_s[None])
```

```python
import functools

import jax
import jax.numpy as jnp
from jax import lax
from jax.experimental import pallas as pl
from jax.experimental.pallas import tpu as pltpu

F32 = jnp.float32
BF16 = jnp.bfloat16
I32 = jnp.int32

D_MODEL = 4096
BATCH = 4
SEQ = 2048
DEC_BATCH = 128
DEC_SEQ = 4
CONV_CH = 2048
CONV_K = 3
HG_HEADS = 16
HG_DK = 128
HG_DV = 128
HG_W = HG_HEADS * HG_DK
HG_CHUNK = 64
IN_COLS = 3 * CONV_CH + 4 * HG_W
N_GROUPS = 8
EXP_PER_GROUP = 8
N_EXPERTS = 64
TOP_K = 2
D_EXPERT = 1024
EPS = 1e-6

N_PROMPT = BATCH * SEQ
N_SAMPLE = DEC_BATCH * DEC_SEQ
N_TOK = N_PROMPT + N_SAMPLE
N_ASSIGN = N_TOK * TOP_K

LANES = 128
SUBLANES = 8
ROUTE_COLS = LANES

NORM_ROWS = 256
MM_ROWS = 512
MM_COLS = 1024
CONV_ROWS = 512
CONV_COLS = 512
HG_ROWS = 512
HG_SAMPLE_SEQS = 4
MOE_BLK = 128
MOE_NB = N_ASSIGN // MOE_BLK + N_EXPERTS
MOE_ROWS = MOE_NB * MOE_BLK
MOE_FC = 512
MOE_DC = 2048
MOE_CHUNK_BLKS = 4
MOE_NCH = N_EXPERTS + N_ASSIGN // (MOE_BLK * MOE_CHUNK_BLKS)
MOE_IN_SLOTS = 3
MOE_OUT_SLOTS = 2
MOE_K_CHUNKS = 4
PLAN_ROWS = 512
DISPATCH_PART = 32
COMB_ROWS = 128

MIB = 1 << 20


def _params(sem, vmem_mib, **kwargs):
    return pltpu.CompilerParams(dimension_semantics=sem, vmem_limit_bytes=vmem_mib * MIB, **kwargs)


def _sigmoid(x):
    return 1.0 / (1.0 + jnp.exp(-x))


def _rmsnorm_rows(x, g):
    return x * lax.rsqrt(jnp.mean(x * x, axis=-1, keepdims=True) + EPS) * g


def _norm_in_kernel(xp_ref, xs_ref, g_ref, o_ref, *, n_prompt_tiles):
    i = pl.program_id(0)

    @pl.when(i < n_prompt_tiles)
    def _():
        o_ref[...] = _rmsnorm_rows(xp_ref[...], g_ref[...]).astype(o_ref.dtype)

    @pl.when(i >= n_prompt_tiles)
    def _():
        o_ref[...] = _rmsnorm_rows(xs_ref[...], g_ref[...]).astype(o_ref.dtype)


def _norm_in(xp, xs, g):
    npt = N_PROMPT // NORM_ROWS
    nst = N_SAMPLE // NORM_ROWS
    return pl.pallas_call(
        functools.partial(_norm_in_kernel, n_prompt_tiles=npt),
        grid=(npt + nst,),
        in_specs=[
            pl.BlockSpec((NORM_ROWS, D_MODEL), lambda i: (jnp.minimum(i, npt - 1), 0)),
            pl.BlockSpec((NORM_ROWS, D_MODEL), lambda i: (jnp.maximum(i - npt, 0), 0)),
            pl.BlockSpec((1, D_MODEL), lambda i: (0, 0)),
        ],
        out_specs=pl.BlockSpec((NORM_ROWS, D_MODEL), lambda i: (i, 0)),
        out_shape=jax.ShapeDtypeStruct((N_TOK, D_MODEL), BF16),
        compiler_params=_params(("arbitrary",), 40),
        name="norm_in",
    )(xp, xs, g)


def _in_proj_kernel(x_ref, w_ref, o_ref, wb_ref):
    @pl.when(pl.program_id(1) == 0)
    def _():
        wb_ref[...] = w_ref[...].astype(BF16)

    o_ref[...] = jnp.dot(x_ref[...], wb_ref[...], preferred_element_type=F32).astype(o_ref.dtype)


def _in_proj(xn, w):
    return pl.pallas_call(
        _in_proj_kernel,
        grid=(IN_COLS // MM_COLS, N_TOK // MM_ROWS),
        in_specs=[
            pl.BlockSpec((MM_ROWS, D_MODEL), lambda n, m: (m, 0)),
            pl.BlockSpec((D_MODEL, MM_COLS), lambda n, m: (0, n)),
        ],
        out_specs=pl.BlockSpec((MM_ROWS, MM_COLS), lambda n, m: (m, n)),
        out_shape=jax.ShapeDtypeStruct((N_TOK, IN_COLS), BF16),
        scratch_shapes=[pltpu.VMEM((D_MODEL, MM_COLS), BF16)],
        compiler_params=_params(("arbitrary", "arbitrary"), 58),
        name="in_proj",
    )(xn, w)


def _conv_taps(u, um1, um2, cb, w):
    conv = um2 * w[0:1] + um1 * w[1:2] + u * w[2:3]
    return cb * conv


def _conv_prompt_kernel(cb_ref, cc_ref, ch_ref, w_ref, y_ref, nb_ref, carry_ref):
    t = pl.program_id(2)

    @pl.when(t == 0)
    def _():
        carry_ref[...] = jnp.zeros_like(carry_ref)

    u = cc_ref[...].astype(F32) * ch_ref[...].astype(F32)
    prev = carry_ref[...]
    p1 = prev[SUBLANES - 1:SUBLANES]
    p2 = prev[SUBLANES - 2:SUBLANES - 1]
    row = lax.broadcasted_iota(I32, u.shape, 0)
    um1 = jnp.where(row == 0, p1, pltpu.roll(u, 1, axis=0))
    um2 = jnp.where(row == 0, p2, jnp.where(row == 1, p1, pltpu.roll(u, 2, axis=0)))
    y_ref[...] = _conv_taps(u, um1, um2, cb_ref[...].astype(F32), w_ref[...]).astype(y_ref.dtype)
    carry_ref[...] = u[CONV_ROWS - SUBLANES:CONV_ROWS]
    nb_ref[0] = u[CONV_ROWS - (CONV_K - 1):CONV_ROWS]


def _conv_prompt(proj, conv_w):
    nt = SEQ // CONV_ROWS
    nc = CONV_CH // CONV_COLS
    return pl.pallas_call(
        _conv_prompt_kernel,
        grid=(BATCH, nc, nt),
        in_specs=[
            pl.BlockSpec((CONV_ROWS, CONV_COLS), lambda b, c, t: (b * nt + t, c)),
            pl.BlockSpec((CONV_ROWS, CONV_COLS), lambda b, c, t: (b * nt + t, nc + c)),
            pl.BlockSpec((CONV_ROWS, CONV_COLS), lambda b, c, t: (b * nt + t, 2 * nc + c)),
            pl.BlockSpec((CONV_K, CONV_COLS), lambda b, c, t: (0, c)),
        ],
        out_specs=[
            pl.BlockSpec((CONV_ROWS, CONV_COLS), lambda b, c, t: (b * nt + t, c)),
            pl.BlockSpec((1, CONV_K - 1, CONV_COLS), lambda b, c, t: (b, 0, c)),
        ],
        out_shape=[
            jax.ShapeDtypeStruct((N_PROMPT, CONV_CH), BF16),
            jax.ShapeDtypeStruct((BATCH, CONV_K - 1, CONV_CH), F32),
        ],
        scratch_shapes=[pltpu.VMEM((SUBLANES, CONV_COLS), F32)],
        compiler_params=_params(("arbitrary", "arbitrary", "arbitrary"), 32),
        name="conv_prompt",
    )(proj, proj, proj, conv_w)


def _conv_sample_kernel(cb_ref, cc_ref, ch_ref, w_ref, e1_ref, e2_ref, y_ref, u_ref):
    u =cc_ref[...].astype(F32) * ch_ref[...].astype(F32)
    tpos = lax.broadcasted_iota(I32, u.shape, 0) % DEC_SEQ
    um1 = jnp.where(tpos == 0, e1_ref[...], pltpu.roll(u, 1, axis=0))
    um2 = jnp.where(tpos <= 1, e2_ref[...], pltpu.roll(u, 2, axis=0))
    y_ref[...] = _conv_taps(u, um1, um2, cb_ref[...].astype(F32), w_ref[...]).astype(y_ref.dtype)
    u_ref[...] = u


def _conv_sample(proj, conv_w, e1, e2):
    nc = CONV_CH // CONV_COLS
    rb = N_PROMPT // N_SAMPLE
    return pl.pallas_call(
        _conv_sample_kernel,
        grid=(nc,),
        in_specs=[
            pl.BlockSpec((N_SAMPLE, CONV_COLS), lambda c: (rb, c)),
            pl.BlockSpec((N_SAMPLE, CONV_COLS), lambda c: (rb, nc + c)),
            pl.BlockSpec((N_SAMPLE, CONV_COLS), lambda c: (rb, 2 * nc + c)),
            pl.BlockSpec((CONV_K, CONV_COLS), lambda c: (0, c)),
            pl.BlockSpec((N_SAMPLE, CONV_COLS), lambda c: (0, c)),
            pl.BlockSpec((N_SAMPLE, CONV_COLS), lambda c: (0, c)),
        ],
        out_specs=[
            pl.BlockSpec((N_SAMPLE, CONV_COLS), lambda c: (0, c)),
            pl.BlockSpec((N_SAMPLE, CONV_COLS), lambda c: (0, c)),
        ],
        out_shape=[
            jax.ShapeDtypeStruct((N_SAMPLE, CONV_CH), BF16),
            jax.ShapeDtypeStruct((N_SAMPLE, CONV_CH), F32),
        ],
        compiler_params=_params(("arbitrary",), 32),
        name="conv_sample",
    )(proj, proj, proj, conv_w, e1, e2)


_NT = (((1,), (1,)), ((), ()))
_TN = (((0,), (0,)), ((), ()))


def _hgrn_factors(q, fz, lb, seq_len):
    rows, width = q.shape
    one_m_lb = 1.0 - lb
    log_f = jnp.log(lb + one_m_lb * _sigmoid(fz))
    k = one_m_lb * _sigmoid(-fz)
    row = lax.broadcasted_iota(I32, (rows, width), 0)
    tpos = row % seq_len

    b = log_f
    shift = 1
    while shift < seq_len:
        b = b + jnp.where(tpos >= shift, pltpu.roll(b, shift, axis=0), 0.0)
        shift *= 2

    n_seq = rows // seq_len
    mid = (seq_len - 1) // 2
    if seq_len % SUBLANES == 0:
        b3 = b.reshape(n_seq, seq_len, width)
        b_mid = b3[:, mid:mid + 1]
        b_end = b3[:, seq_len - 1:seq_len]

        def per_row(per_chunk):
            return jnp.broadcast_to(per_chunk, b3.shape).reshape(rows, width)

        b_ref, b_last = per_row(b_mid), per_row(b_end)
        e_ref, e_last = per_row(jnp.exp(b_mid)), per_row(jnp.exp(b_end - b_mid))
    else:
        seq = row // seq_len
        b_ref = b[mid:mid + 1]
        b_last = b[seq_len - 1:seq_len]
        for j in range(1, n_seq):
            b_ref = jnp.where(seq == j, b[j * seq_len + mid:j * seq_len + mid + 1], b_ref)
            b_last = jnp.where(seq == j, b[(j + 1) * seq_len - 1:(j + 1) * seq_len], b_last)
        e_ref, e_last = jnp.exp(b_ref), jnp.exp(b_last - b_ref)

    qe = q * jnp.exp(b - b_ref)
    ke = k * jnp.exp(b_ref - b)
    return qe, ke, qe * e_ref, ke * e_last, log_f, b_last


def _split3_bf16(x):
    hi = x.astype(BF16).astype(F32)
    rest = x - hi
    mid = rest.astype(BF16).astype(F32)
    lo = (rest - mid).astype(BF16).astype(F32)
    return hi, mid, lo


def _head_out(o, gate, norm_g):
    o_n = o * lax.rsqrt(jnp.mean(o * o, axis=-1, keepdims=True) + EPS)
    return o_n * norm_g * (gate * _sigmoid(gate))


def _hgrn_prompt_kernel(q_ref, f_ref, v_ref, g_ref, lb_ref, ng_ref, y_ref, s_out_ref):
    lb = lb_ref[...]
    ng = ng_ref[...]
    rr = lax.broadcasted_iota(I32, (HG_CHUNK, HG_CHUNK), 0)
    cc = lax.broadcasted_iota(I32, (HG_CHUNK, HG_CHUNK), 1)
    causal = cc <= rr

    def block(i, state_t):
        rows = pl.ds(pl.multiple_of(i * HG_ROWS, HG_ROWS), HG_ROWS)
        qe, ke, qb, kd, _, b_last = _hgrn_factors(q_ref[rows, :].astype(F32), f_ref[rows, :].astype(F32),
                                                  lb, HG_CHUNK)
        qe, ke, qb, kd = (a.astype(BF16) for a in (qe, ke, qb, kd))
        vb = v_ref[rows, :]
        outs = []
        for c in range(HG_ROWS // HG_CHUNK):
            cs = slice(c * HG_CHUNK, (c + 1) * HG_CHUNK)
            scores = lax.dot_general(qe[cs], ke[cs], _NT, preferred_element_type=F32)
            scores = jnp.where(causal, scores, 0.0).astype(BF16)
            outs.append(jnp.dot(scores, vb[cs], preferred_element_type=F32)
                        + lax.dot_general(qb[cs], state_t.astype(BF16), _NT, preferred_element_type=F32))
            inc_t = lax.dot_general(vb[cs], kd[cs], _TN, preferred_element_type=F32)
            state_t = jnp.exp(b_last[c * HG_CHUNK:c * HG_CHUNK + 1]) * state_t + inc_t
        o = jnp.concatenate(outs, axis=0)
        y_ref[rows, :] = _head_out(o, g_ref[rows, :].astype(F32), ng).astype(y_ref.dtype)
        return state_t

    state_t = lax.fori_loop(0, SEQ // HG_ROWS, block, jnp.zeros((HG_DV, HG_DK), F32))
    s_out_ref[0, 0] = state_t.T


def _hgrn_prompt(proj, lb, norm_g):
    cq = 3 * CONV_CH // HG_DK

    def spec(off):
        return pl.BlockSpec((SEQ, HG_DK), lambda b, h: (b, cq + off * HG_HEADS + h))

    return pl.pallas_call(
        _hgrn_prompt_kernel,
        grid=(BATCH, HG_HEADS),
        in_specs=[spec(0), spec(1), spec(2), spec(3),
                  pl.BlockSpec((1, HG_DK), lambda b, h: (0, h)),
                  pl.BlockSpec((1, HG_DV), lambda b, h: (0, h))],
        out_specs=[
            pl.BlockSpec((SEQ, HG_DV), lambda b, h: (b, h)),
            pl.BlockSpec((1, 1, HG_DK, HG_DV), lambda b, h: (b, h, 0, 0)),
        ],
        out_shape=[
            jax.ShapeDtypeStruct((N_PROMPT, HG_W), BF16),
            jax.ShapeDtypeStruct((BATCH, HG_HEADS, HG_DK, HG_DV), F32),
        ],
        compiler_params=_params(("arbitrary", "arbitrary"), 32),
        name="hgrn_prompt",
    )(proj, proj, proj, proj, lb, norm_g)


def _hgrn_sample_kernel(q_ref, f_ref, v_ref, g_ref, lb_ref, ng_ref, s_ref, y_ref, s_out_ref):
    rows = HG_SAMPLE_SEQS * DEC_SEQ
    qe, ke, qb, kd, log_f, _ = _hgrn_factors(q_ref[...].astype(F32), f_ref[...].astype(F32),
                                            lb_ref[...], DEC_SEQ)
    lf_parts = _split3_bf16(log_f)
    v = v_ref[...].astype(F32)
    gate = g_ref[...].astype(F32)
    ng = ng_ref[...]

    in_a = lax.broadcasted_iota(I32, (SUBLANES, HG_DV), 0) < DEC_SEQ
    rr = lax.broadcasted_iota(I32, (SUBLANES, SUBLANES), 0)
    cc = lax.broadcasted_iota(I32, (SUBLANES, SUBLANES), 1)
    causal = (cc <= rr) & ((rr // DEC_SEQ) == (cc // DEC_SEQ))
    zero_tile = jnp.zeros((SUBLANES, HG_DV), F32)
    sel = jnp.concatenate([in_a.astype(F32), 1.0 - in_a.astype(F32)], axis=1)
    dec_rhs = jnp.concatenate([sel, sel, sel, jnp.zeros_like(sel)], axis=0).astype(BF16)

    for h in range(HG_HEADS):
        cols = slice(h * HG_DK, (h + 1) * HG_DK)
        outs = []
        for r in range(rows // SUBLANES):
            rs = slice(r * SUBLANES, (r + 1) * SUBLANES)
            s_cat = jnp.concatenate([s_ref[2 * r, h], s_ref[2 * r + 1, h]], axis=1)
            v_t = v[rs, cols]
            scores = lax.dot_general(qe[rs, cols].astype(BF16), ke[rs, cols].astype(BF16), _NT,
                                     preferred_element_type=F32)
            scores = jnp.where(causal, scores, 0.0).astype(BF16)
            o_state = jnp.dot(qb[rs, cols].astype(BF16), s_cat.astype(BF16), preferred_element_type=F32)
            outs.append(jnp.dot(scores, v_t.astype(BF16), preferred_element_type=F32)
                        + jnp.where(in_a, o_state[:, :HG_DV], o_state[:, HG_DV:]))
            inc_rhs = jnp.concatenate([jnp.where(in_a, v_t, 0.0), jnp.where(in_a, 0.0, v_t)], axis=1)
            inc = lax.dot_general(kd[rs, cols].astype(BF16), inc_rhs.astype(BF16), _TN,
                                  preferred_element_type=F32)
            dec_lhs = jnp.concatenate([p[rs, cols] for p in lf_parts] + [zero_tile], axis=0)
            decay = jnp.exp(lax.dot_general(dec_lhs.astype(BF16), dec_rhs, _TN,
                                            preferred_element_type=F32))
            s_new = decay * s_cat + inc
            s_out_ref[2 * r, h] = s_new[:, :HG_DV]
            s_out_ref[2 * r + 1, h] = s_new[:, HG_DV:]
        o = jnp.concatenate(outs, axis=0)
        y_ref[:, cols] = _head_out(o, gate[:, cols], ng[:, cols]).astype(y_ref.dtype)


def _hgrn_sample(proj, lb, norm_g, state):
    rows = HG_SAMPLE_SEQS * DEC_SEQ
    rb = N_PROMPT // rows
    cq = 3 * CONV_CH // HG_W

    def spec(off):
        return pl.BlockSpec((rows, HG_W), lambda g: (rb + g, cq + off))

    st_spec = pl.BlockSpec((HG_SAMPLE_SEQS, HG_HEADS, HG_DK, HG_DV), lambda g: (g, 0, 0, 0))
    return pl.pallas_call(
        _hgrn_sample_kernel,
        grid=(DEC_BATCH // HG_SAMPLE_SEQS,),
        in_specs=[spec(0), spec(1), spec(2), spec(3),
                  pl.BlockSpec((1, HG_W), lambda g: (0, 0)),
                  pl.BlockSpec((1, HG_W), lambda g: (0, 0)),
                  st_spec],
        out_specs=[pl.BlockSpec((rows, HG_W), lambda g: (g, 0)), st_spec],
        out_shape=[
            jax.ShapeDtypeStruct((N_SAMPLE, HG_W), BF16),
            jax.ShapeDtypeStruct((DEC_BATCH, HG_HEADS, HG_DK, HG_DV), F32),
        ],
        compiler_params=_params(("arbitrary",), 40),
        name="hgrn_sample",
    )(proj, proj, proj, proj, lb, norm_g, state)


def _out_proj_kernel(ycp_ref, yhp_ref, xp_ref, ycs_ref, yhs_ref, xs_ref, wc_ref, wh_ref, o_ref, *,
                     n_prompt_tiles):
    m = pl.program_id(1)

    def residual_mix(yc_ref, yh_ref, x_ref):
        mix = (jnp.dot(yc_ref[...], wc_ref[...], preferred_element_type=F32)
               + jnp.dot(yh_ref[...], wh_ref[...], preferred_element_type=F32))
        o_ref[...] = x_ref[...] + mix

    @pl.when(m < n_prompt_tiles)
    def _():
        residual_mix(ycp_ref, yhp_ref, xp_ref)

    @pl.when(m >= n_prompt_tiles)
    def _():
        residual_mix(ycs_ref, yhs_ref, xs_ref)


def _out_proj(yc_p, yh_p, xp, yc_s, yh_s, xs, w):
    npt = N_PROMPT // MM_ROWS

    def prompt_map(n, m):
        return jnp.minimum(m, npt - 1)

    def sample_map(n, m):
        return jnp.maximum(m - npt, 0)

    def source_specs(row_map):
        return [
            pl.BlockSpec((MM_ROWS, CONV_CH), lambda n, m: (row_map(n, m), 0)),
            pl.BlockSpec((MM_ROWS, HG_W), lambda n, m: (row_map(n, m), 0)),
            pl.BlockSpec((MM_ROWS, MM_COLS), lambda n, m: (row_map(n, m), n)),
        ]

    return pl.pallas_call(
        functools.partial(_out_proj_kernel, n_prompt_tiles=npt),
        grid=(D_MODEL // MM_COLS, N_TOK // MM_ROWS),
        in_specs=source_specs(prompt_map) + source_specs(sample_map) + [
            pl.BlockSpec((CONV_CH, MM_COLS), lambda n, m: (0, n)),
            pl.BlockSpec((HG_W, MM_COLS), lambda n, m: (1, n)),
        ],
        out_specs=pl.BlockSpec((MM_ROWS, MM_COLS), lambda n, m: (m, n)),
        out_shape=jax.ShapeDtypeStruct((N_TOK, D_MODEL), F32),
        compiler_params=_params(("arbitrary", "arbitrary"), 56),
        name="out_proj",
    )(yc_p, yh_p, xp, yc_s, yh_s, xs, w, w)


def _split_bf16(x):
    hi = x.astype(BF16)
    return hi, (x - hi.astype(F32)).astype(BF16)


_HI16 = 0xFFFF0000


def _pack_bf16_halves(x):
    half = x.shape[1] // 2
    bits = lax.bitcast_convert_type(x.astype(F32), jnp.uint32)
    return (bits[:, :half] >> 16) | (bits[:, half:] & jnp.uint32(_HI16))


def _unpack_bf16_half(packed, high):
    bits = (packed & jnp.uint32(_HI16)) if high else (packed << 16)
    return lax.bitcast_convert_type(bits, F32)


def _norm_route_kernel(h_ref, g_ref, wr_hi_ref, wr_lo_ref, br_ref, hn_ref, eidx_ref, gate_ref):
    hn = _rmsnorm_rows(h_ref[...], g_ref[...])
    hn_hi, hn_lo = _split_bf16(hn)
    hn_ref[...] = _pack_bf16_halves(hn_hi)
    logits = (jnp.dot(hn_hi, wr_hi_ref[...], preferred_element_type=F32)
              + (jnp.dot(hn_hi, wr_lo_ref[...], preferred_element_type=F32)
                 + jnp.dot(hn_lo, wr_hi_ref[...], preferred_element_type=F32))) + br_ref[...]
    lane = lax.broadcasted_iota(I32, logits.shape, 1)
    lane_f = lane.astype(F32)
    neg = -jnp.inf

    def first_argmax(vals, vmax):
        first = jnp.min(jnp.where(vals == vmax, lane_f, float(ROUTE_COLS)), axis=-1, keepdims=True)
        return first.astype(I32)

    is_grp = lane < N_GROUPS
    lg = jnp.where(is_grp, logits, neg)
    mg = jnp.max(lg, axis=-1, keepdims=True)
    g_sel = first_argmax(lg, mg)
    p_grp = 1.0 / jnp.sum(jnp.where(is_grp, jnp.exp(logits - mg), 0.0), axis=-1, keepdims=True)

    in_grp = (lane >= N_GROUPS) & (((lane - N_GROUPS) // EXP_PER_GROUP) == g_sel)
    le = jnp.where(in_grp, logits, neg)
    m1 = jnp.max(le, axis=-1, keepdims=True)
    i1 = first_argmax(le, m1)
    le2 = jnp.where(lane == i1, neg, le)
    m2 = jnp.max(le2, axis=-1, keepdims=True)
    i2 = first_argmax(le2, m2)
    e2 = jnp.exp(m2 - m1)
    gate1 = p_grp / (1.0 + e2)
    gate2 = p_grp * e2 / (1.0 + e2)
    eidx_ref[...] = jnp.where(lane == 0, i1 - N_GROUPS, jnp.where(lane == 1, i2 - N_GROUPS, 0))
    gate_ref[...] = jnp.where(lane == 0, gate1, jnp.where(lane == 1, gate2, 0.0))


def _norm_route(h, g, w_r, b_r):
    wr_hi, wr_lo = _split_bf16(w_r)
    return pl.pallas_call(
        _norm_route_kernel,
        grid=(N_TOK // NORM_ROWS,),
        in_specs=[
            pl.BlockSpec((NORM_ROWS, D_MODEL), lambda i: (i, 0)),
            pl.BlockSpec((1, D_MODEL), lambda i: (0, 0)),
            pl.BlockSpec((D_MODEL, ROUTE_COLS), lambda i: (0, 0)),
            pl.BlockSpec((D_MODEL, ROUTE_COLS), lambda i: (0, 0)),
            pl.BlockSpec((1, ROUTE_COLS), lambda i: (0, 0)),
        ],
        out_specs=[
            pl.BlockSpec((NORM_ROWS, D_MODEL // 2), lambda i: (i, 0)),
            pl.BlockSpec((NORM_ROWS, ROUTE_COLS), lambda i: (i, 0)),
            pl.BlockSpec((NORM_ROWS, ROUTE_COLS), lambda i: (i, 0)),
        ],
        out_shape=[
            jax.ShapeDtypeStruct((N_TOK, D_MODEL // 2), jnp.uint32),
            jax.ShapeDtypeStruct((N_TOK, ROUTE_COLS), I32),
            jax.ShapeDtypeStruct((N_TOK, ROUTE_COLS), F32),
        ],
        compiler_params=_params(("arbitrary",), 48),
        name="norm_route",
    )(h, g, wr_hi, wr_lo, b_r)


def _row_copy(src_hbm, row, buf, slot, r, sem):
    return pltpu.make_async_copy(src_hbm.at[pl.ds(row, 1)], buf.at[slot, pl.ds(r, 1)], sem.at[slot])


def _gather_issue(idx_ref, base, row0, n_rows, src_hbm, buf, slot, sem):
    for r in range(row0, row0 + n_rows):
        _row_copy(src_hbm, idx_ref[base + r], buf, slot, r, sem).start()


def _gather_wait(row0, n_rows, src_hbm, buf, slot, sem):
    for r in range(row0, row0 + n_rows, SUBLANES):
        rows = pl.ds(r, SUBLANES)
        pltpu.make_async_copy(src_hbm.at[pl.ds(0, SUBLANES)], buf.at[slot, rows], sem.at[slot]).wait()


def _route_plan_kernel(eidx_ref, dest_ref, counts_ref, start_ref, rank_scr):
    n_tiles = N_TOK // PLAN_ROWS
    lane = lax.broadcasted_iota(I32, (PLAN_ROWS, ROUTE_COLS), 1)
    ri = lax.broadcasted_iota(I32, (PLAN_ROWS, PLAN_ROWS), 0)
    ci = lax.broadcasted_iota(I32, (PLAN_ROWS, PLAN_ROWS), 1)
    earlier = jnp.where(ci < ri, 1.0, 0.0).astype(BF16)

    def tile_rows(t):
        return pl.ds(pl.multiple_of(t * PLAN_ROWS, PLAN_ROWS), PLAN_ROWS)

    def pick(vals, col):
        return jnp.sum(jnp.where(lane == col, vals, 0.0), axis=-1, keepdims=True)

    def on_lanes01(v0, v1):
        return jnp.where(lane == 0, v0, jnp.where(lane == 1, v1, 0.0))

    def rank_tile(t, seen):
        e = eidx_ref[tile_rows(t), :]
        e0, e1 = e[:, 0:1], e[:, 1:2]
        uses = jnp.where((lane == e0) | (lane == e1), 1.0, 0.0)
        before = jnp.dot(earlier, uses.astype(BF16), preferred_element_type=F32) + seen
        rank_scr[tile_rows(t), :] = on_lanes01(pick(before, e0), pick(before, e1))
        return seen + jnp.sum(uses, axis=0, keepdims=True)

    counts = lax.fori_loop(0, n_tiles, rank_tile, jnp.zeros((1, ROUTE_COLS), F32))
    padded = jnp.ceil(counts * (1.0 / MOE_BLK)) * MOE_BLK
    ui = lax.broadcasted_iota(I32, (ROUTE_COLS, ROUTE_COLS), 0)
    uj = lax.broadcasted_iota(I32, (ROUTE_COLS, ROUTE_COLS), 1)
    start = jnp.dot(jnp.broadcast_to(padded, (SUBLANES, ROUTE_COLS)), jnp.where(ui < uj, 1.0, 0.0),
                    precision=lax.Precision.HIGHEST, preferred_element_type=F32)[0:1]
    counts_ref[...] = counts.astype(I32)
    start_ref[...] = start.astype(I32)

    def dest_tile(t, carry):
        e = eidx_ref[tile_rows(t), :]
        rank = rank_scr[tile_rows(t), :]
        d0 = pick(start, e[:, 0:1]) + rank[:, 0:1]
        d1 = pick(start, e[:, 1:2]) + rank[:, 1:2]
        dest_ref[tile_rows(t), :] = on_lanes01(d0, d1).astype(I32)
        return carry

    lax.fori_loop(0, n_tiles, dest_tile, 0)


def _route_plan(eidx):
    return pl.pallas_call(
        _route_plan_kernel,
        out_shape=[
            jax.ShapeDtypeStruct((N_TOK, ROUTE_COLS), I32),
            jax.ShapeDtypeStruct((1, ROUTE_COLS), I32),
            jax.ShapeDtypeStruct((1, ROUTE_COLS), I32),
        ],
        scratch_shapes=[pltpu.VMEM((N_TOK, ROUTE_COLS), F32)],
        compiler_params=pltpu.CompilerParams(vmem_limit_bytes=40 * MIB),
        name="route_plan",
    )(eidx)


def _build_row_tables(dest_ref, counts_ref, start_ref, row_tok, n_parts):
    def clear(blk, carry):
        n_parts[blk] = 0
        return carry

    lax.fori_loop(0, MOE_NB, clear, 0)

    def per_expert(e, carry):
        count, first = counts_ref[e], start_ref[e]

        def per_block(i, c):
            rows = jnp.minimum(count - i * MOE_BLK, MOE_BLK)
            parts = (rows + DISPATCH_PART - 1) // DISPATCH_PART
            n_parts[first // MOE_BLK + i] = parts

            def pad_row(r, c2):
                row_tok[r] = r % N_PROMPT
                return c2

            row0 = first + i * MOE_BLK
            lax.fori_loop(row0 + rows, row0 + parts * DISPATCH_PART, pad_row, 0)
            return c

        lax.fori_loop(0, (count + MOE_BLK - 1) // MOE_BLK, per_block, 0)
        return carry

    lax.fori_loop(0, N_EXPERTS, per_expert, 0)

    def invert(g, carry):
        for j in range(SUBLANES):
            a = g * SUBLANES + j
            row_tok[dest_ref[a]] = g * (SUBLANES // TOP_K) + j // TOP_K
        return carry

    lax.fori_loop(0, N_ASSIGN // SUBLANES, invert, 0)


def _stream_expert_chunks(blk0_ref, n_ref, n_used, in_copy, out_copy, out_buf, compute):
    c = pl.program_id(1)
    n_chunks = pl.num_programs(1)

    def for_blocks(chunk, fn):
        blk0 = blk0_ref[chunk]

        def body(i, carry):
            fn(blk0 + i, i)
            return carry

        lax.fori_loop(0, n_ref[chunk], body, 0)

    def start_loads(chunk):
        for_blocks(chunk, lambda g, i: in_copy(g, chunk % MOE_IN_SLOTS, i).start())

    @pl.when(c == 0)
    def _():
        start_loads(0)
        start_loads(1)

    @pl.when(c + 2 < n_chunks)
    def _():
        start_loads(c + 2)

    in_slot = c % MOE_IN_SLOTS
    out_slot = c % MOE_OUT_SLOTS
    for_blocks(c, lambda g, i: in_copy(g, in_slot, i).wait())

    @pl.when(c >= MOE_OUT_SLOTS)
    def _():
        for_blocks(c - MOE_OUT_SLOTS, lambda g, i: out_copy(g, out_slot, i).wait())

    for m in range(1, MOE_CHUNK_BLKS + 1):
        @pl.when(n_ref[c] == m)
        def _(m=m):
            compute(m, in_slot, out_slot)

    for_blocks(c, lambda g, i: out_copy(g, out_slot, i).start())

    @pl.when(c == n_chunks - 1)
    def _():
        for chunk in (c - 1, c):
            for_blocks(chunk, lambda g, i: out_copy(g, chunk % MOE_OUT_SLOTS, i).wait())
        out_buf[0, 0:MOE_BLK] = jnp.zeros((MOE_BLK,) + out_buf.shape[2:], out_buf.dtype)

        def start_zero(g, carry):
            out_copy(g, 0, 0).start()
            return carry

        def wait_zero(g, carry):
            out_copy(g, 0, 0).wait()
            return carry

        lax.fori_loop(n_used, MOE_NB, start_zero, 0)
        lax.fori_loop(n_used, MOE_NB, wait_zero, 0)


def _block_rows(g):
    return pl.ds(pl.multiple_of(g * MOE_BLK, MOE_BLK), MOE_BLK)


def _slot_rows(i):
    return pl.ds(pl.multiple_of(i * MOE_BLK, MOE_BLK), MOE_BLK)


def _expert_up_kernel(blk0_ref, n_ref, e_ref, nbu_ref, dest_ref, counts_ref, start_ref, wg_ref, wu_ref,
                      hn_hbm, h_hbm, row_tok, n_parts, xbuf, hbuf, zbuf, sem_x, sem_h, sem_z):
    del e_ref
    c = pl.program_id(0)
    j = pl.program_id(1)
    n_sub = pl.num_programs(1)
    t = c * n_sub + j
    cols = pl.ds(pl.multiple_of(j * MOE_FC, MOE_FC), MOE_FC)
    parts_per_blk = MOE_BLK // DISPATCH_PART

    def for_blocks(chunk, fn):
        blk0 = blk0_ref[chunk]

        def body(i, carry):
            fn(blk0 + i, i)
            return carry

        lax.fori_loop(0, n_ref[chunk], body, 0)

    def gather(chunk, wait):
        slot = chunk % MOE_IN_SLOTS

        def per_block(blk, i):
            for p in range(parts_per_blk):
                row0 = i * MOE_BLK + p * DISPATCH_PART

                @pl.when(p < n_parts[blk])
                def _(p=p, row0=row0):
                    if wait:
                        for r in range(0, DISPATCH_PART, SUBLANES):
                            rows = pl.ds(pl.multiple_of(row0 + r, SUBLANES), SUBLANES)
                            pltpu.make_async_copy(hn_hbm.at[pl.ds(0, SUBLANES)], xbuf.at[slot, rows],
                                                  sem_x.at[slot]).wait()
                    else:
                        for r in range(DISPATCH_PART):
                            tok = row_tok[blk * MOE_BLK + p * DISPATCH_PART + r]
                            pltpu.make_async_copy(hn_hbm.at[pl.ds(tok, 1)],
                                                  xbuf.at[slot, pl.ds(row0 + r, 1)], sem_x.at[slot]).start()

                if not wait:
                    @pl.when(p >= n_parts[blk])
                    def _(row0=row0):
                        rows = pl.ds(pl.multiple_of(row0, DISPATCH_PART), DISPATCH_PART)
                        xbuf[slot, rows] = jnp.zeros((DISPATCH_PART,) + xbuf.shape[2:], xbuf.dtype)

        for_blocks(chunk, per_block)

    def h_copy(g, slot, i):
        return pltpu.make_async_copy(hbuf.at[slot, _slot_rows(i)], h_hbm.at[_block_rows(g), cols],
                                     sem_h.at[slot])

    @pl.when(j == 0)
    def _():
        @pl.when(c == 0)
        def _():
            _build_row_tables(dest_ref, counts_ref, start_ref, row_tok, n_parts)
            gather(0, wait=False)
            gather(1, wait=False)

        @pl.when(c + 2 < pl.num_programs(0))
        def _():
            gather(c + 2, wait=False)

        gather(c, wait=True)

    in_slot = c % MOE_IN_SLOTS
    out_slot = t % MOE_OUT_SLOTS

    @pl.when(t >= MOE_OUT_SLOTS)
    def _():
        for_blocks(c - 1, lambda g, i: h_copy(g, out_slot, i).wait())

    def compute(m, in_slot, out_slot):
        rows = m * MOE_BLK
        kc = D_MODEL // MOE_K_CHUNKS
        half_chunks = MOE_K_CHUNKS // 2
        g = u = None
        for i in range(MOE_K_CHUNKS):
            ks = slice(i * kc, (i + 1) * kc)
            ps = slice((i % half_chunks) * kc, (i % half_chunks + 1) * kc)
            xk = _unpack_bf16_half(xbuf[in_slot, :rows, ps], high=i >= half_chunks)
            gi = jnp.dot(xk, wg_ref[0, ks, :], preferred_element_type=F32)
            ui = jnp.dot(xk, wu_ref[0, ks, :], preferred_element_type=F32)
            g, u = (gi, ui) if g is None else (g + gi, u + ui)
        hbuf[out_slot, :rows] = (g * _sigmoid(g) * u).astype(hbuf.dtype)

    for m in range(1, MOE_CHUNK_BLKS + 1):
        @pl.when(n_ref[c] == m)
        def _(m=m):
            compute(m, in_slot, out_slot)

    for_blocks(c, lambda g, i: h_copy(g, out_slot, i).start())

    @pl.when(c == pl.num_programs(0) - 1)
    def _():
        zbuf[...] = jnp.zeros_like(zbuf)

        def z_copy(g):
            return pltpu.make_async_copy(zbuf, h_hbm.at[_block_rows(g), cols], sem_z.at[0])

        def start_zero(g, carry):
            z_copy(g).start()
            return carry

        def wait_zero(g, carry):
            z_copy(g).wait()
            return carry

        lax.fori_loop(nbu_ref[0], MOE_NB, start_zero, 0)
        lax.fori_loop(nbu_ref[0], MOE_NB, wait_zero, 0)

        @pl.when(j == n_sub - 1)
        def _():
            for slot in range(MOE_OUT_SLOTS):
                for_blocks(c, lambda g, i: h_copy(g, slot, i).wait())


def _chunk_scratch(in_cols, in_dtype, out_cols, out_dtype):
    rows = MOE_CHUNK_BLKS * MOE_BLK
    return [pltpu.VMEM((MOE_IN_SLOTS, rows, in_cols), in_dtype),
            pltpu.VMEM((MOE_OUT_SLOTS, rows, out_cols), out_dtype),
            pltpu.SemaphoreType.DMA((MOE_IN_SLOTS,)),
            pltpu.SemaphoreType.DMA((MOE_OUT_SLOTS,))]


def _expert_up(chunks, n_used, dest_flat, counts, start, hn_packed, w_g, w_u):
    assert D_EXPERT // MOE_FC == MOE_OUT_SLOTS

    def w_map(c, j, blk0, n, e, *_):
        return (e[c], 0, j)

    x_scratch, h_scratch, sem_x, sem_h = _chunk_scratch(D_MODEL // 2, jnp.uint32, MOE_FC, BF16)
    return pl.pallas_call(
        _expert_up_kernel,
        grid_spec=pltpu.PrefetchScalarGridSpec(
            num_scalar_prefetch=7,
            grid=(MOE_NCH, D_EXPERT // MOE_FC),
            in_specs=[pl.BlockSpec((1, D_MODEL, MOE_FC), w_map),
                      pl.BlockSpec((1, D_MODEL, MOE_FC), w_map),
                      pl.BlockSpec(memory_space=pl.ANY)],
            out_specs=pl.BlockSpec(memory_space=pl.ANY),
            scratch_shapes=[pltpu.SMEM((MOE_ROWS,), I32), pltpu.SMEM((MOE_NB,), I32),
                            x_scratch, h_scratch, pltpu.VMEM((MOE_BLK, MOE_FC), BF16),
                            sem_x, sem_h, pltpu.SemaphoreType.DMA((1,))],
        ),
        out_shape=jax.ShapeDtypeStruct((MOE_ROWS, D_EXPERT), BF16),
        compiler_params=_params(("arbitrary", "arbitrary"), 58, disable_bounds_checks=True),
        name="expert_up",
    )(*chunks, n_used, dest_flat, counts, start, w_g, w_u, hn_packed)


def _expert_down_kernel(blk0_ref, n_ref, e_ref, nbu_ref, wd_ref, h_hbm, y_hbm, hbuf, ybuf, sem_h, sem_y):
    del e_ref
    wc = MOE_DC // 2
    cols = pl.ds(pl.multiple_of(pl.program_id(0) * wc, wc), wc)

    def h_copy(g, slot, i):
        return pltpu.make_async_copy(h_hbm.at[_block_rows(g)], hbuf.at[slot, _slot_rows(i)], sem_h.at[slot])

    def y_copy(g, slot, i):
        return pltpu.make_async_copy(ybuf.at[slot, _slot_rows(i)], y_hbm.at[_block_rows(g), cols],
                                     sem_y.at[slot])

    def compute(m, in_slot, out_slot):
        rows = m * MOE_BLK
        y = jnp.dot(hbuf[in_slot, :rows].astype(F32), wd_ref[0], preferred_element_type=F32)
        ybuf[out_slot, :rows] = _pack_bf16_halves(y.astype(BF16))

    _stream_expert_chunks(blk0_ref, n_ref, nbu_ref[0], h_copy, y_copy, ybuf, compute)


def _expert_down(chunks, n_used, hb, w_d):
    return pl.pallas_call(
        _expert_down_kernel,
        grid_spec=pltpu.PrefetchScalarGridSpec(
            num_scalar_prefetch=4,
            grid=(D_MODEL // MOE_DC, MOE_NCH),
            in_specs=[pl.BlockSpec((1, D_EXPERT, MOE_DC), lambda n, c, blk0, nb, e, nbu: (e[c], 0, n)),
                      pl.BlockSpec(memory_space=pl.ANY)],
            out_specs=pl.BlockSpec(memory_space=pl.ANY),
            scratch_shapes=_chunk_scratch(D_EXPERT, BF16, MOE_DC // 2, jnp.uint32),
        ),
        out_shape=jax.ShapeDtypeStruct((MOE_ROWS, D_MODEL // 2), jnp.uint32),
        compiler_params=_params(("arbitrary", "arbitrary"), 48),
        name="expert_down",
    )(*chunks, n_used, w_d, hb)


def _combine_kernel(dest_ref, h_ref, gate_ref, g_ref, y_hbm, o_ref, buf, sem, *, tile0):
    i = pl.program_id(0)
    n = pl.num_programs(0)
    slot = i % 2
    rows = TOP_K * COMB_ROWS
    base = (tile0 + i) * rows

    @pl.when(i == 0)
    def _():
        _gather_issue(dest_ref, base, 0, rows, y_hbm, buf, 0, sem)

    @pl.when(i + 1 < n)
    def _():
        _gather_issue(dest_ref, base + rows, 0, rows, y_hbm, buf, 1 - slot, sem)

    _gather_wait(0, rows, y_hbm, buf, slot, sem)

    def expert_rows(k):
        words = buf[slot, k * COMB_ROWS:(k + 1) * COMB_ROWS]
        wc = MOE_DC // 2
        parts = []
        for p in range(D_MODEL // MOE_DC):
            w = words[:, p * wc:(p + 1) * wc]
            parts += [_unpack_bf16_half(w, high=False), _unpack_bf16_half(w, high=True)]
        return jnp.concatenate(parts, axis=1)

    gates = gate_ref[...]
    ff = gates[:, 0:1] * expert_rows(0) + gates[:, 1:2] * expert_rows(1)
    o_ref[...] = _rmsnorm_rows(h_ref[...] + ff, g_ref[...])


def _combine(dest, h, gates, g_final, y_buf, tile0, n_tiles):
    return pl.pallas_call(
        functools.partial(_combine_kernel, tile0=tile0),
        grid_spec=pltpu.PrefetchScalarGridSpec(
            num_scalar_prefetch=1,
            grid=(n_tiles,),
            in_specs=[
                pl.BlockSpec((COMB_ROWS, D_MODEL), lambda i, d: (tile0 + i, 0)),
                pl.BlockSpec((COMB_ROWS, ROUTE_COLS), lambda i, d: (tile0 + i, 0)),
                pl.BlockSpec((1, D_MODEL), lambda i, d: (0, 0)),
                pl.BlockSpec(memory_space=pl.ANY),
            ],
            out_specs=pl.BlockSpec((COMB_ROWS, D_MODEL), lambda i, d: (i, 0)),
            scratch_shapes=[pltpu.VMEM((2, TOP_K * COMB_ROWS, D_MODEL // 2), jnp.uint32),
                            pltpu.SemaphoreType.DMA((2,))],
        ),
        out_shape=jax.ShapeDtypeStruct((n_tiles * COMB_ROWS, D_MODEL), F32),
        compiler_params=_params(("arbitrary",), 40, disable_bounds_checks=True),
        name="moe_combine",
    )(dest, h, gates, g_final, y_buf)


def _chunk_list(counts, start):
    blk0 = start // MOE_BLK
    nblk = (counts + MOE_BLK - 1) // MOE_BLK
    n_used = (blk0[-1] + nblk[-1]).astype(I32)
    nch = (nblk + MOE_CHUNK_BLKS - 1) // MOE_CHUNK_BLKS
    ch_end = jnp.cumsum(nch)
    ch = jnp.arange(MOE_NCH, dtype=I32)
    ch_e = jnp.minimum(jnp.searchsorted(ch_end, ch, side='right'), N_EXPERTS - 1)
    local = ch - (ch_end - nch)[ch_e]
    live = ch < ch_end[-1]
    ch_blk0 = (blk0[ch_e] + MOE_CHUNK_BLKS * local).astype(I32)
    ch_n = jnp.where(live, jnp.clip(nblk[ch_e] - MOE_CHUNK_BLKS * local, 0, MOE_CHUNK_BLKS), 0).astype(I32)
    ch_e = jnp.where(live, ch_e, ch_e[ch_end[-1] - 1]).astype(I32)
    return (ch_blk0, ch_n, ch_e), n_used.reshape(1)


def kernel(x_prompt, x_sample, cache_conv, state_hgrn, norm_mix_g, w_in, conv_w, lb_param, hg_norm_g,
           w_out, norm_ffn_g, w_group_router, b_group_router, w_expert_router, b_expert_router,
           w_exp_gate, w_exp_up, w_exp_down, norm_final_g):
    xp = x_prompt.reshape(N_PROMPT, D_MODEL)
    xs = x_sample.reshape(N_SAMPLE, D_MODEL)
    lb = jnp.cumsum(jax.nn.softmax(lb_param.astype(F32), axis=0), axis=0)[0].reshape(1, HG_W)

    xn = _norm_in(xp, xs, norm_mix_g[0].reshape(1, D_MODEL))
    proj = _in_proj(xn, w_in[0])

    buf = cache_conv[0]
    zeros = jnp.zeros((DEC_BATCH, DEC_SEQ - 1, CONV_CH), F32)
    e1 = jnp.concatenate([buf[:, 1:2], zeros], axis=1).reshape(N_SAMPLE, CONV_CH)
    e2 = jnp.concatenate([buf, zeros[:, :DEC_SEQ - 2]], axis=1).reshape(N_SAMPLE, CONV_CH)
    yc_p, conv_p = _conv_prompt(proj, conv_w[0])
    yc_s, u_s = _conv_sample(proj, conv_w[0], e1, e2)
    conv_s = u_s.reshape(DEC_BATCH, DEC_SEQ, CONV_CH)[:, DEC_SEQ - (CONV_K - 1):]

    ng = hg_norm_g[0].reshape(1, HG_W)
    yh_p, hgrn_p = _hgrn_prompt(proj, lb, ng)
    yh_s, hgrn_s = _hgrn_sample(proj, lb, ng, state_hgrn[0])

    h = _out_proj(yc_p, yh_p, xp, yc_s, yh_s, xs, w_out[0].astype(BF16))

    pad = jnp.zeros((D_MODEL, ROUTE_COLS - N_GROUPS - N_EXPERTS), F32)
    w_r = jnp.concatenate([w_group_router[0], w_expert_router[0], pad], axis=1)
    b_r = jnp.concatenate([b_group_router[0], b_expert_router[0], pad[0]]).reshape(1, ROUTE_COLS)
    hn_packed, eidx, gates = _norm_route(h, norm_ffn_g[0].reshape(1, D_MODEL), w_r, b_r)
    dest, counts, start = _route_plan(eidx)
    dest = dest[:, :TOP_K]
    counts, start = counts[0, :N_EXPERTS], start[0, :N_EXPERTS]
    chunks, n_used = _chunk_list(counts, start)
    hb = _expert_up(chunks, n_used, dest.reshape(N_ASSIGN), counts, start, hn_packed,
                    w_exp_gate[0], w_exp_up[0])
    yb = _expert_down(chunks, n_used, hb, w_exp_down[0])

    dest_t = dest.reshape(N_TOK // COMB_ROWS, COMB_ROWS, TOP_K).transpose(0, 2, 1).reshape(N_ASSIGN)
    gf = norm_final_g.reshape(1, D_MODEL)
    y_p = _combine(dest_t, h, gates, gf, yb, 0, N_PROMPT // COMB_ROWS)
    y_s = _combine(dest_t, h, gates, gf, yb, N_PROMPT // COMB_ROWS, N_SAMPLE // COMB_ROWS)

    return (y_p.reshape(BATCH, SEQ, D_MODEL), y_s.reshape(DEC_BATCH, DEC_SEQ, D_MODEL),
            conv_p[None], hgrn_p[None], conv_s[None], hgrn_s[None])
```

```python
import functools

import jax
import jax.numpy as jnp
from jax import lax
from jax.experimental import pallas as pl
from jax.experimental.pallas import tpu as pltpu

F32 = jnp.float32
BF16 = jnp.bfloat16
I32 = jnp.int32

D_MODEL = 4096
BATCH = 4
SEQ = 2048
DEC_BATCH = 128
DEC_SEQ = 4
CONV_CH = 2048
CONV_K = 3
HG_HEADS = 16
HG_DK = 128
HG_DV = 128
HG_W = HG_HEADS * HG_DK
HG_CHUNK = 64
IN_COLS = 3 * CONV_CH + 4 * HG_W
N_GROUPS = 8
EXP_PER_GROUP = 8
N_EXPERTS = 64
TOP_K = 2
D_EXPERT = 1024
EPS = 1e-6

N_PROMPT = BATCH * SEQ
N_SAMPLE = DEC_BATCH * DEC_SEQ
N_TOK = N_PROMPT + N_SAMPLE
N_ASSIGN = N_TOK * TOP_K

LANES = 128
SUBLANES = 8
ROUTE_COLS = LANES

NORM_ROWS = 256
MM_ROWS = 512
MM_COLS = 1024
CONV_ROWS = 512
CONV_COLS = 512
HG_ROWS = 512
HG_SAMPLE_SEQS = 4
MOE_BLK = 128
MOE_NB = N_ASSIGN // MOE_BLK + N_EXPERTS
MOE_ROWS = MOE_NB * MOE_BLK
MOE_FC = 512
MOE_DC = 2048
MOE_CHUNK_BLKS = 4
MOE_NCH = N_EXPERTS + N_ASSIGN // (MOE_BLK * MOE_CHUNK_BLKS)
MOE_IN_SLOTS = 3
MOE_OUT_SLOTS = 2
MOE_K_CHUNKS = 4
PLAN_ROWS = 512
DISPATCH_PART = 32
COMB_ROWS = 128

MIB = 1 << 20


def _params(sem, vmem_mib, **kwargs):
    return pltpu.CompilerParams(dimension_semantics=sem, vmem_limit_bytes=vmem_mib * MIB, **kwargs)


def _sigmoid(x):
    return 1.0 / (1.0 + jnp.exp(-x))


def _rmsnorm_rows(x, g):
    return x * lax.rsqrt(jnp.mean(x * x, axis=-1, keepdims=True) + EPS) * g


def _norm_in_kernel(xp_ref, xs_ref, g_ref, o_ref, *, n_prompt_tiles):
    i = pl.program_id(0)

    @pl.when(i < n_prompt_tiles)
    def _():
        o_ref[...] = _rmsnorm_rows(xp_ref[...], g_ref[...]).astype(o_ref.dtype)

    @pl.when(i >= n_prompt_tiles)
    def _():
        o_ref[...] = _rmsnorm_rows(xs_ref[...], g_ref[...]).astype(o_ref.dtype)


def _norm_in(xp, xs, g):
    npt = N_PROMPT // NORM_ROWS
    nst = N_SAMPLE // NORM_ROWS
    return pl.pallas_call(
        functools.partial(_norm_in_kernel, n_prompt_tiles=npt),
        grid=(npt + nst,),
        in_specs=[
            pl.BlockSpec((NORM_ROWS, D_MODEL), lambda i: (jnp.minimum(i, npt - 1), 0)),
            pl.BlockSpec((NORM_ROWS, D_MODEL), lambda i: (jnp.maximum(i - npt, 0), 0)),
            pl.BlockSpec((1, D_MODEL), lambda i: (0, 0)),
        ],
        out_specs=pl.BlockSpec((NORM_ROWS, D_MODEL), lambda i: (i, 0)),
        out_shape=jax.ShapeDtypeStruct((N_TOK, D_MODEL), BF16),
        compiler_params=_params(("arbitrary",), 40),
        name="norm_in",
    )(xp, xs, g)


def _in_proj_kernel(x_ref, w_ref, o_ref, wb_ref):
    @pl.when(pl.program_id(1) == 0)
    def _():
        wb_ref[...] = w_ref[...].astype(BF16)

    o_ref[...] = jnp.dot(x_ref[...], wb_ref[...], preferred_element_type=F32).astype(o_ref.dtype)


def _in_proj(xn, w):
    return pl.pallas_call(
        _in_proj_kernel,
        grid=(IN_COLS // MM_COLS, N_TOK // MM_ROWS),
        in_specs=[
            pl.BlockSpec((MM_ROWS, D_MODEL), lambda n, m: (m, 0)),
            pl.BlockSpec((D_MODEL, MM_COLS), lambda n, m: (0, n)),
        ],
        out_specs=pl.BlockSpec((MM_ROWS, MM_COLS), lambda n, m: (m, n)),
        out_shape=jax.ShapeDtypeStruct((N_TOK, IN_COLS), BF16),
        scratch_shapes=[pltpu.VMEM((D_MODEL, MM_COLS), BF16)],
        compiler_params=_params(("arbitrary", "arbitrary"), 58),
        name="in_proj",
    )(xn, w)


def _conv_taps(u, um1, um2, cb, w):
    conv = um2 * w[0:1] + um1 * w[1:2] + u * w[2:3]
    return cb * conv


def _conv_prompt_kernel(cb_ref, cc_ref, ch_ref, w_ref, y_ref, nb_ref, carry_ref):
    t = pl.program_id(2)

    @pl.when(t == 0)
    def _():
        carry_ref[...] = jnp.zeros_like(carry_ref)

    u = cc_ref[...].astype(F32) * ch_ref[...].astype(F32)
    prev = carry_ref[...]
    p1 = prev[SUBLANES - 1:SUBLANES]
    p2 = prev[SUBLANES - 2:SUBLANES - 1]
    row = lax.broadcasted_iota(I32, u.shape, 0)
    um1 = jnp.where(row == 0, p1, pltpu.roll(u, 1, axis=0))
    um2 = jnp.where(row == 0, p2, jnp.where(row == 1, p1, pltpu.roll(u, 2, axis=0)))
    y_ref[...] = _conv_taps(u, um1, um2, cb_ref[...].astype(F32), w_ref[...]).astype(y_ref.dtype)
    carry_ref[...] = u[CONV_ROWS - SUBLANES:CONV_ROWS]
    nb_ref[0] = u[CONV_ROWS - (CONV_K - 1):CONV_ROWS]


def _conv_prompt(proj, conv_w):
    nt = SEQ // CONV_ROWS
    nc = CONV_CH // CONV_COLS
    return pl.pallas_call(
        _conv_prompt_kernel,
        grid=(BATCH, nc, nt),
        in_specs=[
            pl.BlockSpec((CONV_ROWS, CONV_COLS), lambda b, c, t: (b * nt + t, c)),
            pl.BlockSpec((CONV_ROWS, CONV_COLS), lambda b, c, t: (b * nt + t, nc + c)),
            pl.BlockSpec((CONV_ROWS, CONV_COLS), lambda b, c, t: (b * nt + t, 2 * nc + c)),
            pl.BlockSpec((CONV_K, CONV_COLS), lambda b, c, t: (0, c)),
        ],
        out_specs=[
            pl.BlockSpec((CONV_ROWS, CONV_COLS), lambda b, c, t: (b * nt + t, c)),
            pl.BlockSpec((1, CONV_K - 1, CONV_COLS), lambda b, c, t: (b, 0, c)),
        ],
        out_shape=[
            jax.ShapeDtypeStruct((N_PROMPT, CONV_CH), BF16),
            jax.ShapeDtypeStruct((BATCH, CONV_K - 1, CONV_CH), F32),
        ],
        scratch_shapes=[pltpu.VMEM((SUBLANES, CONV_COLS), F32)],
        compiler_params=_params(("arbitrary", "arbitrary", "arbitrary"), 32),
        name="conv_prompt",
    )(proj, proj, proj, conv_w)


def _conv_sample_kernel(cb_ref, cc_ref, ch_ref, w_ref, e1_ref, e2_ref, y_ref, u_ref):
    u =cc_ref[...].astype(F32) * ch_ref[...].astype(F32)
    tpos = lax.broadcasted_iota(I32, u.shape, 0) % DEC_SEQ
    um1 = jnp.where(tpos == 0, e1_ref[...], pltpu.roll(u, 1, axis=0))
    um2 = jnp.where(tpos <= 1, e2_ref[...], pltpu.roll(u, 2, axis=0))
    y_ref[...] = _conv_taps(u, um1, um2, cb_ref[...].astype(F32), w_ref[...]).astype(y_ref.dtype)
    u_ref[...] = u


def _conv_sample(proj, conv_w, e1, e2):
    nc = CONV_CH // CONV_COLS
    rb = N_PROMPT // N_SAMPLE
    return pl.pallas_call(
        _conv_sample_kernel,
        grid=(nc,),
        in_specs=[
            pl.BlockSpec((N_SAMPLE, CONV_COLS), lambda c: (rb, c)),
            pl.BlockSpec((N_SAMPLE, CONV_COLS), lambda c: (rb, nc + c)),
            pl.BlockSpec((N_SAMPLE, CONV_COLS), lambda c: (rb, 2 * nc + c)),
            pl.BlockSpec((CONV_K, CONV_COLS), lambda c: (0, c)),
            pl.BlockSpec((N_SAMPLE, CONV_COLS), lambda c: (0, c)),
            pl.BlockSpec((N_SAMPLE, CONV_COLS), lambda c: (0, c)),
        ],
        out_specs=[
            pl.BlockSpec((N_SAMPLE, CONV_COLS), lambda c: (0, c)),
            pl.BlockSpec((N_SAMPLE, CONV_COLS), lambda c: (0, c)),
        ],
        out_shape=[
            jax.ShapeDtypeStruct((N_SAMPLE, CONV_CH), BF16),
            jax.ShapeDtypeStruct((N_SAMPLE, CONV_CH), F32),
        ],
        compiler_params=_params(("arbitrary",), 32),
        name="conv_sample",
    )(proj, proj, proj, conv_w, e1, e2)


_NT = (((1,), (1,)), ((), ()))
_TN = (((0,), (0,)), ((), ()))


def _hgrn_factors(q, fz, lb, seq_len):
    rows, width = q.shape
    one_m_lb = 1.0 - lb
    sig = _sigmoid(fz)
    log_f = jnp.log(lb + one_m_lb * sig)
    k = one_m_lb * (1.0 - sig)
    row = lax.broadcasted_iota(I32, (rows, width), 0)
    tpos = row % seq_len

    b = log_f
    shift = 1
    while shift < seq_len:
        b = b + jnp.where(tpos >= shift, pltpu.roll(b, shift, axis=0), 0.0)
        shift *= 2

    n_seq = rows // seq_len
    mid = (seq_len - 1) // 2
    if seq_len % SUBLANES == 0:
        b3 = b.reshape(n_seq, seq_len, width)
        b_mid = b3[:, mid:mid + 1]
        b_end = b3[:, seq_len - 1:seq_len]

        def per_row(per_chunk):
            return jnp.broadcast_to(per_chunk, b3.shape).reshape(rows, width)

        b_ref, b_last = per_row(b_mid), per_row(b_end)
        e_ref, e_last = per_row(jnp.exp(b_mid)), per_row(jnp.exp(b_end - b_mid))
    else:
        seq = row // seq_len
        b_ref = b[mid:mid + 1]
        b_last = b[seq_len - 1:seq_len]
        for j in range(1, n_seq):
            b_ref = jnp.where(seq == j, b[j * seq_len + mid:j * seq_len + mid + 1], b_ref)
            b_last = jnp.where(seq == j, b[(j + 1) * seq_len - 1:(j + 1) * seq_len], b_last)
        e_ref, e_last = jnp.exp(b_ref), jnp.exp(b_last - b_ref)

    qe = q * jnp.exp(b - b_ref)
    ke = k * jnp.exp(b_ref - b)
    return qe, ke, qe * e_ref, ke * e_last, log_f, b_last


def _split3_bf16(x):
    hi = x.astype(BF16).astype(F32)
    rest = x - hi
    mid = rest.astype(BF16).astype(F32)
    lo = (rest - mid).astype(BF16).astype(F32)
    return hi, mid, lo


def _head_out(o, gate, norm_g):
    o_n = o * lax.rsqrt(jnp.mean(o * o, axis=-1, keepdims=True) + EPS)
    return o_n * norm_g * (gate * _sigmoid(gate))


def _hgrn_prompt_kernel(q_ref, f_ref, v_ref, g_ref, lb_ref, ng_ref, y_ref, s_out_ref):
    lb = lb_ref[...]
    ng = ng_ref[...]
    rr = lax.broadcasted_iota(I32, (HG_CHUNK, HG_CHUNK), 0)
    cc = lax.broadcasted_iota(I32, (HG_CHUNK, HG_CHUNK), 1)
    causal = cc <= rr

    def block(i, state_t):
        rows = pl.ds(pl.multiple_of(i * HG_ROWS, HG_ROWS), HG_ROWS)
        qe, ke, qb, kd, _, b_last = _hgrn_factors(q_ref[rows, :].astype(F32), f_ref[rows, :].astype(F32),
                                                  lb, HG_CHUNK)
        qe, ke, qb, kd = (a.astype(BF16) for a in (qe, ke, qb, kd))
        vb = v_ref[rows, :]
        outs = []
        for c in range(HG_ROWS // HG_CHUNK):
            cs = slice(c * HG_CHUNK, (c + 1) * HG_CHUNK)
            scores = lax.dot_general(qe[cs], ke[cs], _NT, preferred_element_type=F32)
            scores = jnp.where(causal, scores, 0.0).astype(BF16)
            outs.append(jnp.dot(scores, vb[cs], preferred_element_type=F32)
                        + lax.dot_general(qb[cs], state_t.astype(BF16), _NT, preferred_element_type=F32))
            inc_t = lax.dot_general(vb[cs], kd[cs], _TN, preferred_element_type=F32)
            state_t = jnp.exp(b_last[c * HG_CHUNK:c * HG_CHUNK + 1]) * state_t + inc_t
        o = jnp.concatenate(outs, axis=0)
        y_ref[rows, :] = _head_out(o, g_ref[rows, :].astype(F32), ng).astype(y_ref.dtype)
        return state_t

    state_t = lax.fori_loop(0, SEQ // HG_ROWS, block, jnp.zeros((HG_DV, HG_DK), F32))
    s_out_ref[0, 0] = state_t.T


def _hgrn_prompt(proj, lb, norm_g):
    cq = 3 * CONV_CH // HG_DK

    def spec(off):
        return pl.BlockSpec((SEQ, HG_DK), lambda b, h: (b, cq + off * HG_HEADS + h))

    return pl.pallas_call(
        _hgrn_prompt_kernel,
        grid=(BATCH, HG_HEADS),
        in_specs=[spec(0), spec(1), spec(2), spec(3),
                  pl.BlockSpec((1, HG_DK), lambda b, h: (0, h)),
                  pl.BlockSpec((1, HG_DV), lambda b, h: (0, h))],
        out_specs=[
            pl.BlockSpec((SEQ, HG_DV), lambda b, h: (b, h)),
            pl.BlockSpec((1, 1, HG_DK, HG_DV), lambda b, h: (b, h, 0, 0)),
        ],
        out_shape=[
            jax.ShapeDtypeStruct((N_PROMPT, HG_W), BF16),
            jax.ShapeDtypeStruct((BATCH, HG_HEADS, HG_DK, HG_DV), F32),
        ],
        compiler_params=_params(("arbitrary", "arbitrary"), 32),
        name="hgrn_prompt",
    )(proj, proj, proj, proj, lb, norm_g)


def _hgrn_sample_kernel(q_ref, f_ref, v_ref, g_ref, lb_ref, ng_ref, s_ref, y_ref, s_out_ref):
    rows = HG_SAMPLE_SEQS * DEC_SEQ
    qe, ke, qb, kd, log_f, _ = _hgrn_factors(q_ref[...].astype(F32), f_ref[...].astype(F32),
                                            lb_ref[...], DEC_SEQ)
    lf_parts = _split3_bf16(log_f)
    v = v_ref[...].astype(F32)
    gate = g_ref[...].astype(F32)
    ng = ng_ref[...]

    in_a = lax.broadcasted_iota(I32, (SUBLANES, HG_DV), 0) < DEC_SEQ
    rr = lax.broadcasted_iota(I32, (SUBLANES, SUBLANES), 0)
    cc = lax.broadcasted_iota(I32, (SUBLANES, SUBLANES), 1)
    causal = (cc <= rr) & ((rr // DEC_SEQ) == (cc // DEC_SEQ))
    zero_tile = jnp.zeros((SUBLANES, HG_DV), F32)
    sel = jnp.concatenate([in_a.astype(F32), 1.0 - in_a.astype(F32)], axis=1)
    dec_rhs = jnp.concatenate([sel, sel, sel, jnp.zeros_like(sel)], axis=0).astype(BF16)

    for h in range(HG_HEADS):
        cols = slice(h * HG_DK, (h + 1) * HG_DK)
        outs = []
        for r in range(rows // SUBLANES):
            rs = slice(r * SUBLANES, (r + 1) * SUBLANES)
            s_cat = jnp.concatenate([s_ref[2 * r, h], s_ref[2 * r + 1, h]], axis=1)
            v_t = v[rs, cols]
            scores = lax.dot_general(qe[rs, cols].astype(BF16), ke[rs, cols].astype(BF16), _NT,
                                     preferred_element_type=F32)
            scores = jnp.where(causal, scores, 0.0).astype(BF16)
            o_state = jnp.dot(qb[rs, cols].astype(BF16), s_cat.astype(BF16), preferred_element_type=F32)
            outs.append(jnp.dot(scores, v_t.astype(BF16), preferred_element_type=F32)
                        + jnp.where(in_a, o_state[:, :HG_DV], o_state[:, HG_DV:]))
            inc_rhs = jnp.concatenate([jnp.where(in_a, v_t, 0.0), jnp.where(in_a, 0.0, v_t)], axis=1)
            inc = lax.dot_general(kd[rs, cols].astype(BF16), inc_rhs.astype(BF16), _TN,
                                  preferred_element_type=F32)
            dec_lhs = jnp.concatenate([p[rs, cols] for p in lf_parts] + [zero_tile], axis=0)
            decay = jnp.exp(lax.dot_general(dec_lhs.astype(BF16), dec_rhs, _TN,
                                            preferred_element_type=F32))
            s_new = decay * s_cat + inc
            s_out_ref[2 * r, h] = s_new[:, :HG_DV]
            s_out_ref[2 * r + 1, h] = s_new[:, HG_DV:]
        o = jnp.concatenate(outs, axis=0)
        y_ref[:, cols] = _head_out(o, gate[:, cols], ng[:, cols]).astype(y_ref.dtype)


def _hgrn_sample(proj, lb, norm_g, state):
    rows = HG_SAMPLE_SEQS * DEC_SEQ
    rb = N_PROMPT // rows
    cq = 3 * CONV_CH // HG_W

    def spec(off):
        return pl.BlockSpec((rows, HG_W), lambda g: (rb + g, cq + off))

    st_spec = pl.BlockSpec((HG_SAMPLE_SEQS, HG_HEADS, HG_DK, HG_DV), lambda g: (g, 0, 0, 0))
    return pl.pallas_call(
        _hgrn_sample_kernel,
        grid=(DEC_BATCH // HG_SAMPLE_SEQS,),
        in_specs=[spec(0), spec(1), spec(2), spec(3),
                  pl.BlockSpec((1, HG_W), lambda g: (0, 0)),
                  pl.BlockSpec((1, HG_W), lambda g: (0, 0)),
                  st_spec],
        out_specs=[pl.BlockSpec((rows, HG_W), lambda g: (g, 0)), st_spec],
        out_shape=[
            jax.ShapeDtypeStruct((N_SAMPLE, HG_W), BF16),
            jax.ShapeDtypeStruct((DEC_BATCH, HG_HEADS, HG_DK, HG_DV), F32),
        ],
        compiler_params=_params(("arbitrary",), 40),
        name="hgrn_sample",
    )(proj, proj, proj, proj, lb, norm_g, state)


def _out_proj_kernel(ycp_ref, yhp_ref, xp_ref, ycs_ref, yhs_ref, xs_ref, wc_ref, wh_ref, o_ref, *,
                     n_prompt_tiles):
    m = pl.program_id(1)

    def residual_mix(yc_ref, yh_ref, x_ref):
        mix = (jnp.dot(yc_ref[...], wc_ref[...], preferred_element_type=F32)
               + jnp.dot(yh_ref[...], wh_ref[...], preferred_element_type=F32))
        o_ref[...] = x_ref[...] + mix

    @pl.when(m < n_prompt_tiles)
    def _():
        residual_mix(ycp_ref, yhp_ref, xp_ref)

    @pl.when(m >= n_prompt_tiles)
    def _():
        residual_mix(ycs_ref, yhs_ref, xs_ref)


def _out_proj(yc_p, yh_p, xp, yc_s, yh_s, xs, w):
    npt = N_PROMPT // MM_ROWS

    def prompt_map(n, m):
        return jnp.minimum(m, npt - 1)

    def sample_map(n, m):
        return jnp.maximum(m - npt, 0)

    def source_specs(row_map):
        return [
            pl.BlockSpec((MM_ROWS, CONV_CH), lambda n, m: (row_map(n, m), 0)),
            pl.BlockSpec((MM_ROWS, HG_W), lambda n, m: (row_map(n, m), 0)),
            pl.BlockSpec((MM_ROWS, MM_COLS), lambda n, m: (row_map(n, m), n)),
        ]

    return pl.pallas_call(
        functools.partial(_out_proj_kernel, n_prompt_tiles=npt),
        grid=(D_MODEL // MM_COLS, N_TOK // MM_ROWS),
        in_specs=source_specs(prompt_map) + source_specs(sample_map) + [
            pl.BlockSpec((CONV_CH, MM_COLS), lambda n, m: (0, n)),
            pl.BlockSpec((HG_W, MM_COLS), lambda n, m: (1, n)),
        ],
        out_specs=pl.BlockSpec((MM_ROWS, MM_COLS), lambda n, m: (m, n)),
        out_shape=jax.ShapeDtypeStruct((N_TOK, D_MODEL), F32),
        compiler_params=_params(("arbitrary", "arbitrary"), 56),
        name="out_proj",
    )(yc_p, yh_p, xp, yc_s, yh_s, xs, w, w)


def _split_bf16(x):
    hi = x.astype(BF16)
    return hi, (x - hi.astype(F32)).astype(BF16)


_HI16 = 0xFFFF0000


def _pack_bf16_halves(x):
    half = x.shape[1] // 2
    bits = lax.bitcast_convert_type(x.astype(F32), jnp.uint32)
    return (bits[:, :half] >> 16) | (bits[:, half:] & jnp.uint32(_HI16))


def _unpack_bf16_half(packed, high):
    bits = (packed & jnp.uint32(_HI16)) if high else (packed << 16)
    return lax.bitcast_convert_type(bits, F32)


def _norm_route_kernel(h_ref, g_ref, wr_hi_ref, wr_lo_ref, br_ref, hn_ref, eidx_ref, gate_ref):
    hn = _rmsnorm_rows(h_ref[...], g_ref[...])
    hn_hi, hn_lo = _split_bf16(hn)
    hn_ref[...] = _pack_bf16_halves(hn_hi)
    logits = (jnp.dot(hn_hi, wr_hi_ref[...], preferred_element_type=F32)
              + (jnp.dot(hn_hi, wr_lo_ref[...], preferred_element_type=F32)
                 + jnp.dot(hn_lo, wr_hi_ref[...], preferred_element_type=F32))) + br_ref[...]
    lane = lax.broadcasted_iota(I32, logits.shape, 1)
    lane_f = lane.astype(F32)
    neg = -jnp.inf

    def first_argmax(vals, vmax):
        first = jnp.min(jnp.where(vals == vmax, lane_f, float(ROUTE_COLS)), axis=-1, keepdims=True)
        return first.astype(I32)

    is_grp = lane < N_GROUPS
    lg = jnp.where(is_grp, logits, neg)
    mg = jnp.max(lg, axis=-1, keepdims=True)
    g_sel = first_argmax(lg, mg)
    p_grp = 1.0 / jnp.sum(jnp.where(is_grp, jnp.exp(logits - mg), 0.0), axis=-1, keepdims=True)

    in_grp = (lane >= N_GROUPS) & (((lane - N_GROUPS) // EXP_PER_GROUP) == g_sel)
    le = jnp.where(in_grp, logits, neg)
    m1 = jnp.max(le, axis=-1, keepdims=True)
    i1 = first_argmax(le, m1)
    le2 = jnp.where(lane == i1, neg, le)
    m2 = jnp.max(le2, axis=-1, keepdims=True)
    i2 = first_argmax(le2, m2)
    e2 = jnp.exp(m2 - m1)
    gate1 = p_grp / (1.0 + e2)
    gate2 = p_grp * e2 / (1.0 + e2)
    eidx_ref[...] = jnp.where(lane == 0, i1 - N_GROUPS, jnp.where(lane == 1, i2 - N_GROUPS, 0))
    gate_ref[...] = jnp.where(lane == 0, gate1, jnp.where(lane == 1, gate2, 0.0))


def _norm_route(h, g, w_r, b_r):
    wr_hi, wr_lo = _split_bf16(w_r)
    return pl.pallas_call(
        _norm_route_kernel,
        grid=(N_TOK // NORM_ROWS,),
        in_specs=[
            pl.BlockSpec((NORM_ROWS, D_MODEL), lambda i: (i, 0)),
            pl.BlockSpec((1, D_MODEL), lambda i: (0, 0)),
            pl.BlockSpec((D_MODEL, ROUTE_COLS), lambda i: (0, 0)),
            pl.BlockSpec((D_MODEL, ROUTE_COLS), lambda i: (0, 0)),
            pl.BlockSpec((1, ROUTE_COLS), lambda i: (0, 0)),
        ],
        out_specs=[
            pl.BlockSpec((NORM_ROWS, D_MODEL // 2), lambda i: (i, 0)),
            pl.BlockSpec((NORM_ROWS, ROUTE_COLS), lambda i: (i, 0)),
            pl.BlockSpec((NORM_ROWS, ROUTE_COLS), lambda i: (i, 0)),
        ],
        out_shape=[
            jax.ShapeDtypeStruct((N_TOK, D_MODEL // 2), jnp.uint32),
            jax.ShapeDtypeStruct((N_TOK, ROUTE_COLS), I32),
            jax.ShapeDtypeStruct((N_TOK, ROUTE_COLS), F32),
        ],
        compiler_params=_params(("arbitrary",), 48),
        name="norm_route",
    )(h, g, wr_hi, wr_lo, b_r)


def _row_copy(src_hbm, row, buf, slot, r, sem):
    return pltpu.make_async_copy(src_hbm.at[pl.ds(row, 1)], buf.at[slot, pl.ds(r, 1)], sem.at[slot])


def _gather_issue(idx_ref, base, row0, n_rows, src_hbm, buf, slot, sem):
    for r in range(row0, row0 + n_rows):
        _row_copy(src_hbm, idx_ref[base + r], buf, slot, r, sem).start()


def _gather_wait(row0, n_rows, src_hbm, buf, slot, sem):
    for r in range(row0, row0 + n_rows, SUBLANES):
        rows = pl.ds(r, SUBLANES)
        pltpu.make_async_copy(src_hbm.at[pl.ds(0, SUBLANES)], buf.at[slot, rows], sem.at[slot]).wait()


def _route_plan_kernel(eidx_ref, dest_ref, counts_ref, start_ref, rank_scr):
    n_tiles = N_TOK // PLAN_ROWS
    lane = lax.broadcasted_iota(I32, (PLAN_ROWS, ROUTE_COLS), 1)
    ri = lax.broadcasted_iota(I32, (PLAN_ROWS, PLAN_ROWS), 0)
    ci = lax.broadcasted_iota(I32, (PLAN_ROWS, PLAN_ROWS), 1)
    earlier = jnp.where(ci < ri, 1.0, 0.0).astype(BF16)

    def tile_rows(t):
        return pl.ds(pl.multiple_of(t * PLAN_ROWS, PLAN_ROWS), PLAN_ROWS)

    def pick(vals, col):
        return jnp.sum(jnp.where(lane == col, vals, 0.0), axis=-1, keepdims=True)

    def on_lanes01(v0, v1):
        return jnp.where(lane == 0, v0, jnp.where(lane == 1, v1, 0.0))

    def rank_tile(t, seen):
        e = eidx_ref[tile_rows(t), :]
        e0, e1 = e[:, 0:1], e[:, 1:2]
        uses = jnp.where((lane == e0) | (lane == e1), 1.0, 0.0)
        before = jnp.dot(earlier, uses.astype(BF16), preferred_element_type=F32) + seen
        rank_scr[tile_rows(t), :] = on_lanes01(pick(before, e0), pick(before, e1))
        return seen + jnp.sum(uses, axis=0, keepdims=True)

    counts = lax.fori_loop(0, n_tiles, rank_tile, jnp.zeros((1, ROUTE_COLS), F32))
    padded = jnp.ceil(counts * (1.0 / MOE_BLK)) * MOE_BLK
    ui = lax.broadcasted_iota(I32, (ROUTE_COLS, ROUTE_COLS), 0)
    uj = lax.broadcasted_iota(I32, (ROUTE_COLS, ROUTE_COLS), 1)
    start = jnp.dot(jnp.broadcast_to(padded, (SUBLANES, ROUTE_COLS)), jnp.where(ui < uj, 1.0, 0.0),
                    precision=lax.Precision.HIGHEST, preferred_element_type=F32)[0:1]
    counts_ref[...] = counts.astype(I32)
    start_ref[...] = start.astype(I32)

    def dest_tile(t, carry):
        e = eidx_ref[tile_rows(t), :]
        rank = rank_scr[tile_rows(t), :]
        d0 = pick(start, e[:, 0:1]) + rank[:, 0:1]
        d1 = pick(start, e[:, 1:2]) + rank[:, 1:2]
        dest_ref[tile_rows(t), :] = on_lanes01(d0, d1).astype(I32)
        return carry

    lax.fori_loop(0, n_tiles, dest_tile, 0)


def _route_plan(eidx):
    return pl.pallas_call(
        _route_plan_kernel,
        out_shape=[
            jax.ShapeDtypeStruct((N_TOK, ROUTE_COLS), I32),
            jax.ShapeDtypeStruct((1, ROUTE_COLS), I32),
            jax.ShapeDtypeStruct((1, ROUTE_COLS), I32),
        ],
        scratch_shapes=[pltpu.VMEM((N_TOK, ROUTE_COLS), F32)],
        compiler_params=pltpu.CompilerParams(vmem_limit_bytes=40 * MIB),
        name="route_plan",
    )(eidx)


def _dispatch_kernel(dest_ref, counts_ref, start_ref, hn_hbm, o_ref, row_tok, n_parts, buf, sem):
    b = pl.program_id(0)
    nb = pl.num_programs(0)
    slot = b % 2
    parts_per_blk = MOE_BLK // DISPATCH_PART

    def fetch(blk, dst_slot):
        for p in range(parts_per_blk):
            @pl.when(p < n_parts[blk])
            def _(p=p):
                _gather_issue(row_tok, blk * MOE_BLK, p * DISPATCH_PART, DISPATCH_PART, hn_hbm, buf,
                              dst_slot, sem)

            @pl.when(p >= n_parts[blk])
            def _(p=p):
                buf[dst_slot, p * DISPATCH_PART:(p + 1) * DISPATCH_PART] = jnp.zeros(
                    (DISPATCH_PART,) + buf.shape[2:], buf.dtype)

    @pl.when(b == 0)
    def _():
        def clear(blk, carry):
            n_parts[blk] = 0
            return carry

        lax.fori_loop(0, MOE_NB, clear, 0)

        def per_expert(e, carry):
            count, first = counts_ref[e], start_ref[e]

            def per_block(i, c):
                rows = jnp.minimum(count - i * MOE_BLK, MOE_BLK)
                parts = (rows + DISPATCH_PART - 1) // DISPATCH_PART
                n_parts[first // MOE_BLK + i] = parts

                def pad_row(r, c2):
                    row_tok[r] = r % N_PROMPT
                    return c2

                row0 = first + i * MOE_BLK
                lax.fori_loop(row0 + rows, row0 + parts * DISPATCH_PART, pad_row, 0)
                return c

            lax.fori_loop(0, (count + MOE_BLK - 1) // MOE_BLK, per_block, 0)
            return carry

        lax.fori_loop(0, N_EXPERTS, per_expert, 0)

        def invert(g, carry):
            for j in range(SUBLANES):
                a = g * SUBLANES + j
                row_tok[dest_ref[a]] = g * (SUBLANES // TOP_K) + j // TOP_K
            return carry

        lax.fori_loop(0, N_ASSIGN // SUBLANES, invert, 0)
        fetch(0, 0)

    @pl.when(b + 1 < nb)
    def _():
        fetch(b + 1, 1 - slot)

    for p in range(parts_per_blk):
        @pl.when(p < n_parts[b])
        def _(p=p):
            _gather_wait(p * DISPATCH_PART, DISPATCH_PART, hn_hbm, buf, slot, sem)

    o_ref[...] = buf[slot]


def _dispatch(dest_flat, counts, start, hn_packed):
    return pl.pallas_call(
        _dispatch_kernel,
        grid_spec=pltpu.PrefetchScalarGridSpec(
            num_scalar_prefetch=3,
            grid=(MOE_NB,),
            in_specs=[pl.BlockSpec(memory_space=pl.ANY)],
            out_specs=pl.BlockSpec((MOE_BLK, D_MODEL // 2), lambda b, *_: (b, 0)),
            scratch_shapes=[pltpu.SMEM((MOE_ROWS,), I32),
                            pltpu.SMEM((MOE_NB,), I32),
                            pltpu.VMEM((2, MOE_BLK, D_MODEL // 2), jnp.uint32),
                            pltpu.SemaphoreType.DMA((2,))],
        ),
        out_shape=jax.ShapeDtypeStruct((MOE_ROWS, D_MODEL // 2), jnp.uint32),
        compiler_params=_params(("arbitrary",), 16, disable_bounds_checks=True),
        name="moe_dispatch",
    )(dest_flat, counts, start, hn_packed)


def _stream_expert_chunks(blk0_ref, n_ref, n_used, in_copy, out_copy, out_buf, compute):
    c = pl.program_id(1)
    n_chunks = pl.num_programs(1)

    def for_blocks(chunk, fn):
        blk0 = blk0_ref[chunk]

        def body(i, carry):
            fn(blk0 + i, i)
            return carry

        lax.fori_loop(0, n_ref[chunk], body, 0)

    def start_loads(chunk):
        for_blocks(chunk, lambda g, i: in_copy(g, chunk % MOE_IN_SLOTS, i).start())

    @pl.when(c == 0)
    def _():
        start_loads(0)
        start_loads(1)

    @pl.when(c + 2 < n_chunks)
    def _():
        start_loads(c + 2)

    in_slot = c % MOE_IN_SLOTS
    out_slot = c % MOE_OUT_SLOTS
    for_blocks(c, lambda g, i: in_copy(g, in_slot, i).wait())

    @pl.when(c >= MOE_OUT_SLOTS)
    def _():
        for_blocks(c - MOE_OUT_SLOTS, lambda g, i: out_copy(g, out_slot, i).wait())

    for m in range(1, MOE_CHUNK_BLKS + 1):
        @pl.when(n_ref[c] == m)
        def _(m=m):
            compute(m, in_slot, out_slot)

    for_blocks(c, lambda g, i: out_copy(g, out_slot, i).start())

    @pl.when(c == n_chunks - 1)
    def _():
        for chunk in (c - 1, c):
            for_blocks(chunk, lambda g, i: out_copy(g, chunk % MOE_OUT_SLOTS, i).wait())
        out_buf[0, 0:MOE_BLK] = jnp.zeros((MOE_BLK,) + out_buf.shape[2:], out_buf.dtype)

        def start_zero(g, carry):
            out_copy(g, 0, 0).start()
            return carry

        def wait_zero(g, carry):
            out_copy(g, 0, 0).wait()
            return carry

        lax.fori_loop(n_used, MOE_NB, start_zero, 0)
        lax.fori_loop(n_used, MOE_NB, wait_zero, 0)


def _block_rows(g):
    return pl.ds(pl.multiple_of(g * MOE_BLK, MOE_BLK), MOE_BLK)


def _slot_rows(i):
    return pl.ds(pl.multiple_of(i * MOE_BLK, MOE_BLK), MOE_BLK)


def _expert_up_kernel(blk0_ref, n_ref, e_ref, nbu_ref, wg_ref, wu_ref, x_hbm, h_hbm,
                      xbuf, hbuf, zbuf, sem_x, sem_h, sem_z):
    del e_ref
    c = pl.program_id(0)
    j = pl.program_id(1)
    n_sub = pl.num_programs(1)
    t = c * n_sub + j
    cols = pl.ds(pl.multiple_of(j * MOE_FC, MOE_FC), MOE_FC)

    def for_blocks(chunk, fn):
        blk0 = blk0_ref[chunk]

        def body(i, carry):
            fn(blk0 + i, i)
            return carry

        lax.fori_loop(0, n_ref[chunk], body, 0)

    def x_copy(g, slot, i):
        return pltpu.make_async_copy(x_hbm.at[_block_rows(g)], xbuf.at[slot, _slot_rows(i)], sem_x.at[slot])

    def start_loads(chunk):
        for_blocks(chunk, lambda g, i: x_copy(g, chunk % MOE_IN_SLOTS, i).start())

    def h_copy(g, slot, i):
        return pltpu.make_async_copy(hbuf.at[slot, _slot_rows(i)], h_hbm.at[_block_rows(g), cols],
                                     sem_h.at[slot])

    @pl.when(j == 0)
    def _():
        @pl.when(c == 0)
        def _():
            start_loads(0)
            start_loads(1)

        @pl.when(c + 2 < pl.num_programs(0))
        def _():
            start_loads(c + 2)

        for_blocks(c, lambda g, i: x_copy(g, c % MOE_IN_SLOTS, i).wait())

    in_slot = c % MOE_IN_SLOTS
    out_slot = t % MOE_OUT_SLOTS

    @pl.when(t >= MOE_OUT_SLOTS)
    def _():
        for_blocks(c - 1, lambda g, i: h_copy(g, out_slot, i).wait())

    def compute(m, in_slot, out_slot):
        rows = m * MOE_BLK
        kc = D_MODEL // MOE_K_CHUNKS
        half_chunks = MOE_K_CHUNKS // 2
        g = u = None
        for i in range(MOE_K_CHUNKS):
            ks = slice(i * kc, (i + 1) * kc)
            ps = slice((i % half_chunks) * kc, (i % half_chunks + 1) * kc)
            xk = _unpack_bf16_half(xbuf[in_slot, :rows, ps], high=i >= half_chunks)
            gi = jnp.dot(xk, wg_ref[0, ks, :], preferred_element_type=F32)
            ui = jnp.dot(xk, wu_ref[0, ks, :], preferred_element_type=F32)
            g, u = (gi, ui) if g is None else (g + gi, u + ui)
        hbuf[out_slot, :rows] = (g * _sigmoid(g) * u).astype(hbuf.dtype)

    for m in range(1, MOE_CHUNK_BLKS + 1):
        @pl.when(n_ref[c] == m)
        def _(m=m):
            compute(m, in_slot, out_slot)

    for_blocks(c, lambda g, i: h_copy(g, out_slot, i).start())

    @pl.when(c == pl.num_programs(0) - 1)
    def _():
        zbuf[...] = jnp.zeros_like(zbuf)

        def z_copy(g):
            return pltpu.make_async_copy(zbuf, h_hbm.at[_block_rows(g), cols], sem_z.at[0])

        def start_zero(g, carry):
            z_copy(g).start()
            return carry

        def wait_zero(g, carry):
            z_copy(g).wait()
            return carry

        lax.fori_loop(nbu_ref[0], MOE_NB, start_zero, 0)
        lax.fori_loop(nbu_ref[0], MOE_NB, wait_zero, 0)

        @pl.when(j == n_sub - 1)
        def _():
            for slot in range(MOE_OUT_SLOTS):
                for_blocks(c, lambda g, i: h_copy(g, slot, i).wait())


def _chunk_scratch(in_cols, in_dtype, out_cols, out_dtype):
    rows = MOE_CHUNK_BLKS * MOE_BLK
    return [pltpu.VMEM((MOE_IN_SLOTS, rows, in_cols), in_dtype),
            pltpu.VMEM((MOE_OUT_SLOTS, rows, out_cols), out_dtype),
            pltpu.SemaphoreType.DMA((MOE_IN_SLOTS,)),
            pltpu.SemaphoreType.DMA((MOE_OUT_SLOTS,))]


def _expert_up(chunks, n_used, xb, w_g, w_u):
    assert D_EXPERT // MOE_FC == MOE_OUT_SLOTS

    def w_map(c, j, blk0, n, e, nbu):
        return (e[c], 0, j)

    x_scratch, h_scratch, sem_x, sem_h = _chunk_scratch(D_MODEL // 2, jnp.uint32, MOE_FC, BF16)
    return pl.pallas_call(
        _expert_up_kernel,
        grid_spec=pltpu.PrefetchScalarGridSpec(
            num_scalar_prefetch=4,
            grid=(MOE_NCH, D_EXPERT // MOE_FC),
            in_specs=[pl.BlockSpec((1, D_MODEL, MOE_FC), w_map),
                      pl.BlockSpec((1, D_MODEL, MOE_FC), w_map),
                      pl.BlockSpec(memory_space=pl.ANY)],
            out_specs=pl.BlockSpec(memory_space=pl.ANY),
            scratch_shapes=[x_scratch, h_scratch, pltpu.VMEM((MOE_BLK, MOE_FC), BF16),
                            sem_x, sem_h, pltpu.SemaphoreType.DMA((1,))],
        ),
        out_shape=jax.ShapeDtypeStruct((MOE_ROWS, D_EXPERT), BF16),
        compiler_params=_params(("arbitrary", "arbitrary"), 58),
        name="expert_up",
    )(*chunks, n_used, w_g, w_u, xb)


def _expert_down_kernel(blk0_ref, n_ref, e_ref, nbu_ref, wd_ref, h_hbm, y_hbm, hbuf, ybuf, sem_h, sem_y):
    del e_ref
    wc = MOE_DC // 2
    cols = pl.ds(pl.multiple_of(pl.program_id(0) * wc, wc), wc)

    def h_copy(g, slot, i):
        return pltpu.make_async_copy(h_hbm.at[_block_rows(g)], hbuf.at[slot, _slot_rows(i)], sem_h.at[slot])

    def y_copy(g, slot, i):
        return pltpu.make_async_copy(ybuf.at[slot, _slot_rows(i)], y_hbm.at[_block_rows(g), cols],
                                     sem_y.at[slot])

    def compute(m, in_slot, out_slot):
        rows = m * MOE_BLK
        y = jnp.dot(hbuf[in_slot, :rows].astype(F32), wd_ref[0], preferred_element_type=F32)
        ybuf[out_slot, :rows] = _pack_bf16_halves(y.astype(BF16))

    _stream_expert_chunks(blk0_ref, n_ref, nbu_ref[0], h_copy, y_copy, ybuf, compute)


def _expert_down(chunks, n_used, hb, w_d):
    return pl.pallas_call(
        _expert_down_kernel,
        grid_spec=pltpu.PrefetchScalarGridSpec(
            num_scalar_prefetch=4,
            grid=(D_MODEL // MOE_DC, MOE_NCH),
            in_specs=[pl.BlockSpec((1, D_EXPERT, MOE_DC), lambda n, c, blk0, nb, e, nbu: (e[c], 0, n)),
                      pl.BlockSpec(memory_space=pl.ANY)],
            out_specs=pl.BlockSpec(memory_space=pl.ANY),
            scratch_shapes=_chunk_scratch(D_EXPERT, BF16, MOE_DC // 2, jnp.uint32),
        ),
        out_shape=jax.ShapeDtypeStruct((MOE_ROWS, D_MODEL // 2), jnp.uint32),
        compiler_params=_params(("arbitrary", "arbitrary"), 48),
        name="expert_down",
    )(*chunks, n_used, w_d, hb)


def _combine_kernel(dest_ref, h_ref, gate_ref, g_ref, y_hbm, o_ref, buf, sem, *, tile0):
    i = pl.program_id(0)
    n = pl.num_programs(0)
    slot = i % 2
    rows = TOP_K * COMB_ROWS
    base = (tile0 + i) * rows

    @pl.when(i == 0)
    def _():
        _gather_issue(dest_ref, base, 0, rows, y_hbm, buf, 0, sem)

    @pl.when(i + 1 < n)
    def _():
        _gather_issue(dest_ref, base + rows, 0, rows, y_hbm, buf, 1 - slot, sem)

    _gather_wait(0, rows, y_hbm, buf, slot, sem)

    def expert_rows(k):
        words = buf[slot, k * COMB_ROWS:(k + 1) * COMB_ROWS]
        wc = MOE_DC // 2
        parts = []
        for p in range(D_MODEL // MOE_DC):
            w = words[:, p * wc:(p + 1) * wc]
            parts += [_unpack_bf16_half(w, high=False), _unpack_bf16_half(w, high=True)]
        return jnp.concatenate(parts, axis=1)

    gates = gate_ref[...]
    ff = gates[:, 0:1] * expert_rows(0) + gates[:, 1:2] * expert_rows(1)
    o_ref[...] = _rmsnorm_rows(h_ref[...] + ff, g_ref[...])


def _combine(dest, h, gates, g_final, y_buf, tile0, n_tiles):
    return pl.pallas_call(
        functools.partial(_combine_kernel, tile0=tile0),
        grid_spec=pltpu.PrefetchScalarGridSpec(
            num_scalar_prefetch=1,
            grid=(n_tiles,),
            in_specs=[
                pl.BlockSpec((COMB_ROWS, D_MODEL), lambda i, d: (tile0 + i, 0)),
                pl.BlockSpec((COMB_ROWS, ROUTE_COLS), lambda i, d: (tile0 + i, 0)),
                pl.BlockSpec((1, D_MODEL), lambda i, d: (0, 0)),
                pl.BlockSpec(memory_space=pl.ANY),
            ],
            out_specs=pl.BlockSpec((COMB_ROWS, D_MODEL), lambda i, d: (i, 0)),
            scratch_shapes=[pltpu.VMEM((2, TOP_K * COMB_ROWS, D_MODEL // 2), jnp.uint32),
                            pltpu.SemaphoreType.DMA((2,))],
        ),
        out_shape=jax.ShapeDtypeStruct((n_tiles * COMB_ROWS, D_MODEL), F32),
        compiler_params=_params(("arbitrary",), 40, disable_bounds_checks=True),
        name="moe_combine",
    )(dest, h, gates, g_final, y_buf)


def _chunk_list(counts, start):
    blk0 = start // MOE_BLK
    nblk = (counts + MOE_BLK - 1) // MOE_BLK
    n_used = (blk0[-1] + nblk[-1]).astype(I32)
    nch = (nblk + MOE_CHUNK_BLKS - 1) // MOE_CHUNK_BLKS
    ch_end = jnp.cumsum(nch)
    ch = jnp.arange(MOE_NCH, dtype=I32)
    ch_e = jnp.minimum(jnp.searchsorted(ch_end, ch, side='right'), N_EXPERTS - 1)
    local = ch - (ch_end - nch)[ch_e]
    live = ch < ch_end[-1]
    ch_blk0 = (blk0[ch_e] + MOE_CHUNK_BLKS * local).astype(I32)
    ch_n = jnp.where(live, jnp.clip(nblk[ch_e] - MOE_CHUNK_BLKS * local, 0, MOE_CHUNK_BLKS), 0).astype(I32)
    ch_e = jnp.where(live, ch_e, ch_e[ch_end[-1] - 1]).astype(I32)
    return (ch_blk0, ch_n, ch_e), n_used.reshape(1)


def kernel(x_prompt, x_sample, cache_conv, state_hgrn, norm_mix_g, w_in, conv_w, lb_param, hg_norm_g,
           w_out, norm_ffn_g, w_group_router, b_group_router, w_expert_router, b_expert_router,
           w_exp_gate, w_exp_up, w_exp_down, norm_final_g):
    xp = x_prompt.reshape(N_PROMPT, D_MODEL)
    xs = x_sample.reshape(N_SAMPLE, D_MODEL)
    lb = jnp.cumsum(jax.nn.softmax(lb_param.astype(F32), axis=0), axis=0)[0].reshape(1, HG_W)

    xn = _norm_in(xp, xs, norm_mix_g[0].reshape(1, D_MODEL))
    proj = _in_proj(xn, w_in[0])

    buf = cache_conv[0]
    zeros = jnp.zeros((DEC_BATCH, DEC_SEQ - 1, CONV_CH), F32)
    e1 = jnp.concatenate([buf[:, 1:2], zeros], axis=1).reshape(N_SAMPLE, CONV_CH)
    e2 = jnp.concatenate([buf, zeros[:, :DEC_SEQ - 2]], axis=1).reshape(N_SAMPLE, CONV_CH)
    yc_p, conv_p = _conv_prompt(proj, conv_w[0])
    yc_s, u_s = _conv_sample(proj, conv_w[0], e1, e2)
    conv_s = u_s.reshape(DEC_BATCH, DEC_SEQ, CONV_CH)[:, DEC_SEQ - (CONV_K - 1):]

    ng = hg_norm_g[0].reshape(1, HG_W)
    yh_p, hgrn_p = _hgrn_prompt(proj, lb, ng)
    yh_s, hgrn_s = _hgrn_sample(proj, lb, ng, state_hgrn[0])

    h = _out_proj(yc_p, yh_p, xp, yc_s, yh_s, xs, w_out[0].astype(BF16))

    pad = jnp.zeros((D_MODEL, ROUTE_COLS - N_GROUPS - N_EXPERTS), F32)
    w_r = jnp.concatenate([w_group_router[0], w_expert_router[0], pad], axis=1)
    b_r = jnp.concatenate([b_group_router[0], b_expert_router[0], pad[0]]).reshape(1, ROUTE_COLS)
    hn_packed, eidx, gates = _norm_route(h, norm_ffn_g[0].reshape(1, D_MODEL), w_r, b_r)
    dest, counts, start = _route_plan(eidx)
    dest = dest[:, :TOP_K]
    counts, start = counts[0, :N_EXPERTS], start[0, :N_EXPERTS]
    chunks, n_used = _chunk_list(counts, start)
    xb = _dispatch(dest.reshape(N_ASSIGN), counts, start, hn_packed)
    hb = _expert_up(chunks, n_used, xb, w_exp_gate[0], w_exp_up[0])
    yb = _expert_down(chunks, n_used, hb, w_exp_down[0])

    dest_t = dest.reshape(N_TOK // COMB_ROWS, COMB_ROWS, TOP_K).transpose(0, 2, 1).reshape(N_ASSIGN)
    gf = norm_final_g.reshape(1, D_MODEL)
    y_p = _combine(dest_t, h, gates, gf, yb, 0, N_PROMPT // COMB_ROWS)
    y_s = _combine(dest_t, h, gates, gf, yb, N_PROMPT // COMB_ROWS, N_SAMPLE // COMB_ROWS)

    return (y_p.reshape(BATCH, SEQ, D_MODEL), y_s.reshape(DEC_BATCH, DEC_SEQ, D_MODEL),
            conv_p[None], hgrn_p[None], conv_s[None], hgrn_s[None])
```

```python
import functools

import jax
import jax.numpy as jnp
from jax import lax
from jax.experimental import pallas as pl
from jax.experimental.pallas import tpu as pltpu

F32 = jnp.float32
BF16 = jnp.bfloat16
I32 = jnp.int32

D_MODEL = 4096
BATCH = 4
SEQ = 2048
DEC_BATCH = 128
DEC_SEQ = 4
CONV_CH = 2048
CONV_K = 3
HG_HEADS = 16
HG_DK = 128
HG_DV = 128
HG_W = HG_HEADS * HG_DK
HG_CHUNK = 64
IN_COLS = 3 * CONV_CH + 4 * HG_W
N_GROUPS = 8
EXP_PER_GROUP = 8
N_EXPERTS = 64
TOP_K = 2
D_EXPERT = 1024
EPS = 1e-6

N_PROMPT = BATCH * SEQ
N_SAMPLE = DEC_BATCH * DEC_SEQ
N_TOK = N_PROMPT + N_SAMPLE
N_ASSIGN = N_TOK * TOP_K

LANES = 128
SUBLANES = 8
ROUTE_COLS = LANES

NORM_ROWS = 256
MM_ROWS = 512
MM_COLS = 1024
CONV_ROWS = 512
CONV_COLS = 2048
HG_ROWS = 512
HG_SAMPLE_SEQS = 4
MOE_BLK = 128
MOE_NB = N_ASSIGN // MOE_BLK + N_EXPERTS
MOE_ROWS = MOE_NB * MOE_BLK
MOE_FC = 512
MOE_DC = 2048
MOE_CHUNK_BLKS = 4
MOE_NCH = N_EXPERTS + N_ASSIGN // (MOE_BLK * MOE_CHUNK_BLKS)
MOE_IN_SLOTS = 3
MOE_OUT_SLOTS = 2
MOE_K_CHUNKS = 4
PLAN_ROWS = 512
DISPATCH_PART = 32
COMB_ROWS = 128

MIB = 1 << 20


def _params(sem, vmem_mib, **kwargs):
    return pltpu.CompilerParams(dimension_semantics=sem, vmem_limit_bytes=vmem_mib * MIB, **kwargs)


def _sigmoid(x):
    return 1.0 / (1.0 + jnp.exp(-x))


def _rmsnorm_rows(x, g):
    return x * lax.rsqrt(jnp.mean(x * x, axis=-1, keepdims=True) + EPS) * g


def _norm_in_kernel(xp_ref, xs_ref, g_ref, o_ref, *, n_prompt_tiles):
    i = pl.program_id(0)

    @pl.when(i < n_prompt_tiles)
    def _():
        o_ref[...] = _rmsnorm_rows(xp_ref[...], g_ref[...]).astype(o_ref.dtype)

    @pl.when(i >= n_prompt_tiles)
    def _():
        o_ref[...] = _rmsnorm_rows(xs_ref[...], g_ref[...]).astype(o_ref.dtype)


def _norm_in(xp, xs, g):
    npt = N_PROMPT // NORM_ROWS
    nst = N_SAMPLE // NORM_ROWS
    return pl.pallas_call(
        functools.partial(_norm_in_kernel, n_prompt_tiles=npt),
        grid=(npt + nst,),
        in_specs=[
            pl.BlockSpec((NORM_ROWS, D_MODEL), lambda i: (jnp.minimum(i, npt - 1), 0)),
            pl.BlockSpec((NORM_ROWS, D_MODEL), lambda i: (jnp.maximum(i - npt, 0), 0)),
            pl.BlockSpec((1, D_MODEL), lambda i: (0, 0)),
        ],
        out_specs=pl.BlockSpec((NORM_ROWS, D_MODEL), lambda i: (i, 0)),
        out_shape=jax.ShapeDtypeStruct((N_TOK, D_MODEL), BF16),
        compiler_params=_params(("arbitrary",), 40),
        name="norm_in",
    )(xp, xs, g)


def _in_proj_kernel(x_ref, w_ref, o_ref, wb_ref):
    @pl.when(pl.program_id(1) == 0)
    def _():
        wb_ref[...] = w_ref[...].astype(BF16)

    o_ref[...] = jnp.dot(x_ref[...], wb_ref[...], preferred_element_type=F32).astype(o_ref.dtype)


def _in_proj(xn, w):
    return pl.pallas_call(
        _in_proj_kernel,
        grid=(IN_COLS // MM_COLS, N_TOK // MM_ROWS),
        in_specs=[
            pl.BlockSpec((MM_ROWS, D_MODEL), lambda n, m: (m, 0)),
            pl.BlockSpec((D_MODEL, MM_COLS), lambda n, m: (0, n)),
        ],
        out_specs=pl.BlockSpec((MM_ROWS, MM_COLS), lambda n, m: (m, n)),
        out_shape=jax.ShapeDtypeStruct((N_TOK, IN_COLS), BF16),
        scratch_shapes=[pltpu.VMEM((D_MODEL, MM_COLS), BF16)],
        compiler_params=_params(("arbitrary", "arbitrary"), 58),
        name="in_proj",
    )(xn, w)


def _conv_taps(u, um1, um2, cb, w):
    conv = um2 * w[0:1] + um1 * w[1:2] + u * w[2:3]
    return cb * conv


def _conv_prompt_kernel(cb_ref, cc_ref, ch_ref, w_ref, y_ref, nb_ref, carry_ref):
    t = pl.program_id(2)

    @pl.when(t == 0)
    def _():
        carry_ref[...] = jnp.zeros_like(carry_ref)

    u = cc_ref[...].astype(F32) * ch_ref[...].astype(F32)
    prev = carry_ref[...]
    p1 = prev[SUBLANES - 1:SUBLANES]
    p2 = prev[SUBLANES - 2:SUBLANES - 1]
    row = lax.broadcasted_iota(I32, u.shape, 0)
    um1 = jnp.where(row == 0, p1, pltpu.roll(u, 1, axis=0))
    um2 = jnp.where(row == 0, p2, jnp.where(row == 1, p1, pltpu.roll(u, 2, axis=0)))
    y_ref[...] = _conv_taps(u, um1, um2, cb_ref[...].astype(F32), w_ref[...]).astype(y_ref.dtype)
    carry_ref[...] = u[CONV_ROWS - SUBLANES:CONV_ROWS]
    nb_ref[0] = u[CONV_ROWS - (CONV_K - 1):CONV_ROWS]


def _conv_prompt(proj, conv_w):
    nt = SEQ // CONV_ROWS
    nc = CONV_CH // CONV_COLS
    return pl.pallas_call(
        _conv_prompt_kernel,
        grid=(BATCH, nc, nt),
        in_specs=[
            pl.BlockSpec((CONV_ROWS, CONV_COLS), lambda b, c, t: (b * nt + t, c)),
            pl.BlockSpec((CONV_ROWS, CONV_COLS), lambda b, c, t: (b * nt + t, nc + c)),
            pl.BlockSpec((CONV_ROWS, CONV_COLS), lambda b, c, t: (b * nt + t, 2 * nc + c)),
            pl.BlockSpec((CONV_K, CONV_COLS), lambda b, c, t: (0, c)),
        ],
        out_specs=[
            pl.BlockSpec((CONV_ROWS, CONV_COLS), lambda b, c, t: (b * nt + t, c)),
            pl.BlockSpec((1, CONV_K - 1, CONV_COLS), lambda b, c, t: (b, 0, c)),
        ],
        out_shape=[
            jax.ShapeDtypeStruct((N_PROMPT, CONV_CH), BF16),
            jax.ShapeDtypeStruct((BATCH, CONV_K - 1, CONV_CH), F32),
        ],
        scratch_shapes=[pltpu.VMEM((SUBLANES, CONV_COLS), F32)],
        compiler_params=_params(("arbitrary", "arbitrary", "arbitrary"), 56),
        name="conv_prompt",
    )(proj, proj, proj, conv_w)


def _conv_sample_kernel(cb_ref, cc_ref, ch_ref, w_ref, e1_ref, e2_ref, y_ref, u_ref):
    u =cc_ref[...].astype(F32) * ch_ref[...].astype(F32)
    tpos = lax.broadcasted_iota(I32, u.shape, 0) % DEC_SEQ
    um1 = jnp.where(tpos == 0, e1_ref[...], pltpu.roll(u, 1, axis=0))
    um2 = jnp.where(tpos <= 1, e2_ref[...], pltpu.roll(u, 2, axis=0))
    y_ref[...] = _conv_taps(u, um1, um2, cb_ref[...].astype(F32), w_ref[...]).astype(y_ref.dtype)
    u_ref[...] = u


def _conv_sample(proj, conv_w, e1, e2):
    nc = CONV_CH // CONV_COLS
    rb = N_PROMPT // N_SAMPLE
    return pl.pallas_call(
        _conv_sample_kernel,
        grid=(nc,),
        in_specs=[
            pl.BlockSpec((N_SAMPLE, CONV_COLS), lambda c: (rb, c)),
            pl.BlockSpec((N_SAMPLE, CONV_COLS), lambda c: (rb, nc + c)),
            pl.BlockSpec((N_SAMPLE, CONV_COLS), lambda c: (rb, 2 * nc + c)),
            pl.BlockSpec((CONV_K, CONV_COLS), lambda c: (0, c)),
            pl.BlockSpec((N_SAMPLE, CONV_COLS), lambda c: (0, c)),
            pl.BlockSpec((N_SAMPLE, CONV_COLS), lambda c: (0, c)),
        ],
        out_specs=[
            pl.BlockSpec((N_SAMPLE, CONV_COLS), lambda c: (0, c)),
            pl.BlockSpec((N_SAMPLE, CONV_COLS), lambda c: (0, c)),
        ],
        out_shape=[
            jax.ShapeDtypeStruct((N_SAMPLE, CONV_CH), BF16),
            jax.ShapeDtypeStruct((N_SAMPLE, CONV_CH), F32),
        ],
        compiler_params=_params(("arbitrary",), 56),
        name="conv_sample",
    )(proj, proj, proj, conv_w, e1, e2)


_NT = (((1,), (1,)), ((), ()))
_TN = (((0,), (0,)), ((), ()))


def _hgrn_factors(q, fz, lb, seq_len):
    rows, width = q.shape
    one_m_lb = 1.0 - lb
    sig = _sigmoid(fz)
    log_f = jnp.log(lb + one_m_lb * sig)
    k = one_m_lb * (1.0 - sig)
    row = lax.broadcasted_iota(I32, (rows, width), 0)
    tpos = row % seq_len

    b = log_f
    shift = 1
    while shift < seq_len:
        b = b + jnp.where(tpos >= shift, pltpu.roll(b, shift, axis=0), 0.0)
        shift *= 2

    n_seq = rows // seq_len
    mid = (seq_len - 1) // 2
    if seq_len % SUBLANES == 0:
        b3 = b.reshape(n_seq, seq_len, width)
        b_mid = b3[:, mid:mid + 1]
        b_end = b3[:, seq_len - 1:seq_len]

        def per_row(per_chunk):
            return jnp.broadcast_to(per_chunk, b3.shape).reshape(rows, width)

        b_ref, b_last = per_row(b_mid), per_row(b_end)
        e_ref, e_last = per_row(jnp.exp(b_mid)), per_row(jnp.exp(b_end - b_mid))
    else:
        seq = row // seq_len
        b_ref = b[mid:mid + 1]
        b_last = b[seq_len - 1:seq_len]
        for j in range(1, n_seq):
            b_ref = jnp.where(seq == j, b[j * seq_len + mid:j * seq_len + mid + 1], b_ref)
            b_last = jnp.where(seq == j, b[(j + 1) * seq_len - 1:(j + 1) * seq_len], b_last)
        e_ref, e_last = jnp.exp(b_ref), jnp.exp(b_last - b_ref)

    qe = q * jnp.exp(b - b_ref)
    ke = k * jnp.exp(b_ref - b)
    return qe, ke, qe * e_ref, ke * e_last, log_f, b_last


def _split3_bf16(x):
    hi = x.astype(BF16).astype(F32)
    rest = x - hi
    mid = rest.astype(BF16).astype(F32)
    lo = (rest - mid).astype(BF16).astype(F32)
    return hi, mid, lo


def _head_out(o, gate, norm_g):
    o_n = o * lax.rsqrt(jnp.mean(o * o, axis=-1, keepdims=True) + EPS)
    return o_n * norm_g * (gate * _sigmoid(gate))


def _hgrn_prompt_kernel(q_ref, f_ref, v_ref, g_ref, lb_ref, ng_ref, y_ref, s_out_ref):
    lb = lb_ref[...]
    ng = ng_ref[...]
    rr = lax.broadcasted_iota(I32, (HG_CHUNK, HG_CHUNK), 0)
    cc = lax.broadcasted_iota(I32, (HG_CHUNK, HG_CHUNK), 1)
    causal = cc <= rr

    def block(i, state_t):
        rows = pl.ds(pl.multiple_of(i * HG_ROWS, HG_ROWS), HG_ROWS)
        qe, ke, qb, kd, _, b_last = _hgrn_factors(q_ref[rows, :].astype(F32), f_ref[rows, :].astype(F32),
                                                  lb, HG_CHUNK)
        qe, ke, qb, kd = (a.astype(BF16) for a in (qe, ke, qb, kd))
        vb = v_ref[rows, :]
        outs = []
        for c in range(HG_ROWS // HG_CHUNK):
            cs = slice(c * HG_CHUNK, (c + 1) * HG_CHUNK)
            scores = lax.dot_general(qe[cs], ke[cs], _NT, preferred_element_type=F32)
            scores = jnp.where(causal, scores, 0.0).astype(BF16)
            outs.append(jnp.dot(scores, vb[cs], preferred_element_type=F32)
                        + lax.dot_general(qb[cs], state_t.astype(BF16), _NT, preferred_element_type=F32))
            inc_t = lax.dot_general(vb[cs], kd[cs], _TN, preferred_element_type=F32)
            state_t = jnp.exp(b_last[c * HG_CHUNK:c * HG_CHUNK + 1]) * state_t + inc_t
        o = jnp.concatenate(outs, axis=0)
        y_ref[rows, :] = _head_out(o, g_ref[rows, :].astype(F32), ng).astype(y_ref.dtype)
        return state_t

    state_t = lax.fori_loop(0, SEQ // HG_ROWS, block, jnp.zeros((HG_DV, HG_DK), F32))
    s_out_ref[0, 0] = state_t.T


def _hgrn_prompt(proj, lb, norm_g):
    cq = 3 * CONV_CH // HG_DK

    def spec(off):
        return pl.BlockSpec((SEQ, HG_DK), lambda b, h: (b, cq + off * HG_HEADS + h))

    return pl.pallas_call(
        _hgrn_prompt_kernel,
        grid=(BATCH, HG_HEADS),
        in_specs=[spec(0), spec(1), spec(2), spec(3),
                  pl.BlockSpec((1, HG_DK), lambda b, h: (0, h)),
                  pl.BlockSpec((1, HG_DV), lambda b, h: (0, h))],
        out_specs=[
            pl.BlockSpec((SEQ, HG_DV), lambda b, h: (b, h)),
            pl.BlockSpec((1, 1, HG_DK, HG_DV), lambda b, h: (b, h, 0, 0)),
        ],
        out_shape=[
            jax.ShapeDtypeStruct((N_PROMPT, HG_W), BF16),
            jax.ShapeDtypeStruct((BATCH, HG_HEADS, HG_DK, HG_DV), F32),
        ],
        compiler_params=_params(("arbitrary", "arbitrary"), 32),
        name="hgrn_prompt",
    )(proj, proj, proj, proj, lb, norm_g)


def _hgrn_sample_kernel(q_ref, f_ref, v_ref, g_ref, lb_ref, ng_ref, s_ref, y_ref, s_out_ref):
    rows = HG_SAMPLE_SEQS * DEC_SEQ
    qe, ke, qb, kd, log_f, _ = _hgrn_factors(q_ref[...].astype(F32), f_ref[...].astype(F32),
                                            lb_ref[...], DEC_SEQ)
    lf_parts = _split3_bf16(log_f)
    v = v_ref[...].astype(F32)
    gate = g_ref[...].astype(F32)
    ng = ng_ref[...]

    in_a = lax.broadcasted_iota(I32, (SUBLANES, HG_DV), 0) < DEC_SEQ
    rr = lax.broadcasted_iota(I32, (SUBLANES, SUBLANES), 0)
    cc = lax.broadcasted_iota(I32, (SUBLANES, SUBLANES), 1)
    causal = (cc <= rr) & ((rr // DEC_SEQ) == (cc // DEC_SEQ))
    zero_tile = jnp.zeros((SUBLANES, HG_DV), F32)
    sel = jnp.concatenate([in_a.astype(F32), 1.0 - in_a.astype(F32)], axis=1)
    dec_rhs = jnp.concatenate([sel, sel, sel, jnp.zeros_like(sel)], axis=0).astype(BF16)

    for h in range(HG_HEADS):
        cols = slice(h * HG_DK, (h + 1) * HG_DK)
        outs = []
        for r in range(rows // SUBLANES):
            rs = slice(r * SUBLANES, (r + 1) * SUBLANES)
            s_cat = jnp.concatenate([s_ref[2 * r, h], s_ref[2 * r + 1, h]], axis=1)
            v_t = v[rs, cols]
            scores = lax.dot_general(qe[rs, cols].astype(BF16), ke[rs, cols].astype(BF16), _NT,
                                     preferred_element_type=F32)
            scores = jnp.where(causal, scores, 0.0).astype(BF16)
            o_state = jnp.dot(qb[rs, cols].astype(BF16), s_cat.astype(BF16), preferred_element_type=F32)
            outs.append(jnp.dot(scores, v_t.astype(BF16), preferred_element_type=F32)
                        + jnp.where(in_a, o_state[:, :HG_DV], o_state[:, HG_DV:]))
            inc_rhs = jnp.concatenate([jnp.where(in_a, v_t, 0.0), jnp.where(in_a, 0.0, v_t)], axis=1)
            inc = lax.dot_general(kd[rs, cols].astype(BF16), inc_rhs.astype(BF16), _TN,
                                  preferred_element_type=F32)
            dec_lhs = jnp.concatenate([p[rs, cols] for p in lf_parts] + [zero_tile], axis=0)
            decay = jnp.exp(lax.dot_general(dec_lhs.astype(BF16), dec_rhs, _TN,
                                            preferred_element_type=F32))
            s_new = decay * s_cat + inc
            s_out_ref[2 * r, h] = s_new[:, :HG_DV]
            s_out_ref[2 * r + 1, h] = s_new[:, HG_DV:]
        o = jnp.concatenate(outs, axis=0)
        y_ref[:, cols] = _head_out(o, gate[:, cols], ng[:, cols]).astype(y_ref.dtype)


def _hgrn_sample(proj, lb, norm_g, state):
    rows = HG_SAMPLE_SEQS * DEC_SEQ
    rb = N_PROMPT // rows
    cq = 3 * CONV_CH // HG_W

    def spec(off):
        return pl.BlockSpec((rows, HG_W), lambda g: (rb + g, cq + off))

    st_spec = pl.BlockSpec((HG_SAMPLE_SEQS, HG_HEADS, HG_DK, HG_DV), lambda g: (g, 0, 0, 0))
    return pl.pallas_call(
        _hgrn_sample_kernel,
        grid=(DEC_BATCH // HG_SAMPLE_SEQS,),
        in_specs=[spec(0), spec(1), spec(2), spec(3),
                  pl.BlockSpec((1, HG_W), lambda g: (0, 0)),
                  pl.BlockSpec((1, HG_W), lambda g: (0, 0)),
                  st_spec],
        out_specs=[pl.BlockSpec((rows, HG_W), lambda g: (g, 0)), st_spec],
        out_shape=[
            jax.ShapeDtypeStruct((N_SAMPLE, HG_W), BF16),
            jax.ShapeDtypeStruct((DEC_BATCH, HG_HEADS, HG_DK, HG_DV), F32),
        ],
        compiler_params=_params(("arbitrary",), 40),
        name="hgrn_sample",
    )(proj, proj, proj, proj, lb, norm_g, state)


def _out_proj_kernel(ycp_ref, yhp_ref, xp_ref, ycs_ref, yhs_ref, xs_ref, wc_ref, wh_ref, o_ref, *,
                     n_prompt_tiles):
    m = pl.program_id(1)

    def residual_mix(yc_ref, yh_ref, x_ref):
        mix = (jnp.dot(yc_ref[...], wc_ref[...], preferred_element_type=F32)
               + jnp.dot(yh_ref[...], wh_ref[...], preferred_element_type=F32))
        o_ref[...] = x_ref[...] + mix

    @pl.when(m < n_prompt_tiles)
    def _():
        residual_mix(ycp_ref, yhp_ref, xp_ref)

    @pl.when(m >= n_prompt_tiles)
    def _():
        residual_mix(ycs_ref, yhs_ref, xs_ref)


def _out_proj(yc_p, yh_p, xp, yc_s, yh_s, xs, w):
    npt = N_PROMPT // MM_ROWS

    def prompt_map(n, m):
        return jnp.minimum(m, npt - 1)

    def sample_map(n, m):
        return jnp.maximum(m - npt, 0)

    def source_specs(row_map):
        return [
            pl.BlockSpec((MM_ROWS, CONV_CH), lambda n, m: (row_map(n, m), 0)),
            pl.BlockSpec((MM_ROWS, HG_W), lambda n, m: (row_map(n, m), 0)),
            pl.BlockSpec((MM_ROWS, MM_COLS), lambda n, m: (row_map(n, m), n)),
        ]

    return pl.pallas_call(
        functools.partial(_out_proj_kernel, n_prompt_tiles=npt),
        grid=(D_MODEL // MM_COLS, N_TOK // MM_ROWS),
        in_specs=source_specs(prompt_map) + source_specs(sample_map) + [
            pl.BlockSpec((CONV_CH, MM_COLS), lambda n, m: (0, n)),
            pl.BlockSpec((HG_W, MM_COLS), lambda n, m: (1, n)),
        ],
        out_specs=pl.BlockSpec((MM_ROWS, MM_COLS), lambda n, m: (m, n)),
        out_shape=jax.ShapeDtypeStruct((N_TOK, D_MODEL), F32),
        compiler_params=_params(("arbitrary", "arbitrary"), 56),
        name="out_proj",
    )(yc_p, yh_p, xp, yc_s, yh_s, xs, w, w)


def _split_bf16(x):
    hi = x.astype(BF16)
    return hi, (x - hi.astype(F32)).astype(BF16)


_HI16 = 0xFFFF0000


def _pack_bf16_halves(x):
    half = x.shape[1] // 2
    bits = lax.bitcast_convert_type(x.astype(F32), jnp.uint32)
    return (bits[:, :half] >> 16) | (bits[:, half:] & jnp.uint32(_HI16))


def _unpack_bf16_half(packed, high):
    bits = (packed & jnp.uint32(_HI16)) if high else (packed << 16)
    return lax.bitcast_convert_type(bits, F32)


def _norm_route_kernel(h_ref, g_ref, wr_hi_ref, wr_lo_ref, br_ref, hn_ref, eidx_ref, gate_ref):
    hn = _rmsnorm_rows(h_ref[...], g_ref[...])
    hn_hi, hn_lo = _split_bf16(hn)
    hn_ref[...] = _pack_bf16_halves(hn_hi)
    logits = (jnp.dot(hn_hi, wr_hi_ref[...], preferred_element_type=F32)
              + (jnp.dot(hn_hi, wr_lo_ref[...], preferred_element_type=F32)
                 + jnp.dot(hn_lo, wr_hi_ref[...], preferred_element_type=F32))) + br_ref[...]
    lane = lax.broadcasted_iota(I32, logits.shape, 1)
    lane_f = lane.astype(F32)
    neg = -jnp.inf

    def first_argmax(vals, vmax):
        first = jnp.min(jnp.where(vals == vmax, lane_f, float(ROUTE_COLS)), axis=-1, keepdims=True)
        return first.astype(I32)

    is_grp = lane < N_GROUPS
    lg = jnp.where(is_grp, logits, neg)
    mg = jnp.max(lg, axis=-1, keepdims=True)
    g_sel = first_argmax(lg, mg)
    p_grp = 1.0 / jnp.sum(jnp.where(is_grp, jnp.exp(logits - mg), 0.0), axis=-1, keepdims=True)

    in_grp = (lane >= N_GROUPS) & (((lane - N_GROUPS) // EXP_PER_GROUP) == g_sel)
    le = jnp.where(in_grp, logits, neg)
    m1 = jnp.max(le, axis=-1, keepdims=True)
    i1 = first_argmax(le, m1)
    le2 = jnp.where(lane == i1, neg, le)
    m2 = jnp.max(le2, axis=-1, keepdims=True)
    i2 = first_argmax(le2, m2)
    e2 = jnp.exp(m2 - m1)
    gate1 = p_grp / (1.0 + e2)
    gate2 = p_grp * e2 / (1.0 + e2)
    eidx_ref[...] = jnp.where(lane == 0, i1 - N_GROUPS, jnp.where(lane == 1, i2 - N_GROUPS, 0))
    gate_ref[...] = jnp.where(lane == 0, gate1, jnp.where(lane == 1, gate2, 0.0))


def _norm_route(h, g, w_r, b_r):
    wr_hi, wr_lo = _split_bf16(w_r)
    return pl.pallas_call(
        _norm_route_kernel,
        grid=(N_TOK // NORM_ROWS,),
        in_specs=[
            pl.BlockSpec((NORM_ROWS, D_MODEL), lambda i: (i, 0)),
            pl.BlockSpec((1, D_MODEL), lambda i: (0, 0)),
            pl.BlockSpec((D_MODEL, ROUTE_COLS), lambda i: (0, 0)),
            pl.BlockSpec((D_MODEL, ROUTE_COLS), lambda i: (0, 0)),
            pl.BlockSpec((1, ROUTE_COLS), lambda i: (0, 0)),
        ],
        out_specs=[
            pl.BlockSpec((NORM_ROWS, D_MODEL // 2), lambda i: (i, 0)),
            pl.BlockSpec((NORM_ROWS, ROUTE_COLS), lambda i: (i, 0)),
            pl.BlockSpec((NORM_ROWS, ROUTE_COLS), lambda i: (i, 0)),
        ],
        out_shape=[
            jax.ShapeDtypeStruct((N_TOK, D_MODEL // 2), jnp.uint32),
            jax.ShapeDtypeStruct((N_TOK, ROUTE_COLS), I32),
            jax.ShapeDtypeStruct((N_TOK, ROUTE_COLS), F32),
        ],
        compiler_params=_params(("arbitrary",), 48),
        name="norm_route",
    )(h, g, wr_hi, wr_lo, b_r)


def _row_copy(src_hbm, row, buf, slot, r, sem):
    return pltpu.make_async_copy(src_hbm.at[pl.ds(row, 1)], buf.at[slot, pl.ds(r, 1)], sem.at[slot])


def _gather_issue(idx_ref, base, row0, n_rows, src_hbm, buf, slot, sem):
    for r in range(row0, row0 + n_rows):
        _row_copy(src_hbm, idx_ref[base + r], buf, slot, r, sem).start()


def _gather_wait(row0, n_rows, src_hbm, buf, slot, sem):
    for r in range(row0, row0 + n_rows, SUBLANES):
        rows = pl.ds(r, SUBLANES)
        pltpu.make_async_copy(src_hbm.at[pl.ds(0, SUBLANES)], buf.at[slot, rows], sem.at[slot]).wait()


def _route_plan_kernel(eidx_ref, dest_ref, counts_ref, start_ref, rank_scr):
    n_tiles = N_TOK // PLAN_ROWS
    lane = lax.broadcasted_iota(I32, (PLAN_ROWS, ROUTE_COLS), 1)
    ri = lax.broadcasted_iota(I32, (PLAN_ROWS, PLAN_ROWS), 0)
    ci = lax.broadcasted_iota(I32, (PLAN_ROWS, PLAN_ROWS), 1)
    earlier = jnp.where(ci < ri, 1.0, 0.0).astype(BF16)

    def tile_rows(t):
        return pl.ds(pl.multiple_of(t * PLAN_ROWS, PLAN_ROWS), PLAN_ROWS)

    def pick(vals, col):
        return jnp.sum(jnp.where(lane == col, vals, 0.0), axis=-1, keepdims=True)

    def on_lanes01(v0, v1):
        return jnp.where(lane == 0, v0, jnp.where(lane == 1, v1, 0.0))

    def rank_tile(t, seen):
        e = eidx_ref[tile_rows(t), :]
        e0, e1 = e[:, 0:1], e[:, 1:2]
        uses = jnp.where((lane == e0) | (lane == e1), 1.0, 0.0)
        before = jnp.dot(earlier, uses.astype(BF16), preferred_element_type=F32) + seen
        rank_scr[tile_rows(t), :] = on_lanes01(pick(before, e0), pick(before, e1))
        return seen + jnp.sum(uses, axis=0, keepdims=True)

    counts = lax.fori_loop(0, n_tiles, rank_tile, jnp.zeros((1, ROUTE_COLS), F32))
    padded = jnp.ceil(counts * (1.0 / MOE_BLK)) * MOE_BLK
    ui = lax.broadcasted_iota(I32, (ROUTE_COLS, ROUTE_COLS), 0)
    uj = lax.broadcasted_iota(I32, (ROUTE_COLS, ROUTE_COLS), 1)
    start = jnp.dot(jnp.broadcast_to(padded, (SUBLANES, ROUTE_COLS)), jnp.where(ui < uj, 1.0, 0.0),
                    precision=lax.Precision.HIGHEST, preferred_element_type=F32)[0:1]
    counts_ref[...] = counts.astype(I32)
    start_ref[...] = start.astype(I32)

    def dest_tile(t, carry):
        e = eidx_ref[tile_rows(t), :]
        rank = rank_scr[tile_rows(t), :]
        d0 = pick(start, e[:, 0:1]) + rank[:, 0:1]
        d1 = pick(start, e[:, 1:2]) + rank[:, 1:2]
        dest_ref[tile_rows(t), :] = on_lanes01(d0, d1).astype(I32)
        return carry

    lax.fori_loop(0, n_tiles, dest_tile, 0)


def _route_plan(eidx):
    return pl.pallas_call(
        _route_plan_kernel,
        out_shape=[
            jax.ShapeDtypeStruct((N_TOK, ROUTE_COLS), I32),
            jax.ShapeDtypeStruct((1, ROUTE_COLS), I32),
            jax.ShapeDtypeStruct((1, ROUTE_COLS), I32),
        ],
        scratch_shapes=[pltpu.VMEM((N_TOK, ROUTE_COLS), F32)],
        compiler_params=pltpu.CompilerParams(vmem_limit_bytes=40 * MIB),
        name="route_plan",
    )(eidx)


def _dispatch_kernel(dest_ref, counts_ref, start_ref, hn_hbm, o_ref, row_tok, n_parts, buf, sem):
    b = pl.program_id(0)
    nb = pl.num_programs(0)
    slot = b % 2
    parts_per_blk = MOE_BLK // DISPATCH_PART

    def fetch(blk, dst_slot):
        for p in range(parts_per_blk):
            @pl.when(p < n_parts[blk])
            def _(p=p):
                _gather_issue(row_tok, blk * MOE_BLK, p * DISPATCH_PART, DISPATCH_PART, hn_hbm, buf,
                              dst_slot, sem)

            @pl.when(p >= n_parts[blk])
            def _(p=p):
                buf[dst_slot, p * DISPATCH_PART:(p + 1) * DISPATCH_PART] = jnp.zeros(
                    (DISPATCH_PART,) + buf.shape[2:], buf.dtype)

    @pl.when(b == 0)
    def _():
        def clear(blk, carry):
            n_parts[blk] = 0
            return carry

        lax.fori_loop(0, MOE_NB, clear, 0)

        def per_expert(e, carry):
            count, first = counts_ref[e], start_ref[e]

            def per_block(i, c):
                rows = jnp.minimum(count - i * MOE_BLK, MOE_BLK)
                parts = (rows + DISPATCH_PART - 1) // DISPATCH_PART
                n_parts[first // MOE_BLK + i] = parts

                def pad_row(r, c2):
                    row_tok[r] = r % N_PROMPT
                    return c2

                row0 = first + i * MOE_BLK
                lax.fori_loop(row0 + rows, row0 + parts * DISPATCH_PART, pad_row, 0)
                return c

            lax.fori_loop(0, (count + MOE_BLK - 1) // MOE_BLK, per_block, 0)
            return carry

        lax.fori_loop(0, N_EXPERTS, per_expert, 0)

        def invert(g, carry):
            for j in range(SUBLANES):
                a = g * SUBLANES + j
                row_tok[dest_ref[a]] = g * (SUBLANES // TOP_K) + j // TOP_K
            return carry

        lax.fori_loop(0, N_ASSIGN // SUBLANES, invert, 0)
        fetch(0, 0)

    @pl.when(b + 1 < nb)
    def _():
        fetch(b + 1, 1 - slot)

    for p in range(parts_per_blk):
        @pl.when(p < n_parts[b])
        def _(p=p):
            _gather_wait(p * DISPATCH_PART, DISPATCH_PART, hn_hbm, buf, slot, sem)

    o_ref[...] = buf[slot]


def _dispatch(dest_flat, counts, start, hn_packed):
    return pl.pallas_call(
        _dispatch_kernel,
        grid_spec=pltpu.PrefetchScalarGridSpec(
            num_scalar_prefetch=3,
            grid=(MOE_NB,),
            in_specs=[pl.BlockSpec(memory_space=pl.ANY)],
            out_specs=pl.BlockSpec((MOE_BLK, D_MODEL // 2), lambda b, *_: (b, 0)),
            scratch_shapes=[pltpu.SMEM((MOE_ROWS,), I32),
                            pltpu.SMEM((MOE_NB,), I32),
                            pltpu.VMEM((2, MOE_BLK, D_MODEL // 2), jnp.uint32),
                            pltpu.SemaphoreType.DMA((2,))],
        ),
        out_shape=jax.ShapeDtypeStruct((MOE_ROWS, D_MODEL // 2), jnp.uint32),
        compiler_params=_params(("arbitrary",), 16, disable_bounds_checks=True),
        name="moe_dispatch",
    )(dest_flat, counts, start, hn_packed)


def _stream_expert_chunks(blk0_ref, n_ref, n_used, in_copy, out_copy, out_buf, compute):
    c = pl.program_id(1)
    n_chunks = pl.num_programs(1)

    def for_blocks(chunk, fn):
        blk0 = blk0_ref[chunk]

        def body(i, carry):
            fn(blk0 + i, i)
            return carry

        lax.fori_loop(0, n_ref[chunk], body, 0)

    def start_loads(chunk):
        for_blocks(chunk, lambda g, i: in_copy(g, chunk % MOE_IN_SLOTS, i).start())

    @pl.when(c == 0)
    def _():
        start_loads(0)
        start_loads(1)

    @pl.when(c + 2 < n_chunks)
    def _():
        start_loads(c + 2)

    in_slot = c % MOE_IN_SLOTS
    out_slot = c % MOE_OUT_SLOTS
    for_blocks(c, lambda g, i: in_copy(g, in_slot, i).wait())

    @pl.when(c >= MOE_OUT_SLOTS)
    def _():
        for_blocks(c - MOE_OUT_SLOTS, lambda g, i: out_copy(g, out_slot, i).wait())

    for m in range(1, MOE_CHUNK_BLKS + 1):
        @pl.when(n_ref[c] == m)
        def _(m=m):
            compute(m, in_slot, out_slot)

    for_blocks(c, lambda g, i: out_copy(g, out_slot, i).start())

    @pl.when(c == n_chunks - 1)
    def _():
        for chunk in (c - 1, c):
            for_blocks(chunk, lambda g, i: out_copy(g, chunk % MOE_OUT_SLOTS, i).wait())
        out_buf[0, 0:MOE_BLK] = jnp.zeros((MOE_BLK,) + out_buf.shape[2:], out_buf.dtype)

        def start_zero(g, carry):
            out_copy(g, 0, 0).start()
            return carry

        def wait_zero(g, carry):
            out_copy(g, 0, 0).wait()
            return carry

        lax.fori_loop(n_used, MOE_NB, start_zero, 0)
        lax.fori_loop(n_used, MOE_NB, wait_zero, 0)


def _block_rows(g):
    return pl.ds(pl.multiple_of(g * MOE_BLK, MOE_BLK), MOE_BLK)


def _slot_rows(i):
    return pl.ds(pl.multiple_of(i * MOE_BLK, MOE_BLK), MOE_BLK)


def _expert_up_kernel(blk0_ref, n_ref, e_ref, nbu_ref, wg_ref, wu_ref, x_hbm, h_hbm, xbuf, hbuf, sem_x, sem_h):
    del e_ref
    cols = pl.ds(pl.multiple_of(pl.program_id(0) * MOE_FC, MOE_FC), MOE_FC)

    def x_copy(g, slot, i):
        return pltpu.make_async_copy(x_hbm.at[_block_rows(g)], xbuf.at[slot, _slot_rows(i)], sem_x.at[slot])

    def h_copy(g, slot, i):
        return pltpu.make_async_copy(hbuf.at[slot, _slot_rows(i)], h_hbm.at[_block_rows(g), cols],
                                     sem_h.at[slot])

    def compute(m, in_slot, out_slot):
        rows = m * MOE_BLK
        kc = D_MODEL // MOE_K_CHUNKS
        half_chunks = MOE_K_CHUNKS // 2
        g = u = None
        for i in range(MOE_K_CHUNKS):
            ks = slice(i * kc, (i + 1) * kc)
            ps = slice((i % half_chunks) * kc, (i % half_chunks + 1) * kc)
            xk = _unpack_bf16_half(xbuf[in_slot, :rows, ps], high=i >= half_chunks)
            gi = jnp.dot(xk, wg_ref[0, ks, :], preferred_element_type=F32)
            ui = jnp.dot(xk, wu_ref[0, ks, :], preferred_element_type=F32)
            g, u = (gi, ui) if g is None else (g + gi, u + ui)
        hbuf[out_slot, :rows] = (g * _sigmoid(g) * u).astype(hbuf.dtype)

    _stream_expert_chunks(blk0_ref, n_ref, nbu_ref[0], x_copy, h_copy, hbuf, compute)


def _chunk_scratch(in_cols, in_dtype, out_cols, out_dtype):
    rows = MOE_CHUNK_BLKS * MOE_BLK
    return [pltpu.VMEM((MOE_IN_SLOTS, rows, in_cols), in_dtype),
            pltpu.VMEM((MOE_OUT_SLOTS, rows, out_cols), out_dtype),
            pltpu.SemaphoreType.DMA((MOE_IN_SLOTS,)),
            pltpu.SemaphoreType.DMA((MOE_OUT_SLOTS,))]


def _expert_up(chunks, n_used, xb, w_g, w_u):
    def w_map(j, c, blk0, n, e, nbu):
        return (e[c], 0, j)

    return pl.pallas_call(
        _expert_up_kernel,
        grid_spec=pltpu.PrefetchScalarGridSpec(
            num_scalar_prefetch=4,
            grid=(D_EXPERT // MOE_FC, MOE_NCH),
            in_specs=[pl.BlockSpec((1, D_MODEL, MOE_FC), w_map),
                      pl.BlockSpec((1, D_MODEL, MOE_FC), w_map),
                      pl.BlockSpec(memory_space=pl.ANY)],
            out_specs=pl.BlockSpec(memory_space=pl.ANY),
            scratch_shapes=_chunk_scratch(D_MODEL // 2, jnp.uint32, MOE_FC, BF16),
        ),
        out_shape=jax.ShapeDtypeStruct((MOE_ROWS, D_EXPERT), BF16),
        compiler_params=_params(("arbitrary", "arbitrary"), 58),
        name="expert_up",
    )(*chunks, n_used, w_g, w_u, xb)


def _expert_down_kernel(blk0_ref, n_ref, e_ref, nbu_ref, wd_ref, h_hbm, y_hbm, hbuf, ybuf, sem_h, sem_y):
    del e_ref
    wc = MOE_DC // 2
    cols = pl.ds(pl.multiple_of(pl.program_id(0) * wc, wc), wc)

    def h_copy(g, slot, i):
        return pltpu.make_async_copy(h_hbm.at[_block_rows(g)], hbuf.at[slot, _slot_rows(i)], sem_h.at[slot])

    def y_copy(g, slot, i):
        return pltpu.make_async_copy(ybuf.at[slot, _slot_rows(i)], y_hbm.at[_block_rows(g), cols],
                                     sem_y.at[slot])

    def compute(m, in_slot, out_slot):
        rows = m * MOE_BLK
        y = jnp.dot(hbuf[in_slot, :rows].astype(F32), wd_ref[0], preferred_element_type=F32)
        ybuf[out_slot, :rows] = _pack_bf16_halves(y.astype(BF16))

    _stream_expert_chunks(blk0_ref, n_ref, nbu_ref[0], h_copy, y_copy, ybuf, compute)


def _expert_down(chunks, n_used, hb, w_d):
    return pl.pallas_call(
        _expert_down_kernel,
        grid_spec=pltpu.PrefetchScalarGridSpec(
            num_scalar_prefetch=4,
            grid=(D_MODEL // MOE_DC, MOE_NCH),
            in_specs=[pl.BlockSpec((1, D_EXPERT, MOE_DC), lambda n, c, blk0, nb, e, nbu: (e[c], 0, n)),
                      pl.BlockSpec(memory_space=pl.ANY)],
            out_specs=pl.BlockSpec(memory_space=pl.ANY),
            scratch_shapes=_chunk_scratch(D_EXPERT, BF16, MOE_DC // 2, jnp.uint32),
        ),
        out_shape=jax.ShapeDtypeStruct((MOE_ROWS, D_MODEL // 2), jnp.uint32),
        compiler_params=_params(("arbitrary", "arbitrary"), 48),
        name="expert_down",
    )(*chunks, n_used, w_d, hb)


def _combine_kernel(dest_ref, h_ref, gate_ref, g_ref, y_hbm, o_ref, buf, sem, *, tile0):
    i = pl.program_id(0)
    n = pl.num_programs(0)
    slot = i % 2
    rows = TOP_K * COMB_ROWS
    base = (tile0 + i) * rows

    @pl.when(i == 0)
    def _():
        _gather_issue(dest_ref, base, 0, rows, y_hbm, buf, 0, sem)

    @pl.when(i + 1 < n)
    def _():
        _gather_issue(dest_ref, base + rows, 0, rows, y_hbm, buf, 1 - slot, sem)

    _gather_wait(0, rows, y_hbm, buf, slot, sem)

    def expert_rows(k):
        words = buf[slot, k * COMB_ROWS:(k + 1) * COMB_ROWS]
        wc = MOE_DC // 2
        parts = []
        for p in range(D_MODEL // MOE_DC):
            w = words[:, p * wc:(p + 1) * wc]
            parts += [_unpack_bf16_half(w, high=False), _unpack_bf16_half(w, high=True)]
        return jnp.concatenate(parts, axis=1)

    gates = gate_ref[...]
    ff = gates[:, 0:1] * expert_rows(0) + gates[:, 1:2] * expert_rows(1)
    o_ref[...] = _rmsnorm_rows(h_ref[...] + ff, g_ref[...])


def _combine(dest, h, gates, g_final, y_buf, tile0, n_tiles):
    return pl.pallas_call(
        functools.partial(_combine_kernel, tile0=tile0),
        grid_spec=pltpu.PrefetchScalarGridSpec(
            num_scalar_prefetch=1,
            grid=(n_tiles,),
            in_specs=[
                pl.BlockSpec((COMB_ROWS, D_MODEL), lambda i, d: (tile0 + i, 0)),
                pl.BlockSpec((COMB_ROWS, ROUTE_COLS), lambda i, d: (tile0 + i, 0)),
                pl.BlockSpec((1, D_MODEL), lambda i, d: (0, 0)),
                pl.BlockSpec(memory_space=pl.ANY),
            ],
            out_specs=pl.BlockSpec((COMB_ROWS, D_MODEL), lambda i, d: (i, 0)),
            scratch_shapes=[pltpu.VMEM((2, TOP_K * COMB_ROWS, D_MODEL // 2), jnp.uint32),
                            pltpu.SemaphoreType.DMA((2,))],
        ),
        out_shape=jax.ShapeDtypeStruct((n_tiles * COMB_ROWS, D_MODEL), F32),
        compiler_params=_params(("arbitrary",), 40, disable_bounds_checks=True),
        name="moe_combine",
    )(dest, h, gates, g_final, y_buf)


def _chunk_list(counts, start):
    blk0 = start // MOE_BLK
    nblk = (counts + MOE_BLK - 1) // MOE_BLK
    n_used = (blk0[-1] + nblk[-1]).astype(I32)
    nch = (nblk + MOE_CHUNK_BLKS - 1) // MOE_CHUNK_BLKS
    ch_end = jnp.cumsum(nch)
    ch = jnp.arange(MOE_NCH, dtype=I32)
    ch_e = jnp.minimum(jnp.searchsorted(ch_end, ch, side='right'), N_EXPERTS - 1)
    local = ch - (ch_end - nch)[ch_e]
    live = ch < ch_end[-1]
    ch_blk0 = (blk0[ch_e] + MOE_CHUNK_BLKS * local).astype(I32)
    ch_n = jnp.where(live, jnp.clip(nblk[ch_e] - MOE_CHUNK_BLKS * local, 0, MOE_CHUNK_BLKS), 0).astype(I32)
    ch_e = jnp.where(live, ch_e, ch_e[ch_end[-1] - 1]).astype(I32)
    return (ch_blk0, ch_n, ch_e), n_used.reshape(1)


def kernel(x_prompt, x_sample, cache_conv, state_hgrn, norm_mix_g, w_in, conv_w, lb_param, hg_norm_g,
           w_out, norm_ffn_g, w_group_router, b_group_router, w_expert_router, b_expert_router,
           w_exp_gate, w_exp_up, w_exp_down, norm_final_g):
    xp = x_prompt.reshape(N_PROMPT, D_MODEL)
    xs = x_sample.reshape(N_SAMPLE, D_MODEL)
    lb = jnp.cumsum(jax.nn.softmax(lb_param.astype(F32), axis=0), axis=0)[0].reshape(1, HG_W)

    xn = _norm_in(xp, xs, norm_mix_g[0].reshape(1, D_MODEL))
    proj = _in_proj(xn, w_in[0])

    buf = cache_conv[0]
    zeros = jnp.zeros((DEC_BATCH, DEC_SEQ - 1, CONV_CH), F32)
    e1 = jnp.concatenate([buf[:, 1:2], zeros], axis=1).reshape(N_SAMPLE, CONV_CH)
    e2 = jnp.concatenate([buf, zeros[:, :DEC_SEQ - 2]], axis=1).reshape(N_SAMPLE, CONV_CH)
    yc_p, conv_p = _conv_prompt(proj, conv_w[0])
    yc_s, u_s = _conv_sample(proj, conv_w[0], e1, e2)
    conv_s = u_s.reshape(DEC_BATCH, DEC_SEQ, CONV_CH)[:, DEC_SEQ - (CONV_K - 1):]

    ng = hg_norm_g[0].reshape(1, HG_W)
    yh_p, hgrn_p = _hgrn_prompt(proj, lb, ng)
    yh_s, hgrn_s = _hgrn_sample(proj, lb, ng, state_hgrn[0])

    h = _out_proj(yc_p, yh_p, xp, yc_s, yh_s, xs, w_out[0].astype(BF16))

    pad = jnp.zeros((D_MODEL, ROUTE_COLS - N_GROUPS - N_EXPERTS), F32)
    w_r = jnp.concatenate([w_group_router[0], w_expert_router[0], pad], axis=1)
    b_r = jnp.concatenate([b_group_router[0], b_expert_router[0], pad[0]]).reshape(1, ROUTE_COLS)
    hn_packed, eidx, gates = _norm_route(h, norm_ffn_g[0].reshape(1, D_MODEL), w_r, b_r)
    dest, counts, start = _route_plan(eidx)
    dest = dest[:, :TOP_K]
    counts, start = counts[0, :N_EXPERTS], start[0, :N_EXPERTS]
    chunks, n_used = _chunk_list(counts, start)
    xb = _dispatch(dest.reshape(N_ASSIGN), counts, start, hn_packed)
    hb = _expert_up(chunks, n_used, xb, w_exp_gate[0], w_exp_up[0])
    yb = _expert_down(chunks, n_used, hb, w_exp_down[0])

    dest_t = dest.reshape(N_TOK // COMB_ROWS, COMB_ROWS, TOP_K).transpose(0, 2, 1).reshape(N_ASSIGN)
    gf = norm_final_g.reshape(1, D_MODEL)
    y_p = _combine(dest_t, h, gates, gf, yb, 0, N_PROMPT // COMB_ROWS)
    y_s = _combine(dest_t, h, gates, gf, yb, N_PROMPT // COMB_ROWS, N_SAMPLE // COMB_ROWS)

    return (y_p.reshape(BATCH, SEQ, D_MODEL), y_s.reshape(DEC_BATCH, DEC_SEQ, D_MODEL),
            conv_p[None], hgrn_p[None], conv_s[None], hgrn_s[None])
```

```python
import functools

import jax
import jax.numpy as jnp
from jax import lax
from jax.experimental import pallas as pl
from jax.experimental.pallas import tpu as pltpu

F32 = jnp.float32
BF16 = jnp.bfloat16
I32 = jnp.int32

D_MODEL = 4096
BATCH = 4
SEQ = 2048
DEC_BATCH = 128
DEC_SEQ = 4
CONV_CH = 2048
CONV_K = 3
HG_HEADS = 16
HG_DK = 128
HG_DV = 128
HG_W = HG_HEADS * HG_DK
HG_CHUNK = 64
IN_COLS = 3 * CONV_CH + 4 * HG_W
N_GROUPS = 8
EXP_PER_GROUP = 8
N_EXPERTS = 64
TOP_K = 2
D_EXPERT = 1024
EPS = 1e-6

N_PROMPT = BATCH * SEQ
N_SAMPLE = DEC_BATCH * DEC_SEQ
N_TOK = N_PROMPT + N_SAMPLE
N_ASSIGN = N_TOK * TOP_K

LANES = 128
SUBLANES = 8
ROUTE_COLS = LANES

NORM_ROWS = 256
MM_ROWS = 512
MM_COLS = 1024
CONV_ROWS = 512
CONV_COLS = 2048
HG_ROWS = 512
HG_SAMPLE_SEQS = 4
MOE_BLK = 128
MOE_NB = N_ASSIGN // MOE_BLK + N_EXPERTS
MOE_ROWS = MOE_NB * MOE_BLK
MOE_FC = 512
MOE_DC = 2048
MOE_CHUNK_BLKS = 4
MOE_NCH = N_EXPERTS + N_ASSIGN // (MOE_BLK * MOE_CHUNK_BLKS)
MOE_IN_SLOTS = 3
MOE_OUT_SLOTS = 2
MOE_K_CHUNKS = 4
PLAN_ROWS = 512
DISPATCH_PART = 32
DISPATCH_BLKS = 2
COMB_ROWS = 128

MIB = 1 << 20


def _params(sem, vmem_mib, **kwargs):
    return pltpu.CompilerParams(dimension_semantics=sem, vmem_limit_bytes=vmem_mib * MIB, **kwargs)


def _sigmoid(x):
    return 1.0 / (1.0 + jnp.exp(-x))


def _rmsnorm_rows(x, g):
    return x * lax.rsqrt(jnp.mean(x * x, axis=-1, keepdims=True) + EPS) * g


def _norm_in_kernel(xp_ref, xs_ref, g_ref, o_ref, *, n_prompt_tiles):
    i = pl.program_id(0)

    @pl.when(i < n_prompt_tiles)
    def _():
        o_ref[...] = _rmsnorm_rows(xp_ref[...], g_ref[...]).astype(o_ref.dtype)

    @pl.when(i >= n_prompt_tiles)
    def _():
        o_ref[...] = _rmsnorm_rows(xs_ref[...], g_ref[...]).astype(o_ref.dtype)


def _norm_in(xp, xs, g):
    npt = N_PROMPT // NORM_ROWS
    nst = N_SAMPLE // NORM_ROWS
    return pl.pallas_call(
        functools.partial(_norm_in_kernel, n_prompt_tiles=npt),
        grid=(npt + nst,),
        in_specs=[
            pl.BlockSpec((NORM_ROWS, D_MODEL), lambda i: (jnp.minimum(i, npt - 1), 0)),
            pl.BlockSpec((NORM_ROWS, D_MODEL), lambda i: (jnp.maximum(i - npt, 0), 0)),
            pl.BlockSpec((1, D_MODEL), lambda i: (0, 0)),
        ],
        out_specs=pl.BlockSpec((NORM_ROWS, D_MODEL), lambda i: (i, 0)),
        out_shape=jax.ShapeDtypeStruct((N_TOK, D_MODEL), BF16),
        compiler_params=_params(("arbitrary",), 40),
        name="norm_in",
    )(xp, xs, g)


def _in_proj_kernel(x_ref, w_ref, o_ref, wb_ref):
    @pl.when(pl.program_id(1) == 0)
    def _():
        wb_ref[...] = w_ref[...].astype(BF16)

    o_ref[...] = jnp.dot(x_ref[...], wb_ref[...], preferred_element_type=F32).astype(o_ref.dtype)


def _in_proj(xn, w):
    return pl.pallas_call(
        _in_proj_kernel,
        grid=(IN_COLS // MM_COLS, N_TOK // MM_ROWS),
        in_specs=[
            pl.BlockSpec((MM_ROWS, D_MODEL), lambda n, m: (m, 0)),
            pl.BlockSpec((D_MODEL, MM_COLS), lambda n, m: (0, n)),
        ],
        out_specs=pl.BlockSpec((MM_ROWS, MM_COLS), lambda n, m: (m, n)),
        out_shape=jax.ShapeDtypeStruct((N_TOK, IN_COLS), BF16),
        scratch_shapes=[pltpu.VMEM((D_MODEL, MM_COLS), BF16)],
        compiler_params=_params(("arbitrary", "arbitrary"), 58),
        name="in_proj",
    )(xn, w)


def _conv_taps(u, um1, um2, cb, w):
    conv = um2 * w[0:1] + um1 * w[1:2] + u * w[2:3]
    return cb * conv


def _conv_prompt_kernel(cb_ref, cc_ref, ch_ref, w_ref, y_ref, nb_ref, carry_ref):
    t = pl.program_id(2)

    @pl.when(t == 0)
    def _():
        carry_ref[...] = jnp.zeros_like(carry_ref)

    u = cc_ref[...].astype(F32) * ch_ref[...].astype(F32)
    prev = carry_ref[...]
    p1 = prev[SUBLANES - 1:SUBLANES]
    p2 = prev[SUBLANES - 2:SUBLANES - 1]
    row = lax.broadcasted_iota(I32, u.shape, 0)
    um1 = jnp.where(row == 0, p1, pltpu.roll(u, 1, axis=0))
    um2 = jnp.where(row == 0, p2, jnp.where(row == 1, p1, pltpu.roll(u, 2, axis=0)))
    y_ref[...] = _conv_taps(u, um1, um2, cb_ref[...].astype(F32), w_ref[...]).astype(y_ref.dtype)
    carry_ref[...] = u[CONV_ROWS - SUBLANES:CONV_ROWS]
    nb_ref[0] = u[CONV_ROWS - (CONV_K - 1):CONV_ROWS]


def _conv_prompt(proj, conv_w):
    nt = SEQ // CONV_ROWS
    nc = CONV_CH // CONV_COLS
    return pl.pallas_call(
        _conv_prompt_kernel,
        grid=(BATCH, nc, nt),
        in_specs=[
            pl.BlockSpec((CONV_ROWS, CONV_COLS), lambda b, c, t: (b * nt + t, c)),
            pl.BlockSpec((CONV_ROWS, CONV_COLS), lambda b, c, t: (b * nt + t, nc + c)),
            pl.BlockSpec((CONV_ROWS, CONV_COLS), lambda b, c, t: (b * nt + t, 2 * nc + c)),
            pl.BlockSpec((CONV_K, CONV_COLS), lambda b, c, t: (0, c)),
        ],
        out_specs=[
            pl.BlockSpec((CONV_ROWS, CONV_COLS), lambda b, c, t: (b * nt + t, c)),
            pl.BlockSpec((1, CONV_K - 1, CONV_COLS), lambda b, c, t: (b, 0, c)),
        ],
        out_shape=[
            jax.ShapeDtypeStruct((N_PROMPT, CONV_CH), BF16),
            jax.ShapeDtypeStruct((BATCH, CONV_K - 1, CONV_CH), F32),
        ],
        scratch_shapes=[pltpu.VMEM((SUBLANES, CONV_COLS), F32)],
        compiler_params=_params(("arbitrary", "arbitrary", "arbitrary"), 56),
        name="conv_prompt",
    )(proj, proj, proj, conv_w)


def _conv_sample_kernel(cb_ref, cc_ref, ch_ref, w_ref, e1_ref, e2_ref, y_ref, u_ref):
    u =cc_ref[...].astype(F32) * ch_ref[...].astype(F32)
    tpos = lax.broadcasted_iota(I32, u.shape, 0) % DEC_SEQ
    um1 = jnp.where(tpos == 0, e1_ref[...], pltpu.roll(u, 1, axis=0))
    um2 = jnp.where(tpos <= 1, e2_ref[...], pltpu.roll(u, 2, axis=0))
    y_ref[...] = _conv_taps(u, um1, um2, cb_ref[...].astype(F32), w_ref[...]).astype(y_ref.dtype)
    u_ref[...] = u


def _conv_sample(proj, conv_w, e1, e2):
    nc = CONV_CH // CONV_COLS
    rb = N_PROMPT // N_SAMPLE
    return pl.pallas_call(
        _conv_sample_kernel,
        grid=(nc,),
        in_specs=[
            pl.BlockSpec((N_SAMPLE, CONV_COLS), lambda c: (rb, c)),
            pl.BlockSpec((N_SAMPLE, CONV_COLS), lambda c: (rb, nc + c)),
            pl.BlockSpec((N_SAMPLE, CONV_COLS), lambda c: (rb, 2 * nc + c)),
            pl.BlockSpec((CONV_K, CONV_COLS), lambda c: (0, c)),
            pl.BlockSpec((N_SAMPLE, CONV_COLS), lambda c: (0, c)),
            pl.BlockSpec((N_SAMPLE, CONV_COLS), lambda c: (0, c)),
        ],
        out_specs=[
            pl.BlockSpec((N_SAMPLE, CONV_COLS), lambda c: (0, c)),
            pl.BlockSpec((N_SAMPLE, CONV_COLS), lambda c: (0, c)),
        ],
        out_shape=[
            jax.ShapeDtypeStruct((N_SAMPLE, CONV_CH), BF16),
            jax.ShapeDtypeStruct((N_SAMPLE, CONV_CH), F32),
        ],
        compiler_params=_params(("arbitrary",), 56),
        name="conv_sample",
    )(proj, proj, proj, conv_w, e1, e2)


_NT = (((1,), (1,)), ((), ()))
_TN = (((0,), (0,)), ((), ()))


def _hgrn_factors(q, fz, lb, seq_len):
    rows, width = q.shape
    one_m_lb = 1.0 - lb
    sig = _sigmoid(fz)
    log_f = jnp.log(lb + one_m_lb * sig)
    k = one_m_lb * (1.0 - sig)
    row = lax.broadcasted_iota(I32, (rows, width), 0)
    tpos = row % seq_len

    b = log_f
    shift = 1
    while shift < seq_len:
        b = b + jnp.where(tpos >= shift, pltpu.roll(b, shift, axis=0), 0.0)
        shift *= 2

    n_seq = rows // seq_len
    mid = (seq_len - 1) // 2
    if seq_len % SUBLANES == 0:
        b3 = b.reshape(n_seq, seq_len, width)
        b_mid = b3[:, mid:mid + 1]
        b_end = b3[:, seq_len - 1:seq_len]

        def per_row(per_chunk):
            return jnp.broadcast_to(per_chunk, b3.shape).reshape(rows, width)

        b_ref, b_last = per_row(b_mid), per_row(b_end)
        e_ref, e_last = per_row(jnp.exp(b_mid)), per_row(jnp.exp(b_end - b_mid))
    else:
        seq = row // seq_len
        b_ref = b[mid:mid + 1]
        b_last = b[seq_len - 1:seq_len]
        for j in range(1, n_seq):
            b_ref = jnp.where(seq == j, b[j * seq_len + mid:j * seq_len + mid + 1], b_ref)
            b_last = jnp.where(seq == j, b[(j + 1) * seq_len - 1:(j + 1) * seq_len], b_last)
        e_ref, e_last = jnp.exp(b_ref), jnp.exp(b_last - b_ref)

    qe = q * jnp.exp(b - b_ref)
    ke = k * jnp.exp(b_ref - b)
    return qe, ke, qe * e_ref, ke * e_last, log_f, b_last


def _split3_bf16(x):
    hi = x.astype(BF16).astype(F32)
    rest = x - hi
    mid = rest.astype(BF16).astype(F32)
    lo = (rest - mid).astype(BF16).astype(F32)
    return hi, mid, lo


def _head_out(o, gate, norm_g):
    o_n = o * lax.rsqrt(jnp.mean(o * o, axis=-1, keepdims=True) + EPS)
    return o_n * norm_g * (gate * _sigmoid(gate))


def _hgrn_prompt_kernel(q_ref, f_ref, v_ref, g_ref, lb_ref, ng_ref, y_ref, s_out_ref):
    lb = lb_ref[...]
    ng = ng_ref[...]
    rr = lax.broadcasted_iota(I32, (HG_CHUNK, HG_CHUNK), 0)
    cc = lax.broadcasted_iota(I32, (HG_CHUNK, HG_CHUNK), 1)
    causal = cc <= rr

    def block(i, state_t):
        rows = pl.ds(pl.multiple_of(i * HG_ROWS, HG_ROWS), HG_ROWS)
        qe, ke, qb, kd, _, b_last = _hgrn_factors(q_ref[rows, :].astype(F32), f_ref[rows, :].astype(F32),
                                                  lb, HG_CHUNK)
        qe, ke, qb, kd = (a.astype(BF16) for a in (qe, ke, qb, kd))
        vb = v_ref[rows, :]
        outs = []
        for c in range(HG_ROWS // HG_CHUNK):
            cs = slice(c * HG_CHUNK, (c + 1) * HG_CHUNK)
            scores = lax.dot_general(qe[cs], ke[cs], _NT, preferred_element_type=F32)
            scores = jnp.where(causal, scores, 0.0).astype(BF16)
            outs.append(jnp.dot(scores, vb[cs], preferred_element_type=F32)
                        + lax.dot_general(qb[cs], state_t.astype(BF16), _NT, preferred_element_type=F32))
            inc_t = lax.dot_general(vb[cs], kd[cs], _TN, preferred_element_type=F32)
            state_t = jnp.exp(b_last[c * HG_CHUNK:c * HG_CHUNK + 1]) * state_t + inc_t
        o = jnp.concatenate(outs, axis=0)
        y_ref[rows, :] = _head_out(o, g_ref[rows, :].astype(F32), ng).astype(y_ref.dtype)
        return state_t

    state_t = lax.fori_loop(0, SEQ // HG_ROWS, block, jnp.zeros((HG_DV, HG_DK), F32))
    s_out_ref[0, 0] = state_t.T


def _hgrn_prompt(proj, lb, norm_g):
    cq = 3 * CONV_CH // HG_DK

    def spec(off):
        return pl.BlockSpec((SEQ, HG_DK), lambda b, h: (b, cq + off * HG_HEADS + h))

    return pl.pallas_call(
        _hgrn_prompt_kernel,
        grid=(BATCH, HG_HEADS),
        in_specs=[spec(0), spec(1), spec(2), spec(3),
                  pl.BlockSpec((1, HG_DK), lambda b, h: (0, h)),
                  pl.BlockSpec((1, HG_DV), lambda b, h: (0, h))],
        out_specs=[
            pl.BlockSpec((SEQ, HG_DV), lambda b, h: (b, h)),
            pl.BlockSpec((1, 1, HG_DK, HG_DV), lambda b, h: (b, h, 0, 0)),
        ],
        out_shape=[
            jax.ShapeDtypeStruct((N_PROMPT, HG_W), BF16),
            jax.ShapeDtypeStruct((BATCH, HG_HEADS, HG_DK, HG_DV), F32),
        ],
        compiler_params=_params(("arbitrary", "arbitrary"), 32),
        name="hgrn_prompt",
    )(proj, proj, proj, proj, lb, norm_g)


def _hgrn_sample_kernel(q_ref, f_ref, v_ref, g_ref, lb_ref, ng_ref, s_ref, y_ref, s_out_ref):
    rows = HG_SAMPLE_SEQS * DEC_SEQ
    qe, ke, qb, kd, log_f, _ = _hgrn_factors(q_ref[...].astype(F32), f_ref[...].astype(F32),
                                            lb_ref[...], DEC_SEQ)
    lf_parts = _split3_bf16(log_f)
    v = v_ref[...].astype(F32)
    gate = g_ref[...].astype(F32)
    ng = ng_ref[...]

    in_a = lax.broadcasted_iota(I32, (SUBLANES, HG_DV), 0) < DEC_SEQ
    rr = lax.broadcasted_iota(I32, (SUBLANES, SUBLANES), 0)
    cc = lax.broadcasted_iota(I32, (SUBLANES, SUBLANES), 1)
    causal = (cc <= rr) & ((rr // DEC_SEQ) == (cc // DEC_SEQ))
    zero_tile = jnp.zeros((SUBLANES, HG_DV), F32)
    sel = jnp.concatenate([in_a.astype(F32), 1.0 - in_a.astype(F32)], axis=1)
    dec_rhs = jnp.concatenate([sel, sel, sel, jnp.zeros_like(sel)], axis=0).astype(BF16)

    for h in range(HG_HEADS):
        cols = slice(h * HG_DK, (h + 1) * HG_DK)
        outs = []
        for r in range(rows // SUBLANES):
            rs = slice(r * SUBLANES, (r + 1) * SUBLANES)
            s_cat = jnp.concatenate([s_ref[2 * r, h], s_ref[2 * r + 1, h]], axis=1)
            v_t = v[rs, cols]
            scores = lax.dot_general(qe[rs, cols].astype(BF16), ke[rs, cols].astype(BF16), _NT,
                                     preferred_element_type=F32)
            scores = jnp.where(causal, scores, 0.0).astype(BF16)
            o_state = jnp.dot(qb[rs, cols].astype(BF16), s_cat.astype(BF16), preferred_element_type=F32)
            outs.append(jnp.dot(scores, v_t.astype(BF16), preferred_element_type=F32)
                        + jnp.where(in_a, o_state[:, :HG_DV], o_state[:, HG_DV:]))
            inc_rhs = jnp.concatenate([jnp.where(in_a, v_t, 0.0), jnp.where(in_a, 0.0, v_t)], axis=1)
            inc = lax.dot_general(kd[rs, cols].astype(BF16), inc_rhs.astype(BF16), _TN,
                                  preferred_element_type=F32)
            dec_lhs = jnp.concatenate([p[rs, cols] for p in lf_parts] + [zero_tile], axis=0)
            decay = jnp.exp(lax.dot_general(dec_lhs.astype(BF16), dec_rhs, _TN,
                                            preferred_element_type=F32))
            s_new = decay * s_cat + inc
            s_out_ref[2 * r, h] = s_new[:, :HG_DV]
            s_out_ref[2 * r + 1, h] = s_new[:, HG_DV:]
        o = jnp.concatenate(outs, axis=0)
        y_ref[:, cols] = _head_out(o, gate[:, cols], ng[:, cols]).astype(y_ref.dtype)


def _hgrn_sample(proj, lb, norm_g, state):
    rows = HG_SAMPLE_SEQS * DEC_SEQ
    rb = N_PROMPT // rows
    cq = 3 * CONV_CH // HG_W

    def spec(off):
        return pl.BlockSpec((rows, HG_W), lambda g: (rb + g, cq + off))

    st_spec = pl.BlockSpec((HG_SAMPLE_SEQS, HG_HEADS, HG_DK, HG_DV), lambda g: (g, 0, 0, 0))
    return pl.pallas_call(
        _hgrn_sample_kernel,
        grid=(DEC_BATCH // HG_SAMPLE_SEQS,),
        in_specs=[spec(0), spec(1), spec(2), spec(3),
                  pl.BlockSpec((1, HG_W), lambda g: (0, 0)),
                  pl.BlockSpec((1, HG_W), lambda g: (0, 0)),
                  st_spec],
        out_specs=[pl.BlockSpec((rows, HG_W), lambda g: (g, 0)), st_spec],
        out_shape=[
            jax.ShapeDtypeStruct((N_SAMPLE, HG_W), BF16),
            jax.ShapeDtypeStruct((DEC_BATCH, HG_HEADS, HG_DK, HG_DV), F32),
        ],
        compiler_params=_params(("arbitrary",), 40),
        name="hgrn_sample",
    )(proj, proj, proj, proj, lb, norm_g, state)


def _out_proj_kernel(ycp_ref, yhp_ref, xp_ref, ycs_ref, yhs_ref, xs_ref, wc_ref, wh_ref, o_ref, *,
                     n_prompt_tiles):
    m = pl.program_id(1)

    def residual_mix(yc_ref, yh_ref, x_ref):
        mix = (jnp.dot(yc_ref[...], wc_ref[...], preferred_element_type=F32)
               + jnp.dot(yh_ref[...], wh_ref[...], preferred_element_type=F32))
        o_ref[...] = x_ref[...] + mix

    @pl.when(m < n_prompt_tiles)
    def _():
        residual_mix(ycp_ref, yhp_ref, xp_ref)

    @pl.when(m >= n_prompt_tiles)
    def _():
        residual_mix(ycs_ref, yhs_ref, xs_ref)


def _out_proj(yc_p, yh_p, xp, yc_s, yh_s, xs, w):
    npt = N_PROMPT // MM_ROWS

    def prompt_map(n, m):
        return jnp.minimum(m, npt - 1)

    def sample_map(n, m):
        return jnp.maximum(m - npt, 0)

    def source_specs(row_map):
        return [
            pl.BlockSpec((MM_ROWS, CONV_CH), lambda n, m: (row_map(n, m), 0)),
            pl.BlockSpec((MM_ROWS, HG_W), lambda n, m: (row_map(n, m), 0)),
            pl.BlockSpec((MM_ROWS, MM_COLS), lambda n, m: (row_map(n, m), n)),
        ]

    return pl.pallas_call(
        functools.partial(_out_proj_kernel, n_prompt_tiles=npt),
        grid=(D_MODEL // MM_COLS, N_TOK // MM_ROWS),
        in_specs=source_specs(prompt_map) + source_specs(sample_map) + [
            pl.BlockSpec((CONV_CH, MM_COLS), lambda n, m: (0, n)),
            pl.BlockSpec((HG_W, MM_COLS), lambda n, m: (1, n)),
        ],
        out_specs=pl.BlockSpec((MM_ROWS, MM_COLS), lambda n, m: (m, n)),
        out_shape=jax.ShapeDtypeStruct((N_TOK, D_MODEL), F32),
        compiler_params=_params(("arbitrary", "arbitrary"), 56),
        name="out_proj",
    )(yc_p, yh_p, xp, yc_s, yh_s, xs, w, w)


def _split_bf16(x):
    hi = x.astype(BF16)
    return hi, (x - hi.astype(F32)).astype(BF16)


_HI16 = 0xFFFF0000


def _pack_bf16_halves(x):
    half = x.shape[1] // 2
    bits = lax.bitcast_convert_type(x.astype(F32), jnp.uint32)
    return (bits[:, :half] >> 16) | (bits[:, half:] & jnp.uint32(_HI16))


def _unpack_bf16_half(packed, high):
    bits = (packed & jnp.uint32(_HI16)) if high else (packed << 16)
    return lax.bitcast_convert_type(bits, F32)


def _norm_route_kernel(h_ref, g_ref, wr_ref, br_ref, hn_ref, eidx_ref, gate_ref):
    hn = _rmsnorm_rows(h_ref[...], g_ref[...])
    hn_hi, hn_lo = _split_bf16(hn)
    hn_ref[...] = _pack_bf16_halves(hn_hi)
    hi_both = jnp.dot(hn_hi, wr_ref[...], preferred_element_type=F32)
    lo_hi = jnp.dot(hn_lo, wr_ref[:, :ROUTE_COLS], preferred_element_type=F32)
    logits = (hi_both[:, :ROUTE_COLS] + (hi_both[:, ROUTE_COLS:] + lo_hi)) + br_ref[...]
    lane = lax.broadcasted_iota(I32, logits.shape, 1)
    lane_f = lane.astype(F32)
    neg = -jnp.inf

    def first_argmax(vals, vmax):
        first = jnp.min(jnp.where(vals == vmax, lane_f, float(ROUTE_COLS)), axis=-1, keepdims=True)
        return first.astype(I32)

    is_grp = lane < N_GROUPS
    lg = jnp.where(is_grp, logits, neg)
    mg = jnp.max(lg, axis=-1, keepdims=True)
    g_sel = first_argmax(lg, mg)
    p_grp = 1.0 / jnp.sum(jnp.where(is_grp, jnp.exp(logits - mg), 0.0), axis=-1, keepdims=True)

    in_grp = (lane >= N_GROUPS) & (((lane - N_GROUPS) // EXP_PER_GROUP) == g_sel)
    le = jnp.where(in_grp, logits, neg)
    m1 = jnp.max(le, axis=-1, keepdims=True)
    i1 = first_argmax(le, m1)
    le2 = jnp.where(lane == i1, neg, le)
    m2 = jnp.max(le2, axis=-1, keepdims=True)
    i2 = first_argmax(le2, m2)
    e2 = jnp.exp(m2 - m1)
    gate1 = p_grp / (1.0 + e2)
    gate2 = p_grp * e2 / (1.0 + e2)
    eidx_ref[...] = jnp.where(lane == 0, i1 - N_GROUPS, jnp.where(lane == 1, i2 - N_GROUPS, 0))
    gate_ref[...] = jnp.where(lane == 0, gate1, jnp.where(lane == 1, gate2, 0.0))


def _norm_route(h, g, w_r, b_r):
    wr_both = jnp.concatenate(_split_bf16(w_r), axis=1)
    return pl.pallas_call(
        _norm_route_kernel,
        grid=(N_TOK // NORM_ROWS,),
        in_specs=[
            pl.BlockSpec((NORM_ROWS, D_MODEL), lambda i: (i, 0)),
            pl.BlockSpec((1, D_MODEL), lambda i: (0, 0)),
            pl.BlockSpec((D_MODEL, 2 * ROUTE_COLS), lambda i: (0, 0)),
            pl.BlockSpec((1, ROUTE_COLS), lambda i: (0, 0)),
        ],
        out_specs=[
            pl.BlockSpec((NORM_ROWS, D_MODEL // 2), lambda i: (i, 0)),
            pl.BlockSpec((NORM_ROWS, ROUTE_COLS), lambda i: (i, 0)),
            pl.BlockSpec((NORM_ROWS, ROUTE_COLS), lambda i: (i, 0)),
        ],
        out_shape=[
            jax.ShapeDtypeStruct((N_TOK, D_MODEL // 2), jnp.uint32),
            jax.ShapeDtypeStruct((N_TOK, ROUTE_COLS), I32),
            jax.ShapeDtypeStruct((N_TOK, ROUTE_COLS), F32),
        ],
        compiler_params=_params(("arbitrary",), 48),
        name="norm_route",
    )(h, g, wr_both, b_r)


def _row_copy(src_hbm, row, buf, slot, r, sem):
    return pltpu.make_async_copy(src_hbm.at[pl.ds(row, 1)], buf.at[slot, pl.ds(r, 1)], sem.at[slot])


def _gather_issue(idx_ref, base, row0, n_rows, src_hbm, buf, slot, sem):
    for r in range(row0, row0 + n_rows):
        _row_copy(src_hbm, idx_ref[base + r], buf, slot, r, sem).start()


def _gather_wait(row0, n_rows, src_hbm, buf, slot, sem):
    for r in range(row0, row0 + n_rows, SUBLANES):
        rows = pl.ds(r, SUBLANES)
        pltpu.make_async_copy(src_hbm.at[pl.ds(0, SUBLANES)], buf.at[slot, rows], sem.at[slot]).wait()


def _route_plan_kernel(eidx_ref, dest_ref, counts_ref, start_ref, rank_scr):
    n_tiles = N_TOK // PLAN_ROWS
    lane = lax.broadcasted_iota(I32, (PLAN_ROWS, ROUTE_COLS), 1)
    ri = lax.broadcasted_iota(I32, (PLAN_ROWS, PLAN_ROWS), 0)
    ci = lax.broadcasted_iota(I32, (PLAN_ROWS, PLAN_ROWS), 1)
    earlier = jnp.where(ci < ri, 1.0, 0.0).astype(BF16)

    def tile_rows(t):
        return pl.ds(pl.multiple_of(t * PLAN_ROWS, PLAN_ROWS), PLAN_ROWS)

    def pick(vals, col):
        return jnp.sum(jnp.where(lane == col, vals, 0.0), axis=-1, keepdims=True)

    def on_lanes01(v0, v1):
        return jnp.where(lane == 0, v0, jnp.where(lane == 1, v1, 0.0))

    def rank_tile(t, seen):
        e = eidx_ref[tile_rows(t), :]
        e0, e1 = e[:, 0:1], e[:, 1:2]
        uses = jnp.where((lane == e0) | (lane == e1), 1.0, 0.0)
        before = jnp.dot(earlier, uses.astype(BF16), preferred_element_type=F32) + seen
        rank_scr[tile_rows(t), :] = on_lanes01(pick(before, e0), pick(before, e1))
        return seen + jnp.sum(uses, axis=0, keepdims=True)

    counts = lax.fori_loop(0, n_tiles, rank_tile, jnp.zeros((1, ROUTE_COLS), F32))
    padded = jnp.ceil(counts * (1.0 / MOE_BLK)) * MOE_BLK
    ui = lax.broadcasted_iota(I32, (ROUTE_COLS, ROUTE_COLS), 0)
    uj = lax.broadcasted_iota(I32, (ROUTE_COLS, ROUTE_COLS), 1)
    start = jnp.dot(jnp.broadcast_to(padded, (SUBLANES, ROUTE_COLS)), jnp.where(ui < uj, 1.0, 0.0),
                    precision=lax.Precision.HIGHEST, preferred_element_type=F32)[0:1]
    counts_ref[...] = counts.astype(I32)
    start_ref[...] = start.astype(I32)

    def dest_tile(t, carry):
        e = eidx_ref[tile_rows(t), :]
        rank = rank_scr[tile_rows(t), :]
        d0 = pick(start, e[:, 0:1]) + rank[:, 0:1]
        d1 = pick(start, e[:, 1:2]) + rank[:, 1:2]
        dest_ref[tile_rows(t), :] = on_lanes01(d0, d1).astype(I32)
        return carry

    lax.fori_loop(0, n_tiles, dest_tile, 0)


def _route_plan(eidx):
    return pl.pallas_call(
        _route_plan_kernel,
        out_shape=[
            jax.ShapeDtypeStruct((N_TOK, ROUTE_COLS), I32),
            jax.ShapeDtypeStruct((1, ROUTE_COLS), I32),
            jax.ShapeDtypeStruct((1, ROUTE_COLS), I32),
        ],
        scratch_shapes=[pltpu.VMEM((N_TOK, ROUTE_COLS), F32)],
        compiler_params=pltpu.CompilerParams(vmem_limit_bytes=40 * MIB),
        name="route_plan",
    )(eidx)


def _dispatch_kernel(dest_ref, counts_ref, start_ref, hn_hbm, o_ref, row_tok, n_parts, buf, sem):
    b = pl.program_id(0)
    nb = pl.num_programs(0)
    slot = b % 2
    parts_per_blk = MOE_BLK // DISPATCH_PART

    def for_parts(step, fn):
        for sb in range(DISPATCH_BLKS):
            for p in range(parts_per_blk):
                fn(step * DISPATCH_BLKS + sb, p, sb * MOE_BLK + p * DISPATCH_PART)

    def fetch(step, dst_slot):
        def part(blk, p, row0):
            @pl.when(p < n_parts[blk])
            def _():
                _gather_issue(row_tok, step * DISPATCH_BLKS * MOE_BLK, row0, DISPATCH_PART, hn_hbm, buf,
                              dst_slot, sem)

            @pl.when(p >= n_parts[blk])
            def _():
                buf[dst_slot, row0:row0 + DISPATCH_PART] = jnp.zeros(
                    (DISPATCH_PART,) + buf.shape[2:], buf.dtype)

        for_parts(step, part)

    @pl.when(b == 0)
    def _():
        def clear(blk, carry):
            n_parts[blk] = 0
            return carry

        lax.fori_loop(0, MOE_NB, clear, 0)

        def per_expert(e, carry):
            count, first = counts_ref[e], start_ref[e]

            def per_block(i, c):
                rows = jnp.minimum(count - i * MOE_BLK, MOE_BLK)
                parts = (rows + DISPATCH_PART - 1) // DISPATCH_PART
                n_parts[first // MOE_BLK + i] = parts

                def pad_row(r, c2):
                    row_tok[r] = r % N_PROMPT
                    return c2

                row0 = first + i * MOE_BLK
                lax.fori_loop(row0 + rows, row0 + parts * DISPATCH_PART, pad_row, 0)
                return c

            lax.fori_loop(0, (count + MOE_BLK - 1) // MOE_BLK, per_block, 0)
            return carry

        lax.fori_loop(0, N_EXPERTS, per_expert, 0)

        def invert(g, carry):
            for j in range(SUBLANES):
                a = g * SUBLANES + j
                row_tok[dest_ref[a]] = g * (SUBLANES // TOP_K) + j // TOP_K
            return carry

        lax.fori_loop(0, N_ASSIGN // SUBLANES, invert, 0)
        fetch(0, 0)

    @pl.when(b + 1 < nb)
    def _():
        fetch(b + 1, 1 - slot)

    def wait_part(blk, p, row0):
        @pl.when(p < n_parts[blk])
        def _():
            _gather_wait(row0, DISPATCH_PART, hn_hbm, buf, slot, sem)

    for_parts(b, wait_part)
    o_ref[...] = buf[slot]


def _dispatch(dest_flat, counts, start, hn_packed):
    return pl.pallas_call(
        _dispatch_kernel,
        grid_spec=pltpu.PrefetchScalarGridSpec(
            num_scalar_prefetch=3,
            grid=(MOE_NB // DISPATCH_BLKS,),
            in_specs=[pl.BlockSpec(memory_space=pl.ANY)],
            out_specs=pl.BlockSpec((DISPATCH_BLKS * MOE_BLK, D_MODEL // 2), lambda b, *_: (b, 0)),
            scratch_shapes=[pltpu.SMEM((MOE_ROWS,), I32),
                            pltpu.SMEM((MOE_NB,), I32),
                            pltpu.VMEM((2, DISPATCH_BLKS * MOE_BLK, D_MODEL // 2), jnp.uint32),
                            pltpu.SemaphoreType.DMA((2,))],
        ),
        out_shape=jax.ShapeDtypeStruct((MOE_ROWS, D_MODEL // 2), jnp.uint32),
        compiler_params=_params(("arbitrary",), 16, disable_bounds_checks=True),
        name="moe_dispatch",
    )(dest_flat, counts, start, hn_packed)


def _stream_expert_chunks(blk0_ref, n_ref, n_used, in_copy, out_copy, out_buf, compute):
    c = pl.program_id(1)
    n_chunks = pl.num_programs(1)

    def for_blocks(chunk, fn):
        blk0 = blk0_ref[chunk]

        def body(i, carry):
            fn(blk0 + i, i)
            return carry

        lax.fori_loop(0, n_ref[chunk], body, 0)

    def start_loads(chunk):
        for_blocks(chunk, lambda g, i: in_copy(g, chunk % MOE_IN_SLOTS, i).start())

    @pl.when(c == 0)
    def _():
        start_loads(0)
        start_loads(1)

    @pl.when(c + 2 < n_chunks)
    def _():
        start_loads(c + 2)

    in_slot = c % MOE_IN_SLOTS
    out_slot = c % MOE_OUT_SLOTS
    for_blocks(c, lambda g, i: in_copy(g, in_slot, i).wait())

    @pl.when(c >= MOE_OUT_SLOTS)
    def _():
        for_blocks(c - MOE_OUT_SLOTS, lambda g, i: out_copy(g, out_slot, i).wait())

    for m in range(1, MOE_CHUNK_BLKS + 1):
        @pl.when(n_ref[c] == m)
        def _(m=m):
            compute(m, in_slot, out_slot)

    for_blocks(c, lambda g, i: out_copy(g, out_slot, i).start())

    @pl.when(c == n_chunks - 1)
    def _():
        for chunk in (c - 1, c):
            for_blocks(chunk, lambda g, i: out_copy(g, chunk % MOE_OUT_SLOTS, i).wait())
        out_buf[0, 0:MOE_BLK] = jnp.zeros((MOE_BLK,) + out_buf.shape[2:], out_buf.dtype)

        def start_zero(g, carry):
            out_copy(g, 0, 0).start()
            return carry

        def wait_zero(g, carry):
            out_copy(g, 0, 0).wait()
            return carry

        lax.fori_loop(n_used, MOE_NB, start_zero, 0)
        lax.fori_loop(n_used, MOE_NB, wait_zero, 0)


def _block_rows(g):
    return pl.ds(pl.multiple_of(g * MOE_BLK, MOE_BLK), MOE_BLK)


def _slot_rows(i):
    return pl.ds(pl.multiple_of(i * MOE_BLK, MOE_BLK), MOE_BLK)


def _expert_up_kernel(blk0_ref, n_ref, e_ref, nbu_ref, wg_ref, wu_ref, x_hbm, h_hbm, xbuf, hbuf, sem_x, sem_h):
    del e_ref
    cols = pl.ds(pl.multiple_of(pl.program_id(0) * MOE_FC, MOE_FC), MOE_FC)

    def x_copy(g, slot, i):
        return pltpu.make_async_copy(x_hbm.at[_block_rows(g)], xbuf.at[slot, _slot_rows(i)], sem_x.at[slot])

    def h_copy(g, slot, i):
        return pltpu.make_async_copy(hbuf.at[slot, _slot_rows(i)], h_hbm.at[_block_rows(g), cols],
                                     sem_h.at[slot])

    def compute(m, in_slot, out_slot):
        rows = m * MOE_BLK
        kc = D_MODEL // MOE_K_CHUNKS
        half_chunks = MOE_K_CHUNKS // 2
        g = u = None
        for i in range(MOE_K_CHUNKS):
            ks = slice(i * kc, (i + 1) * kc)
            ps = slice((i % half_chunks) * kc, (i % half_chunks + 1) * kc)
            xk = _unpack_bf16_half(xbuf[in_slot, :rows, ps], high=i >= half_chunks)
            gi = jnp.dot(xk, wg_ref[0, ks, :], preferred_element_type=F32)
            ui = jnp.dot(xk, wu_ref[0, ks, :], preferred_element_type=F32)
            g, u = (gi, ui) if g is None else (g + gi, u + ui)
        hbuf[out_slot, :rows] = (g * _sigmoid(g) * u).astype(hbuf.dtype)

    _stream_expert_chunks(blk0_ref, n_ref, nbu_ref[0], x_copy, h_copy, hbuf, compute)


def _chunk_scratch(in_cols, in_dtype, out_cols, out_dtype):
    rows = MOE_CHUNK_BLKS * MOE_BLK
    return [pltpu.VMEM((MOE_IN_SLOTS, rows, in_cols), in_dtype),
            pltpu.VMEM((MOE_OUT_SLOTS, rows, out_cols), out_dtype),
            pltpu.SemaphoreType.DMA((MOE_IN_SLOTS,)),
            pltpu.SemaphoreType.DMA((MOE_OUT_SLOTS,))]


def _expert_up(chunks, n_used, xb, w_g, w_u):
    def w_map(j, c, blk0, n, e, nbu):
        return (e[c], 0, j)

    return pl.pallas_call(
        _expert_up_kernel,
        grid_spec=pltpu.PrefetchScalarGridSpec(
            num_scalar_prefetch=4,
            grid=(D_EXPERT // MOE_FC, MOE_NCH),
            in_specs=[pl.BlockSpec((1, D_MODEL, MOE_FC), w_map),
                      pl.BlockSpec((1, D_MODEL, MOE_FC), w_map),
                      pl.BlockSpec(memory_space=pl.ANY)],
            out_specs=pl.BlockSpec(memory_space=pl.ANY),
            scratch_shapes=_chunk_scratch(D_MODEL // 2, jnp.uint32, MOE_FC, BF16),
        ),
        out_shape=jax.ShapeDtypeStruct((MOE_ROWS, D_EXPERT), BF16),
        compiler_params=_params(("arbitrary", "arbitrary"), 58),
        name="expert_up",
    )(*chunks, n_used, w_g, w_u, xb)


def _expert_down_kernel(blk0_ref, n_ref, e_ref, nbu_ref, wd_ref, h_hbm, y_hbm, hbuf, ybuf, sem_h, sem_y):
    del e_ref
    wc = MOE_DC // 2
    cols = pl.ds(pl.multiple_of(pl.program_id(0) * wc, wc), wc)

    def h_copy(g, slot, i):
        return pltpu.make_async_copy(h_hbm.at[_block_rows(g)], hbuf.at[slot, _slot_rows(i)], sem_h.at[slot])

    def y_copy(g, slot, i):
        return pltpu.make_async_copy(ybuf.at[slot, _slot_rows(i)], y_hbm.at[_block_rows(g), cols],
                                     sem_y.at[slot])

    def compute(m, in_slot, out_slot):
        rows = m * MOE_BLK
        y = jnp.dot(hbuf[in_slot, :rows].astype(F32), wd_ref[0], preferred_element_type=F32)
        ybuf[out_slot, :rows] = _pack_bf16_halves(y.astype(BF16))

    _stream_expert_chunks(blk0_ref, n_ref, nbu_ref[0], h_copy, y_copy, ybuf, compute)


def _expert_down(chunks, n_used, hb, w_d):
    return pl.pallas_call(
        _expert_down_kernel,
        grid_spec=pltpu.PrefetchScalarGridSpec(
            num_scalar_prefetch=4,
            grid=(D_MODEL // MOE_DC, MOE_NCH),
            in_specs=[pl.BlockSpec((1, D_EXPERT, MOE_DC), lambda n, c, blk0, nb, e, nbu: (e[c], 0, n)),
                      pl.BlockSpec(memory_space=pl.ANY)],
            out_specs=pl.BlockSpec(memory_space=pl.ANY),
            scratch_shapes=_chunk_scratch(D_EXPERT, BF16, MOE_DC // 2, jnp.uint32),
        ),
        out_shape=jax.ShapeDtypeStruct((MOE_ROWS, D_MODEL // 2), jnp.uint32),
        compiler_params=_params(("arbitrary", "arbitrary"), 48),
        name="expert_down",
    )(*chunks, n_used, w_d, hb)


def _combine_kernel(dest_ref, h_ref, gate_ref, g_ref, y_hbm, o_ref, buf, sem, *, tile0):
    i = pl.program_id(0)
    n = pl.num_programs(0)
    slot = i % 2
    rows = TOP_K * COMB_ROWS
    base = (tile0 + i) * rows

    @pl.when(i == 0)
    def _():
        _gather_issue(dest_ref, base, 0, rows, y_hbm, buf, 0, sem)

    @pl.when(i + 1 < n)
    def _():
        _gather_issue(dest_ref, base + rows, 0, rows, y_hbm, buf, 1 - slot, sem)

    _gather_wait(0, rows, y_hbm, buf, slot, sem)

    def expert_rows(k):
        words = buf[slot, k * COMB_ROWS:(k + 1) * COMB_ROWS]
        wc = MOE_DC // 2
        parts = []
        for p in range(D_MODEL // MOE_DC):
            w = words[:, p * wc:(p + 1) * wc]
            parts += [_unpack_bf16_half(w, high=False), _unpack_bf16_half(w, high=True)]
        return jnp.concatenate(parts, axis=1)

    gates = gate_ref[...]
    ff = gates[:, 0:1] * expert_rows(0) + gates[:, 1:2] * expert_rows(1)
    o_ref[...] = _rmsnorm_rows(h_ref[...] + ff, g_ref[...])


def _combine(dest, h, gates, g_final, y_buf, tile0, n_tiles):
    return pl.pallas_call(
        functools.partial(_combine_kernel, tile0=tile0),
        grid_spec=pltpu.PrefetchScalarGridSpec(
            num_scalar_prefetch=1,
            grid=(n_tiles,),
            in_specs=[
                pl.BlockSpec((COMB_ROWS, D_MODEL), lambda i, d: (tile0 + i, 0)),
                pl.BlockSpec((COMB_ROWS, ROUTE_COLS), lambda i, d: (tile0 + i, 0)),
                pl.BlockSpec((1, D_MODEL), lambda i, d: (0, 0)),
                pl.BlockSpec(memory_space=pl.ANY),
            ],
            out_specs=pl.BlockSpec((COMB_ROWS, D_MODEL), lambda i, d: (i, 0)),
            scratch_shapes=[pltpu.VMEM((2, TOP_K * COMB_ROWS, D_MODEL // 2), jnp.uint32),
                            pltpu.SemaphoreType.DMA((2,))],
        ),
        out_shape=jax.ShapeDtypeStruct((n_tiles * COMB_ROWS, D_MODEL), F32),
        compiler_params=_params(("arbitrary",), 40, disable_bounds_checks=True),
        name="moe_combine",
    )(dest, h, gates, g_final, y_buf)


def _chunk_list(counts, start):
    blk0 = start // MOE_BLK
    nblk = (counts + MOE_BLK - 1) // MOE_BLK
    n_used = (blk0[-1] + nblk[-1]).astype(I32)
    nch = (nblk + MOE_CHUNK_BLKS - 1) // MOE_CHUNK_BLKS
    ch_end = jnp.cumsum(nch)
    ch = jnp.arange(MOE_NCH, dtype=I32)
    ch_e = jnp.minimum(jnp.searchsorted(ch_end, ch, side='right'), N_EXPERTS - 1)
    local = ch - (ch_end - nch)[ch_e]
    live = ch < ch_end[-1]
    ch_blk0 = (blk0[ch_e] + MOE_CHUNK_BLKS * local).astype(I32)
    ch_n = jnp.where(live, jnp.clip(nblk[ch_e] - MOE_CHUNK_BLKS * local, 0, MOE_CHUNK_BLKS), 0).astype(I32)
    ch_e = jnp.where(live, ch_e, ch_e[ch_end[-1] - 1]).astype(I32)
    return (ch_blk0, ch_n, ch_e), n_used.reshape(1)


def kernel(x_prompt, x_sample, cache_conv, state_hgrn, norm_mix_g, w_in, conv_w, lb_param, hg_norm_g,
           w_out, norm_ffn_g, w_group_router, b_group_router, w_expert_router, b_expert_router,
           w_exp_gate, w_exp_up, w_exp_down, norm_final_g):
    xp = x_prompt.reshape(N_PROMPT, D_MODEL)
    xs = x_sample.reshape(N_SAMPLE, D_MODEL)
    lb = jnp.cumsum(jax.nn.softmax(lb_param.astype(F32), axis=0), axis=0)[0].reshape(1, HG_W)

    xn = _norm_in(xp, xs, norm_mix_g[0].reshape(1, D_MODEL))
    proj = _in_proj(xn, w_in[0])

    buf = cache_conv[0]
    zeros = jnp.zeros((DEC_BATCH, DEC_SEQ - 1, CONV_CH), F32)
    e1 = jnp.concatenate([buf[:, 1:2], zeros], axis=1).reshape(N_SAMPLE, CONV_CH)
    e2 = jnp.concatenate([buf, zeros[:, :DEC_SEQ - 2]], axis=1).reshape(N_SAMPLE, CONV_CH)
    yc_p, conv_p = _conv_prompt(proj, conv_w[0])
    yc_s, u_s = _conv_sample(proj, conv_w[0], e1, e2)
    conv_s = u_s.reshape(DEC_BATCH, DEC_SEQ, CONV_CH)[:, DEC_SEQ - (CONV_K - 1):]

    ng = hg_norm_g[0].reshape(1, HG_W)
    yh_p, hgrn_p = _hgrn_prompt(proj, lb, ng)
    yh_s, hgrn_s = _hgrn_sample(proj, lb, ng, state_hgrn[0])

    h = _out_proj(yc_p, yh_p, xp, yc_s, yh_s, xs, w_out[0].astype(BF16))

    pad = jnp.zeros((D_MODEL, ROUTE_COLS - N_GROUPS - N_EXPERTS), F32)
    w_r = jnp.concatenate([w_group_router[0], w_expert_router[0], pad], axis=1)
    b_r = jnp.concatenate([b_group_router[0], b_expert_router[0], pad[0]]).reshape(1, ROUTE_COLS)
    hn_packed, eidx, gates = _norm_route(h, norm_ffn_g[0].reshape(1, D_MODEL), w_r, b_r)
    dest, counts, start = _route_plan(eidx)
    dest = dest[:, :TOP_K]
    counts, start = counts[0, :N_EXPERTS], start[0, :N_EXPERTS]
    chunks, n_used = _chunk_list(counts, start)
    xb = _dispatch(dest.reshape(N_ASSIGN), counts, start, hn_packed)
    hb = _expert_up(chunks, n_used, xb, w_exp_gate[0], w_exp_up[0])
    yb = _expert_down(chunks, n_used, hb, w_exp_down[0])

    dest_t = dest.reshape(N_TOK // COMB_ROWS, COMB_ROWS, TOP_K).transpose(0, 2, 1).reshape(N_ASSIGN)
    gf = norm_final_g.reshape(1, D_MODEL)
    y_p = _combine(dest_t, h, gates, gf, yb, 0, N_PROMPT // COMB_ROWS)
    y_s = _combine(dest_t, h, gates, gf, yb, N_PROMPT // COMB_ROWS, N_SAMPLE // COMB_ROWS)

    return (y_p.reshape(BATCH, SEQ, D_MODEL), y_s.reshape(DEC_BATCH, DEC_SEQ, D_MODEL),
            conv_p[None], hgrn_p[None], conv_s[None], hgrn_s[None])
```

```python
import functools

import jax
import jax.numpy as jnp
from jax import lax
from jax.experimental import pallas as pl
from jax.experimental.pallas import tpu as pltpu

F32 = jnp.float32
BF16 = jnp.bfloat16
I32 = jnp.int32

D_MODEL = 4096
BATCH = 4
SEQ = 2048
DEC_BATCH = 128
DEC_SEQ = 4
CONV_CH = 2048
CONV_K = 3
HG_HEADS = 16
HG_DK = 128
HG_DV = 128
HG_W = HG_HEADS * HG_DK
HG_CHUNK = 64
IN_COLS = 3 * CONV_CH + 4 * HG_W
N_GROUPS = 8
EXP_PER_GROUP = 8
N_EXPERTS = 64
TOP_K = 2
D_EXPERT = 1024
EPS = 1e-6

N_PROMPT = BATCH * SEQ
N_SAMPLE = DEC_BATCH * DEC_SEQ
N_TOK = N_PROMPT + N_SAMPLE
N_ASSIGN = N_TOK * TOP_K

LANES = 128
SUBLANES = 8
ROUTE_COLS = LANES

NORM_ROWS = 256
MM_ROWS = 512
MM_COLS = 1024
CONV_ROWS = 512
CONV_COLS = 2048
HG_ROWS = 512
HG_SAMPLE_SEQS = 4
MOE_BLK = 128
MOE_NB = N_ASSIGN // MOE_BLK + N_EXPERTS
MOE_ROWS = MOE_NB * MOE_BLK
MOE_FC = 512
MOE_DC = 2048
MOE_CHUNK_BLKS = 4
MOE_NCH = N_EXPERTS + N_ASSIGN // (MOE_BLK * MOE_CHUNK_BLKS)
MOE_IN_SLOTS = 3
MOE_OUT_SLOTS = 2
MOE_K_CHUNKS = 4
PLAN_ROWS = 512
DISPATCH_PART = 32
DISPATCH_BLKS = 4
COMB_ROWS = 256

MIB = 1 << 20


def _params(sem, vmem_mib, **kwargs):
    return pltpu.CompilerParams(dimension_semantics=sem, vmem_limit_bytes=vmem_mib * MIB, **kwargs)


def _sigmoid(x):
    return 1.0 / (1.0 + jnp.exp(-x))


def _rmsnorm_rows(x, g):
    return x * lax.rsqrt(jnp.mean(x * x, axis=-1, keepdims=True) + EPS) * g


def _norm_in_kernel(xp_ref, xs_ref, g_ref, o_ref, *, n_prompt_tiles):
    i = pl.program_id(0)

    @pl.when(i < n_prompt_tiles)
    def _():
        o_ref[...] = _rmsnorm_rows(xp_ref[...], g_ref[...]).astype(o_ref.dtype)

    @pl.when(i >= n_prompt_tiles)
    def _():
        o_ref[...] = _rmsnorm_rows(xs_ref[...], g_ref[...]).astype(o_ref.dtype)


def _norm_in(xp, xs, g):
    npt = N_PROMPT // NORM_ROWS
    nst = N_SAMPLE // NORM_ROWS
    return pl.pallas_call(
        functools.partial(_norm_in_kernel, n_prompt_tiles=npt),
        grid=(npt + nst,),
        in_specs=[
            pl.BlockSpec((NORM_ROWS, D_MODEL), lambda i: (jnp.minimum(i, npt - 1), 0)),
            pl.BlockSpec((NORM_ROWS, D_MODEL), lambda i: (jnp.maximum(i - npt, 0), 0)),
            pl.BlockSpec((1, D_MODEL), lambda i: (0, 0)),
        ],
        out_specs=pl.BlockSpec((NORM_ROWS, D_MODEL), lambda i: (i, 0)),
        out_shape=jax.ShapeDtypeStruct((N_TOK, D_MODEL), BF16),
        compiler_params=_params(("arbitrary",), 40),
        name="norm_in",
    )(xp, xs, g)


def _in_proj_kernel(x_ref, w_ref, o_ref, wb_ref):
    @pl.when(pl.program_id(1) == 0)
    def _():
        wb_ref[...] = w_ref[...].astype(BF16)

    o_ref[...] = jnp.dot(x_ref[...], wb_ref[...], preferred_element_type=F32).astype(o_ref.dtype)


def _in_proj(xn, w):
    return pl.pallas_call(
        _in_proj_kernel,
        grid=(IN_COLS // MM_COLS, N_TOK // MM_ROWS),
        in_specs=[
            pl.BlockSpec((MM_ROWS, D_MODEL), lambda n, m: (m, 0)),
            pl.BlockSpec((D_MODEL, MM_COLS), lambda n, m: (0, n)),
        ],
        out_specs=pl.BlockSpec((MM_ROWS, MM_COLS), lambda n, m: (m, n)),
        out_shape=jax.ShapeDtypeStruct((N_TOK, IN_COLS), BF16),
        scratch_shapes=[pltpu.VMEM((D_MODEL, MM_COLS), BF16)],
        compiler_params=_params(("arbitrary", "arbitrary"), 58),
        name="in_proj",
    )(xn, w)


def _conv_taps(u, um1, um2, cb, w):
    conv = um2 * w[0:1] + um1 * w[1:2] + u * w[2:3]
    return cb * conv


def _conv_prompt_kernel(cb_ref, cc_ref, ch_ref, w_ref, y_ref, nb_ref, carry_ref):
    t = pl.program_id(2)

    @pl.when(t == 0)
    def _():
        carry_ref[...] = jnp.zeros_like(carry_ref)

    u = cc_ref[...].astype(F32) * ch_ref[...].astype(F32)
    prev = carry_ref[...]
    p1 = prev[SUBLANES - 1:SUBLANES]
    p2 = prev[SUBLANES - 2:SUBLANES - 1]
    row = lax.broadcasted_iota(I32, u.shape, 0)
    um1 = jnp.where(row == 0, p1, pltpu.roll(u, 1, axis=0))
    um2 = jnp.where(row == 0, p2, jnp.where(row == 1, p1, pltpu.roll(u, 2, axis=0)))
    y_ref[...] = _conv_taps(u, um1, um2, cb_ref[...].astype(F32), w_ref[...]).astype(y_ref.dtype)
    carry_ref[...] = u[CONV_ROWS - SUBLANES:CONV_ROWS]
    nb_ref[0] = u[CONV_ROWS - (CONV_K - 1):CONV_ROWS]


def _conv_prompt(proj, conv_w):
    nt = SEQ // CONV_ROWS
    nc = CONV_CH // CONV_COLS
    return pl.pallas_call(
        _conv_prompt_kernel,
        grid=(BATCH, nc, nt),
        in_specs=[
            pl.BlockSpec((CONV_ROWS, CONV_COLS), lambda b, c, t: (b * nt + t, c)),
            pl.BlockSpec((CONV_ROWS, CONV_COLS), lambda b, c, t: (b * nt + t, nc + c)),
            pl.BlockSpec((CONV_ROWS, CONV_COLS), lambda b, c, t: (b * nt + t, 2 * nc + c)),
            pl.BlockSpec((CONV_K, CONV_COLS), lambda b, c, t: (0, c)),
        ],
        out_specs=[
            pl.BlockSpec((CONV_ROWS, CONV_COLS), lambda b, c, t: (b * nt + t, c)),
            pl.BlockSpec((1, CONV_K - 1, CONV_COLS), lambda b, c, t: (b, 0, c)),
        ],
        out_shape=[
            jax.ShapeDtypeStruct((N_PROMPT, CONV_CH), BF16),
            jax.ShapeDtypeStruct((BATCH, CONV_K - 1, CONV_CH), F32),
        ],
        scratch_shapes=[pltpu.VMEM((SUBLANES, CONV_COLS), F32)],
        compiler_params=_params(("arbitrary", "arbitrary", "arbitrary"), 56),
        name="conv_prompt",
    )(proj, proj, proj, conv_w)


def _conv_sample_kernel(cb_ref, cc_ref, ch_ref, w_ref, e1_ref, e2_ref, y_ref, u_ref):
    u =cc_ref[...].astype(F32) * ch_ref[...].astype(F32)
    tpos = lax.broadcasted_iota(I32, u.shape, 0) % DEC_SEQ
    um1 = jnp.where(tpos == 0, e1_ref[...], pltpu.roll(u, 1, axis=0))
    um2 = jnp.where(tpos <= 1, e2_ref[...], pltpu.roll(u, 2, axis=0))
    y_ref[...] = _conv_taps(u, um1, um2, cb_ref[...].astype(F32), w_ref[...]).astype(y_ref.dtype)
    u_ref[...] = u


def _conv_sample(proj, conv_w, e1, e2):
    nc = CONV_CH // CONV_COLS
    rb = N_PROMPT // N_SAMPLE
    return pl.pallas_call(
        _conv_sample_kernel,
        grid=(nc,),
        in_specs=[
            pl.BlockSpec((N_SAMPLE, CONV_COLS), lambda c: (rb, c)),
            pl.BlockSpec((N_SAMPLE, CONV_COLS), lambda c: (rb, nc + c)),
            pl.BlockSpec((N_SAMPLE, CONV_COLS), lambda c: (rb, 2 * nc + c)),
            pl.BlockSpec((CONV_K, CONV_COLS), lambda c: (0, c)),
            pl.BlockSpec((N_SAMPLE, CONV_COLS), lambda c: (0, c)),
            pl.BlockSpec((N_SAMPLE, CONV_COLS), lambda c: (0, c)),
        ],
        out_specs=[
            pl.BlockSpec((N_SAMPLE, CONV_COLS), lambda c: (0, c)),
            pl.BlockSpec((N_SAMPLE, CONV_COLS), lambda c: (0, c)),
        ],
        out_shape=[
            jax.ShapeDtypeStruct((N_SAMPLE, CONV_CH), BF16),
            jax.ShapeDtypeStruct((N_SAMPLE, CONV_CH), F32),
        ],
        compiler_params=_params(("arbitrary",), 56),
        name="conv_sample",
    )(proj, proj, proj, conv_w, e1, e2)


_NT = (((1,), (1,)), ((), ()))
_TN = (((0,), (0,)), ((), ()))


def _hgrn_factors(q, fz, lb, seq_len):
    rows, width = q.shape
    one_m_lb = 1.0 - lb
    sig = _sigmoid(fz)
    log_f = jnp.log(lb + one_m_lb * sig)
    k = one_m_lb * (1.0 - sig)
    row = lax.broadcasted_iota(I32, (rows, width), 0)
    tpos = row % seq_len

    b = log_f
    shift = 1
    while shift < seq_len:
        b = b + jnp.where(tpos >= shift, pltpu.roll(b, shift, axis=0), 0.0)
        shift *= 2

    n_seq = rows // seq_len
    mid = (seq_len - 1) // 2
    if seq_len % SUBLANES == 0:
        b3 = b.reshape(n_seq, seq_len, width)
        b_mid = b3[:, mid:mid + 1]
        b_end = b3[:, seq_len - 1:seq_len]

        def per_row(per_chunk):
            return jnp.broadcast_to(per_chunk, b3.shape).reshape(rows, width)

        b_ref, b_last = per_row(b_mid), per_row(b_end)
        e_ref, e_last = per_row(jnp.exp(b_mid)), per_row(jnp.exp(b_end - b_mid))
    else:
        seq = row // seq_len
        b_ref = b[mid:mid + 1]
        b_last = b[seq_len - 1:seq_len]
        for j in range(1, n_seq):
            b_ref = jnp.where(seq == j, b[j * seq_len + mid:j * seq_len + mid + 1], b_ref)
            b_last = jnp.where(seq == j, b[(j + 1) * seq_len - 1:(j + 1) * seq_len], b_last)
        e_ref, e_last = jnp.exp(b_ref), jnp.exp(b_last - b_ref)

    qe = q * jnp.exp(b - b_ref)
    ke = k * jnp.exp(b_ref - b)
    return qe, ke, qe * e_ref, ke * e_last, log_f, b_last


def _split3_bf16(x):
    hi = x.astype(BF16).astype(F32)
    rest = x - hi
    mid = rest.astype(BF16).astype(F32)
    lo = (rest - mid).astype(BF16).astype(F32)
    return hi, mid, lo


def _head_out(o, gate, norm_g):
    o_n = o * lax.rsqrt(jnp.mean(o * o, axis=-1, keepdims=True) + EPS)
    return o_n * norm_g * (gate * _sigmoid(gate))


def _hgrn_prompt_kernel(q_ref, f_ref, v_ref, g_ref, lb_ref, ng_ref, y_ref, s_out_ref):
    lb = lb_ref[...]
    ng = ng_ref[...]
    rr = lax.broadcasted_iota(I32, (HG_CHUNK, HG_CHUNK), 0)
    cc = lax.broadcasted_iota(I32, (HG_CHUNK, HG_CHUNK), 1)
    causal = cc <= rr

    def block(i, state_t):
        rows = pl.ds(pl.multiple_of(i * HG_ROWS, HG_ROWS), HG_ROWS)
        qe, ke, qb, kd, _, b_last = _hgrn_factors(q_ref[rows, :].astype(F32), f_ref[rows, :].astype(F32),
                                                  lb, HG_CHUNK)
        qe, ke, qb, kd = (a.astype(BF16) for a in (qe, ke, qb, kd))
        vb = v_ref[rows, :]
        outs = []
        for c in range(HG_ROWS // HG_CHUNK):
            cs = slice(c * HG_CHUNK, (c + 1) * HG_CHUNK)
            scores = lax.dot_general(qe[cs], ke[cs], _NT, preferred_element_type=F32)
            scores = jnp.where(causal, scores, 0.0).astype(BF16)
            outs.append(jnp.dot(scores, vb[cs], preferred_element_type=F32)
                        + lax.dot_general(qb[cs], state_t.astype(BF16), _NT, preferred_element_type=F32))
            inc_t = lax.dot_general(vb[cs], kd[cs], _TN, preferred_element_type=F32)
            state_t = jnp.exp(b_last[c * HG_CHUNK:c * HG_CHUNK + 1]) * state_t + inc_t
        o = jnp.concatenate(outs, axis=0)
        y_ref[rows, :] = _head_out(o, g_ref[rows, :].astype(F32), ng).astype(y_ref.dtype)
        return state_t

    state_t = lax.fori_loop(0, SEQ // HG_ROWS, block, jnp.zeros((HG_DV, HG_DK), F32))
    s_out_ref[0, 0] = state_t.T


def _hgrn_prompt(proj, lb, norm_g):
    cq = 3 * CONV_CH // HG_DK

    def spec(off):
        return pl.BlockSpec((SEQ, HG_DK), lambda b, h: (b, cq + off * HG_HEADS + h))

    return pl.pallas_call(
        _hgrn_prompt_kernel,
        grid=(BATCH, HG_HEADS),
        in_specs=[spec(0), spec(1), spec(2), spec(3),
                  pl.BlockSpec((1, HG_DK), lambda b, h: (0, h)),
                  pl.BlockSpec((1, HG_DV), lambda b, h: (0, h))],
        out_specs=[
            pl.BlockSpec((SEQ, HG_DV), lambda b, h: (b, h)),
            pl.BlockSpec((1, 1, HG_DK, HG_DV), lambda b, h: (b, h, 0, 0)),
        ],
        out_shape=[
            jax.ShapeDtypeStruct((N_PROMPT, HG_W), BF16),
            jax.ShapeDtypeStruct((BATCH, HG_HEADS, HG_DK, HG_DV), F32),
        ],
        compiler_params=_params(("arbitrary", "arbitrary"), 32),
        name="hgrn_prompt",
    )(proj, proj, proj, proj, lb, norm_g)


def _hgrn_sample_kernel(q_ref, f_ref, v_ref, g_ref, lb_ref, ng_ref, s_ref, y_ref, s_out_ref):
    rows = HG_SAMPLE_SEQS * DEC_SEQ
    qe, ke, qb, kd, log_f, _ = _hgrn_factors(q_ref[...].astype(F32), f_ref[...].astype(F32),
                                            lb_ref[...], DEC_SEQ)
    lf_parts = _split3_bf16(log_f)
    v = v_ref[...].astype(F32)
    gate = g_ref[...].astype(F32)
    ng = ng_ref[...]

    in_a = lax.broadcasted_iota(I32, (SUBLANES, HG_DV), 0) < DEC_SEQ
    rr = lax.broadcasted_iota(I32, (SUBLANES, SUBLANES), 0)
    cc = lax.broadcasted_iota(I32, (SUBLANES, SUBLANES), 1)
    causal = (cc <= rr) & ((rr // DEC_SEQ) == (cc // DEC_SEQ))
    zero_tile = jnp.zeros((SUBLANES, HG_DV), F32)
    sel = jnp.concatenate([in_a.astype(F32), 1.0 - in_a.astype(F32)], axis=1)
    dec_rhs = jnp.concatenate([sel, sel, sel, jnp.zeros_like(sel)], axis=0).astype(BF16)

    for h in range(HG_HEADS):
        cols = slice(h * HG_DK, (h + 1) * HG_DK)
        outs = []
        for r in range(rows // SUBLANES):
            rs = slice(r * SUBLANES, (r + 1) * SUBLANES)
            s_cat = jnp.concatenate([s_ref[2 * r, h], s_ref[2 * r + 1, h]], axis=1)
            v_t = v[rs, cols]
            scores = lax.dot_general(qe[rs, cols].astype(BF16), ke[rs, cols].astype(BF16), _NT,
                                     preferred_element_type=F32)
            scores = jnp.where(causal, scores, 0.0).astype(BF16)
            o_state = jnp.dot(qb[rs, cols].astype(BF16), s_cat.astype(BF16), preferred_element_type=F32)
            outs.append(jnp.dot(scores, v_t.astype(BF16), preferred_element_type=F32)
                        + jnp.where(in_a, o_state[:, :HG_DV], o_state[:, HG_DV:]))
            inc_rhs = jnp.concatenate([jnp.where(in_a, v_t, 0.0), jnp.where(in_a, 0.0, v_t)], axis=1)
            inc = lax.dot_general(kd[rs, cols].astype(BF16), inc_rhs.astype(BF16), _TN,
                                  preferred_element_type=F32)
            dec_lhs = jnp.concatenate([p[rs, cols] for p in lf_parts] + [zero_tile], axis=0)
            decay = jnp.exp(lax.dot_general(dec_lhs.astype(BF16), dec_rhs, _TN,
                                            preferred_element_type=F32))
            s_new = decay * s_cat + inc
            s_out_ref[2 * r, h] = s_new[:, :HG_DV]
            s_out_ref[2 * r + 1, h] = s_new[:, HG_DV:]
        o = jnp.concatenate(outs, axis=0)
        y_ref[:, cols] = _head_out(o, gate[:, cols], ng[:, cols]).astype(y_ref.dtype)


def _hgrn_sample(proj, lb, norm_g, state):
    rows = HG_SAMPLE_SEQS * DEC_SEQ
    rb = N_PROMPT // rows
    cq = 3 * CONV_CH // HG_W

    def spec(off):
        return pl.BlockSpec((rows, HG_W), lambda g: (rb + g, cq + off))

    st_spec = pl.BlockSpec((HG_SAMPLE_SEQS, HG_HEADS, HG_DK, HG_DV), lambda g: (g, 0, 0, 0))
    return pl.pallas_call(
        _hgrn_sample_kernel,
        grid=(DEC_BATCH // HG_SAMPLE_SEQS,),
        in_specs=[spec(0), spec(1), spec(2), spec(3),
                  pl.BlockSpec((1, HG_W), lambda g: (0, 0)),
                  pl.BlockSpec((1, HG_W), lambda g: (0, 0)),
                  st_spec],
        out_specs=[pl.BlockSpec((rows, HG_W), lambda g: (g, 0)), st_spec],
        out_shape=[
            jax.ShapeDtypeStruct((N_SAMPLE, HG_W), BF16),
            jax.ShapeDtypeStruct((DEC_BATCH, HG_HEADS, HG_DK, HG_DV), F32),
        ],
        compiler_params=_params(("arbitrary",), 40),
        name="hgrn_sample",
    )(proj, proj, proj, proj, lb, norm_g, state)


def _out_proj_kernel(ycp_ref, yhp_ref, xp_ref, ycs_ref, yhs_ref, xs_ref, wc_ref, wh_ref, o_ref, *,
                     n_prompt_tiles):
    m = pl.program_id(1)

    def residual_mix(yc_ref, yh_ref, x_ref):
        mix = (jnp.dot(yc_ref[...], wc_ref[...], preferred_element_type=F32)
               + jnp.dot(yh_ref[...], wh_ref[...], preferred_element_type=F32))
        o_ref[...] = x_ref[...] + mix

    @pl.when(m < n_prompt_tiles)
    def _():
        residual_mix(ycp_ref, yhp_ref, xp_ref)

    @pl.when(m >= n_prompt_tiles)
    def _():
        residual_mix(ycs_ref, yhs_ref, xs_ref)


def _out_proj(yc_p, yh_p, xp, yc_s, yh_s, xs, w):
    npt = N_PROMPT // MM_ROWS

    def prompt_map(n, m):
        return jnp.minimum(m, npt - 1)

    def sample_map(n, m):
        return jnp.maximum(m - npt, 0)

    def source_specs(row_map):
        return [
            pl.BlockSpec((MM_ROWS, CONV_CH), lambda n, m: (row_map(n, m), 0)),
            pl.BlockSpec((MM_ROWS, HG_W), lambda n, m: (row_map(n, m), 0)),
            pl.BlockSpec((MM_ROWS, MM_COLS), lambda n, m: (row_map(n, m), n)),
        ]

    return pl.pallas_call(
        functools.partial(_out_proj_kernel, n_prompt_tiles=npt),
        grid=(D_MODEL // MM_COLS, N_TOK // MM_ROWS),
        in_specs=source_specs(prompt_map) + source_specs(sample_map) + [
            pl.BlockSpec((CONV_CH, MM_COLS), lambda n, m: (0, n)),
            pl.BlockSpec((HG_W, MM_COLS), lambda n, m: (1, n)),
        ],
        out_specs=pl.BlockSpec((MM_ROWS, MM_COLS), lambda n, m: (m, n)),
        out_shape=jax.ShapeDtypeStruct((N_TOK, D_MODEL), F32),
        compiler_params=_params(("arbitrary", "arbitrary"), 56),
        name="out_proj",
    )(yc_p, yh_p, xp, yc_s, yh_s, xs, w, w)


def _split_bf16(x):
    hi = x.astype(BF16)
    return hi, (x - hi.astype(F32)).astype(BF16)


_HI16 = 0xFFFF0000


def _pack_bf16_halves(x):
    half = x.shape[1] // 2
    bits = lax.bitcast_convert_type(x.astype(F32), jnp.uint32)
    return (bits[:, :half] >> 16) | (bits[:, half:] & jnp.uint32(_HI16))


def _unpack_bf16_half(packed, high):
    bits = (packed & jnp.uint32(_HI16)) if high else (packed << 16)
    return lax.bitcast_convert_type(bits, F32)


def _norm_route_kernel(h_ref, g_ref, wr_ref, br_ref, hn_ref, eidx_ref, gate_ref):
    hn = _rmsnorm_rows(h_ref[...], g_ref[...])
    hn_hi, hn_lo = _split_bf16(hn)
    hn_ref[...] = _pack_bf16_halves(hn_hi)
    hi_both = jnp.dot(hn_hi, wr_ref[...], preferred_element_type=F32)
    lo_hi = jnp.dot(hn_lo, wr_ref[:, :ROUTE_COLS], preferred_element_type=F32)
    logits = (hi_both[:, :ROUTE_COLS] + (hi_both[:, ROUTE_COLS:] + lo_hi)) + br_ref[...]
    lane = lax.broadcasted_iota(I32, logits.shape, 1)
    lane_f = lane.astype(F32)
    neg = -jnp.inf

    def first_argmax(vals, vmax):
        first = jnp.min(jnp.where(vals == vmax, lane_f, float(ROUTE_COLS)), axis=-1, keepdims=True)
        return first.astype(I32)

    is_grp = lane < N_GROUPS
    lg = jnp.where(is_grp, logits, neg)
    mg = jnp.max(lg, axis=-1, keepdims=True)
    g_sel = first_argmax(lg, mg)
    p_grp = 1.0 / jnp.sum(jnp.where(is_grp, jnp.exp(logits - mg), 0.0), axis=-1, keepdims=True)

    in_grp = (lane >= N_GROUPS) & (((lane - N_GROUPS) // EXP_PER_GROUP) == g_sel)
    le = jnp.where(in_grp, logits, neg)
    m1 = jnp.max(le, axis=-1, keepdims=True)
    i1 = first_argmax(le, m1)
    le2 = jnp.where(lane == i1, neg, le)
    m2 = jnp.max(le2, axis=-1, keepdims=True)
    i2 = first_argmax(le2, m2)
    e2 = jnp.exp(m2 - m1)
    gate1 = p_grp / (1.0 + e2)
    gate2 = p_grp * e2 / (1.0 + e2)
    eidx_ref[...] = jnp.where(lane == 0, i1 - N_GROUPS, jnp.where(lane == 1, i2 - N_GROUPS, 0))
    gate_ref[...] = jnp.where(lane == 0, gate1, jnp.where(lane == 1, gate2, 0.0))


def _norm_route(h, g, w_r, b_r):
    wr_both = jnp.concatenate(_split_bf16(w_r), axis=1)
    return pl.pallas_call(
        _norm_route_kernel,
        grid=(N_TOK // NORM_ROWS,),
        in_specs=[
            pl.BlockSpec((NORM_ROWS, D_MODEL), lambda i: (i, 0)),
            pl.BlockSpec((1, D_MODEL), lambda i: (0, 0)),
            pl.BlockSpec((D_MODEL, 2 * ROUTE_COLS), lambda i: (0, 0)),
            pl.BlockSpec((1, ROUTE_COLS), lambda i: (0, 0)),
        ],
        out_specs=[
            pl.BlockSpec((NORM_ROWS, D_MODEL // 2), lambda i: (i, 0)),
            pl.BlockSpec((NORM_ROWS, ROUTE_COLS), lambda i: (i, 0)),
            pl.BlockSpec((NORM_ROWS, ROUTE_COLS), lambda i: (i, 0)),
        ],
        out_shape=[
            jax.ShapeDtypeStruct((N_TOK, D_MODEL // 2), jnp.uint32),
            jax.ShapeDtypeStruct((N_TOK, ROUTE_COLS), I32),
            jax.ShapeDtypeStruct((N_TOK, ROUTE_COLS), F32),
        ],
        compiler_params=_params(("arbitrary",), 48),
        name="norm_route",
    )(h, g, wr_both, b_r)


def _row_copy(src_hbm, row, buf, slot, r, sem):
    return pltpu.make_async_copy(src_hbm.at[pl.ds(row, 1)], buf.at[slot, pl.ds(r, 1)], sem.at[slot])


def _gather_issue(idx_ref, base, row0, n_rows, src_hbm, buf, slot, sem):
    for r in range(row0, row0 + n_rows):
        _row_copy(src_hbm, idx_ref[base + r], buf, slot, r, sem).start()


def _gather_wait(row0, n_rows, src_hbm, buf, slot, sem):
    for r in range(row0, row0 + n_rows, SUBLANES):
        rows = pl.ds(r, SUBLANES)
        pltpu.make_async_copy(src_hbm.at[pl.ds(0, SUBLANES)], buf.at[slot, rows], sem.at[slot]).wait()


def _route_plan_kernel(eidx_ref, dest_ref, counts_ref, start_ref, rank_scr):
    n_tiles = N_TOK // PLAN_ROWS
    lane = lax.broadcasted_iota(I32, (PLAN_ROWS, ROUTE_COLS), 1)
    ri = lax.broadcasted_iota(I32, (PLAN_ROWS, PLAN_ROWS), 0)
    ci = lax.broadcasted_iota(I32, (PLAN_ROWS, PLAN_ROWS), 1)
    earlier = jnp.where(ci < ri, 1.0, 0.0).astype(BF16)

    def tile_rows(t):
        return pl.ds(pl.multiple_of(t * PLAN_ROWS, PLAN_ROWS), PLAN_ROWS)

    def pick(vals, col):
        return jnp.sum(jnp.where(lane == col, vals, 0.0), axis=-1, keepdims=True)

    def on_lanes01(v0, v1):
        return jnp.where(lane == 0, v0, jnp.where(lane == 1, v1, 0.0))

    def rank_tile(t, seen):
        e = eidx_ref[tile_rows(t), :]
        e0, e1 = e[:, 0:1], e[:, 1:2]
        uses = jnp.where((lane == e0) | (lane == e1), 1.0, 0.0)
        before = jnp.dot(earlier, uses.astype(BF16), preferred_element_type=F32) + seen
        rank_scr[tile_rows(t), :] = on_lanes01(pick(before, e0), pick(before, e1))
        return seen + jnp.sum(uses, axis=0, keepdims=True)

    counts = lax.fori_loop(0, n_tiles, rank_tile, jnp.zeros((1, ROUTE_COLS), F32))
    padded = jnp.ceil(counts * (1.0 / MOE_BLK)) * MOE_BLK
    ui = lax.broadcasted_iota(I32, (ROUTE_COLS, ROUTE_COLS), 0)
    uj = lax.broadcasted_iota(I32, (ROUTE_COLS, ROUTE_COLS), 1)
    start = jnp.dot(jnp.broadcast_to(padded, (SUBLANES, ROUTE_COLS)), jnp.where(ui < uj, 1.0, 0.0),
                    precision=lax.Precision.HIGHEST, preferred_element_type=F32)[0:1]
    counts_ref[...] = counts.astype(I32)
    start_ref[...] = start.astype(I32)

    def dest_tile(t, carry):
        e = eidx_ref[tile_rows(t), :]
        rank = rank_scr[tile_rows(t), :]
        d0 = pick(start, e[:, 0:1]) + rank[:, 0:1]
        d1 = pick(start, e[:, 1:2]) + rank[:, 1:2]
        dest_ref[tile_rows(t), :] = on_lanes01(d0, d1).astype(I32)
        return carry

    lax.fori_loop(0, n_tiles, dest_tile, 0)


def _route_plan(eidx):
    return pl.pallas_call(
        _route_plan_kernel,
        out_shape=[
            jax.ShapeDtypeStruct((N_TOK, ROUTE_COLS), I32),
            jax.ShapeDtypeStruct((1, ROUTE_COLS), I32),
            jax.ShapeDtypeStruct((1, ROUTE_COLS), I32),
        ],
        scratch_shapes=[pltpu.VMEM((N_TOK, ROUTE_COLS), F32)],
        compiler_params=pltpu.CompilerParams(vmem_limit_bytes=40 * MIB),
        name="route_plan",
    )(eidx)


def _dispatch_kernel(dest_ref, counts_ref, start_ref, hn_hbm, o_ref, row_tok, n_parts, buf, sem):
    b = pl.program_id(0)
    nb = pl.num_programs(0)
    slot = b % 2
    parts_per_blk = MOE_BLK // DISPATCH_PART

    def for_parts(step, fn):
        for sb in range(DISPATCH_BLKS):
            for p in range(parts_per_blk):
                fn(step * DISPATCH_BLKS + sb, p, sb * MOE_BLK + p * DISPATCH_PART)

    def fetch(step, dst_slot):
        def part(blk, p, row0):
            @pl.when(p < n_parts[blk])
            def _():
                _gather_issue(row_tok, step * DISPATCH_BLKS * MOE_BLK, row0, DISPATCH_PART, hn_hbm, buf,
                              dst_slot, sem)

            @pl.when(p >= n_parts[blk])
            def _():
                buf[dst_slot, row0:row0 + DISPATCH_PART] = jnp.zeros(
                    (DISPATCH_PART,) + buf.shape[2:], buf.dtype)

        for_parts(step, part)

    @pl.when(b == 0)
    def _():
        def clear(blk, carry):
            n_parts[blk] = 0
            return carry

        lax.fori_loop(0, MOE_NB, clear, 0)

        def per_expert(e, carry):
            count, first = counts_ref[e], start_ref[e]

            def per_block(i, c):
                rows = jnp.minimum(count - i * MOE_BLK, MOE_BLK)
                parts = (rows + DISPATCH_PART - 1) // DISPATCH_PART
                n_parts[first // MOE_BLK + i] = parts

                def pad_row(r, c2):
                    row_tok[r] = r % N_PROMPT
                    return c2

                row0 = first + i * MOE_BLK
                lax.fori_loop(row0 + rows, row0 + parts * DISPATCH_PART, pad_row, 0)
                return c

            lax.fori_loop(0, (count + MOE_BLK - 1) // MOE_BLK, per_block, 0)
            return carry

        lax.fori_loop(0, N_EXPERTS, per_expert, 0)

        def invert(g, carry):
            for j in range(SUBLANES):
                a = g * SUBLANES + j
                row_tok[dest_ref[a]] = g * (SUBLANES // TOP_K) + j // TOP_K
            return carry

        lax.fori_loop(0, N_ASSIGN // SUBLANES, invert, 0)
        fetch(0, 0)

    @pl.when(b + 1 < nb)
    def _():
        fetch(b + 1, 1 - slot)

    def wait_part(blk, p, row0):
        @pl.when(p < n_parts[blk])
        def _():
            _gather_wait(row0, DISPATCH_PART, hn_hbm, buf, slot, sem)

    for_parts(b, wait_part)
    o_ref[...] = buf[slot]


def _dispatch(dest_flat, counts, start, hn_packed):
    return pl.pallas_call(
        _dispatch_kernel,
        grid_spec=pltpu.PrefetchScalarGridSpec(
            num_scalar_prefetch=3,
            grid=(MOE_NB // DISPATCH_BLKS,),
            in_specs=[pl.BlockSpec(memory_space=pl.ANY)],
            out_specs=pl.BlockSpec((DISPATCH_BLKS * MOE_BLK, D_MODEL // 2), lambda b, *_: (b, 0)),
            scratch_shapes=[pltpu.SMEM((MOE_ROWS,), I32),
                            pltpu.SMEM((MOE_NB,), I32),
                            pltpu.VMEM((2, DISPATCH_BLKS * MOE_BLK, D_MODEL // 2), jnp.uint32),
                            pltpu.SemaphoreType.DMA((2,))],
        ),
        out_shape=jax.ShapeDtypeStruct((MOE_ROWS, D_MODEL // 2), jnp.uint32),
        compiler_params=_params(("arbitrary",), 32, disable_bounds_checks=True),
        name="moe_dispatch",
    )(dest_flat, counts, start, hn_packed)


def _stream_expert_chunks(blk0_ref, n_ref, n_used, in_copy, out_copy, out_buf, compute):
    c = pl.program_id(1)
    n_chunks = pl.num_programs(1)

    def for_blocks(chunk, fn):
        blk0 = blk0_ref[chunk]

        def body(i, carry):
            fn(blk0 + i, i)
            return carry

        lax.fori_loop(0, n_ref[chunk], body, 0)

    def start_loads(chunk):
        for_blocks(chunk, lambda g, i: in_copy(g, chunk % MOE_IN_SLOTS, i).start())

    @pl.when(c == 0)
    def _():
        start_loads(0)
        start_loads(1)

    @pl.when(c + 2 < n_chunks)
    def _():
        start_loads(c + 2)

    in_slot = c % MOE_IN_SLOTS
    out_slot = c % MOE_OUT_SLOTS
    for_blocks(c, lambda g, i: in_copy(g, in_slot, i).wait())

    @pl.when(c >= MOE_OUT_SLOTS)
    def _():
        for_blocks(c - MOE_OUT_SLOTS, lambda g, i: out_copy(g, out_slot, i).wait())

    for m in range(1, MOE_CHUNK_BLKS + 1):
        @pl.when(n_ref[c] == m)
        def _(m=m):
            compute(m, in_slot, out_slot)

    for_blocks(c, lambda g, i: out_copy(g, out_slot, i).start())

    @pl.when(c == n_chunks - 1)
    def _():
        for chunk in (c - 1, c):
            for_blocks(chunk, lambda g, i: out_copy(g, chunk % MOE_OUT_SLOTS, i).wait())
        out_buf[0, 0:MOE_BLK] = jnp.zeros((MOE_BLK,) + out_buf.shape[2:], out_buf.dtype)

        def start_zero(g, carry):
            out_copy(g, 0, 0).start()
            return carry

        def wait_zero(g, carry):
            out_copy(g, 0, 0).wait()
            return carry

        lax.fori_loop(n_used, MOE_NB, start_zero, 0)
        lax.fori_loop(n_used, MOE_NB, wait_zero, 0)


def _block_rows(g):
    return pl.ds(pl.multiple_of(g * MOE_BLK, MOE_BLK), MOE_BLK)


def _slot_rows(i):
    return pl.ds(pl.multiple_of(i * MOE_BLK, MOE_BLK), MOE_BLK)


def _expert_up_kernel(blk0_ref, n_ref, e_ref, nbu_ref, wg_ref, wu_ref, x_hbm, h_hbm, xbuf, hbuf, sem_x, sem_h):
    del e_ref
    cols = pl.ds(pl.multiple_of(pl.program_id(0) * MOE_FC, MOE_FC), MOE_FC)

    def x_copy(g, slot, i):
        return pltpu.make_async_copy(x_hbm.at[_block_rows(g)], xbuf.at[slot, _slot_rows(i)], sem_x.at[slot])

    def h_copy(g, slot, i):
        return pltpu.make_async_copy(hbuf.at[slot, _slot_rows(i)], h_hbm.at[_block_rows(g), cols],
                                     sem_h.at[slot])

    def compute(m, in_slot, out_slot):
        rows = m * MOE_BLK
        kc = D_MODEL // MOE_K_CHUNKS
        half_chunks = MOE_K_CHUNKS // 2
        g = u = None
        for i in range(MOE_K_CHUNKS):
            ks = slice(i * kc, (i + 1) * kc)
            ps = slice((i % half_chunks) * kc, (i % half_chunks + 1) * kc)
            xk = _unpack_bf16_half(xbuf[in_slot, :rows, ps], high=i >= half_chunks)
            gi = jnp.dot(xk, wg_ref[0, ks, :], preferred_element_type=F32)
            ui = jnp.dot(xk, wu_ref[0, ks, :], preferred_element_type=F32)
            g, u = (gi, ui) if g is None else (g + gi, u + ui)
        hbuf[out_slot, :rows] = (g * _sigmoid(g) * u).astype(hbuf.dtype)

    _stream_expert_chunks(blk0_ref, n_ref, nbu_ref[0], x_copy, h_copy, hbuf, compute)


def _chunk_scratch(in_cols, in_dtype, out_cols, out_dtype):
    rows = MOE_CHUNK_BLKS * MOE_BLK
    return [pltpu.VMEM((MOE_IN_SLOTS, rows, in_cols), in_dtype),
            pltpu.VMEM((MOE_OUT_SLOTS, rows, out_cols), out_dtype),
            pltpu.SemaphoreType.DMA((MOE_IN_SLOTS,)),
            pltpu.SemaphoreType.DMA((MOE_OUT_SLOTS,))]


def _expert_up(chunks, n_used, xb, w_g, w_u):
    def w_map(j, c, blk0, n, e, nbu):
        return (e[c], 0, j)

    return pl.pallas_call(
        _expert_up_kernel,
        grid_spec=pltpu.PrefetchScalarGridSpec(
            num_scalar_prefetch=4,
            grid=(D_EXPERT // MOE_FC, MOE_NCH),
            in_specs=[pl.BlockSpec((1, D_MODEL, MOE_FC), w_map),
                      pl.BlockSpec((1, D_MODEL, MOE_FC), w_map),
                      pl.BlockSpec(memory_space=pl.ANY)],
            out_specs=pl.BlockSpec(memory_space=pl.ANY),
            scratch_shapes=_chunk_scratch(D_MODEL // 2, jnp.uint32, MOE_FC, BF16),
        ),
        out_shape=jax.ShapeDtypeStruct((MOE_ROWS, D_EXPERT), BF16),
        compiler_params=_params(("arbitrary", "arbitrary"), 58),
        name="expert_up",
    )(*chunks, n_used, w_g, w_u, xb)


def _expert_down_kernel(blk0_ref, n_ref, e_ref, nbu_ref, wd_ref, h_hbm, y_hbm, hbuf, ybuf, sem_h, sem_y):
    del e_ref
    wc = MOE_DC // 2
    cols = pl.ds(pl.multiple_of(pl.program_id(0) * wc, wc), wc)

    def h_copy(g, slot, i):
        return pltpu.make_async_copy(h_hbm.at[_block_rows(g)], hbuf.at[slot, _slot_rows(i)], sem_h.at[slot])

    def y_copy(g, slot, i):
        return pltpu.make_async_copy(ybuf.at[slot, _slot_rows(i)], y_hbm.at[_block_rows(g), cols],
                                     sem_y.at[slot])

    def compute(m, in_slot, out_slot):
        rows = m * MOE_BLK
        y = jnp.dot(hbuf[in_slot, :rows].astype(F32), wd_ref[0], preferred_element_type=F32)
        ybuf[out_slot, :rows] = _pack_bf16_halves(y.astype(BF16))

    _stream_expert_chunks(blk0_ref, n_ref, nbu_ref[0], h_copy, y_copy, ybuf, compute)


def _expert_down(chunks, n_used, hb, w_d):
    return pl.pallas_call(
        _expert_down_kernel,
        grid_spec=pltpu.PrefetchScalarGridSpec(
            num_scalar_prefetch=4,
            grid=(D_MODEL // MOE_DC, MOE_NCH),
            in_specs=[pl.BlockSpec((1, D_EXPERT, MOE_DC), lambda n, c, blk0, nb, e, nbu: (e[c], 0, n)),
                      pl.BlockSpec(memory_space=pl.ANY)],
            out_specs=pl.BlockSpec(memory_space=pl.ANY),
            scratch_shapes=_chunk_scratch(D_EXPERT, BF16, MOE_DC // 2, jnp.uint32),
        ),
        out_shape=jax.ShapeDtypeStruct((MOE_ROWS, D_MODEL // 2), jnp.uint32),
        compiler_params=_params(("arbitrary", "arbitrary"), 48),
        name="expert_down",
    )(*chunks, n_used, w_d, hb)


def _combine_kernel(dest_ref, h_ref, gate_ref, g_ref, y_hbm, o_ref, buf, sem, *, tile0):
    i = pl.program_id(0)
    n = pl.num_programs(0)
    slot = i % 2
    rows = TOP_K * COMB_ROWS
    base = (tile0 + i) * rows

    @pl.when(i == 0)
    def _():
        _gather_issue(dest_ref, base, 0, rows, y_hbm, buf, 0, sem)

    @pl.when(i + 1 < n)
    def _():
        _gather_issue(dest_ref, base + rows, 0, rows, y_hbm, buf, 1 - slot, sem)

    _gather_wait(0, rows, y_hbm, buf, slot, sem)

    def expert_rows(k):
        words = buf[slot, k * COMB_ROWS:(k + 1) * COMB_ROWS]
        wc = MOE_DC // 2
        parts = []
        for p in range(D_MODEL // MOE_DC):
            w = words[:, p * wc:(p + 1) * wc]
            parts += [_unpack_bf16_half(w, high=False), _unpack_bf16_half(w, high=True)]
        return jnp.concatenate(parts, axis=1)

    gates = gate_ref[...]
    ff = gates[:, 0:1] * expert_rows(0) + gates[:, 1:2] * expert_rows(1)
    o_ref[...] = _rmsnorm_rows(h_ref[...] + ff, g_ref[...])


def _combine(dest, h, gates, g_final, y_buf, tile0, n_tiles):
    return pl.pallas_call(
        functools.partial(_combine_kernel, tile0=tile0),
        grid_spec=pltpu.PrefetchScalarGridSpec(
            num_scalar_prefetch=1,
            grid=(n_tiles,),
            in_specs=[
                pl.BlockSpec((COMB_ROWS, D_MODEL), lambda i, d: (tile0 + i, 0)),
                pl.BlockSpec((COMB_ROWS, ROUTE_COLS), lambda i, d: (tile0 + i, 0)),
                pl.BlockSpec((1, D_MODEL), lambda i, d: (0, 0)),
                pl.BlockSpec(memory_space=pl.ANY),
            ],
            out_specs=pl.BlockSpec((COMB_ROWS, D_MODEL), lambda i, d: (i, 0)),
            scratch_shapes=[pltpu.VMEM((2, TOP_K * COMB_ROWS, D_MODEL // 2), jnp.uint32),
                            pltpu.SemaphoreType.DMA((2,))],
        ),
        out_shape=jax.ShapeDtypeStruct((n_tiles * COMB_ROWS, D_MODEL), F32),
        compiler_params=_params(("arbitrary",), 56, disable_bounds_checks=True),
        name="moe_combine",
    )(dest, h, gates, g_final, y_buf)


def _chunk_list(counts, start):
    blk0 = start // MOE_BLK
    nblk = (counts + MOE_BLK - 1) // MOE_BLK
    n_used = (blk0[-1] + nblk[-1]).astype(I32)
    nch = (nblk + MOE_CHUNK_BLKS - 1) // MOE_CHUNK_BLKS
    ch_end = jnp.cumsum(nch)
    ch = jnp.arange(MOE_NCH, dtype=I32)
    ch_e = jnp.minimum(jnp.searchsorted(ch_end, ch, side='right'), N_EXPERTS - 1)
    local = ch - (ch_end - nch)[ch_e]
    live = ch < ch_end[-1]
    ch_blk0 = (blk0[ch_e] + MOE_CHUNK_BLKS * local).astype(I32)
    ch_n = jnp.where(live, jnp.clip(nblk[ch_e] - MOE_CHUNK_BLKS * local, 0, MOE_CHUNK_BLKS), 0).astype(I32)
    ch_e = jnp.where(live, ch_e, ch_e[ch_end[-1] - 1]).astype(I32)
    return (ch_blk0, ch_n, ch_e), n_used.reshape(1)


def kernel(x_prompt, x_sample, cache_conv, state_hgrn, norm_mix_g, w_in, conv_w, lb_param, hg_norm_g,
           w_out, norm_ffn_g, w_group_router, b_group_router, w_expert_router, b_expert_router,
           w_exp_gate, w_exp_up, w_exp_down, norm_final_g):
    xp = x_prompt.reshape(N_PROMPT, D_MODEL)
    xs = x_sample.reshape(N_SAMPLE, D_MODEL)
    lb = jnp.cumsum(jax.nn.softmax(lb_param.astype(F32), axis=0), axis=0)[0].reshape(1, HG_W)

    xn = _norm_in(xp, xs, norm_mix_g[0].reshape(1, D_MODEL))
    proj = _in_proj(xn, w_in[0])

    buf = cache_conv[0]
    zeros = jnp.zeros((DEC_BATCH, DEC_SEQ - 1, CONV_CH), F32)
    e1 = jnp.concatenate([buf[:, 1:2], zeros], axis=1).reshape(N_SAMPLE, CONV_CH)
    e2 = jnp.concatenate([buf, zeros[:, :DEC_SEQ - 2]], axis=1).reshape(N_SAMPLE, CONV_CH)
    yc_p, conv_p = _conv_prompt(proj, conv_w[0])
    yc_s, u_s = _conv_sample(proj, conv_w[0], e1, e2)
    conv_s = u_s.reshape(DEC_BATCH, DEC_SEQ, CONV_CH)[:, DEC_SEQ - (CONV_K - 1):]

    ng = hg_norm_g[0].reshape(1, HG_W)
    yh_p, hgrn_p = _hgrn_prompt(proj, lb, ng)
    yh_s, hgrn_s = _hgrn_sample(proj, lb, ng, state_hgrn[0])

    h = _out_proj(yc_p, yh_p, xp, yc_s, yh_s, xs, w_out[0].astype(BF16))

    pad = jnp.zeros((D_MODEL, ROUTE_COLS - N_GROUPS - N_EXPERTS), F32)
    w_r = jnp.concatenate([w_group_router[0], w_expert_router[0], pad], axis=1)
    b_r = jnp.concatenate([b_group_router[0], b_expert_router[0], pad[0]]).reshape(1, ROUTE_COLS)
    hn_packed, eidx, gates = _norm_route(h, norm_ffn_g[0].reshape(1, D_MODEL), w_r, b_r)
    dest, counts, start = _route_plan(eidx)
    dest = dest[:, :TOP_K]
    counts, start = counts[0, :N_EXPERTS], start[0, :N_EXPERTS]
    chunks, n_used = _chunk_list(counts, start)
    xb = _dispatch(dest.reshape(N_ASSIGN), counts, start, hn_packed)
    hb = _expert_up(chunks, n_used, xb, w_exp_gate[0], w_exp_up[0])
    yb = _expert_down(chunks, n_used, hb, w_exp_down[0])

    dest_t = dest.reshape(N_TOK // COMB_ROWS, COMB_ROWS, TOP_K).transpose(0, 2, 1).reshape(N_ASSIGN)
    gf = norm_final_g.reshape(1, D_MODEL)
    y_p = _combine(dest_t, h, gates, gf, yb, 0, N_PROMPT // COMB_ROWS)
    y_s = _combine(dest_t, h, gates, gf, yb, N_PROMPT // COMB_ROWS, N_SAMPLE // COMB_ROWS)

    return (y_p.reshape(BATCH, SEQ, D_MODEL), y_s.reshape(DEC_BATCH, DEC_SEQ, D_MODEL),
            conv_p[None], hgrn_p[None], conv_s[None], hgrn_s[None])
```

```python
import functools

import jax
import jax.numpy as jnp
from jax import lax
from jax.experimental import pallas as pl
from jax.experimental.pallas import tpu as pltpu

F32 = jnp.float32
BF16 = jnp.bfloat16
I32 = jnp.int32

D_MODEL = 4096
BATCH = 4
SEQ = 2048
DEC_BATCH = 128
DEC_SEQ = 4
CONV_CH = 2048
CONV_K = 3
HG_HEADS = 16
HG_DK = 128
HG_DV = 128
HG_W = HG_HEADS * HG_DK
HG_CHUNK = 64
IN_COLS = 3 * CONV_CH + 4 * HG_W
N_GROUPS = 8
EXP_PER_GROUP = 8
N_EXPERTS = 64
TOP_K = 2
D_EXPERT = 1024
EPS = 1e-6

N_PROMPT = BATCH * SEQ
N_SAMPLE = DEC_BATCH * DEC_SEQ
N_TOK = N_PROMPT + N_SAMPLE
N_ASSIGN = N_TOK * TOP_K

LANES = 128
SUBLANES = 8
ROUTE_COLS = LANES

NORM_ROWS = 256
MM_ROWS = 512
MM_COLS = 1024
CONV_ROWS = 512
CONV_COLS = 2048
HG_ROWS = 512
HG_SAMPLE_SEQS = 4
MOE_BLK = 128
MOE_NB = N_ASSIGN // MOE_BLK + N_EXPERTS
MOE_ROWS = MOE_NB * MOE_BLK
MOE_FC = 512
MOE_DC = 2048
MOE_CHUNK_BLKS = 4
MOE_NCH = N_EXPERTS + N_ASSIGN // (MOE_BLK * MOE_CHUNK_BLKS)
MOE_IN_SLOTS = 3
MOE_OUT_SLOTS = 2
MOE_W_SLOTS = 3
MOE_K_CHUNKS = 4
PLAN_ROWS = 512
DISPATCH_PART = 32
DISPATCH_BLKS = 4
COMB_ROWS = 256

MIB = 1 << 20


def _params(sem, vmem_mib, **kwargs):
    return pltpu.CompilerParams(dimension_semantics=sem, vmem_limit_bytes=vmem_mib * MIB, **kwargs)


def _sigmoid(x):
    return 1.0 / (1.0 + jnp.exp(-x))


def _rmsnorm_rows(x, g):
    return x * lax.rsqrt(jnp.mean(x * x, axis=-1, keepdims=True) + EPS) * g


def _norm_in_kernel(xp_ref, xs_ref, g_ref, o_ref, *, n_prompt_tiles):
    i = pl.program_id(0)

    @pl.when(i < n_prompt_tiles)
    def _():
        o_ref[...] = _rmsnorm_rows(xp_ref[...], g_ref[...]).astype(o_ref.dtype)

    @pl.when(i >= n_prompt_tiles)
    def _():
        o_ref[...] = _rmsnorm_rows(xs_ref[...], g_ref[...]).astype(o_ref.dtype)


def _norm_in(xp, xs, g):
    npt = N_PROMPT // NORM_ROWS
    nst = N_SAMPLE // NORM_ROWS
    return pl.pallas_call(
        functools.partial(_norm_in_kernel, n_prompt_tiles=npt),
        grid=(npt + nst,),
        in_specs=[
            pl.BlockSpec((NORM_ROWS, D_MODEL), lambda i: (jnp.minimum(i, npt - 1), 0)),
            pl.BlockSpec((NORM_ROWS, D_MODEL), lambda i: (jnp.maximum(i - npt, 0), 0)),
            pl.BlockSpec((1, D_MODEL), lambda i: (0, 0)),
        ],
        out_specs=pl.BlockSpec((NORM_ROWS, D_MODEL), lambda i: (i, 0)),
        out_shape=jax.ShapeDtypeStruct((N_TOK, D_MODEL), BF16),
        compiler_params=_params(("arbitrary",), 40),
        name="norm_in",
    )(xp, xs, g)


def _in_proj_kernel(x_ref, w_ref, o_ref, wb_ref):
    @pl.when(pl.program_id(1) == 0)
    def _():
        wb_ref[...] = w_ref[...].astype(BF16)

    o_ref[...] = jnp.dot(x_ref[...], wb_ref[...], preferred_element_type=F32).astype(o_ref.dtype)


def _in_proj(xn, w):
    return pl.pallas_call(
        _in_proj_kernel,
        grid=(IN_COLS // MM_COLS, N_TOK // MM_ROWS),
        in_specs=[
            pl.BlockSpec((MM_ROWS, D_MODEL), lambda n, m: (m, 0)),
            pl.BlockSpec((D_MODEL, MM_COLS), lambda n, m: (0, n)),
        ],
        out_specs=pl.BlockSpec((MM_ROWS, MM_COLS), lambda n, m: (m, n)),
        out_shape=jax.ShapeDtypeStruct((N_TOK, IN_COLS), BF16),
        scratch_shapes=[pltpu.VMEM((D_MODEL, MM_COLS), BF16)],
        compiler_params=_params(("arbitrary", "arbitrary"), 58),
        name="in_proj",
    )(xn, w)


def _conv_taps(u, um1, um2, cb, w):
    conv = um2 * w[0:1] + um1 * w[1:2] + u * w[2:3]
    return cb * conv


def _conv_prompt_kernel(cb_ref, cc_ref, ch_ref, w_ref, y_ref, nb_ref, carry_ref):
    t = pl.program_id(2)

    @pl.when(t == 0)
    def _():
        carry_ref[...] = jnp.zeros_like(carry_ref)

    u = cc_ref[...].astype(F32) * ch_ref[...].astype(F32)
    prev = carry_ref[...]
    p1 = prev[SUBLANES - 1:SUBLANES]
    p2 = prev[SUBLANES - 2:SUBLANES - 1]
    row = lax.broadcasted_iota(I32, u.shape, 0)
    um1 = jnp.where(row == 0, p1, pltpu.roll(u, 1, axis=0))
    um2 = jnp.where(row == 0, p2, jnp.where(row == 1, p1, pltpu.roll(u, 2, axis=0)))
    y_ref[...] = _conv_taps(u, um1, um2, cb_ref[...].astype(F32), w_ref[...]).astype(y_ref.dtype)
    carry_ref[...] = u[CONV_ROWS - SUBLANES:CONV_ROWS]
    nb_ref[0] = u[CONV_ROWS - (CONV_K - 1):CONV_ROWS]


def _conv_prompt(proj, conv_w):
    nt = SEQ // CONV_ROWS
    nc = CONV_CH // CONV_COLS
    return pl.pallas_call(
        _conv_prompt_kernel,
        grid=(BATCH, nc, nt),
        in_specs=[
            pl.BlockSpec((CONV_ROWS, CONV_COLS), lambda b, c, t: (b * nt + t, c)),
            pl.BlockSpec((CONV_ROWS, CONV_COLS), lambda b, c, t: (b * nt + t, nc + c)),
            pl.BlockSpec((CONV_ROWS, CONV_COLS), lambda b, c, t: (b * nt + t, 2 * nc + c)),
            pl.BlockSpec((CONV_K, CONV_COLS), lambda b, c, t: (0, c)),
        ],
        out_specs=[
            pl.BlockSpec((CONV_ROWS, CONV_COLS), lambda b, c, t: (b * nt + t, c)),
            pl.BlockSpec((1, CONV_K - 1, CONV_COLS), lambda b, c, t: (b, 0, c)),
        ],
        out_shape=[
            jax.ShapeDtypeStruct((N_PROMPT, CONV_CH), BF16),
            jax.ShapeDtypeStruct((BATCH, CONV_K - 1, CONV_CH), F32),
        ],
        scratch_shapes=[pltpu.VMEM((SUBLANES, CONV_COLS), F32)],
        compiler_params=_params(("arbitrary", "arbitrary", "arbitrary"), 56),
        name="conv_prompt",
    )(proj, proj, proj, conv_w)


def _conv_sample_kernel(cb_ref, cc_ref, ch_ref, w_ref, e1_ref, e2_ref, y_ref, u_ref):
    u =cc_ref[...].astype(F32) * ch_ref[...].astype(F32)
    tpos = lax.broadcasted_iota(I32, u.shape, 0) % DEC_SEQ
    um1 = jnp.where(tpos == 0, e1_ref[...], pltpu.roll(u, 1, axis=0))
    um2 = jnp.where(tpos <= 1, e2_ref[...], pltpu.roll(u, 2, axis=0))
    y_ref[...] = _conv_taps(u, um1, um2, cb_ref[...].astype(F32), w_ref[...]).astype(y_ref.dtype)
    u_ref[...] = u


def _conv_sample(proj, conv_w, e1, e2):
    nc = CONV_CH // CONV_COLS
    rb = N_PROMPT // N_SAMPLE
    return pl.pallas_call(
        _conv_sample_kernel,
        grid=(nc,),
        in_specs=[
            pl.BlockSpec((N_SAMPLE, CONV_COLS), lambda c: (rb, c)),
            pl.BlockSpec((N_SAMPLE, CONV_COLS), lambda c: (rb, nc + c)),
            pl.BlockSpec((N_SAMPLE, CONV_COLS), lambda c: (rb, 2 * nc + c)),
            pl.BlockSpec((CONV_K, CONV_COLS), lambda c: (0, c)),
            pl.BlockSpec((N_SAMPLE, CONV_COLS), lambda c: (0, c)),
            pl.BlockSpec((N_SAMPLE, CONV_COLS), lambda c: (0, c)),
        ],
        out_specs=[
            pl.BlockSpec((N_SAMPLE, CONV_COLS), lambda c: (0, c)),
            pl.BlockSpec((N_SAMPLE, CONV_COLS), lambda c: (0, c)),
        ],
        out_shape=[
            jax.ShapeDtypeStruct((N_SAMPLE, CONV_CH), BF16),
            jax.ShapeDtypeStruct((N_SAMPLE, CONV_CH), F32),
        ],
        compiler_params=_params(("arbitrary",), 56),
        name="conv_sample",
    )(proj, proj, proj, conv_w, e1, e2)


_NT = (((1,), (1,)), ((), ()))
_TN = (((0,), (0,)), ((), ()))


def _hgrn_factors(q, fz, lb, seq_len):
    rows, width = q.shape
    one_m_lb = 1.0 - lb
    sig = _sigmoid(fz)
    log_f = jnp.log(lb + one_m_lb * sig)
    k = one_m_lb * (1.0 - sig)
    row = lax.broadcasted_iota(I32, (rows, width), 0)
    tpos = row % seq_len

    b = log_f
    shift = 1
    while shift < seq_len:
        b = b + jnp.where(tpos >= shift, pltpu.roll(b, shift, axis=0), 0.0)
        shift *= 2

    n_seq = rows // seq_len
    mid = (seq_len - 1) // 2
    if seq_len % SUBLANES == 0:
        b3 = b.reshape(n_seq, seq_len, width)
        b_mid = b3[:, mid:mid + 1]
        b_end = b3[:, seq_len - 1:seq_len]

        def per_row(per_chunk):
            return jnp.broadcast_to(per_chunk, b3.shape).reshape(rows, width)

        b_ref, b_last = per_row(b_mid), per_row(b_end)
        e_ref, e_last = per_row(jnp.exp(b_mid)), per_row(jnp.exp(b_end - b_mid))
    else:
        seq = row // seq_len
        b_ref = b[mid:mid + 1]
        b_last = b[seq_len - 1:seq_len]
        for j in range(1, n_seq):
            b_ref = jnp.where(seq == j, b[j * seq_len + mid:j * seq_len + mid + 1], b_ref)
            b_last = jnp.where(seq == j, b[(j + 1) * seq_len - 1:(j + 1) * seq_len], b_last)
        e_ref, e_last = jnp.exp(b_ref), jnp.exp(b_last - b_ref)

    qe = q * jnp.exp(b - b_ref)
    ke = k * jnp.exp(b_ref - b)
    return qe, ke, qe * e_ref, ke * e_last, log_f, b_last


def _split3_bf16(x):
    hi = x.astype(BF16).astype(F32)
    rest = x - hi
    mid = rest.astype(BF16).astype(F32)
    lo = (rest - mid).astype(BF16).astype(F32)
    return hi, mid, lo


def _head_out(o, gate, norm_g):
    o_n = o * lax.rsqrt(jnp.mean(o * o, axis=-1, keepdims=True) + EPS)
    return o_n * norm_g * (gate * _sigmoid(gate))


def _hgrn_prompt_kernel(q_ref, f_ref, v_ref, g_ref, lb_ref, ng_ref, y_ref, s_out_ref):
    lb = lb_ref[...]
    ng = ng_ref[...]
    rr = lax.broadcasted_iota(I32, (HG_CHUNK, HG_CHUNK), 0)
    cc = lax.broadcasted_iota(I32, (HG_CHUNK, HG_CHUNK), 1)
    causal = cc <= rr

    def block(i, state_t):
        rows = pl.ds(pl.multiple_of(i * HG_ROWS, HG_ROWS), HG_ROWS)
        qe, ke, qb, kd, _, b_last = _hgrn_factors(q_ref[rows, :].astype(F32), f_ref[rows, :].astype(F32),
                                                  lb, HG_CHUNK)
        qe, ke, qb, kd = (a.astype(BF16) for a in (qe, ke, qb, kd))
        vb = v_ref[rows, :]
        outs = []
        for c in range(HG_ROWS // HG_CHUNK):
            cs = slice(c * HG_CHUNK, (c + 1) * HG_CHUNK)
            scores = lax.dot_general(qe[cs], ke[cs], _NT, preferred_element_type=F32)
            scores = jnp.where(causal, scores, 0.0).astype(BF16)
            outs.append(jnp.dot(scores, vb[cs], preferred_element_type=F32)
                        + lax.dot_general(qb[cs], state_t.astype(BF16), _NT, preferred_element_type=F32))
            inc_t = lax.dot_general(vb[cs], kd[cs], _TN, preferred_element_type=F32)
            state_t = jnp.exp(b_last[c * HG_CHUNK:c * HG_CHUNK + 1]) * state_t + inc_t
        o = jnp.concatenate(outs, axis=0)
        y_ref[rows, :] = _head_out(o, g_ref[rows, :].astype(F32), ng).astype(y_ref.dtype)
        return state_t

    state_t = lax.fori_loop(0, SEQ // HG_ROWS, block, jnp.zeros((HG_DV, HG_DK), F32))
    s_out_ref[0, 0] = state_t.T


def _hgrn_prompt(proj, lb, norm_g):
    cq = 3 * CONV_CH // HG_DK

    def spec(off):
        return pl.BlockSpec((SEQ, HG_DK), lambda b, h: (b, cq + off * HG_HEADS + h))

    return pl.pallas_call(
        _hgrn_prompt_kernel,
        grid=(BATCH, HG_HEADS),
        in_specs=[spec(0), spec(1), spec(2), spec(3),
                  pl.BlockSpec((1, HG_DK), lambda b, h: (0, h)),
                  pl.BlockSpec((1, HG_DV), lambda b, h: (0, h))],
        out_specs=[
            pl.BlockSpec((SEQ, HG_DV), lambda b, h: (b, h)),
            pl.BlockSpec((1, 1, HG_DK, HG_DV), lambda b, h: (b, h, 0, 0)),
        ],
        out_shape=[
            jax.ShapeDtypeStruct((N_PROMPT, HG_W), BF16),
            jax.ShapeDtypeStruct((BATCH, HG_HEADS, HG_DK, HG_DV), F32),
        ],
        compiler_params=_params(("arbitrary", "arbitrary"), 32),
        name="hgrn_prompt",
    )(proj, proj, proj, proj, lb, norm_g)


def _hgrn_sample_kernel(q_ref, f_ref, v_ref, g_ref, lb_ref, ng_ref, s_ref, y_ref, s_out_ref):
    rows = HG_SAMPLE_SEQS * DEC_SEQ
    qe, ke, qb, kd, log_f, _ = _hgrn_factors(q_ref[...].astype(F32), f_ref[...].astype(F32),
                                            lb_ref[...], DEC_SEQ)
    lf_parts = _split3_bf16(log_f)
    v = v_ref[...].astype(F32)
    gate = g_ref[...].astype(F32)
    ng = ng_ref[...]

    in_a = lax.broadcasted_iota(I32, (SUBLANES, HG_DV), 0) < DEC_SEQ
    rr = lax.broadcasted_iota(I32, (SUBLANES, SUBLANES), 0)
    cc = lax.broadcasted_iota(I32, (SUBLANES, SUBLANES), 1)
    causal = (cc <= rr) & ((rr // DEC_SEQ) == (cc // DEC_SEQ))
    zero_tile = jnp.zeros((SUBLANES, HG_DV), F32)
    sel = jnp.concatenate([in_a.astype(F32), 1.0 - in_a.astype(F32)], axis=1)
    dec_rhs = jnp.concatenate([sel, sel, sel, jnp.zeros_like(sel)], axis=0).astype(BF16)

    for h in range(HG_HEADS):
        cols = slice(h * HG_DK, (h + 1) * HG_DK)
        outs = []
        for r in range(rows // SUBLANES):
            rs = slice(r * SUBLANES, (r + 1) * SUBLANES)
            s_cat = jnp.concatenate([s_ref[2 * r, h], s_ref[2 * r + 1, h]], axis=1)
            v_t = v[rs, cols]
            scores = lax.dot_general(qe[rs, cols].astype(BF16), ke[rs, cols].astype(BF16), _NT,
                                     preferred_element_type=F32)
            scores = jnp.where(causal, scores, 0.0).astype(BF16)
            o_state = jnp.dot(qb[rs, cols].astype(BF16), s_cat.astype(BF16), preferred_element_type=F32)
            outs.append(jnp.dot(scores, v_t.astype(BF16), preferred_element_type=F32)
                        + jnp.where(in_a, o_state[:, :HG_DV], o_state[:, HG_DV:]))
            inc_rhs = jnp.concatenate([jnp.where(in_a, v_t, 0.0), jnp.where(in_a, 0.0, v_t)], axis=1)
            inc = lax.dot_general(kd[rs, cols].astype(BF16), inc_rhs.astype(BF16), _TN,
                                  preferred_element_type=F32)
            dec_lhs = jnp.concatenate([p[rs, cols] for p in lf_parts] + [zero_tile], axis=0)
            decay = jnp.exp(lax.dot_general(dec_lhs.astype(BF16), dec_rhs, _TN,
                                            preferred_element_type=F32))
            s_new = decay * s_cat + inc
            s_out_ref[2 * r, h] = s_new[:, :HG_DV]
            s_out_ref[2 * r + 1, h] = s_new[:, HG_DV:]
        o = jnp.concatenate(outs, axis=0)
        y_ref[:, cols] = _head_out(o, gate[:, cols], ng[:, cols]).astype(y_ref.dtype)


def _hgrn_sample(proj, lb, norm_g, state):
    rows = HG_SAMPLE_SEQS * DEC_SEQ
    rb = N_PROMPT // rows
    cq = 3 * CONV_CH // HG_W

    def spec(off):
        return pl.BlockSpec((rows, HG_W), lambda g: (rb + g, cq + off))

    st_spec = pl.BlockSpec((HG_SAMPLE_SEQS, HG_HEADS, HG_DK, HG_DV), lambda g: (g, 0, 0, 0))
    return pl.pallas_call(
        _hgrn_sample_kernel,
        grid=(DEC_BATCH // HG_SAMPLE_SEQS,),
        in_specs=[spec(0), spec(1), spec(2), spec(3),
                  pl.BlockSpec((1, HG_W), lambda g: (0, 0)),
                  pl.BlockSpec((1, HG_W), lambda g: (0, 0)),
                  st_spec],
        out_specs=[pl.BlockSpec((rows, HG_W), lambda g: (g, 0)), st_spec],
        out_shape=[
            jax.ShapeDtypeStruct((N_SAMPLE, HG_W), BF16),
            jax.ShapeDtypeStruct((DEC_BATCH, HG_HEADS, HG_DK, HG_DV), F32),
        ],
        compiler_params=_params(("arbitrary",), 40),
        name="hgrn_sample",
    )(proj, proj, proj, proj, lb, norm_g, state)


def _out_proj_kernel(ycp_ref, yhp_ref, xp_ref, ycs_ref, yhs_ref, xs_ref, wc_ref, wh_ref, o_ref, *,
                     n_prompt_tiles):
    m = pl.program_id(1)

    def residual_mix(yc_ref, yh_ref, x_ref):
        mix = (jnp.dot(yc_ref[...], wc_ref[...], preferred_element_type=F32)
               + jnp.dot(yh_ref[...], wh_ref[...], preferred_element_type=F32))
        o_ref[...] = x_ref[...] + mix

    @pl.when(m < n_prompt_tiles)
    def _():
        residual_mix(ycp_ref, yhp_ref, xp_ref)

    @pl.when(m >= n_prompt_tiles)
    def _():
        residual_mix(ycs_ref, yhs_ref, xs_ref)


def _out_proj(yc_p, yh_p, xp, yc_s, yh_s, xs, w):
    npt = N_PROMPT // MM_ROWS

    def prompt_map(n, m):
        return jnp.minimum(m, npt - 1)

    def sample_map(n, m):
        return jnp.maximum(m - npt, 0)

    def source_specs(row_map):
        return [
            pl.BlockSpec((MM_ROWS, CONV_CH), lambda n, m: (row_map(n, m), 0)),
            pl.BlockSpec((MM_ROWS, HG_W), lambda n, m: (row_map(n, m), 0)),
            pl.BlockSpec((MM_ROWS, MM_COLS), lambda n, m: (row_map(n, m), n)),
        ]

    return pl.pallas_call(
        functools.partial(_out_proj_kernel, n_prompt_tiles=npt),
        grid=(D_MODEL // MM_COLS, N_TOK // MM_ROWS),
        in_specs=source_specs(prompt_map) + source_specs(sample_map) + [
            pl.BlockSpec((CONV_CH, MM_COLS), lambda n, m: (0, n)),
            pl.BlockSpec((HG_W, MM_COLS), lambda n, m: (1, n)),
        ],
        out_specs=pl.BlockSpec((MM_ROWS, MM_COLS), lambda n, m: (m, n)),
        out_shape=jax.ShapeDtypeStruct((N_TOK, D_MODEL), F32),
        compiler_params=_params(("arbitrary", "arbitrary"), 56),
        name="out_proj",
    )(yc_p, yh_p, xp, yc_s, yh_s, xs, w, w)


def _split_bf16(x):
    hi = x.astype(BF16)
    return hi, (x - hi.astype(F32)).astype(BF16)


_HI16 = 0xFFFF0000


def _pack_bf16_halves(x):
    half = x.shape[1] // 2
    bits = lax.bitcast_convert_type(x.astype(F32), jnp.uint32)
    return (bits[:, :half] >> 16) | (bits[:, half:] & jnp.uint32(_HI16))


def _unpack_bf16_half(packed, high):
    bits = (packed & jnp.uint32(_HI16)) if high else (packed << 16)
    return lax.bitcast_convert_type(bits, F32)


def _norm_route_kernel(h_ref, g_ref, wr_ref, br_ref, hn_ref, eidx_ref, gate_ref):
    hn = _rmsnorm_rows(h_ref[...], g_ref[...])
    hn_hi, hn_lo = _split_bf16(hn)
    hn_ref[...] = _pack_bf16_halves(hn_hi)
    hi_both = jnp.dot(hn_hi, wr_ref[...], preferred_element_type=F32)
    lo_hi = jnp.dot(hn_lo, wr_ref[:, :ROUTE_COLS], preferred_element_type=F32)
    logits = (hi_both[:, :ROUTE_COLS] + (hi_both[:, ROUTE_COLS:] + lo_hi)) + br_ref[...]
    lane = lax.broadcasted_iota(I32, logits.shape, 1)
    lane_f = lane.astype(F32)
    neg = -jnp.inf

    def first_argmax(vals, vmax):
        first = jnp.min(jnp.where(vals == vmax, lane_f, float(ROUTE_COLS)), axis=-1, keepdims=True)
        return first.astype(I32)

    is_grp = lane < N_GROUPS
    lg = jnp.where(is_grp, logits, neg)
    mg = jnp.max(lg, axis=-1, keepdims=True)
    g_sel = first_argmax(lg, mg)
    p_grp = 1.0 / jnp.sum(jnp.where(is_grp, jnp.exp(logits - mg), 0.0), axis=-1, keepdims=True)

    in_grp = (lane >= N_GROUPS) & (((lane - N_GROUPS) // EXP_PER_GROUP) == g_sel)
    le = jnp.where(in_grp, logits, neg)
    m1 = jnp.max(le, axis=-1, keepdims=True)
    i1 = first_argmax(le, m1)
    le2 = jnp.where(lane == i1, neg, le)
    m2 = jnp.max(le2, axis=-1, keepdims=True)
    i2 = first_argmax(le2, m2)
    e2 = jnp.exp(m2 - m1)
    gate1 = p_grp / (1.0 + e2)
    gate2 = p_grp * e2 / (1.0 + e2)
    eidx_ref[...] = jnp.where(lane == 0, i1 - N_GROUPS, jnp.where(lane == 1, i2 - N_GROUPS, 0))
    gate_ref[...] = jnp.where(lane == 0, gate1, jnp.where(lane == 1, gate2, 0.0))


def _norm_route(h, g, w_r, b_r):
    wr_both = jnp.concatenate(_split_bf16(w_r), axis=1)
    return pl.pallas_call(
        _norm_route_kernel,
        grid=(N_TOK // NORM_ROWS,),
        in_specs=[
            pl.BlockSpec((NORM_ROWS, D_MODEL), lambda i: (i, 0)),
            pl.BlockSpec((1, D_MODEL), lambda i: (0, 0)),
            pl.BlockSpec((D_MODEL, 2 * ROUTE_COLS), lambda i: (0, 0)),
            pl.BlockSpec((1, ROUTE_COLS), lambda i: (0, 0)),
        ],
        out_specs=[
            pl.BlockSpec((NORM_ROWS, D_MODEL // 2), lambda i: (i, 0)),
            pl.BlockSpec((NORM_ROWS, ROUTE_COLS), lambda i: (i, 0)),
            pl.BlockSpec((NORM_ROWS, ROUTE_COLS), lambda i: (i, 0)),
        ],
        out_shape=[
            jax.ShapeDtypeStruct((N_TOK, D_MODEL // 2), jnp.uint32),
            jax.ShapeDtypeStruct((N_TOK, ROUTE_COLS), I32),
            jax.ShapeDtypeStruct((N_TOK, ROUTE_COLS), F32),
        ],
        compiler_params=_params(("arbitrary",), 48),
        name="norm_route",
    )(h, g, wr_both, b_r)


def _row_copy(src_hbm, row, buf, slot, r, sem):
    return pltpu.make_async_copy(src_hbm.at[pl.ds(row, 1)], buf.at[slot, pl.ds(r, 1)], sem.at[slot])


def _gather_issue(idx_ref, base, row0, n_rows, src_hbm, buf, slot, sem):
    for r in range(row0, row0 + n_rows):
        _row_copy(src_hbm, idx_ref[base + r], buf, slot, r, sem).start()


def _gather_wait(row0, n_rows, src_hbm, buf, slot, sem):
    for r in range(row0, row0 + n_rows, SUBLANES):
        rows = pl.ds(r, SUBLANES)
        pltpu.make_async_copy(src_hbm.at[pl.ds(0, SUBLANES)], buf.at[slot, rows], sem.at[slot]).wait()


def _route_plan_kernel(eidx_ref, dest_ref, counts_ref, start_ref, rank_scr):
    n_tiles = N_TOK // PLAN_ROWS
    lane = lax.broadcasted_iota(I32, (PLAN_ROWS, ROUTE_COLS), 1)
    ri = lax.broadcasted_iota(I32, (PLAN_ROWS, PLAN_ROWS), 0)
    ci = lax.broadcasted_iota(I32, (PLAN_ROWS, PLAN_ROWS), 1)
    earlier = jnp.where(ci < ri, 1.0, 0.0).astype(BF16)

    def tile_rows(t):
        return pl.ds(pl.multiple_of(t * PLAN_ROWS, PLAN_ROWS), PLAN_ROWS)

    def pick(vals, col):
        return jnp.sum(jnp.where(lane == col, vals, 0.0), axis=-1, keepdims=True)

    def on_lanes01(v0, v1):
        return jnp.where(lane == 0, v0, jnp.where(lane == 1, v1, 0.0))

    def rank_tile(t, seen):
        e = eidx_ref[tile_rows(t), :]
        e0, e1 = e[:, 0:1], e[:, 1:2]
        uses = jnp.where((lane == e0) | (lane == e1), 1.0, 0.0)
        before = jnp.dot(earlier, uses.astype(BF16), preferred_element_type=F32) + seen
        rank_scr[tile_rows(t), :] = on_lanes01(pick(before, e0), pick(before, e1))
        return seen + jnp.sum(uses, axis=0, keepdims=True)

    counts = lax.fori_loop(0, n_tiles, rank_tile, jnp.zeros((1, ROUTE_COLS), F32))
    padded = jnp.ceil(counts * (1.0 / MOE_BLK)) * MOE_BLK
    ui = lax.broadcasted_iota(I32, (ROUTE_COLS, ROUTE_COLS), 0)
    uj = lax.broadcasted_iota(I32, (ROUTE_COLS, ROUTE_COLS), 1)
    start = jnp.dot(jnp.broadcast_to(padded, (SUBLANES, ROUTE_COLS)), jnp.where(ui < uj, 1.0, 0.0),
                    precision=lax.Precision.HIGHEST, preferred_element_type=F32)[0:1]
    counts_ref[...] = counts.astype(I32)
    start_ref[...] = start.astype(I32)

    def dest_tile(t, carry):
        e = eidx_ref[tile_rows(t), :]
        rank = rank_scr[tile_rows(t), :]
        d0 = pick(start, e[:, 0:1]) + rank[:, 0:1]
        d1 = pick(start, e[:, 1:2]) + rank[:, 1:2]
        dest_ref[tile_rows(t), :] = on_lanes01(d0, d1).astype(I32)
        return carry

    lax.fori_loop(0, n_tiles, dest_tile, 0)


def _route_plan(eidx):
    return pl.pallas_call(
        _route_plan_kernel,
        out_shape=[
            jax.ShapeDtypeStruct((N_TOK, ROUTE_COLS), I32),
            jax.ShapeDtypeStruct((1, ROUTE_COLS), I32),
            jax.ShapeDtypeStruct((1, ROUTE_COLS), I32),
        ],
        scratch_shapes=[pltpu.VMEM((N_TOK, ROUTE_COLS), F32)],
        compiler_params=pltpu.CompilerParams(vmem_limit_bytes=40 * MIB),
        name="route_plan",
    )(eidx)


def _dispatch_kernel(dest_ref, counts_ref, start_ref, hn_hbm, o_ref, row_tok, n_parts, buf, sem):
    b = pl.program_id(0)
    nb = pl.num_programs(0)
    slot = b % 2
    parts_per_blk = MOE_BLK // DISPATCH_PART

    def for_parts(step, fn):
        for sb in range(DISPATCH_BLKS):
            for p in range(parts_per_blk):
                fn(step * DISPATCH_BLKS + sb, p, sb * MOE_BLK + p * DISPATCH_PART)

    def fetch(step, dst_slot):
        def part(blk, p, row0):
            @pl.when(p < n_parts[blk])
            def _():
                _gather_issue(row_tok, step * DISPATCH_BLKS * MOE_BLK, row0, DISPATCH_PART, hn_hbm, buf,
                              dst_slot, sem)

            @pl.when(p >= n_parts[blk])
            def _():
                buf[dst_slot, row0:row0 + DISPATCH_PART] = jnp.zeros(
                    (DISPATCH_PART,) + buf.shape[2:], buf.dtype)

        for_parts(step, part)

    @pl.when(b == 0)
    def _():
        def clear(blk, carry):
            n_parts[blk] = 0
            return carry

        lax.fori_loop(0, MOE_NB, clear, 0)

        def per_expert(e, carry):
            count, first = counts_ref[e], start_ref[e]

            def per_block(i, c):
                rows = jnp.minimum(count - i * MOE_BLK, MOE_BLK)
                parts = (rows + DISPATCH_PART - 1) // DISPATCH_PART
                n_parts[first // MOE_BLK + i] = parts

                def pad_row(r, c2):
                    row_tok[r] = r % N_PROMPT
                    return c2

                row0 = first + i * MOE_BLK
                lax.fori_loop(row0 + rows, row0 + parts * DISPATCH_PART, pad_row, 0)
                return c

            lax.fori_loop(0, (count + MOE_BLK - 1) // MOE_BLK, per_block, 0)
            return carry

        lax.fori_loop(0, N_EXPERTS, per_expert, 0)

        def invert(g, carry):
            for j in range(SUBLANES):
                a = g * SUBLANES + j
                row_tok[dest_ref[a]] = g * (SUBLANES // TOP_K) + j // TOP_K
            return carry

        lax.fori_loop(0, N_ASSIGN // SUBLANES, invert, 0)
        fetch(0, 0)

    @pl.when(b + 1 < nb)
    def _():
        fetch(b + 1, 1 - slot)

    def wait_part(blk, p, row0):
        @pl.when(p < n_parts[blk])
        def _():
            _gather_wait(row0, DISPATCH_PART, hn_hbm, buf, slot, sem)

    for_parts(b, wait_part)
    o_ref[...] = buf[slot]


def _dispatch(dest_flat, counts, start, hn_packed):
    return pl.pallas_call(
        _dispatch_kernel,
        grid_spec=pltpu.PrefetchScalarGridSpec(
            num_scalar_prefetch=3,
            grid=(MOE_NB // DISPATCH_BLKS,),
            in_specs=[pl.BlockSpec(memory_space=pl.ANY)],
            out_specs=pl.BlockSpec((DISPATCH_BLKS * MOE_BLK, D_MODEL // 2), lambda b, *_: (b, 0)),
            scratch_shapes=[pltpu.SMEM((MOE_ROWS,), I32),
                            pltpu.SMEM((MOE_NB,), I32),
                            pltpu.VMEM((2, DISPATCH_BLKS * MOE_BLK, D_MODEL // 2), jnp.uint32),
                            pltpu.SemaphoreType.DMA((2,))],
        ),
        out_shape=jax.ShapeDtypeStruct((MOE_ROWS, D_MODEL // 2), jnp.uint32),
        compiler_params=_params(("arbitrary",), 32, disable_bounds_checks=True),
        name="moe_dispatch",
    )(dest_flat, counts, start, hn_packed)


def _stream_expert_chunks(blk0_ref, n_ref, n_used, in_copy, out_copy, out_buf, compute):
    c = pl.program_id(1)
    n_chunks = pl.num_programs(1)

    def for_blocks(chunk, fn):
        blk0 = blk0_ref[chunk]

        def body(i, carry):
            fn(blk0 + i, i)
            return carry

        lax.fori_loop(0, n_ref[chunk], body, 0)

    def start_loads(chunk):
        for_blocks(chunk, lambda g, i: in_copy(g, chunk % MOE_IN_SLOTS, i).start())

    @pl.when(c == 0)
    def _():
        start_loads(0)
        start_loads(1)

    @pl.when(c + 2 < n_chunks)
    def _():
        start_loads(c + 2)

    in_slot = c % MOE_IN_SLOTS
    out_slot = c % MOE_OUT_SLOTS
    for_blocks(c, lambda g, i: in_copy(g, in_slot, i).wait())

    @pl.when(c >= MOE_OUT_SLOTS)
    def _():
        for_blocks(c - MOE_OUT_SLOTS, lambda g, i: out_copy(g, out_slot, i).wait())

    for m in range(1, MOE_CHUNK_BLKS + 1):
        @pl.when(n_ref[c] == m)
        def _(m=m):
            compute(m, in_slot, out_slot)

    for_blocks(c, lambda g, i: out_copy(g, out_slot, i).start())

    @pl.when(c == n_chunks - 1)
    def _():
        for chunk in (c - 1, c):
            for_blocks(chunk, lambda g, i: out_copy(g, chunk % MOE_OUT_SLOTS, i).wait())
        out_buf[0, 0:MOE_BLK] = jnp.zeros((MOE_BLK,) + out_buf.shape[2:], out_buf.dtype)

        def start_zero(g, carry):
            out_copy(g, 0, 0).start()
            return carry

        def wait_zero(g, carry):
            out_copy(g, 0, 0).wait()
            return carry

        lax.fori_loop(n_used, MOE_NB, start_zero, 0)
        lax.fori_loop(n_used, MOE_NB, wait_zero, 0)


def _block_rows(g):
    return pl.ds(pl.multiple_of(g * MOE_BLK, MOE_BLK), MOE_BLK)


def _slot_rows(i):
    return pl.ds(pl.multiple_of(i * MOE_BLK, MOE_BLK), MOE_BLK)


def _expert_up_kernel(blk0_ref, n_ref, e_ref, nbu_ref, wg_ref, wu_ref, x_hbm, h_hbm, xbuf, hbuf, sem_x, sem_h):
    del e_ref
    cols = pl.ds(pl.multiple_of(pl.program_id(0) * MOE_FC, MOE_FC), MOE_FC)

    def x_copy(g, slot, i):
        return pltpu.make_async_copy(x_hbm.at[_block_rows(g)], xbuf.at[slot, _slot_rows(i)], sem_x.at[slot])

    def h_copy(g, slot, i):
        return pltpu.make_async_copy(hbuf.at[slot, _slot_rows(i)], h_hbm.at[_block_rows(g), cols],
                                     sem_h.at[slot])

    def compute(m, in_slot, out_slot):
        rows = m * MOE_BLK
        kc = D_MODEL // MOE_K_CHUNKS
        half_chunks = MOE_K_CHUNKS // 2
        g = u = None
        for i in range(MOE_K_CHUNKS):
            ks = slice(i * kc, (i + 1) * kc)
            ps = slice((i % half_chunks) * kc, (i % half_chunks + 1) * kc)
            xk = _unpack_bf16_half(xbuf[in_slot, :rows, ps], high=i >= half_chunks)
            gi = jnp.dot(xk, wg_ref[0, ks, :], preferred_element_type=F32)
            ui = jnp.dot(xk, wu_ref[0, ks, :], preferred_element_type=F32)
            g, u = (gi, ui) if g is None else (g + gi, u + ui)
        hbuf[out_slot, :rows] = (g * _sigmoid(g) * u).astype(hbuf.dtype)

    _stream_expert_chunks(blk0_ref, n_ref, nbu_ref[0], x_copy, h_copy, hbuf, compute)


def _chunk_scratch(in_cols, in_dtype, out_cols, out_dtype):
    rows = MOE_CHUNK_BLKS * MOE_BLK
    return [pltpu.VMEM((MOE_IN_SLOTS, rows, in_cols), in_dtype),
            pltpu.VMEM((MOE_OUT_SLOTS, rows, out_cols), out_dtype),
            pltpu.SemaphoreType.DMA((MOE_IN_SLOTS,)),
            pltpu.SemaphoreType.DMA((MOE_OUT_SLOTS,))]


def _expert_up(chunks, n_used, xb, w_g, w_u):
    def w_map(j, c, blk0, n, e, nbu):
        return (e[c], 0, j)

    return pl.pallas_call(
        _expert_up_kernel,
        grid_spec=pltpu.PrefetchScalarGridSpec(
            num_scalar_prefetch=4,
            grid=(D_EXPERT // MOE_FC, MOE_NCH),
            in_specs=[pl.BlockSpec((1, D_MODEL, MOE_FC), w_map),
                      pl.BlockSpec((1, D_MODEL, MOE_FC), w_map),
                      pl.BlockSpec(memory_space=pl.ANY)],
            out_specs=pl.BlockSpec(memory_space=pl.ANY),
            scratch_shapes=_chunk_scratch(D_MODEL // 2, jnp.uint32, MOE_FC, BF16),
        ),
        out_shape=jax.ShapeDtypeStruct((MOE_ROWS, D_EXPERT), BF16),
        compiler_params=_params(("arbitrary", "arbitrary"), 58),
        name="expert_up",
    )(*chunks, n_used, w_g, w_u, xb)


def _expert_down_kernel(blk0_ref, n_ref, e_ref, nbu_ref, wd_hbm, h_hbm, y_hbm, hbuf, ybuf, sem_h, sem_y,
                        wbuf, sem_w):
    c = pl.program_id(1)
    wc = MOE_DC // 2
    cols = pl.ds(pl.multiple_of(pl.program_id(0) * wc, wc), wc)
    w_cols = pl.ds(pl.multiple_of(pl.program_id(0) * MOE_DC, MOE_DC), MOE_DC)

    def w_copy(chunk):
        slot = chunk % MOE_W_SLOTS
        return pltpu.make_async_copy(wd_hbm.at[e_ref[chunk], :, w_cols], wbuf.at[slot], sem_w.at[slot])

    def start_weights(chunk):
        @pl.when(n_ref[chunk] > 0)
        def _():
            w_copy(chunk).start()

    @pl.when(c == 0)
    def _():
        start_weights(0)
        start_weights(1)

    @pl.when(c + 2 < pl.num_programs(1))
    def _():
        start_weights(c + 2)

    @pl.when(n_ref[c] > 0)
    def _():
        w_copy(c).wait()

    def h_copy(g, slot, i):
        return pltpu.make_async_copy(h_hbm.at[_block_rows(g)], hbuf.at[slot, _slot_rows(i)], sem_h.at[slot])

    def y_copy(g, slot, i):
        return pltpu.make_async_copy(ybuf.at[slot, _slot_rows(i)], y_hbm.at[_block_rows(g), cols],
                                     sem_y.at[slot])

    def compute(m, in_slot, out_slot):
        rows = m * MOE_BLK
        y = jnp.dot(hbuf[in_slot, :rows].astype(F32), wbuf[c % MOE_W_SLOTS], preferred_element_type=F32)
        ybuf[out_slot, :rows] = _pack_bf16_halves(y.astype(BF16))

    _stream_expert_chunks(blk0_ref, n_ref, nbu_ref[0], h_copy, y_copy, ybuf, compute)


def _expert_down(chunks, n_used, hb, w_d):
    return pl.pallas_call(
        _expert_down_kernel,
        grid_spec=pltpu.PrefetchScalarGridSpec(
            num_scalar_prefetch=4,
            grid=(D_MODEL // MOE_DC, MOE_NCH),
            in_specs=[pl.BlockSpec(memory_space=pl.ANY), pl.BlockSpec(memory_space=pl.ANY)],
            out_specs=pl.BlockSpec(memory_space=pl.ANY),
            scratch_shapes=_chunk_scratch(D_EXPERT, BF16, MOE_DC // 2, jnp.uint32) + [
                pltpu.VMEM((MOE_W_SLOTS, D_EXPERT, MOE_DC), F32),
                pltpu.SemaphoreType.DMA((MOE_W_SLOTS,))],
        ),
        out_shape=jax.ShapeDtypeStruct((MOE_ROWS, D_MODEL // 2), jnp.uint32),
        compiler_params=_params(("arbitrary", "arbitrary"), 48),
        name="expert_down",
    )(*chunks, n_used, w_d, hb)


def _combine_kernel(dest_ref, h_ref, gate_ref, g_ref, y_hbm, o_ref, buf, sem, *, tile0):
    i = pl.program_id(0)
    n = pl.num_programs(0)
    slot = i % 2
    rows = TOP_K * COMB_ROWS
    base = (tile0 + i) * rows

    @pl.when(i == 0)
    def _():
        _gather_issue(dest_ref, base, 0, rows, y_hbm, buf, 0, sem)

    @pl.when(i + 1 < n)
    def _():
        _gather_issue(dest_ref, base + rows, 0, rows, y_hbm, buf, 1 - slot, sem)

    _gather_wait(0, rows, y_hbm, buf, slot, sem)

    def expert_rows(k):
        words = buf[slot, k * COMB_ROWS:(k + 1) * COMB_ROWS]
        wc = MOE_DC // 2
        parts = []
        for p in range(D_MODEL // MOE_DC):
            w = words[:, p * wc:(p + 1) * wc]
            parts += [_unpack_bf16_half(w, high=False), _unpack_bf16_half(w, high=True)]
        return jnp.concatenate(parts, axis=1)

    gates = gate_ref[...]
    ff = gates[:, 0:1] * expert_rows(0) + gates[:, 1:2] * expert_rows(1)
    o_ref[...] = _rmsnorm_rows(h_ref[...] + ff, g_ref[...])


def _combine(dest, h, gates, g_final, y_buf, tile0, n_tiles):
    return pl.pallas_call(
        functools.partial(_combine_kernel, tile0=tile0),
        grid_spec=pltpu.PrefetchScalarGridSpec(
            num_scalar_prefetch=1,
            grid=(n_tiles,),
            in_specs=[
                pl.BlockSpec((COMB_ROWS, D_MODEL), lambda i, d: (tile0 + i, 0)),
                pl.BlockSpec((COMB_ROWS, ROUTE_COLS), lambda i, d: (tile0 + i, 0)),
                pl.BlockSpec((1, D_MODEL), lambda i, d: (0, 0)),
                pl.BlockSpec(memory_space=pl.ANY),
            ],
            out_specs=pl.BlockSpec((COMB_ROWS, D_MODEL), lambda i, d: (i, 0)),
            scratch_shapes=[pltpu.VMEM((2, TOP_K * COMB_ROWS, D_MODEL // 2), jnp.uint32),
                            pltpu.SemaphoreType.DMA((2,))],
        ),
        out_shape=jax.ShapeDtypeStruct((n_tiles * COMB_ROWS, D_MODEL), F32),
        compiler_params=_params(("arbitrary",), 56, disable_bounds_checks=True),
        name="moe_combine",
    )(dest, h, gates, g_final, y_buf)


def _chunk_list(counts, start):
    blk0 = start // MOE_BLK
    nblk = (counts + MOE_BLK - 1) // MOE_BLK
    n_used = (blk0[-1] + nblk[-1]).astype(I32)
    nch = (nblk + MOE_CHUNK_BLKS - 1) // MOE_CHUNK_BLKS
    ch_end = jnp.cumsum(nch)
    ch = jnp.arange(MOE_NCH, dtype=I32)
    ch_e = jnp.minimum(jnp.searchsorted(ch_end, ch, side='right'), N_EXPERTS - 1)
    local = ch - (ch_end - nch)[ch_e]
    live = ch < ch_end[-1]
    ch_blk0 = (blk0[ch_e] + MOE_CHUNK_BLKS * local).astype(I32)
    ch_n = jnp.where(live, jnp.clip(nblk[ch_e] - MOE_CHUNK_BLKS * local, 0, MOE_CHUNK_BLKS), 0).astype(I32)
    ch_e = jnp.where(live, ch_e, ch_e[ch_end[-1] - 1]).astype(I32)
    return (ch_blk0, ch_n, ch_e), n_used.reshape(1)


def kernel(x_prompt, x_sample, cache_conv, state_hgrn, norm_mix_g, w_in, conv_w, lb_param, hg_norm_g,
           w_out, norm_ffn_g, w_group_router, b_group_router, w_expert_router, b_expert_router,
           w_exp_gate, w_exp_up, w_exp_down, norm_final_g):
    xp = x_prompt.reshape(N_PROMPT, D_MODEL)
    xs = x_sample.reshape(N_SAMPLE, D_MODEL)
    lb = jnp.cumsum(jax.nn.softmax(lb_param.astype(F32), axis=0), axis=0)[0].reshape(1, HG_W)

    xn = _norm_in(xp, xs, norm_mix_g[0].reshape(1, D_MODEL))
    proj = _in_proj(xn, w_in[0])

    buf = cache_conv[0]
    zeros = jnp.zeros((DEC_BATCH, DEC_SEQ - 1, CONV_CH), F32)
    e1 = jnp.concatenate([buf[:, 1:2], zeros], axis=1).reshape(N_SAMPLE, CONV_CH)
    e2 = jnp.concatenate([buf, zeros[:, :DEC_SEQ - 2]], axis=1).reshape(N_SAMPLE, CONV_CH)
    yc_p, conv_p = _conv_prompt(proj, conv_w[0])
    yc_s, u_s = _conv_sample(proj, conv_w[0], e1, e2)
    conv_s = u_s.reshape(DEC_BATCH, DEC_SEQ, CONV_CH)[:, DEC_SEQ - (CONV_K - 1):]

    ng = hg_norm_g[0].reshape(1, HG_W)
    yh_p, hgrn_p = _hgrn_prompt(proj, lb, ng)
    yh_s, hgrn_s = _hgrn_sample(proj, lb, ng, state_hgrn[0])

    h = _out_proj(yc_p, yh_p, xp, yc_s, yh_s, xs, w_out[0].astype(BF16))

    pad = jnp.zeros((D_MODEL, ROUTE_COLS - N_GROUPS - N_EXPERTS), F32)
    w_r = jnp.concatenate([w_group_router[0], w_expert_router[0], pad], axis=1)
    b_r = jnp.concatenate([b_group_router[0], b_expert_router[0], pad[0]]).reshape(1, ROUTE_COLS)
    hn_packed, eidx, gates = _norm_route(h, norm_ffn_g[0].reshape(1, D_MODEL), w_r, b_r)
    dest, counts, start = _route_plan(eidx)
    dest = dest[:, :TOP_K]
    counts, start = counts[0, :N_EXPERTS], start[0, :N_EXPERTS]
    chunks, n_used = _chunk_list(counts, start)
    xb = _dispatch(dest.reshape(N_ASSIGN), counts, start, hn_packed)
    hb = _expert_up(chunks, n_used, xb, w_exp_gate[0], w_exp_up[0])
    yb = _expert_down(chunks, n_used, hb, w_exp_down[0])

    dest_t = dest.reshape(N_TOK // COMB_ROWS, COMB_ROWS, TOP_K).transpose(0, 2, 1).reshape(N_ASSIGN)
    gf = norm_final_g.reshape(1, D_MODEL)
    y_p = _combine(dest_t, h, gates, gf, yb, 0, N_PROMPT // COMB_ROWS)
    y_s = _combine(dest_t, h, gates, gf, yb, N_PROMPT // COMB_ROWS, N_SAMPLE // COMB_ROWS)

    return (y_p.reshape(BATCH, SEQ, D_MODEL), y_s.reshape(DEC_BATCH, DEC_SEQ, D_MODEL),
            conv_p[None], hgrn_p[None], conv_s[None], hgrn_s[None])
```
